```python
import jax
import jax.numpy as jnp
from jax import lax
import numpy as np

D_MODEL = 1024
BATCH = 4
SEQ = 4096
DEPTH = 2
DEC_BATCH = 32
DEC_SEQ = 8
PAST_LEN = 8192
PAGE_SIZE = 128

N_MIXERS = 3
N_HEADS = 8
N_KV_HEADS = 4
GROUP = N_HEADS // N_KV_HEADS
HEAD_DIM = 64
MIX_WIDTH = N_HEADS * HEAD_DIM
KV_WIDTH = N_KV_HEADS * HEAD_DIM
ROT_DIM = HEAD_DIM // 4
ROPE_THETA = 500000.0
ATTN_SCALE = HEAD_DIM ** -0.5
MOBA_BLOCK = 256
MOBA_TOPK = 3
MOBA_Q_BLOCK = 64
IDX_HEADS = 8
IDX_DIM = 64
IDX_SCALE = (IDX_DIM * IDX_HEADS) ** -0.5
DSA_TOPK = 256
Q_BLOCK = 128
NORM_EPS = 1e-6
FORGET_BIAS = 3.0
SPLIT_SIZES = (
    MIX_WIDTH, KV_WIDTH, KV_WIDTH, MIX_WIDTH,
    MIX_WIDTH, KV_WIDTH, KV_WIDTH, N_HEADS, MIX_WIDTH,
    MIX_WIDTH, KV_WIDTH, KV_WIDTH, IDX_HEADS * IDX_DIM, IDX_DIM, IDX_HEADS, MIX_WIDTH,
    N_MIXERS * D_MODEL,
)
N_IN = sum(SPLIT_SIZES)

kernel_name = 'hybrid_moba_fox_dsa_adaln_step'


def rms_norm(x, w):
    x32 = x.astype(jnp.float32)
    y = x32 * lax.rsqrt(jnp.mean(x32 * x32, axis=-1, keepdims=True) + NORM_EPS)
    return (y * w.astype(jnp.float32)).astype(x.dtype)


def partial_rope(x, pos):
    half = ROT_DIM // 2
    expo = jnp.arange(0, ROT_DIM, 2, dtype=jnp.float32) / ROT_DIM
    inv_freq = jnp.power(jnp.float32(ROPE_THETA), -expo)
    ang = pos.astype(jnp.float32)[:, None] * inv_freq[None, :]
    cos = jnp.cos(ang)[None, :, None, :].astype(x.dtype)
    sin = jnp.sin(ang)[None, :, None, :].astype(x.dtype)
    x1, x2, rest = x[..., :half], x[..., half:ROT_DIM], x[..., ROT_DIM:]
    return jnp.concatenate([x1 * cos - x2 * sin, x2 * cos + x1 * sin, rest], axis=-1)


def sweep_queries(fn, q_args, q_pos, blk):
    T = q_pos.shape[0]
    if T <= blk or T % blk:
        return fn(*q_args, q_pos)
    nb = T // blk
    split = lambda a: jnp.swapaxes(a.reshape(a.shape[0], nb, blk, *a.shape[2:]), 0, 1)
    out = lax.map(lambda args: fn(*args), tuple(split(a) for a in q_args) + (q_pos.reshape(nb, blk),))
    out = jnp.swapaxes(out, 0, 1)
    return out.reshape(out.shape[0], T, *out.shape[3:])


def gather_pages(pool, page_table):
    g = pool[page_table]
    return g.reshape(g.shape[0], g.shape[1] * g.shape[2], *g.shape[3:])


def moba_blocks(k, v):
    B, L, KV, D = k.shape
    nb = -(-L // MOBA_BLOCK)
    pad = nb * MOBA_BLOCK - L
    def blockify(a):
        a = jnp.pad(a, ((0, 0), (0, pad), (0, 0), (0, 0)))
        return a.reshape(B, nb, MOBA_BLOCK, KV, D).transpose(0, 3, 1, 2, 4)
    kb, vb = blockify(k), blockify(v)
    kmean = jnp.mean(kb.astype(jnp.float32), axis=3)
    return kb, vb, kmean


def moba_attend(q, q_pos, kb, vb, kmean):
    B, T = q.shape[:2]
    nb = kb.shape[2]
    qg = q.reshape(B, T, N_KV_HEADS, GROUP, HEAD_DIM)
    gate = jnp.einsum('btkgd,bknd->bkgtn', qg.astype(jnp.float32), kmean)
    own = q_pos // MOBA_BLOCK
    past_ok = jnp.arange(nb)[None, :] < own[:, None]
    gate = jnp.where(past_ok, gate, -jnp.inf)
    gval, gidx = lax.top_k(gate, min(MOBA_TOPK, nb))
    own_b = jnp.broadcast_to(own[:, None], gidx.shape[:-1] + (1,)).astype(gidx.dtype)
    sel = jnp.concatenate([gidx, own_b], axis=-1)
    sel_ok = jnp.concatenate([gval > -jnp.inf, jnp.ones(own_b.shape, bool)], axis=-1)
    b_i = jnp.arange(B)[:, None, None, None, None]
    k_i = jnp.arange(N_KV_HEADS)[None, :, None, None, None]
    kg = kb[b_i, k_i, sel]
    vg = vb[b_i, k_i, sel]
    kpos = sel[..., None] * MOBA_BLOCK + jnp.arange(MOBA_BLOCK)
    mask = sel_ok[..., None] & (kpos <= q_pos[:, None, None])
    s = jnp.einsum('btkgd,bkgtjcd->bkgtjc', qg, kg).astype(jnp.float32) * ATTN_SCALE
    s = jnp.where(mask, s, -jnp.inf)
    shp = s.shape
    p = jax.nn.softmax(s.reshape(*shp[:4], -1), axis=-1).reshape(shp).astype(vg.dtype)
    o = jnp.einsum('bkgtjc,bkgtjcd->btkgd', p, vg)
    return o.reshape(B, T, N_HEADS, HEAD_DIM)


def fox_attend(q, q_cum, q_pos, k, v, k_cum):
    B, T = q.shape[:2]
    L = k.shape[1]
    qg = q.reshape(B, T, N_KV_HEADS, GROUP, HEAD_DIM)
    s = jnp.einsum('btkgd,blkd->bkgtl', qg, k).astype(jnp.float32) * ATTN_SCALE
    qc = q_cum.reshape(B, T, N_KV_HEADS, GROUP).transpose(0, 2, 3, 1)[..., None]
    kc = k_cum.reshape(B, L, N_KV_HEADS, GROUP).transpose(0, 2, 3, 1)[..., None, :]
    mask = jnp.arange(L)[None, :] <= q_pos[:, None]
    s = jnp.where(mask, s + (qc - kc), -jnp.inf)
    p = jax.nn.softmax(s, axis=-1).astype(v.dtype)
    o = jnp.einsum('bkgtl,blkd->btkgd', p, v)
    return o.reshape(B, T, N_HEADS, HEAD_DIM)


def dsa_attend(q, q_idx, w_idx, q_pos, k, v, k_idx):
    B, T = q.shape[:2]
    L = k.shape[1]
    topk = min(DSA_TOPK, L // 4)
    sc = jnp.einsum('bthe,ble->bthl', q_idx.astype(jnp.float32), k_idx.astype(jnp.float32))
    isc = jnp.einsum('bthl,bth->btl', jax.nn.relu(sc), w_idx.astype(jnp.float32)) * IDX_SCALE
    mask = jnp.arange(L)[None, :] <= q_pos[:, None]
    isc = jnp.where(mask, isc, -jnp.inf)
    _, idx = lax.top_k(isc, topk)
    b_i = jnp.arange(B)[:, None, None]
    kg = k[b_i, idx]
    vg = v[b_i, idx]
    qg = q.reshape(B, T, N_KV_HEADS, GROUP, HEAD_DIM)
    s = jnp.einsum('btkgd,btnkd->btkgn', qg, kg).astype(jnp.float32) * ATTN_SCALE
    ok = (idx <= q_pos[None, :, None])[:, :, None, None, :]
    s = jnp.where(ok, s, -jnp.inf)
    p = jax.nn.softmax(s, axis=-1).astype(vg.dtype)
    o = jnp.einsum('btkgn,btnkd->btkgd', p, vg)
    return o.reshape(B, T, N_HEADS, HEAD_DIM)


def hybrid_layer(x, c, pos, past, norm_w, w_ada, b_ada, w_in, b_f, w_branch, w_out):
    B, T, _ = x.shape
    shift, scale, gate = jnp.split(jax.nn.silu(c) @ w_ada + b_ada, 3, axis=-1)
    h = rms_norm(x, norm_w) * (1 + scale[:, None, :]) + shift[:, None, :]
    offs = np.cumsum(SPLIT_SIZES)[:-1].tolist()
    (qa, ka, va, za, qb, kb, vb, fb, zb,
     qc, kc, vc, qi, ki, wi, zc, g) = jnp.split(h @ w_in, offs, axis=-1)
    heads = lambda a, n: a.reshape(B, T, n, HEAD_DIM)
    qa = partial_rope(heads(qa, N_HEADS), pos)
    ka = partial_rope(heads(ka, N_KV_HEADS), pos)
    va = heads(va, N_KV_HEADS)
    qb, kb, vb = heads(qb, N_HEADS), heads(kb, N_KV_HEADS), heads(vb, N_KV_HEADS)
    logf = jax.nn.log_sigmoid(fb.astype(jnp.float32) + b_f.astype(jnp.float32))
    qc = partial_rope(heads(qc, N_HEADS), pos)
    kc = partial_rope(heads(kc, N_KV_HEADS), pos)
    vc = heads(vc, N_KV_HEADS)
    qi = partial_rope(qi.reshape(B, T, IDX_HEADS, IDX_DIM), pos)
    ki = partial_rope(ki[:, :, None, :], pos)[:, :, 0, :]
    kv_new = jnp.stack([jnp.stack([ka, va], 2), jnp.stack([kb, vb], 2), jnp.stack([kc, vc], 2)], 2)
    if past is None:
        kv_all, logf_all, kidx_all = kv_new, logf, ki
    else:
        kv_past, logf_past, kidx_past = past
        kv_all = jnp.concatenate([kv_past.astype(kv_new.dtype), kv_new], axis=1)
        logf_all = jnp.concatenate([logf_past.astype(jnp.float32), logf], axis=1)
        kidx_all = jnp.concatenate([kidx_past.astype(ki.dtype), ki], axis=1)
    a_kb, a_vb, a_kmean = moba_blocks(kv_all[:, :, 0, 0], kv_all[:, :, 0, 1])
    o_a = sweep_queries(lambda q, p: moba_attend(q, p, a_kb, a_vb, a_kmean), (qa,), pos, MOBA_Q_BLOCK)
    cum = jnp.cumsum(logf_all, axis=1)
    b_k, b_v = kv_all[:, :, 1, 0], kv_all[:, :, 1, 1]
    o_b = sweep_queries(lambda q, qcum, p: fox_attend(q, qcum, p, b_k, b_v, cum),
                        (qb, cum[:, -T:]), pos, Q_BLOCK)
    c_k, c_v = kv_all[:, :, 2, 0], kv_all[:, :, 2, 1]
    o_c = sweep_queries(lambda q, q_i, w_i, p: dsa_attend(q, q_i, w_i, p, c_k, c_v, kidx_all),
                        (qc, qi, wi), pos, Q_BLOCK)
    gates = jnp.split(g, N_MIXERS, axis=-1)
    branches = [jax.nn.sigmoid(gates[i]) * ((o.reshape(B, T, MIX_WIDTH) * jax.nn.silu(z)) @ w_branch[i])
                for i, (o, z) in enumerate(zip((o_a, o_b, o_c), (za, zb, zc)))]
    merged = branches[0] + branches[1] + branches[2]
    x = x + gate[:, None, :] * (merged @ w_out)
    return x, kv_new, logf.astype(x.dtype), ki


def setup_inputs(seed: int = 0) -> dict:
    key = jax.random.key(seed)
    ks = jax.random.split(key, 18)
    nrm = jax.random.normal
    n_pages = PAST_LEN // PAGE_SIZE
    n_pool = (DEC_BATCH * n_pages * 5) // 4
    x_prompt = nrm(ks[0], (BATCH, SEQ, D_MODEL), jnp.float32)
    x_sample = nrm(ks[1], (DEC_BATCH, DEC_SEQ, D_MODEL), jnp.float32)
    cache_kv = nrm(ks[2], (DEPTH, n_pool, PAGE_SIZE, N_MIXERS, 2, N_KV_HEADS, HEAD_DIM), jnp.float32)
    cache_logf = jax.nn.log_sigmoid(FORGET_BIAS + nrm(ks[3], (DEPTH, n_pool, PAGE_SIZE, N_HEADS), jnp.float32))
    cache_kidx = nrm(ks[4], (DEPTH, n_pool, PAGE_SIZE, IDX_DIM), jnp.float32)
    page_table = jax.random.permutation(ks[5], n_pool)[:DEC_BATCH * n_pages].reshape(DEC_BATCH, n_pages).astype(jnp.int32)
    c_prompt = nrm(ks[6], (BATCH, D_MODEL), jnp.float32)
    c_sample = nrm(ks[7], (DEC_BATCH, D_MODEL), jnp.float32)
    norm_w = 1.0 + 0.02 * nrm(ks[8], (DEPTH, D_MODEL), jnp.float32)
    w_ada = nrm(ks[9], (DEPTH, D_MODEL, 3 * D_MODEL), jnp.float32) * (0.5 * D_MODEL ** -0.5)
    b_ada = 0.02 * nrm(ks[10], (DEPTH, 3 * D_MODEL), jnp.float32)
    w_in = nrm(ks[11], (DEPTH, D_MODEL, N_IN), jnp.float32) * D_MODEL ** -0.5
    b_f = FORGET_BIAS + 0.1 * nrm(ks[12], (DEPTH, N_HEADS), jnp.float32)
    w_branch = nrm(ks[13], (DEPTH, N_MIXERS, MIX_WIDTH, D_MODEL), jnp.float32) * MIX_WIDTH ** -0.5
    w_out = nrm(ks[14], (DEPTH, D_MODEL, D_MODEL), jnp.float32) * D_MODEL ** -0.5
    final_norm_w = 1.0 + 0.02 * nrm(ks[15], (D_MODEL,), jnp.float32)
    return {'x_prompt': x_prompt, 'x_sample': x_sample, 'cache_kv': cache_kv, 'cache_logf': cache_logf,
            'cache_kidx': cache_kidx, 'page_table': page_table, 'c_prompt': c_prompt, 'c_sample': c_sample,
            'norm_w': norm_w, 'w_ada': w_ada, 'b_ada': b_ada, 'w_in': w_in, 'b_f': b_f,
            'w_branch': w_branch, 'w_out': w_out, 'final_norm_w': final_norm_w}


def reference(x_prompt, x_sample, cache_kv, cache_logf, cache_kidx, page_table, c_prompt, c_sample,
              norm_w, w_ada, b_ada, w_in, b_f, w_branch, w_out, final_norm_w):
    past_len = page_table.shape[1] * PAGE_SIZE
    pos_p = jnp.arange(x_prompt.shape[1], dtype=jnp.int32)
    pos_s = past_len + jnp.arange(x_sample.shape[1], dtype=jnp.int32)
    xp, xs = x_prompt, x_sample
    new_p, new_s = [], []
    for l in range(DEPTH):
        lw = (norm_w[l], w_ada[l], b_ada[l], w_in[l], b_f[l], w_branch[l], w_out[l])
        xp, kv_p, lf_p, ki_p = hybrid_layer(xp, c_prompt, pos_p, None, *lw)
        past = (gather_pages(cache_kv[l], page_table),
                gather_pages(cache_logf[l], page_table),
                gather_pages(cache_kidx[l], page_table))
        xs, kv_s, lf_s, ki_s = hybrid_layer(xs, c_sample, pos_s, past, *lw)
        new_p.append((kv_p, lf_p, ki_p))
        new_s.append((kv_s, lf_s, ki_s))
    kv_prompt = jnp.stack([n[0] for n in new_p])
    logf_prompt = jnp.stack([n[1] for n in new_p])
    kidx_prompt = jnp.stack([n[2] for n in new_p])
    kv_sample = jnp.stack([n[0] for n in new_s])
    logf_sample = jnp.stack([n[1] for n in new_s])
    kidx_sample = jnp.stack([n[2] for n in new_s])
    y_prompt = rms_norm(xp, final_norm_w)
    y_sample = rms_norm(xs, final_norm_w)
    return (y_prompt, y_sample, kv_prompt, logf_prompt, kidx_prompt, kv_sample, logf_sample, kidx_sample)
```

```python
import functools

import numpy as np
import jax
import jax.numpy as jnp
from jax import lax
from jax.experimental import pallas as pl
from jax.experimental.pallas import tpu as pltpu

F32 = jnp.float32
BF16 = jnp.bfloat16
HIGHEST = lax.Precision.HIGHEST

N_MIXERS = 3
N_HEADS = 8
N_KV_HEADS = 4
HEAD_DIM = 64
MIX_WIDTH = N_HEADS * HEAD_DIM
KV_WIDTH = N_KV_HEADS * HEAD_DIM
ROT_DIM = HEAD_DIM // 4
ROPE_THETA = 500000.0
ATTN_SCALE = HEAD_DIM ** -0.5
MOBA_BLOCK = 256
MOBA_TOPK = 3
IDX_HEADS = 8
IDX_DIM = 64
IDX_SCALE = (IDX_DIM * IDX_HEADS) ** -0.5
DSA_TOPK = 256
NORM_EPS = 1e-6
PAGE_SIZE = 128

LANES = 128
KMEAN_ROWS = 128
NEG = -1e30
F32_LOWEST = -3.0e38
VMEM_LIMIT = 56 * 1024 * 1024

_NT = (((1,), (1,)), ((), ()))


def _cp(*sem):
    return pltpu.CompilerParams(dimension_semantics=sem, vmem_limit_bytes=VMEM_LIMIT)


def _sigmoid(x):
    return 1.0 / (1.0 + jnp.exp(-x))


def _mod_kernel(c_ref, w_ref, b_ref, o_ref):
    c = c_ref[...]
    sc = c * _sigmoid(c)
    o_ref[0] = jnp.dot(sc, w_ref[0], precision=HIGHEST, preferred_element_type=F32) + b_ref[0]


def _modulation(c_all, w_ada, b_ada):
    depth, d, d3 = w_ada.shape
    rc = c_all.shape[0]
    nj = d3 // d
    return pl.pallas_call(
        _mod_kernel,
        grid=(depth, nj),
        in_specs=[
            pl.BlockSpec((rc, d), lambda l, j: (0, 0)),
            pl.BlockSpec((1, d, d), lambda l, j: (l, 0, j)),
            pl.BlockSpec((1, 1, d), lambda l, j: (l, 0, j)),
        ],
        out_specs=pl.BlockSpec((1, rc, d), lambda l, j: (l, 0, j)),
        out_shape=jax.ShapeDtypeStruct((depth, rc, d3), F32),
        compiler_params=_cp("arbitrary", "arbitrary"),
        name="adaln_mod",
    )(c_all, w_ada, b_ada.reshape(depth, 1, d3))


def _norm_kernel(x_ref, w_ref, sc_ref, sh_ref, h_ref):
    x = x_ref[0]
    ms = jnp.mean(x * x, axis=-1, keepdims=True)
    y = x * lax.rsqrt(ms + NORM_EPS) * w_ref[...]
    h_ref[0] = (y * (1.0 + sc_ref[0]) + sh_ref[0]).astype(BF16)


def _norm_mod(x3, norm_w, scale3, shift3, tm):
    g, r, d = x3.shape
    rm = scale3.shape[1]
    tmod = 1 if rm == 1 else tm
    mod_map = (lambda b, i: (b, 0, 0)) if rm == 1 else (lambda b, i: (b, i, 0))
    return pl.pallas_call(
        _norm_kernel,
        grid=(g, r // tm),
        in_specs=[
            pl.BlockSpec((1, tm, d), lambda b, i: (b, i, 0)),
            pl.BlockSpec((1, d), lambda b, i: (0, 0)),
            pl.BlockSpec((1, tmod, d), mod_map),
            pl.BlockSpec((1, tmod, d), mod_map),
        ],
        out_specs=pl.BlockSpec((1, tm, d), lambda b, i: (b, i, 0)),
        out_shape=jax.ShapeDtypeStruct((g, r, d), BF16),
        compiler_params=_cp("arbitrary", "arbitrary"),
        name="norm_mod",
    )(x3, norm_w.reshape(1, d), scale3, shift3)


def _rope_chunk(x, a, bm, cm):
    return x * a + pltpu.roll(x, LANES - ROT_DIM // 2, 1) * bm + pltpu.roll(x, ROT_DIM // 2, 1) * cm


def _projq_kernel(h_ref, w_ref, ra_ref, rb_ref, rc_ref, qa_ref, qb_ref, qc_ref, qaf_ref):
    y = jnp.dot(h_ref[...], w_ref[...], preferred_element_type=F32)
    a, bm, cm = ra_ref[...], rb_ref[...], rc_ref[...]
    for m, o_ref in enumerate((qa_ref, qb_ref, qc_ref)):
        for c in range(MIX_WIDTH // LANES):
            x = y[:, m * MIX_WIDTH + c * LANES:m * MIX_WIDTH + (c + 1) * LANES]
            if m != 1:
                x = _rope_chunk(x, a, bm, cm)
            if m == 0:
                qaf_ref[:, c * LANES:(c + 1) * LANES] = x
            o_ref[:, c * LANES:(c + 1) * LANES] = (x * ATTN_SCALE).astype(BF16)


def _projkv_kernel(h_ref, w_ref, ra_ref, rb_ref, rc_ref, kv_ref, kvb_ref):
    y = jnp.dot(h_ref[...], w_ref[...], preferred_element_type=F32)
    a, bm, cm = ra_ref[...], rb_ref[...], rc_ref[...]
    per_mixer = 2 * KV_WIDTH // LANES
    for c in range(N_MIXERS * per_mixer):
        x = y[:, c * LANES:(c + 1) * LANES]
        mixer, within = divmod(c, per_mixer)
        if mixer != 1 and within < KV_WIDTH // LANES:
            x = _rope_chunk(x, a, bm, cm)
        kv_ref[:, c * LANES:(c + 1) * LANES] = x
        kvb_ref[:, c * LANES:(c + 1) * LANES] = x.astype(BF16)


def _projz_kernel(h_ref, w_ref, o_ref):
    y = jnp.dot(h_ref[...], w_ref[...], preferred_element_type=F32)
    o_ref[...] = y * _sigmoid(y)


def _projg_kernel(h_ref, w_ref, o_ref):
    y = jnp.dot(h_ref[...], w_ref[...], preferred_element_type=F32)
    o_ref[...] = _sigmoid(y)


MISC_KI = MIX_WIDTH
MISC_WI = MISC_KI + LANES
MISC_FB = MISC_WI + LANES
MISC_N = MISC_FB + LANES


def _projmisc_kernel(h_ref, w_ref, ra_ref, rb_ref, rc_ref, bf_ref,
                     qi_ref, ki_ref, ki2_ref, wi_ref, lf_ref):
    y = jnp.dot(h_ref[...], w_ref[...], preferred_element_type=F32)
    a, bm, cm = ra_ref[...], rb_ref[...], rc_ref[...]
    for c in range(MIX_WIDTH // LANES):
        x = _rope_chunk(y[:, c * LANES:(c + 1) * LANES], a, bm, cm)
        qi_ref[:, c * LANES:(c + 1) * LANES] = x.astype(BF16)
    ki = _rope_chunk(y[:, MISC_KI:MISC_KI + LANES], a, bm, cm)
    ki_ref[...] = ki[:, :IDX_DIM]
    ki2_ref[...] = (ki + pltpu.roll(ki, IDX_DIM, 1)).astype(BF16)
    wi_ref[...] = y[:, MISC_WI:MISC_WI + IDX_HEADS]
    f = y[:, MISC_FB:MISC_FB + LANES] + bf_ref[...]
    logf = jnp.minimum(f, 0.0) - jnp.log(1.0 + jnp.exp(-jnp.abs(f)))
    lf_ref[...] = logf[:, :N_HEADS]


def _proj_call(kernel, h, w, tabs, extra, outs, tm, tn, name):
    m, d = h.shape
    n = w.shape[1]
    nt = tabs[0].shape[0] // tm if tabs else 1
    in_specs = [pl.BlockSpec((tm, d), lambda j, i: (i, 0)),
                pl.BlockSpec((d, tn), lambda j, i: (0, j))]
    in_specs += [pl.BlockSpec((tm, LANES), lambda j, i: (i % nt, 0)) for _ in tabs]
    in_specs += [pl.BlockSpec(e.shape, lambda j, i: (0, 0)) for e in extra]
    out_specs = [pl.BlockSpec((tm, wd if full else tn), (lambda j, i: (i, 0)) if full else (lambda j, i: (i, j)))
                 for (wd, _, full) in outs]
    out_shape = [jax.ShapeDtypeStruct((m, wd), dt) for (wd, dt, _) in outs]
    return pl.pallas_call(
        kernel,
        grid=(n // tn, m // tm),
        in_specs=in_specs,
        out_specs=out_specs,
        out_shape=out_shape,
        compiler_params=_cp("arbitrary", "arbitrary"),
        name=name,
    )(h, w, *tabs, *extra)


def _kmean_kernel(k_ref, o_ref):
    o_ref[0, 0] = jnp.sum(k_ref[0], axis=0, keepdims=True) * (1.0 / MOBA_BLOCK)


def _kmean_prompt(kv3):
    b, t, w = kv3.shape
    nb = t // MOBA_BLOCK
    out = pl.pallas_call(
        _kmean_kernel,
        grid=(b, nb),
        in_specs=[pl.BlockSpec((1, MOBA_BLOCK, KV_WIDTH), lambda i, n: (i, n, 0))],
        out_specs=pl.BlockSpec((1, 1, 1, KV_WIDTH), lambda i, n: (i, n, 0, 0)),
        out_shape=jax.ShapeDtypeStruct((b, nb, 1, KV_WIDTH), F32),
        compiler_params=_cp("arbitrary", "arbitrary"),
        name="moba_kmean",
    )(kv3)
    return out.reshape(b, nb, KV_WIDTH)


def _kmean_paged_kernel(pt_ref, p0_ref, p1_ref, o_ref):
    del pt_ref
    s = jnp.sum(p0_ref[0, 0], axis=0, keepdims=True) + jnp.sum(p1_ref[0, 0], axis=0, keepdims=True)
    o_ref[0, 0] = s * (1.0 / MOBA_BLOCK)


def _kmean_sample(cache4, page_table, layer):
    nseq, n_pages = page_table.shape
    ppb = MOBA_BLOCK // PAGE_SIZE
    nb = n_pages // ppb
    grid_spec = pltpu.PrefetchScalarGridSpec(
        num_scalar_prefetch=1,
        grid=(nseq, nb),
        in_specs=[
            pl.BlockSpec((1, 1, PAGE_SIZE, KV_WIDTH), lambda i, n, pt: (layer, pt[i, ppb * n], 0, 0)),
            pl.BlockSpec((1, 1, PAGE_SIZE, KV_WIDTH), lambda i, n, pt: (layer, pt[i, ppb * n + 1], 0, 0)),
        ],
        out_specs=pl.BlockSpec((1, 1, 1, KV_WIDTH), lambda i, n, pt: (i, n, 0, 0)),
    )
    out = pl.pallas_call(
        _kmean_paged_kernel,
        grid_spec=grid_spec,
        out_shape=jax.ShapeDtypeStruct((nseq, nb, 1, KV_WIDTH), F32),
        compiler_params=_cp("arbitrary", "arbitrary"),
        name="moba_kmean_paged",
    )(page_table, cache4, cache4)
    return out.reshape(nseq, nb, KV_WIDTH)


def _gather_kernel(pt_ref, kv_ref, kx_ref, lf_ref, nkv_ref, nkx_ref, nlf_ref,
                   okv_ref, okx_ref, olf_ref, *, n_pages, t_new):
    del pt_ref
    j = pl.program_id(1)

    @pl.when(j < n_pages)
    def _():
        okv_ref[0] = kv_ref[0, 0].astype(BF16)
        kx = kx_ref[0, 0]
        okx_ref[0] = jnp.concatenate([kx, kx], axis=1).astype(BF16)
        olf_ref[0] = lf_ref[0, 0]

    @pl.when(j == n_pages)
    def _():
        pad = PAGE_SIZE - t_new
        kv = jnp.concatenate([nkv_ref[0], jnp.zeros((pad, nkv_ref.shape[2]), F32)], axis=0)
        okv_ref[0] = kv.astype(BF16)
        kx = jnp.concatenate([nkx_ref[0], jnp.zeros((pad, IDX_DIM), F32)], axis=0)
        okx_ref[0] = jnp.concatenate([kx, kx], axis=1).astype(BF16)
        olf_ref[0] = nlf_ref[0]

    @pl.when(j > n_pages)
    def _():
        okv_ref[...] = jnp.zeros(okv_ref.shape, BF16)
        okx_ref[...] = jnp.zeros(okx_ref.shape, BF16)
        olf_ref[...] = jnp.zeros(olf_ref.shape, F32)


def _gather_pages(cache4, cache_kx, cache_lft, page_table, layer, new_kv, new_kx, new_lft, lp):
    nseq, n_pages = page_table.shape
    t_new = new_kv.shape[1]
    width = cache4.shape[-1]
    last = n_pages - 1
    pg = lambda i, j, pt: (layer, pt[i, jnp.minimum(j, last)], 0, 0)
    grid_spec = pltpu.PrefetchScalarGridSpec(
        num_scalar_prefetch=1,
        grid=(nseq, lp // PAGE_SIZE),
        in_specs=[
            pl.BlockSpec((1, 1, PAGE_SIZE, width), pg),
            pl.BlockSpec((1, 1, PAGE_SIZE, IDX_DIM), pg),
            pl.BlockSpec((1, 1, N_HEADS, PAGE_SIZE), pg),
            pl.BlockSpec((1, t_new, width), lambda i, j, pt: (i, 0, 0)),
            pl.BlockSpec((1, t_new, IDX_DIM), lambda i, j, pt: (i, 0, 0)),
            pl.BlockSpec((1, N_HEADS, PAGE_SIZE), lambda i, j, pt: (i, 0, 0)),
        ],
        out_specs=[
            pl.BlockSpec((1, PAGE_SIZE, width), lambda i, j, pt: (i, j, 0)),
            pl.BlockSpec((1, PAGE_SIZE, 2 * IDX_DIM), lambda i, j, pt: (i, j, 0)),
            pl.BlockSpec((1, N_HEADS, PAGE_SIZE), lambda i, j, pt: (i, 0, j)),
        ],
    )
    return pl.pallas_call(
        functools.partial(_gather_kernel, n_pages=n_pages, t_new=t_new),
        grid_spec=grid_spec,
        out_shape=[
            jax.ShapeDtypeStruct((nseq, lp, width), BF16),
            jax.ShapeDtypeStruct((nseq, lp, 2 * IDX_DIM), BF16),
            jax.ShapeDtypeStruct((nseq, N_HEADS, lp), F32),
        ],
        compiler_params=_cp("arbitrary", "arbitrary"),
        name="page_gather",
    )(page_table, cache4, cache_kx, cache_lft, new_kv, new_kx, new_lft)


CUM_CHUNK = 256


def _cumsum_kernel(x_ref, o_ref):
    n = x_ref.shape[2] // CUM_CHUNK
    r = lax.broadcasted_iota(jnp.int32, (CUM_CHUNK, CUM_CHUNK), 0)
    c = lax.broadcasted_iota(jnp.int32, (CUM_CHUNK, CUM_CHUNK), 1)
    tri = (r <= c).astype(F32)

    def body(i, carry):
        st = pl.multiple_of(i * CUM_CHUNK, CUM_CHUNK)
        x = x_ref[0, :, pl.ds(st, CUM_CHUNK)]
        y = jnp.dot(x, tri, precision=HIGHEST, preferred_element_type=F32) + carry
        o_ref[0, :, pl.ds(st, CUM_CHUNK)] = y
        return y[:, CUM_CHUNK - 1:CUM_CHUNK]

    lax.fori_loop(0, n, body, jnp.zeros((x_ref.shape[1], 1), F32))


def _cumsum(lft):
    b, h, lp = lft.shape
    return pl.pallas_call(
        _cumsum_kernel,
        grid=(b,),
        in_specs=[pl.BlockSpec((1, h, lp), lambda i: (i, 0, 0))],
        out_specs=pl.BlockSpec((1, h, lp), lambda i: (i, 0, 0)),
        out_shape=jax.ShapeDtypeStruct((b, h, lp), F32),
        compiler_params=_cp("arbitrary"),
        name="fox_cumsum",
    )(lft)


def _stack_heads(qblk, par):
    lane = lax.broadcasted_iota(jnp.int32, qblk.shape, 1)
    lo = lane < HEAD_DIM
    rolled = pltpu.roll(qblk, HEAD_DIM, 1)
    if par == 0:
        top, bot = jnp.where(lo, qblk, 0.0), jnp.where(lo, rolled, 0.0)
    else:
        top, bot = jnp.where(lo, 0.0, rolled), jnp.where(lo, 0.0, qblk)
    return jnp.concatenate([top, bot], axis=0)


def _unstack_heads(o, par, tq):
    top, bot = o[:tq], o[tq:]
    lane = lax.broadcasted_iota(jnp.int32, top.shape, 1)
    lo = lane < HEAD_DIM
    if par == 0:
        return jnp.where(lo, top, pltpu.roll(bot, HEAD_DIM, 1))
    return jnp.where(lo, pltpu.roll(top, HEAD_DIM, 1), bot)


def _online(s, v, m, l, acc):
    m_new = jnp.maximum(m, jnp.max(s, axis=-1, keepdims=True))
    alpha = jnp.exp(m - m_new)
    p = jnp.exp(s - m_new)
    l = alpha * l + jnp.sum(p, axis=-1, keepdims=True)
    acc = alpha * acc + jnp.dot(p.astype(BF16), v, preferred_element_type=F32)
    return m_new, l, acc


def _init_state(rows):
    return (jnp.full((rows, 1), -jnp.inf, F32), jnp.zeros((rows, 1), F32), jnp.zeros((rows, LANES), F32))


def _qpos2(q_lo, tq):
    r = lax.broadcasted_iota(jnp.int32, (tq, 1), 0) + q_lo
    return jnp.concatenate([r, r], axis=0)


def _moba_kernel(qf_ref, q_ref, km_ref, k_ref, v_ref, o_ref, *, tq, q_off, topk):
    q_lo = q_off + pl.program_id(1) * tq
    own = q_lo // MOBA_BLOCK
    own_st = pl.multiple_of(own * MOBA_BLOCK, MOBA_BLOCK)
    rows = 2 * tq
    lane = lax.broadcasted_iota(jnp.int32, (rows, LANES), 1)
    lane_f = lane.astype(F32)
    qpos = _qpos2(q_lo, tq)
    kpos_own = own * MOBA_BLOCK + lax.broadcasted_iota(jnp.int32, (rows, MOBA_BLOCK), 1)
    for g in range(N_KV_HEADS):
        c, par = divmod(g, 2)
        cs = slice(c * LANES, (c + 1) * LANES)
        gs = slice(g * LANES, (g + 1) * LANES)
        qs = _stack_heads(q_ref[0, :, gs].astype(F32), par).astype(BF16)
        qfs = _stack_heads(qf_ref[0, :, gs], par)
        gate = lax.dot_general(qfs, km_ref[0, :, cs], _NT, precision=HIGHEST, preferred_element_type=F32)
        gate = jnp.where(lane < own, gate, -jnp.inf)
        selb = jnp.full((rows, LANES), NEG, F32)
        for _ in range(topk):
            mx = jnp.max(gate, axis=-1, keepdims=True)
            is_max = (gate == mx) & (mx > -jnp.inf)
            first = jnp.min(jnp.where(is_max, lane_f, 1e9), axis=-1, keepdims=True)
            pick = lane_f == first
            selb = jnp.where(pick, 0.0, selb)
            gate = jnp.where(pick, -jnp.inf, gate)

        kt = k_ref[0, pl.ds(own_st, MOBA_BLOCK), cs]
        vt = v_ref[0, pl.ds(own_st, MOBA_BLOCK), cs]
        s = lax.dot_general(qs, kt, _NT, preferred_element_type=F32)
        s = jnp.where(kpos_own <= qpos, s, NEG)
        state = _online(s, vt, *_init_state(rows))

        def body(j, st, qs=qs, selb=selb, cs=cs):
            start = pl.multiple_of(j * MOBA_BLOCK, MOBA_BLOCK)
            kt = k_ref[0, pl.ds(start, MOBA_BLOCK), cs]
            vt = v_ref[0, pl.ds(start, MOBA_BLOCK), cs]
            bias = jnp.sum(jnp.where(lane == j, selb, 0.0), axis=-1, keepdims=True)
            s = lax.dot_general(qs, kt, _NT, preferred_element_type=F32) + bias
            return _online(s, vt, *st)

        _, l, acc = lax.fori_loop(0, own, body, state)
        o_ref[0, :, gs] = _unstack_heads(acc / l, par, tq)


def _moba_attn(qf, q, km, kvb, tq, q_off):
    b, t, _ = q.shape
    lp = kvb.shape[1]
    nb_real = lp // MOBA_BLOCK
    kern = functools.partial(_moba_kernel, tq=tq, q_off=q_off, topk=min(MOBA_TOPK, nb_real))
    return pl.pallas_call(
        kern,
        grid=(b, t // tq),
        in_specs=[
            pl.BlockSpec((1, tq, MIX_WIDTH), lambda i, j: (i, j, 0)),
            pl.BlockSpec((1, tq, MIX_WIDTH), lambda i, j: (i, j, 0)),
            pl.BlockSpec((1, KMEAN_ROWS, KV_WIDTH), lambda i, j: (i, 0, 0)),
            pl.BlockSpec((1, lp, KV_WIDTH), lambda i, j: (i, 0, 0)),
            pl.BlockSpec((1, lp, KV_WIDTH), lambda i, j: (i, 0, 1)),
        ],
        out_specs=pl.BlockSpec((1, tq, MIX_WIDTH), lambda i, j: (i, j, 0)),
        out_shape=jax.ShapeDtypeStruct((b, t, MIX_WIDTH), F32),
        compiler_params=_cp("arbitrary", "arbitrary"),
        name="moba_attn",
    )(qf, q, km, kvb, kvb)


def _fox_kernel(q_ref, cum_ref, k_ref, v_ref, o_ref, *, tq, tk, q_off):
    q_lo = q_off + pl.program_id(1) * tq
    n_full = q_lo // tk
    n_all = (q_lo + tq + tk - 1) // tk
    rows = 2 * tq
    qpos = _qpos2(q_lo, tq)
    kiota = lax.broadcasted_iota(jnp.int32, (rows, tk), 1)
    for g in range(N_KV_HEADS):
        c, par = divmod(g, 2)
        cs = slice(c * LANES, (c + 1) * LANES)
        gs = slice(g * LANES, (g + 1) * LANES)
        qs = _stack_heads(q_ref[0, :, gs].astype(F32), par).astype(BF16)

        def scores(j, qs=qs, cs=cs, g=g):
            start = pl.multiple_of(j * tk, tk)
            kt = k_ref[0, pl.ds(start, tk), cs]
            vt = v_ref[0, pl.ds(start, tk), cs]
            s = lax.dot_general(qs, kt, _NT, preferred_element_type=F32)
            c0 = cum_ref[0, 2 * g:2 * g + 1, pl.ds(start, tk)]
            c1 = cum_ref[0, 2 * g + 1:2 * g + 2, pl.ds(start, tk)]
            s = jnp.concatenate([s[:tq] - c0, s[tq:] - c1], axis=0)
            return s, vt

        def full_body(j, st, scores=scores):
            s, vt = scores(j)
            return _online(s, vt, *st)

        def diag_body(j, st, scores=scores):
            s, vt = scores(j)
            s = jnp.where(j * tk + kiota <= qpos, s, NEG)
            return _online(s, vt, *st)

        state = lax.fori_loop(0, n_full, full_body, _init_state(rows))
        _, l, acc = lax.fori_loop(n_full, n_all, diag_body, state)
        o_ref[0, :, gs] = _unstack_heads(acc / l, par, tq)


def _fox_attn(q, cum, kvb, tq, tk, q_off):
    b, t, _ = q.shape
    lp = kvb.shape[1]
    kern = functools.partial(_fox_kernel, tq=tq, tk=tk, q_off=q_off)
    return pl.pallas_call(
        kern,
        grid=(b, t // tq),
        in_specs=[
            pl.BlockSpec((1, tq, MIX_WIDTH), lambda i, j: (i, j, 0)),
            pl.BlockSpec((1, N_HEADS, lp), lambda i, j: (i, 0, 0)),
            pl.BlockSpec((1, lp, KV_WIDTH), lambda i, j: (i, 0, 2)),
            pl.BlockSpec((1, lp, KV_WIDTH), lambda i, j: (i, 0, 3)),
        ],
        out_specs=pl.BlockSpec((1, tq, MIX_WIDTH), lambda i, j: (i, j, 0)),
        out_shape=jax.ShapeDtypeStruct((b, t, MIX_WIDTH), F32),
        compiler_params=_cp("arbitrary", "arbitrary"),
        name="fox_attn",
    )(q, cum, kvb, kvb)


def _dsa_kernel(qi_ref, w_ref, kx_ref, q_ref, k_ref, v_ref, o_ref, isc_ref, *, tq, tk, q_off, topk, idx_bits):
    q_lo = q_off + pl.program_id(1) * tq
    n_all = (q_lo + tq + tk - 1) // tk
    rows = 2 * tq
    nl = tk // LANES
    lane = lax.broadcasted_iota(jnp.int32, (tq, LANES), 1)
    lo = lane < HEAD_DIM
    qpos1 = lax.broadcasted_iota(jnp.int32, (tq, 1), 0) + q_lo
    kiota = lax.broadcasted_iota(jnp.int32, (tq, tk), 1)

    pieces, wpieces = [], []
    for h in range(IDX_HEADS):
        chunk = qi_ref[0, :, (h // 2) * LANES:(h // 2 + 1) * LANES].astype(F32)
        pieces.append(jnp.where(lo, chunk, 0.0) if h % 2 == 0 else jnp.where(lo, 0.0, chunk))
        wpieces.append(jnp.broadcast_to(w_ref[0, :, h:h + 1], (tq, LANES)))
    qstack = jnp.concatenate(pieces, axis=0).astype(BF16)
    wb = jnp.concatenate(wpieces, axis=0)
    wfull = jnp.concatenate([wb] * nl, axis=1)

    def idx_body(j, carry):
        start = pl.multiple_of(j * tk, tk)
        sc = lax.dot_general(qstack, kx_ref[0, pl.ds(start, tk), :], _NT, preferred_element_type=F32)
        contrib = jnp.maximum(sc, 0.0) * wfull
        isc = contrib[:tq]
        for h in range(1, IDX_HEADS):
            isc = isc + contrib[h * tq:(h + 1) * tq]
        isc = jnp.where(j * tk + kiota <= qpos1, isc * IDX_SCALE, -jnp.inf)
        isc_ref[:, pl.ds(start, tk)] = isc
        return carry

    lax.fori_loop(0, n_all, idx_body, 0)

    def count(pred):
        def body(j, acc):
            start = pl.multiple_of(j * tk, tk)
            x = isc_ref[:, pl.ds(start, tk)]
            for cc in range(nl):
                idx = j * tk + cc * LANES + lane
                acc = acc + jnp.where(pred(x[:, cc * LANES:(cc + 1) * LANES], idx), 1.0, 0.0)
            return acc
        acc = lax.fori_loop(0, n_all, body, jnp.zeros((tq, LANES), F32))
        return jnp.sum(acc, axis=-1, keepdims=True)

    def key_to_float(key):
        bits = jnp.where(key < 0, key & jnp.int32(0x7FFFFFFF), ~key)
        return lax.bitcast_convert_type(bits, F32)

    need = jnp.float32(topk)

    def bit_body(i, key):
        cand = key | lax.shift_left(jnp.int32(1), 31 - i)
        thr_b = jnp.broadcast_to(key_to_float(cand), (tq, LANES))
        cnt = count(lambda x, idx: x >= thr_b)
        return jnp.where(cnt >= need, cand, key)

    key = lax.fori_loop(0, 32, bit_body, jnp.zeros((tq, 1), jnp.int32))
    thr = key_to_float(key)
    thr_b = jnp.broadcast_to(thr, (tq, LANES))
    cnt_ge = count(lambda x, idx: x >= thr_b)
    few = qpos1 + 1 <= topk
    has_ties = jnp.max(jnp.where(few, 0.0, cnt_ge - need)) > 0.0

    def tie_cut(_):
        r = need - count(lambda x, idx: x > thr_b)

        def jb(i, cut):
            cand = cut | lax.shift_left(jnp.int32(1), idx_bits - 1 - i)
            cand_b = jnp.broadcast_to(cand, (tq, LANES))
            cnt = count(lambda x, idx: (x == thr_b) & (idx < cand_b))
            return jnp.where(cnt < r, cand, cut)

        return lax.fori_loop(0, idx_bits, jb, jnp.zeros((tq, 1), jnp.int32))

    cut = lax.cond(has_ties, tie_cut, lambda _: jnp.full((tq, 1), 2 ** 30, jnp.int32), 0)
    thr = jnp.where(few, F32_LOWEST, thr)
    cut = jnp.where(few, 2 ** 30, cut)
    thr_t = jnp.broadcast_to(thr, (tq, tk))
    cut_t = jnp.broadcast_to(cut, (tq, tk))

    def bias_body(j, carry):
        start = pl.multiple_of(j * tk, tk)
        x = isc_ref[:, pl.ds(start, tk)]
        keep = (x > thr_t) | ((x == thr_t) & (j * tk + kiota <= cut_t))
        isc_ref[:, pl.ds(start, tk)] = jnp.where(keep, 0.0, NEG)
        return carry

    lax.fori_loop(0, n_all, bias_body, 0)

    for g in range(N_KV_HEADS):
        c, par = divmod(g, 2)
        cs = slice(c * LANES, (c + 1) * LANES)
        gs = slice(g * LANES, (g + 1) * LANES)
        qs = _stack_heads(q_ref[0, :, gs].astype(F32), par).astype(BF16)

        def body(j, st, qs=qs, cs=cs):
            start = pl.multiple_of(j * tk, tk)
            kt = k_ref[0, pl.ds(start, tk), cs]
            vt = v_ref[0, pl.ds(start, tk), cs]
            bias = isc_ref[:, pl.ds(start, tk)]
            s = lax.dot_general(qs, kt, _NT, preferred_element_type=F32)
            s = s + jnp.concatenate([bias, bias], axis=0)
            return _online(s, vt, *st)

        _, l, acc = lax.fori_loop(0, n_all, body, _init_state(rows))
        o_ref[0, :, gs] = _unstack_heads(acc / l, par, tq)


def _dsa_attn(qi, wi, kx2, q, kvb, tq, tk, q_off, l_real):
    b, t, _ = q.shape
    lp = kvb.shape[1]
    topk = min(DSA_TOPK, l_real // 4)
    kern = functools.partial(_dsa_kernel, tq=tq, tk=tk, q_off=q_off, topk=topk,
                             idx_bits=max(1, int(lp).bit_length()))
    return pl.pallas_call(
        kern,
        grid=(b, t // tq),
        in_specs=[
            pl.BlockSpec((1, tq, MIX_WIDTH), lambda i, j: (i, j, 0)),
            pl.BlockSpec((1, tq, IDX_HEADS), lambda i, j: (i, j, 0)),
            pl.BlockSpec((1, lp, 2 * IDX_DIM), lambda i, j: (i, 0, 0)),
            pl.BlockSpec((1, tq, MIX_WIDTH), lambda i, j: (i, j, 0)),
            pl.BlockSpec((1, lp, KV_WIDTH), lambda i, j: (i, 0, 4)),
            pl.BlockSpec((1, lp, KV_WIDTH), lambda i, j: (i, 0, 5)),
        ],
        out_specs=pl.BlockSpec((1, tq, MIX_WIDTH), lambda i, j: (i, j, 0)),
        out_shape=jax.ShapeDtypeStruct((b, t, MIX_WIDTH), F32),
        scratch_shapes=[pltpu.VMEM((tq, lp), F32)],
        compiler_params=_cp("arbitrary", "arbitrary"),
        name="dsa_attn",
    )(qi, wi, kx2, q, kvb, kvb)


def _out_kernel(oa_ref, ob_ref, oc_ref, sz_ref, sg_ref, x_ref, gate_ref, wb_ref, wo_ref, fw_ref, y_ref,
                *, final):
    d = x_ref.shape[2]
    merged = None
    for i, o_ref in enumerate((oa_ref, ob_ref, oc_ref)):
        t = (o_ref[0] * sz_ref[0, :, i * MIX_WIDTH:(i + 1) * MIX_WIDTH]).astype(BF16)
        br = sg_ref[0, :, i * d:(i + 1) * d] * jnp.dot(t, wb_ref[i], preferred_element_type=F32)
        merged = br if merged is None else merged + br
    y = x_ref[0] + gate_ref[0] * jnp.dot(merged.astype(BF16), wo_ref[...], preferred_element_type=F32)
    if final:
        ms = jnp.mean(y * y, axis=-1, keepdims=True)
        y = y * lax.rsqrt(ms + NORM_EPS) * fw_ref[...]
    y_ref[0] = y


def _out_proj(oa, ob, oc, sz, sg, x3, gate3, wb, wo, fw, tm, final):
    g, r, d = x3.shape
    rm = gate3.shape[1]
    tmod = 1 if rm == 1 else tm
    mod_map = (lambda b, i: (b, 0, 0)) if rm == 1 else (lambda b, i: (b, i, 0))
    row = lambda w: pl.BlockSpec((1, tm, w), lambda b, i: (b, i, 0))
    as3 = lambda a: a.reshape(g, r, a.shape[-1])
    return pl.pallas_call(
        functools.partial(_out_kernel, final=final),
        grid=(g, r // tm),
        in_specs=[
            row(MIX_WIDTH), row(MIX_WIDTH), row(MIX_WIDTH), row(N_MIXERS * MIX_WIDTH), row(N_MIXERS * d), row(d),
            pl.BlockSpec((1, tmod, d), mod_map),
            pl.BlockSpec(wb.shape, lambda b, i: (0, 0, 0)),
            pl.BlockSpec(wo.shape, lambda b, i: (0, 0)),
            pl.BlockSpec((1, d), lambda b, i: (0, 0)),
        ],
        out_specs=row(d),
        out_shape=jax.ShapeDtypeStruct((g, r, d), F32),
        compiler_params=_cp("arbitrary", "arbitrary"),
        name="out_proj",
    )(as3(oa), as3(ob), as3(oc), as3(sz), as3(sg), x3, gate3, wb, wo, fw.reshape(1, d))


def _rope_tables(pos):
    half = ROT_DIM // 2
    expo = jnp.arange(0, ROT_DIM, 2, dtype=F32) / ROT_DIM
    inv_freq = jnp.power(jnp.float32(ROPE_THETA), -expo)
    ang = pos.astype(F32)[:, None] * inv_freq[None, :]
    cos, sin = jnp.cos(ang), jnp.sin(ang)
    n = pos.shape[0]
    rest = HEAD_DIM - ROT_DIM
    a = jnp.concatenate([cos, cos, jnp.ones((n, rest), F32)], axis=1)
    bm = jnp.concatenate([-sin, jnp.zeros((n, half + rest), F32)], axis=1)
    cm = jnp.concatenate([jnp.zeros((n, half), F32), sin, jnp.zeros((n, rest), F32)], axis=1)
    rep = LANES // HEAD_DIM
    return tuple(jnp.tile(t, (1, rep)) for t in (a, bm, cm))


def _split_weights(w_in_l, b_f_l):
    o = np.cumsum([0, MIX_WIDTH, KV_WIDTH, KV_WIDTH, MIX_WIDTH,
                   MIX_WIDTH, KV_WIDTH, KV_WIDTH, N_HEADS, MIX_WIDTH,
                   MIX_WIDTH, KV_WIDTH, KV_WIDTH, IDX_HEADS * IDX_DIM, IDX_DIM, IDX_HEADS, MIX_WIDTH]).tolist()
    col = lambda i: w_in_l[:, o[i]:o[i + 1]]
    qa, ka, va, za, qb, kb, vb, fb, zb, qc, kc, vc, qi, ki, wi, zc = (col(i) for i in range(16))
    g = w_in_l[:, o[16]:]
    d = w_in_l.shape[0]
    zpad = lambda w: jnp.zeros((d, w), w_in_l.dtype)
    w_q = jnp.concatenate([qa, qb, qc], axis=1)
    w_kv = jnp.concatenate([ka, va, kb, vb, kc, vc], axis=1)
    w_z = jnp.concatenate([za, zb, zc], axis=1)
    w_misc = jnp.concatenate([qi, ki, zpad(LANES - IDX_DIM), wi, zpad(LANES - IDX_HEADS),
                              fb, zpad(LANES - N_HEADS)], axis=1)
    bf_row = jnp.concatenate([b_f_l, jnp.zeros((LANES - N_HEADS,), b_f_l.dtype)]).reshape(1, LANES)
    cast = lambda w: w.astype(BF16)
    return cast(w_q), cast(w_kv), cast(w_z), cast(g), cast(w_misc), bf_row.astype(F32)


def _layer(x3, scale3, shift3, gate3, tabs, weights, norm_w_l, wb_l, wo_l, fw, *, seq, tm, tm_out, tq, tk,
           q_off, past, final):
    g, r, d = x3.shape
    m = g * r
    nseq = m // seq
    w_q, w_kv, w_z, w_g, w_misc, bf_row = weights
    h = _norm_mod(x3, norm_w_l, scale3, shift3, tm).reshape(m, d)

    qa, qb, qc, qaf = _proj_call(
        _projq_kernel, h, w_q, tabs, (), [(MIX_WIDTH, BF16, True)] * 3 + [(MIX_WIDTH, F32, True)],
        tm, w_q.shape[1], "proj_q")
    kv, kvb = _proj_call(
        _projkv_kernel, h, w_kv, tabs, (), [(w_kv.shape[1], F32, True), (w_kv.shape[1], BF16, True)],
        tm, w_kv.shape[1], "proj_kv")
    (sz,) = _proj_call(_projz_kernel, h, w_z, (), (), [(w_z.shape[1], F32, True)], tm, w_z.shape[1], "proj_z")
    (sg,) = _proj_call(_projg_kernel, h, w_g, (), (), [(w_g.shape[1], F32, False)], tm, d, "proj_g")
    qi, ki, ki2, wi, lf = _proj_call(
        _projmisc_kernel, h, w_misc, tabs, (bf_row,),
        [(MIX_WIDTH, BF16, True), (IDX_DIM, F32, True), (2 * IDX_DIM, BF16, True),
         (IDX_HEADS, F32, True), (N_HEADS, F32, True)],
        tm, MISC_N, "proj_misc")

    kv3 = kv.reshape(nseq, seq, kv.shape[1])
    if past is None:
        l_real = seq
        kvb_all = kvb.reshape(nseq, seq, kvb.shape[1])
        kx2_all = ki2.reshape(nseq, seq, 2 * IDX_DIM)
        lft_all = jnp.swapaxes(lf.reshape(nseq, seq, N_HEADS), 1, 2)
        km = _kmean_prompt(kv3)
    else:
        cache4, cache_kx, cache_lft, page_table, layer = past
        l_real = page_table.shape[1] * PAGE_SIZE + seq
        lp = -(-l_real // MOBA_BLOCK) * MOBA_BLOCK
        new_lft = jnp.swapaxes(lf.reshape(nseq, seq, N_HEADS), 1, 2)
        new_lft = jnp.pad(new_lft, ((0, 0), (0, 0), (0, PAGE_SIZE - seq)))
        kvb_all, kx2_all, lft_all = _gather_pages(
            cache4, cache_kx, cache_lft, page_table, layer, kv3, ki.reshape(nseq, seq, IDX_DIM), new_lft, lp)
        km = _kmean_sample(cache4, page_table, layer)
    km = jnp.pad(km, ((0, 0), (0, KMEAN_ROWS - km.shape[1]), (0, 0)))
    cum = _cumsum(lft_all)

    s3 = lambda a: a.reshape(nseq, seq, a.shape[-1])
    o_a = _moba_attn(s3(qaf), s3(qa), km, kvb_all, tq, q_off)
    o_b = _fox_attn(s3(qb), cum, kvb_all, tq, tk, q_off)
    o_c = _dsa_attn(s3(qi), s3(wi), kx2_all, s3(qc), kvb_all, tq, tk, q_off, l_real)

    x_new = _out_proj(o_a, o_b, o_c, sz, sg, x3, gate3, wb_l, wo_l, fw, tm_out, final)
    return x_new, kv3, lf.reshape(nseq, seq, N_HEADS), ki.reshape(nseq, seq, IDX_DIM)


def kernel(x_prompt, x_sample, cache_kv, cache_logf, cache_kidx, page_table, c_prompt, c_sample,
           norm_w, w_ada, b_ada, w_in, b_f, w_branch, w_out, final_norm_w):
    depth = norm_w.shape[0]
    bp, tp, d = x_prompt.shape
    bs, ts, _ = x_sample.shape
    n_pool = cache_kv.shape[1]
    past_len = page_table.shape[1] * PAGE_SIZE
    kv_cols = N_MIXERS * 2 * KV_WIDTH

    nc = bp + bs
    rc = -(-nc // 8) * 8
    c_all = jnp.concatenate([c_prompt, c_sample, jnp.zeros((rc - nc, d), F32)], axis=0)
    mod = _modulation(c_all, w_ada, b_ada)

    tabs_p = _rope_tables(jnp.arange(tp, dtype=jnp.int32))
    tabs_s = _rope_tables(jnp.tile(past_len + jnp.arange(ts, dtype=jnp.int32), bs))

    cache4 = cache_kv.reshape(depth, n_pool, PAGE_SIZE, kv_cols)
    cache_lft = jnp.swapaxes(cache_logf, 2, 3)

    tm_p = min(512, tp)
    tq_p = min(128, tp)
    tk = MOBA_BLOCK
    ms = bs * ts

    xp = x_prompt
    xs = x_sample.reshape(1, ms, d)
    new_p, new_s = [], []
    for l in range(depth):
        weights = _split_weights(w_in[l], b_f[l])
        wb_l = w_branch[l].astype(BF16)
        wo_l = w_out[l].astype(BF16)
        final = l == depth - 1
        shift, scale, gate = (mod[l, :, i * d:(i + 1) * d] for i in range(3))
        p3 = lambda a: a[:bp].reshape(bp, 1, d)
        s3 = lambda a: jnp.repeat(a[bp:nc], ts, axis=0).reshape(1, ms, d)

        xp, kv_p, lf_p, ki_p = _layer(
            xp, p3(scale), p3(shift), p3(gate), tabs_p, weights, norm_w[l], wb_l, wo_l, final_norm_w,
            seq=tp, tm=tm_p, tm_out=min(256, tp), tq=tq_p, tk=tk, q_off=0, past=None, final=final)
        xs, kv_s, lf_s, ki_s = _layer(
            xs, s3(scale), s3(shift), s3(gate), tabs_s, weights, norm_w[l], wb_l, wo_l, final_norm_w,
            seq=ts, tm=ms, tm_out=ms, tq=ts, tk=tk, q_off=past_len,
            past=(cache4, cache_kidx, cache_lft, page_table, l), final=final)
        new_p.append((kv_p, lf_p, ki_p))
        new_s.append((kv_s, lf_s, ki_s))

    kv_shape = (N_MIXERS, 2, N_KV_HEADS, HEAD_DIM)
    stack = lambda items, i: jnp.stack([n[i] for n in items])
    kv_prompt = stack(new_p, 0).reshape(depth, bp, tp, *kv_shape)
    kv_sample = stack(new_s, 0).reshape(depth, bs, ts, *kv_shape)
    return (xp, xs.reshape(bs, ts, d), kv_prompt, stack(new_p, 1), stack(new_p, 2),
            kv_sample, stack(new_s, 1), stack(new_s, 2))
```

```python
import functools

import numpy as np
import jax
import jax.numpy as jnp
from jax import lax
from jax.experimental import pallas as pl
from jax.experimental.pallas import tpu as pltpu

F32 = jnp.float32
BF16 = jnp.bfloat16
HIGHEST = lax.Precision.HIGHEST

N_MIXERS = 3
N_HEADS = 8
N_KV_HEADS = 4
HEAD_DIM = 64
MIX_WIDTH = N_HEADS * HEAD_DIM
KV_WIDTH = N_KV_HEADS * HEAD_DIM
MIXER_COLS = 2 * KV_WIDTH
ROT_DIM = HEAD_DIM // 4
ROPE_THETA = 500000.0
ATTN_SCALE = HEAD_DIM ** -0.5
MOBA_BLOCK = 256
MOBA_TOPK = 3
IDX_HEADS = 8
IDX_DIM = 64
IDX_SCALE = (IDX_DIM * IDX_HEADS) ** -0.5
DSA_TOPK = 256
NORM_EPS = 1e-6
PAGE_SIZE = 128

LANES = 128
SUBLANES = 8
KMEAN_ROWS = 128
NEG = -1e30
F32_LOWEST = -3.0e38
VMEM_LIMIT = 56 * 1024 * 1024
INDEX_PAGES_PER_STEP = 8

_NT = (((1,), (1,)), ((), ()))


def _cp(*sem):
    return pltpu.CompilerParams(dimension_semantics=sem, vmem_limit_bytes=VMEM_LIMIT)


def _sigmoid(x):
    return 1.0 / (1.0 + jnp.exp(-x))


def _mod_kernel(c_ref, w_ref, b_ref, o_ref):
    c = c_ref[...]
    sc = c * _sigmoid(c)
    o_ref[0] = jnp.dot(sc, w_ref[0], precision=HIGHEST, preferred_element_type=F32) + b_ref[0]


def _modulation(c_all, w_ada, b_ada):
    depth, d, d3 = w_ada.shape
    rc = c_all.shape[0]
    nj = d3 // d
    return pl.pallas_call(
        _mod_kernel,
        grid=(depth, nj),
        in_specs=[
            pl.BlockSpec((rc, d), lambda l, j: (0, 0)),
            pl.BlockSpec((1, d, d), lambda l, j: (l, 0, j)),
            pl.BlockSpec((1, 1, d), lambda l, j: (l, 0, j)),
        ],
        out_specs=pl.BlockSpec((1, rc, d), lambda l, j: (l, 0, j)),
        out_shape=jax.ShapeDtypeStruct((depth, rc, d3), F32),
        compiler_params=_cp("arbitrary", "arbitrary"),
        name="adaln_mod",
    )(c_all, w_ada, b_ada.reshape(depth, 1, d3))


def _norm_kernel(x_ref, w_ref, sc_ref, sh_ref, h_ref):
    x = x_ref[0]
    ms = jnp.mean(x * x, axis=-1, keepdims=True)
    y = x * lax.rsqrt(ms + NORM_EPS) * w_ref[...]
    h_ref[0] = (y * (1.0 + sc_ref[0]) + sh_ref[0]).astype(BF16)


def _norm_mod(x3, norm_w, scale3, shift3, tm):
    g, r, d = x3.shape
    rm = scale3.shape[1]
    tmod = 1 if rm == 1 else tm
    mod_map = (lambda b, i: (b, 0, 0)) if rm == 1 else (lambda b, i: (b, i, 0))
    return pl.pallas_call(
        _norm_kernel,
        grid=(g, r // tm),
        in_specs=[
            pl.BlockSpec((1, tm, d), lambda b, i: (b, i, 0)),
            pl.BlockSpec((1, d), lambda b, i: (0, 0)),
            pl.BlockSpec((1, tmod, d), mod_map),
            pl.BlockSpec((1, tmod, d), mod_map),
        ],
        out_specs=pl.BlockSpec((1, tm, d), lambda b, i: (b, i, 0)),
        out_shape=jax.ShapeDtypeStruct((g, r, d), BF16),
        compiler_params=_cp("arbitrary", "arbitrary"),
        name="norm_mod",
    )(x3, norm_w.reshape(1, d), scale3, shift3)


def _rope_chunk(x, a, bm, cm):
    return x * a + pltpu.roll(x, LANES - ROT_DIM // 2, 1) * bm + pltpu.roll(x, ROT_DIM // 2, 1) * cm


def _projq_kernel(h_ref, w_ref, ra_ref, rb_ref, rc_ref, qa_ref, qb_ref, qc_ref, qaf_ref):
    y = jnp.dot(h_ref[...], w_ref[...], preferred_element_type=F32)
    a, bm, cm = ra_ref[...], rb_ref[...], rc_ref[...]
    for m, o_ref in enumerate((qa_ref, qb_ref, qc_ref)):
        for c in range(MIX_WIDTH // LANES):
            x = y[:, m * MIX_WIDTH + c * LANES:m * MIX_WIDTH + (c + 1) * LANES]
            if m != 1:
                x = _rope_chunk(x, a, bm, cm)
            if m == 0:
                qaf_ref[:, c * LANES:(c + 1) * LANES] = x
            o_ref[:, c * LANES:(c + 1) * LANES] = (x * ATTN_SCALE).astype(BF16)


def _projkv_kernel(h_ref, w_ref, ra_ref, rb_ref, rc_ref, kv_ref, kvb_ref):
    y = jnp.dot(h_ref[...], w_ref[...], preferred_element_type=F32)
    a, bm, cm = ra_ref[...], rb_ref[...], rc_ref[...]
    per_mixer = MIXER_COLS // LANES
    for c in range(N_MIXERS * per_mixer):
        x = y[:, c * LANES:(c + 1) * LANES]
        mixer, within = divmod(c, per_mixer)
        if mixer != 1 and within < KV_WIDTH // LANES:
            x = _rope_chunk(x, a, bm, cm)
        kv_ref[:, c * LANES:(c + 1) * LANES] = x
        kvb_ref[:, c * LANES:(c + 1) * LANES] = x.astype(BF16)


def _projz_kernel(h_ref, w_ref, o_ref):
    y = jnp.dot(h_ref[...], w_ref[...], preferred_element_type=F32)
    o_ref[...] = y * _sigmoid(y)


def _projg_kernel(h_ref, w_ref, o_ref):
    y = jnp.dot(h_ref[...], w_ref[...], preferred_element_type=F32)
    o_ref[...] = _sigmoid(y)


MISC_KI = MIX_WIDTH
MISC_WI = MISC_KI + LANES
MISC_FB = MISC_WI + LANES
MISC_N = MISC_FB + LANES


def _projmisc_kernel(h_ref, w_ref, ra_ref, rb_ref, rc_ref, bf_ref,
                     qi_ref, ki_ref, ki2_ref, wi_ref, lf_ref):
    y = jnp.dot(h_ref[...], w_ref[...], preferred_element_type=F32)
    a, bm, cm = ra_ref[...], rb_ref[...], rc_ref[...]
    for c in range(MIX_WIDTH // LANES):
        x = _rope_chunk(y[:, c * LANES:(c + 1) * LANES], a, bm, cm)
        qi_ref[:, c * LANES:(c + 1) * LANES] = x.astype(BF16)
    ki = _rope_chunk(y[:, MISC_KI:MISC_KI + LANES], a, bm, cm)
    ki_ref[...] = ki[:, :IDX_DIM]
    ki2_ref[...] = (ki + pltpu.roll(ki, IDX_DIM, 1)).astype(BF16)
    wi_ref[...] = y[:, MISC_WI:MISC_WI + IDX_HEADS]
    f = y[:, MISC_FB:MISC_FB + LANES] + bf_ref[...]
    logf = jnp.minimum(f, 0.0) - jnp.log(1.0 + jnp.exp(-jnp.abs(f)))
    lf_ref[...] = logf[:, :N_HEADS]


def _proj_call(kernel, h, w, tabs, extra, outs, tm, tn, name):
    m, d = h.shape
    n = w.shape[1]
    nt = tabs[0].shape[0] // tm if tabs else 1
    in_specs = [pl.BlockSpec((tm, d), lambda j, i: (i, 0)),
                pl.BlockSpec((d, tn), lambda j, i: (0, j))]
    in_specs += [pl.BlockSpec((tm, LANES), lambda j, i: (i % nt, 0)) for _ in tabs]
    in_specs += [pl.BlockSpec(e.shape, lambda j, i: (0, 0)) for e in extra]
    out_specs = [pl.BlockSpec((tm, wd if full else tn), (lambda j, i: (i, 0)) if full else (lambda j, i: (i, j)))
                 for (wd, _, full) in outs]
    out_shape = [jax.ShapeDtypeStruct((m, wd), dt) for (wd, dt, _) in outs]
    return pl.pallas_call(
        kernel,
        grid=(n // tn, m // tm),
        in_specs=in_specs,
        out_specs=out_specs,
        out_shape=out_shape,
        compiler_params=_cp("arbitrary", "arbitrary"),
        name=name,
    )(h, w, *tabs, *extra)


def _kmean_kernel(k_ref, o_ref):
    o_ref[0, 0] = jnp.sum(k_ref[0], axis=0, keepdims=True) * (1.0 / MOBA_BLOCK)


def _kmean_prompt(kv3):
    b, t, _ = kv3.shape
    nb = t // MOBA_BLOCK
    out = pl.pallas_call(
        _kmean_kernel,
        grid=(b, nb),
        in_specs=[pl.BlockSpec((1, MOBA_BLOCK, KV_WIDTH), lambda i, n: (i, n, 0))],
        out_specs=pl.BlockSpec((1, 1, 1, KV_WIDTH), lambda i, n: (i, n, 0, 0)),
        out_shape=jax.ShapeDtypeStruct((b, nb, 1, KV_WIDTH), F32),
        compiler_params=_cp("arbitrary", "arbitrary"),
        name="moba_kmean",
    )(kv3)
    return out.reshape(b, nb, KV_WIDTH)


CUM_CHUNK = 256


def _cumsum_kernel(x_ref, o_ref):
    n = x_ref.shape[2] // CUM_CHUNK
    r = lax.broadcasted_iota(jnp.int32, (CUM_CHUNK, CUM_CHUNK), 0)
    c = lax.broadcasted_iota(jnp.int32, (CUM_CHUNK, CUM_CHUNK), 1)
    tri = (r <= c).astype(F32)

    def body(i, carry):
        st = pl.multiple_of(i * CUM_CHUNK, CUM_CHUNK)
        x = x_ref[0, :, pl.ds(st, CUM_CHUNK)]
        y = jnp.dot(x, tri, precision=HIGHEST, preferred_element_type=F32) + carry
        o_ref[0, :, pl.ds(st, CUM_CHUNK)] = y
        return y[:, CUM_CHUNK - 1:CUM_CHUNK]

    lax.fori_loop(0, n, body, jnp.zeros((x_ref.shape[1], 1), F32))


def _cumsum(lft):
    b, h, lp = lft.shape
    return pl.pallas_call(
        _cumsum_kernel,
        grid=(b,),
        in_specs=[pl.BlockSpec((1, h, lp), lambda i: (i, 0, 0))],
        out_specs=pl.BlockSpec((1, h, lp), lambda i: (i, 0, 0)),
        out_shape=jax.ShapeDtypeStruct((b, h, lp), F32),
        compiler_params=_cp("arbitrary"),
        name="fox_cumsum",
    )(lft)


def _stack4(blk0, blk1):
    lane = lax.broadcasted_iota(jnp.int32, blk0.shape, 1)
    lo = lane < HEAD_DIM
    return jnp.concatenate([
        jnp.where(lo, blk0, 0.0),
        jnp.where(lo, pltpu.roll(blk0, HEAD_DIM, 1), 0.0),
        jnp.where(lo, 0.0, pltpu.roll(blk1, HEAD_DIM, 1)),
        jnp.where(lo, 0.0, blk1)], axis=0)


def _unstack4(o, tq):
    lane = lax.broadcasted_iota(jnp.int32, (tq, LANES), 1)
    lo = lane < HEAD_DIM
    b0 = jnp.where(lo, o[:tq], pltpu.roll(o[tq:2 * tq], HEAD_DIM, 1))
    b1 = jnp.where(lo, pltpu.roll(o[2 * tq:3 * tq], HEAD_DIM, 1), o[3 * tq:])
    return b0, b1


def _stacked_queries(q_ref):
    return [_stack4(q_ref[0, :, (2 * c) * LANES:(2 * c + 1) * LANES].astype(F32),
                    q_ref[0, :, (2 * c + 1) * LANES:(2 * c + 2) * LANES].astype(F32)) for c in range(2)]


def _flash_scratch(tq, tk):
    rows = 4 * tq
    return [pltpu.VMEM((2, rows, tk), F32), pltpu.VMEM((2, rows, tk), BF16),
            pltpu.VMEM((2, rows, LANES), F32), pltpu.VMEM((2, rows, LANES), F32),
            pltpu.VMEM((2, rows, 2 * LANES), F32)]


def _flash_pipeline(n_all, qs, k_ref, v_ref, fix_scores, scratch, o_ref, tq, tk):
    s_ref, p_ref, al_ref, m_ref, acc_ref = scratch
    m_ref[...] = jnp.full(m_ref.shape, -jnp.inf, F32)
    acc_ref[...] = jnp.zeros(acc_ref.shape, F32)
    s_ref[...] = jnp.zeros(s_ref.shape, F32)
    p_ref[...] = jnp.zeros(p_ref.shape, BF16)
    al_ref[...] = jnp.zeros(al_ref.shape, F32)
    last = n_all - 1
    ones = jnp.ones((tk, LANES), BF16)

    def step(i, carry):
        jc = jnp.clip(i - 2, 0, last)
        stc = pl.multiple_of(jc * tk, tk)
        for c in range(2):
            vo = jnp.concatenate([v_ref[0, pl.ds(stc, tk), c * LANES:(c + 1) * LANES], ones], axis=1)
            al = al_ref[c]
            acc_ref[c] = (jnp.concatenate([al, al], axis=1) * acc_ref[c]
                          + jnp.dot(p_ref[c], vo, preferred_element_type=F32))
        jb = i - 1
        valid = (jb >= 0) & (jb <= last)
        jbc = jnp.clip(jb, 0, last)
        for c in range(2):
            s = fix_scores(c, s_ref[c], jbc, valid)
            m_old = m_ref[c]
            m_new = jnp.maximum(m_old, jnp.max(s, axis=-1, keepdims=True))
            al_ref[c] = jnp.exp(m_old - m_new)
            m_ref[c] = m_new
            p_ref[c] = jnp.exp(s - jnp.concatenate([m_new] * (tk // LANES), axis=1)).astype(BF16)
        ja = jnp.minimum(i, last)
        sta = pl.multiple_of(ja * tk, tk)
        for c in range(2):
            s_ref[c] = lax.dot_general(qs[c], k_ref[0, pl.ds(sta, tk), c * LANES:(c + 1) * LANES], _NT,
                                       preferred_element_type=F32)
        return carry

    lax.fori_loop(0, n_all + 2, step, 0)
    for c in range(2):
        b0, b1 = _unstack4(acc_ref[c, :, :LANES] / acc_ref[c, :, LANES:], tq)
        o_ref[0, :, (2 * c) * LANES:(2 * c + 1) * LANES] = b0
        o_ref[0, :, (2 * c + 1) * LANES:(2 * c + 2) * LANES] = b1


def _causal_setup(q_lo, tq, tk):
    r1 = lax.broadcasted_iota(jnp.int32, (tq, 1), 0) + q_lo
    qpos = jnp.concatenate([r1] * 4, axis=0)
    kiota = lax.broadcasted_iota(jnp.int32, (4 * tq, tk), 1)
    return qpos, kiota


def _attn_call(kern, name, ins, in_specs, b, t, tq, scratch):
    return pl.pallas_call(
        kern,
        grid=(b, t // tq),
        in_specs=in_specs,
        out_specs=pl.BlockSpec((1, tq, MIX_WIDTH), lambda i, j: (i, j, 0)),
        out_shape=jax.ShapeDtypeStruct((b, t, MIX_WIDTH), F32),
        scratch_shapes=scratch,
        compiler_params=_cp("arbitrary", "arbitrary"),
        name=name,
    )(*ins)


_q_spec = lambda tq: pl.BlockSpec((1, tq, MIX_WIDTH), lambda i, j: (i, j, 0))
_kv_spec = lambda lp, col: pl.BlockSpec((1, lp, KV_WIDTH), lambda i, j: (i, 0, col))


def _moba_kernel(qf_ref, q_ref, km_ref, k_ref, v_ref, o_ref, *scratch, tq, topk):
    tk = MOBA_BLOCK
    q_lo = pl.program_id(1) * tq
    own = q_lo // MOBA_BLOCK
    rows = 4 * tq
    qpos, kiota = _causal_setup(q_lo, tq, tk)
    lane = lax.broadcasted_iota(jnp.int32, (rows, LANES), 1)
    lane_f = lane.astype(F32)
    qs = [q.astype(BF16) for q in _stacked_queries(q_ref)]
    selb = []
    for c, qf in enumerate(_stacked_queries(qf_ref)):
        gate = lax.dot_general(qf, km_ref[0, :, c * LANES:(c + 1) * LANES], _NT, precision=HIGHEST,
                               preferred_element_type=F32)
        gate = jnp.where(lane < own, gate, -jnp.inf)
        sb = jnp.where(lane == own, 0.0, NEG)
        for _ in range(topk):
            mx = jnp.max(gate, axis=-1, keepdims=True)
            is_max = (gate == mx) & (mx > -jnp.inf)
            first = jnp.min(jnp.where(is_max, lane_f, 1e9), axis=-1, keepdims=True)
            pick = lane_f == first
            sb = jnp.where(pick, 0.0, sb)
            gate = jnp.where(pick, -jnp.inf, gate)
        selb.append(sb)

    def fix(c, s, j, valid):
        bias = jnp.sum(jnp.where(lane == j, selb[c], 0.0), axis=-1, keepdims=True)
        return jnp.where((j * tk + kiota <= qpos) & valid, s + bias, NEG)

    _flash_pipeline(own + 1, qs, k_ref, v_ref, fix, scratch, o_ref, tq, tk)


def _moba_attn(qf, q, km, kvb, tq):
    b, t, _ = q.shape
    lp = kvb.shape[1]
    kern = functools.partial(_moba_kernel, tq=tq, topk=min(MOBA_TOPK, lp // MOBA_BLOCK))
    specs = [_q_spec(tq), _q_spec(tq), pl.BlockSpec((1, KMEAN_ROWS, KV_WIDTH), lambda i, j: (i, 0, 0)),
             _kv_spec(lp, 0), _kv_spec(lp, 1)]
    return _attn_call(kern, "moba_attn", (qf, q, km, kvb, kvb), specs, b, t, tq, _flash_scratch(tq, MOBA_BLOCK))


def _fox_kernel(q_ref, cum_ref, k_ref, v_ref, o_ref, *scratch, tq, tk):
    q_lo = pl.program_id(1) * tq
    n_all = (q_lo + tq + tk - 1) // tk
    qpos, kiota = _causal_setup(q_lo, tq, tk)
    qs = [q.astype(BF16) for q in _stacked_queries(q_ref)]

    def fix(c, s, j, valid):
        st = pl.multiple_of(j * tk, tk)
        parts = [s[hh * tq:(hh + 1) * tq] - cum_ref[0, 4 * c + hh:4 * c + hh + 1, pl.ds(st, tk)]
                 for hh in range(4)]
        return jnp.where((j * tk + kiota <= qpos) & valid, jnp.concatenate(parts, axis=0), NEG)

    _flash_pipeline(n_all, qs, k_ref, v_ref, fix, scratch, o_ref, tq, tk)


def _fox_attn(q, cum, kvb, tq, tk):
    b, t, _ = q.shape
    lp = kvb.shape[1]
    kern = functools.partial(_fox_kernel, tq=tq, tk=tk)
    specs = [_q_spec(tq), pl.BlockSpec((1, N_HEADS, lp), lambda i, j: (i, 0, 0)), _kv_spec(lp, 2), _kv_spec(lp, 3)]
    return _attn_call(kern, "fox_attn", (q, cum, kvb, kvb), specs, b, t, tq, _flash_scratch(tq, tk))


def _key_to_float(key):
    bits = jnp.where(key < 0, key & jnp.int32(0x7FFFFFFF), ~key)
    return lax.bitcast_convert_type(bits, F32)


def _topk_threshold(count, n_rows, need, few, idx_bits):
    def bit_body(i, key):
        cand = key | lax.shift_left(jnp.int32(1), 31 - i)
        thr_c = _key_to_float(cand)
        return jnp.where(count(lambda x, idx: x >= thr_c) >= need, cand, key)

    key = lax.fori_loop(0, 32, bit_body, jnp.zeros((n_rows, 1), jnp.int32))
    thr = _key_to_float(key)
    cnt_ge = count(lambda x, idx: x >= thr)
    has_ties = jnp.max(jnp.where(few, 0.0, cnt_ge - need)) > 0.0

    def tie_cut(_):
        r = need - count(lambda x, idx: x > thr)

        def jb(i, cut):
            cand = cut | lax.shift_left(jnp.int32(1), idx_bits - 1 - i)
            cnt = count(lambda x, idx: (x == thr) & (idx < cand))
            return jnp.where(cnt < r, cand, cut)

        return lax.fori_loop(0, idx_bits, jb, jnp.zeros((n_rows, 1), jnp.int32))

    cut = lax.cond(has_ties, tie_cut, lambda _: jnp.full((n_rows, 1), 2 ** 30, jnp.int32), 0)
    return jnp.where(few, F32_LOWEST, thr), jnp.where(few, 2 ** 30, cut)


def _dsa_kernel(qi_ref, w_ref, kx_ref, q_ref, k_ref, v_ref, o_ref, isc_ref, *scratch, tq, tk, topk, idx_bits):
    q_lo = pl.program_id(1) * tq
    n_all = (q_lo + tq + tk - 1) // tk
    nl = tk // LANES
    lane = lax.broadcasted_iota(jnp.int32, (tq, LANES), 1)
    lo = lane < HEAD_DIM
    qpos1 = lax.broadcasted_iota(jnp.int32, (tq, 1), 0) + q_lo
    kiota1 = lax.broadcasted_iota(jnp.int32, (tq, tk), 1)

    pieces, wpieces = [], []
    for h in range(IDX_HEADS):
        chunk = qi_ref[0, :, (h // 2) * LANES:(h // 2 + 1) * LANES].astype(F32)
        pieces.append(jnp.where(lo, chunk, 0.0) if h % 2 == 0 else jnp.where(lo, 0.0, chunk))
        wpieces.append(jnp.broadcast_to(w_ref[0, :, h:h + 1], (tq, LANES)))
    qstack = jnp.concatenate(pieces, axis=0).astype(BF16)
    wfull = jnp.concatenate([jnp.concatenate(wpieces, axis=0)] * nl, axis=1)

    def idx_body(j, carry):
        start = pl.multiple_of(j * tk, tk)
        sc = lax.dot_general(qstack, kx_ref[0, pl.ds(start, tk), :], _NT, preferred_element_type=F32)
        contrib = jnp.maximum(sc, 0.0) * wfull
        isc = contrib[:tq]
        for h in range(1, IDX_HEADS):
            isc = isc + contrib[h * tq:(h + 1) * tq]
        isc_ref[:, pl.ds(start, tk)] = jnp.where(j * tk + kiota1 <= qpos1, isc * IDX_SCALE, -jnp.inf)
        return carry

    lax.fori_loop(0, n_all, idx_body, 0)

    def count(pred):
        def body(j, acc):
            start = pl.multiple_of(j * tk, tk)
            x = isc_ref[:, pl.ds(start, tk)]
            for cc in range(nl):
                idx = j * tk + cc * LANES + lane
                acc = acc + jnp.where(pred(x[:, cc * LANES:(cc + 1) * LANES], idx), 1.0, 0.0)
            return acc
        acc = lax.fori_loop(0, n_all, body, jnp.zeros((tq, LANES), F32))
        return jnp.sum(acc, axis=-1, keepdims=True)

    few = qpos1 + 1 <= topk
    thr, cut = _topk_threshold(count, tq, jnp.float32(topk), few, idx_bits)

    def bias_body(j, carry):
        start = pl.multiple_of(j * tk, tk)
        x = isc_ref[:, pl.ds(start, tk)]
        keep = (x > thr) | ((x == thr) & (j * tk + kiota1 <= cut))
        isc_ref[:, pl.ds(start, tk)] = jnp.where(keep, 0.0, NEG)
        return carry

    lax.fori_loop(0, n_all, bias_body, 0)

    qs = [q.astype(BF16) for q in _stacked_queries(q_ref)]

    def fix(c, s, j, valid):
        bias = isc_ref[:, pl.ds(pl.multiple_of(j * tk, tk), tk)]
        return jnp.where(valid, s + jnp.concatenate([bias] * 4, axis=0), NEG)

    _flash_pipeline(n_all, qs, k_ref, v_ref, fix, scratch, o_ref, tq, tk)


def _dsa_attn(qi, wi, kx2, q, kvb, tq, tk):
    b, t, _ = q.shape
    lp = kvb.shape[1]
    kern = functools.partial(_dsa_kernel, tq=tq, tk=tk, topk=min(DSA_TOPK, lp // 4),
                             idx_bits=max(1, int(lp).bit_length()))
    specs = [_q_spec(tq), pl.BlockSpec((1, tq, IDX_HEADS), lambda i, j: (i, j, 0)),
             pl.BlockSpec((1, lp, 2 * IDX_DIM), lambda i, j: (i, 0, 0)), _q_spec(tq), _kv_spec(lp, 4), _kv_spec(lp, 5)]
    scratch = [pltpu.VMEM((tq, lp), F32)] + _flash_scratch(tq, tk)
    return _attn_call(kern, "dsa_attn", (qi, wi, kx2, q, kvb, kvb), specs, b, t, tq, scratch)


def _sample_index_kernel(pt_ref, *refs, ppi, n_steps, topk, l_past, t_new, idx_bits):
    del pt_ref
    kx_refs = refs[:ppi]
    qi_ref, w_ref, nkx_ref, o_ref, isc_ref = refs[ppi:]
    j = pl.program_id(1)
    lp = isc_ref.shape[1]
    lane = lax.broadcasted_iota(jnp.int32, (t_new, LANES), 1)
    lo = lane < HEAD_DIM
    pieces, wpieces = [], []
    for h in range(IDX_HEADS):
        chunk = qi_ref[0, :, (h // 2) * LANES:(h // 2 + 1) * LANES].astype(F32)
        pieces.append(jnp.where(lo, chunk, 0.0) if h % 2 == 0 else jnp.where(lo, 0.0, chunk))
        wpieces.append(jnp.broadcast_to(w_ref[0, :, h:h + 1], (t_new, LANES)))
    qstack = jnp.concatenate(pieces, axis=0).astype(BF16)
    wb = jnp.concatenate(wpieces, axis=0)

    def head_sum(sc):
        contrib = jnp.maximum(sc, 0.0) * wb
        isc = contrib[:t_new]
        for h in range(1, IDX_HEADS):
            isc = isc + contrib[h * t_new:(h + 1) * t_new]
        return isc * IDX_SCALE

    for pp in range(ppi):
        kxt = kx_refs[pp][0, 0]
        rhs = jnp.concatenate([kxt, kxt], axis=0).astype(BF16)
        start = pl.multiple_of((j * ppi + pp) * PAGE_SIZE, PAGE_SIZE)
        isc_ref[:, pl.ds(start, PAGE_SIZE)] = head_sum(jnp.dot(qstack, rhs, preferred_element_type=F32))

    @pl.when(j == n_steps - 1)
    def _():
        scn = lax.dot_general(qstack, nkx_ref[0], _NT, preferred_element_type=F32)
        row = lax.broadcasted_iota(jnp.int32, (t_new, LANES), 0)
        isc_ref[:, l_past:l_past + LANES] = jnp.where((lane <= row) & (lane < t_new), head_sum(scn), -jnp.inf)
        if lp > l_past + LANES:
            isc_ref[:, l_past + LANES:] = jnp.full((t_new, lp - l_past - LANES), -jnp.inf, F32)
        x = isc_ref[...]
        idx = lax.broadcasted_iota(jnp.int32, (t_new, lp), 1)

        def count(pred):
            return jnp.sum(jnp.where(pred(x, idx), 1.0, 0.0), axis=-1, keepdims=True)

        qpos1 = lax.broadcasted_iota(jnp.int32, (t_new, 1), 0) + l_past
        thr, cut = _topk_threshold(count, t_new, jnp.float32(topk), qpos1 + 1 <= topk, idx_bits)
        keep = (x > thr) | ((x == thr) & (idx <= cut))
        o_ref[0] = jnp.where(keep, 0.0, NEG)


def _sample_index(cache_kxt, page_table, layer, qi, wi, nkx2, lp):
    nseq, n_pages = page_table.shape
    t_new = qi.shape[1]
    ppi = INDEX_PAGES_PER_STEP
    n_steps = n_pages // ppi
    l_past = n_pages * PAGE_SIZE
    pg = lambda pp: (lambda i, j, pt: (layer, pt[i, j * ppi + pp], 0, 0))
    grid_spec = pltpu.PrefetchScalarGridSpec(
        num_scalar_prefetch=1,
        grid=(nseq, n_steps),
        in_specs=[pl.BlockSpec((1, 1, IDX_DIM, PAGE_SIZE), pg(pp)) for pp in range(ppi)] + [
            pl.BlockSpec((1, t_new, MIX_WIDTH), lambda i, j, pt: (i, 0, 0)),
            pl.BlockSpec((1, t_new, IDX_HEADS), lambda i, j, pt: (i, 0, 0)),
            pl.BlockSpec((1, PAGE_SIZE, 2 * IDX_DIM), lambda i, j, pt: (i, 0, 0)),
        ],
        out_specs=pl.BlockSpec((1, t_new, lp), lambda i, j, pt: (i, 0, 0)),
        scratch_shapes=[pltpu.VMEM((t_new, lp), F32)],
    )
    kern = functools.partial(_sample_index_kernel, ppi=ppi, n_steps=n_steps, topk=min(DSA_TOPK, (l_past + t_new) // 4),
                             l_past=l_past, t_new=t_new, idx_bits=max(1, int(lp).bit_length()))
    return pl.pallas_call(
        kern,
        grid_spec=grid_spec,
        out_shape=jax.ShapeDtypeStruct((nseq, t_new, lp), F32),
        compiler_params=_cp("arbitrary", "arbitrary"),
        name="dsa_index_paged",
    )(page_table, *([cache_kxt] * ppi), qi, wi, nkx2)


def _block_diag_q(q):
    t = q.shape[0]
    lane = lax.broadcasted_iota(jnp.int32, (t, LANES), 1)
    lo = lane < HEAD_DIM
    zero = jnp.zeros((t, LANES), F32)
    out = []
    for h in range(N_HEADS):
        src = q[:, (h // 2) * LANES:(h // 2 + 1) * LANES]
        kvh = h // (N_HEADS // N_KV_HEADS)
        chunk, half = divmod(kvh, 2)
        piece = src if h % 2 == half else pltpu.roll(src, HEAD_DIM, 1)
        piece = jnp.where(lo, piece, 0.0) if half == 0 else jnp.where(lo, 0.0, piece)
        out.append(jnp.concatenate([piece, zero] if chunk == 0 else [zero, piece], axis=1))
    return jnp.concatenate(out, axis=0)


def _token_major(o, t):
    lane = lax.broadcasted_iota(jnp.int32, (t, LANES), 1)
    lo = lane < HEAD_DIM
    out = []
    for co in range(MIX_WIDTH // LANES):
        chunk, half = divmod(co, 2)
        x = o[(2 * co) * t:(2 * co + 1) * t, chunk * LANES:(chunk + 1) * LANES]
        y = o[(2 * co + 1) * t:(2 * co + 2) * t, chunk * LANES:(chunk + 1) * LANES]
        if half == 0:
            out.append(jnp.where(lo, x, pltpu.roll(y, HEAD_DIM, 1)))
        else:
            out.append(jnp.where(lo, pltpu.roll(x, HEAD_DIM, 1), y))
    return jnp.concatenate(out, axis=1)


def _lane_cumsum(x):
    lane = lax.broadcasted_iota(jnp.int32, x.shape, 1)
    d = 1
    while d < LANES:
        x = x + jnp.where(lane >= d, pltpu.roll(x, d, 1), 0.0)
        d *= 2
    return x


def _rows_from_heads(c, t):
    return jnp.concatenate([jnp.broadcast_to(c[h:h + 1], (t, c.shape[1])) for h in range(N_HEADS)], axis=0)


def _online_update(s, pv_fn, m_ref, l_ref, acc_ref):
    m_old = m_ref[...]
    m_new = jnp.maximum(m_old, jnp.max(s, axis=-1, keepdims=True))
    alpha = jnp.exp(m_old - m_new)
    p = jnp.exp(s - jnp.concatenate([m_new] * (s.shape[1] // LANES), axis=1))
    l_ref[...] = alpha * l_ref[...] + jnp.sum(p, axis=-1, keepdims=True)
    acc_ref[...] = jnp.concatenate([alpha] * (acc_ref.shape[1] // LANES), axis=1) * acc_ref[...] + pv_fn(p.astype(BF16))
    m_ref[...] = m_new


def _sample_attn_kernel(pt_ref, pga_ref, pgb_ref, lfa_ref, lfb_ref, bias_ref, qa_ref, qb_ref, qc_ref, qaf_ref,
                        nkv_ref, nlf_ref, oa_ref, ob_ref, oc_ref,
                        qbd_ref, qbdf_ref, km_ref, ms_ref, ls_ref, accs_ref,
                        fm_ref, fl_ref, facc_ref, car_ref, dm_ref, dl_ref, dacc_ref, *, nb, t_new, topk):
    del pt_ref
    j = pl.program_id(1)
    rows = N_HEADS * t_new
    lane = lax.broadcasted_iota(jnp.int32, (rows, LANES), 1)

    @pl.when(j == 0)
    def _():
        for m, q_ref in enumerate((qa_ref, qb_ref, qc_ref)):
            qbd_ref[m] = _block_diag_q(q_ref[0].astype(F32)).astype(BF16)
        qbdf_ref[...] = _block_diag_q(qaf_ref[0])
        km_ref[...] = jnp.zeros(km_ref.shape, F32)
        ms_ref[...] = jnp.zeros(ms_ref.shape, F32)
        ls_ref[...] = jnp.zeros(ls_ref.shape, F32)
        car_ref[...] = jnp.zeros(car_ref.shape, F32)
        for m_ref, l_ref, acc_ref in ((fm_ref, fl_ref, facc_ref), (dm_ref, dl_ref, dacc_ref)):
            m_ref[...] = jnp.full(m_ref.shape, -jnp.inf, F32)
            l_ref[...] = jnp.zeros(l_ref.shape, F32)
            acc_ref[...] = jnp.zeros(acc_ref.shape, F32)

    @pl.when(j < nb)
    def _():
        def kt(m, off):
            lo_r = m * MIXER_COLS + off
            return jnp.concatenate([pga_ref[0, 0, lo_r:lo_r + KV_WIDTH, :], pgb_ref[0, 0, lo_r:lo_r + KV_WIDTH, :]],
                                   axis=1)

        def pv_nt(m):
            vt = kt(m, KV_WIDTH).astype(BF16)
            return lambda p: lax.dot_general(p, vt, _NT, preferred_element_type=F32)

        k0 = kt(0, 0)
        kmcol = jnp.sum(k0, axis=1, keepdims=True) * (1.0 / MOBA_BLOCK)
        blk_lane = lax.broadcasted_iota(jnp.int32, km_ref.shape, 1)
        km_ref[...] = jnp.where(blk_lane == j, kmcol, km_ref[...])
        s = jnp.dot(qbd_ref[0], k0.astype(BF16), preferred_element_type=F32)
        m_n = jnp.max(s, axis=-1, keepdims=True)
        p = jnp.exp(s - m_n)
        ms_ref[...] = jnp.where(lane == j, m_n, ms_ref[...])
        ls_ref[...] = jnp.where(lane == j, jnp.sum(p, axis=-1, keepdims=True), ls_ref[...])
        accs_ref[j] = pv_nt(0)(p.astype(BF16))

        ca = _lane_cumsum(lfa_ref[0, 0]) + car_ref[...]
        cb = _lane_cumsum(lfb_ref[0, 0]) + ca[:, LANES - 1:LANES]
        car_ref[...] = jnp.broadcast_to(cb[:, LANES - 1:LANES], car_ref.shape)
        s = jnp.dot(qbd_ref[1], kt(1, 0).astype(BF16), preferred_element_type=F32)
        s = s - _rows_from_heads(jnp.concatenate([ca, cb], axis=1), t_new)
        _online_update(s, pv_nt(1), fm_ref, fl_ref, facc_ref)

        s = jnp.dot(qbd_ref[2], kt(2, 0).astype(BF16), preferred_element_type=F32)
        s = s + jnp.concatenate([bias_ref[0]] * N_HEADS, axis=0)
        _online_update(s, pv_nt(2), dm_ref, dl_ref, dacc_ref)

    @pl.when(j == nb)
    def _():
        nk = lambda m: nkv_ref[0, :, m * MIXER_COLS:m * MIXER_COLS + KV_WIDTH]
        nv = lambda m: nkv_ref[0, :, m * MIXER_COLS + KV_WIDTH:(m + 1) * MIXER_COLS]
        pv = lambda m: (lambda p: jnp.dot(p, nv(m), preferred_element_type=F32))
        row_t = lax.broadcasted_iota(jnp.int32, (rows, LANES), 0) & (t_new - 1)
        causal = (lane <= row_t) & (lane < t_new)

        s = lax.dot_general(qbd_ref[0], nk(0), _NT, preferred_element_type=F32)
        s = jnp.where(causal, s, NEG)
        m_o = jnp.max(s, axis=-1, keepdims=True)
        p = jnp.exp(s - m_o)
        l_o = jnp.sum(p, axis=-1, keepdims=True)
        acc_o = pv(0)(p.astype(BF16))
        gate = jnp.dot(qbdf_ref[...], km_ref[...], precision=HIGHEST, preferred_element_type=F32)
        gate = jnp.where(lane < nb, gate, -jnp.inf)
        lane_f = lane.astype(F32)
        sel = lane < 0
        for _ in range(topk):
            mx = jnp.max(gate, axis=-1, keepdims=True)
            is_max = (gate == mx) & (mx > -jnp.inf)
            first = jnp.min(jnp.where(is_max, lane_f, 1e9), axis=-1, keepdims=True)
            pick = lane_f == first
            sel = sel | pick
            gate = jnp.where(pick, -jnp.inf, gate)
        ms = ms_ref[...]
        m_all = jnp.maximum(m_o, jnp.max(jnp.where(sel, ms, -jnp.inf), axis=-1, keepdims=True))
        w = jnp.where(sel, jnp.exp(ms - m_all), 0.0)
        w_o = jnp.exp(m_o - m_all)
        l_all = jnp.sum(w * ls_ref[...], axis=-1, keepdims=True) + w_o * l_o

        def merge(n, o):
            wn = jnp.sum(jnp.where(lane == n, w, 0.0), axis=-1, keepdims=True)
            return o + wn * accs_ref[n]

        o_a = lax.fori_loop(0, nb, merge, w_o * acc_o) / l_all
        oa_ref[0] = _token_major(o_a, t_new)

        cn = _lane_cumsum(nlf_ref[0]) + car_ref[...]
        s = lax.dot_general(qbd_ref[1], nk(1), _NT, preferred_element_type=F32) - _rows_from_heads(cn, t_new)
        _online_update(jnp.where(causal, s, NEG), pv(1), fm_ref, fl_ref, facc_ref)
        wide = lambda l_ref: jnp.concatenate([l_ref[...]] * (KV_WIDTH // LANES), axis=1)
        ob_ref[0] = _token_major(facc_ref[...] / wide(fl_ref), t_new)

        s = lax.dot_general(qbd_ref[2], nk(2), _NT, preferred_element_type=F32)
        s = s + jnp.concatenate([bias_ref[0, :, :LANES]] * N_HEADS, axis=0)
        _online_update(s, pv(2), dm_ref, dl_ref, dacc_ref)
        oc_ref[0] = _token_major(dacc_ref[...] / wide(dl_ref), t_new)


def _sample_attn(cache_t, cache_lft, page_table, layer, bias, qa, qb, qc, qaf, nkv, nlf):
    nseq, n_pages = page_table.shape
    t_new = qa.shape[1]
    ppb = MOBA_BLOCK // PAGE_SIZE
    nb = n_pages // ppb
    rows = N_HEADS * t_new
    width = cache_t.shape[2]
    last = n_pages - 1
    page = lambda k: (lambda i, j, pt: (layer, pt[i, jnp.minimum(ppb * j + k, last)], 0, 0))
    seq_blk = lambda shape: pl.BlockSpec((1,) + shape, lambda i, j, pt: (i, 0, 0))
    grid_spec = pltpu.PrefetchScalarGridSpec(
        num_scalar_prefetch=1,
        grid=(nseq, nb + 1),
        in_specs=[
            pl.BlockSpec((1, 1, width, PAGE_SIZE), page(0)),
            pl.BlockSpec((1, 1, width, PAGE_SIZE), page(1)),
            pl.BlockSpec((1, 1, N_HEADS, PAGE_SIZE), page(0)),
            pl.BlockSpec((1, 1, N_HEADS, PAGE_SIZE), page(1)),
            pl.BlockSpec((1, t_new, MOBA_BLOCK), lambda i, j, pt: (i, 0, j)),
            seq_blk((t_new, MIX_WIDTH)), seq_blk((t_new, MIX_WIDTH)), seq_blk((t_new, MIX_WIDTH)),
            seq_blk((t_new, MIX_WIDTH)),
            seq_blk((PAGE_SIZE, width)), seq_blk((N_HEADS, PAGE_SIZE)),
        ],
        out_specs=[seq_blk((t_new, MIX_WIDTH))] * 3,
        scratch_shapes=[
            pltpu.VMEM((N_MIXERS, rows, KV_WIDTH), BF16), pltpu.VMEM((rows, KV_WIDTH), F32),
            pltpu.VMEM((KV_WIDTH, KMEAN_ROWS), F32), pltpu.VMEM((rows, LANES), F32), pltpu.VMEM((rows, LANES), F32),
            pltpu.VMEM((nb, rows, KV_WIDTH), F32),
            pltpu.VMEM((rows, LANES), F32), pltpu.VMEM((rows, LANES), F32), pltpu.VMEM((rows, KV_WIDTH), F32),
            pltpu.VMEM((N_HEADS, LANES), F32),
            pltpu.VMEM((rows, LANES), F32), pltpu.VMEM((rows, LANES), F32), pltpu.VMEM((rows, KV_WIDTH), F32),
        ],
    )
    kern = functools.partial(_sample_attn_kernel, nb=nb, t_new=t_new, topk=min(MOBA_TOPK, nb + 1))
    out = jax.ShapeDtypeStruct((nseq, t_new, MIX_WIDTH), F32)
    return pl.pallas_call(
        kern,
        grid_spec=grid_spec,
        out_shape=[out, out, out],
        compiler_params=_cp("arbitrary", "arbitrary"),
        name="sample_attn",
    )(page_table, cache_t, cache_t, cache_lft, cache_lft, bias, qa, qb, qc, qaf, nkv, nlf)


def _out_kernel(oa_ref, ob_ref, oc_ref, sz_ref, sg_ref, x_ref, gate_ref, wb_ref, wo_ref, fw_ref, y_ref,
                *, final):
    d = x_ref.shape[2]
    merged = None
    for i, o_ref in enumerate((oa_ref, ob_ref, oc_ref)):
        t = (o_ref[0] * sz_ref[0, :, i * MIX_WIDTH:(i + 1) * MIX_WIDTH]).astype(BF16)
        br = sg_ref[0, :, i * d:(i + 1) * d] * jnp.dot(t, wb_ref[i], preferred_element_type=F32)
        merged = br if merged is None else merged + br
    y = x_ref[0] + gate_ref[0] * jnp.dot(merged.astype(BF16), wo_ref[...], preferred_element_type=F32)
    if final:
        ms = jnp.mean(y * y, axis=-1, keepdims=True)
        y = y * lax.rsqrt(ms + NORM_EPS) * fw_ref[...]
    y_ref[0] = y


def _out_proj(oa, ob, oc, sz, sg, x3, gate3, wb, wo, fw, tm, final):
    g, r, d = x3.shape
    rm = gate3.shape[1]
    tmod = 1 if rm == 1 else tm
    mod_map = (lambda b, i: (b, 0, 0)) if rm == 1 else (lambda b, i: (b, i, 0))
    row = lambda w: pl.BlockSpec((1, tm, w), lambda b, i: (b, i, 0))
    as3 = lambda a: a.reshape(g, r, a.shape[-1])
    return pl.pallas_call(
        functools.partial(_out_kernel, final=final),
        grid=(g, r // tm),
        in_specs=[
            row(MIX_WIDTH), row(MIX_WIDTH), row(MIX_WIDTH), row(N_MIXERS * MIX_WIDTH), row(N_MIXERS * d), row(d),
            pl.BlockSpec((1, tmod, d), mod_map),
            pl.BlockSpec(wb.shape, lambda b, i: (0, 0, 0)),
            pl.BlockSpec(wo.shape, lambda b, i: (0, 0)),
            pl.BlockSpec((1, d), lambda b, i: (0, 0)),
        ],
        out_specs=row(d),
        out_shape=jax.ShapeDtypeStruct((g, r, d), F32),
        compiler_params=_cp("arbitrary", "arbitrary"),
        name="out_proj",
    )(as3(oa), as3(ob), as3(oc), as3(sz), as3(sg), x3, gate3, wb, wo, fw.reshape(1, d))


def _rope_tables(pos):
    half = ROT_DIM // 2
    expo = jnp.arange(0, ROT_DIM, 2, dtype=F32) / ROT_DIM
    inv_freq = jnp.power(jnp.float32(ROPE_THETA), -expo)
    ang = pos.astype(F32)[:, None] * inv_freq[None, :]
    cos, sin = jnp.cos(ang), jnp.sin(ang)
    n = pos.shape[0]
    rest = HEAD_DIM - ROT_DIM
    a = jnp.concatenate([cos, cos, jnp.ones((n, rest), F32)], axis=1)
    bm = jnp.concatenate([-sin, jnp.zeros((n, half + rest), F32)], axis=1)
    cm = jnp.concatenate([jnp.zeros((n, half), F32), sin, jnp.zeros((n, rest), F32)], axis=1)
    rep = LANES // HEAD_DIM
    return tuple(jnp.tile(t, (1, rep)) for t in (a, bm, cm))


def _split_weights(w_in_l, b_f_l):
    o = np.cumsum([0, MIX_WIDTH, KV_WIDTH, KV_WIDTH, MIX_WIDTH,
                   MIX_WIDTH, KV_WIDTH, KV_WIDTH, N_HEADS, MIX_WIDTH,
                   MIX_WIDTH, KV_WIDTH, KV_WIDTH, IDX_HEADS * IDX_DIM, IDX_DIM, IDX_HEADS, MIX_WIDTH]).tolist()
    col = lambda i: w_in_l[:, o[i]:o[i + 1]]
    qa, ka, va, za, qb, kb, vb, fb, zb, qc, kc, vc, qi, ki, wi, zc = (col(i) for i in range(16))
    g = w_in_l[:, o[16]:]
    d = w_in_l.shape[0]
    zpad = lambda w: jnp.zeros((d, w), w_in_l.dtype)
    w_q = jnp.concatenate([qa, qb, qc], axis=1)
    w_kv = jnp.concatenate([ka, va, kb, vb, kc, vc], axis=1)
    w_z = jnp.concatenate([za, zb, zc], axis=1)
    w_misc = jnp.concatenate([qi, ki, zpad(LANES - IDX_DIM), wi, zpad(LANES - IDX_HEADS),
                              fb, zpad(LANES - N_HEADS)], axis=1)
    bf_row = jnp.concatenate([b_f_l, jnp.zeros((LANES - N_HEADS,), b_f_l.dtype)]).reshape(1, LANES)
    cast = lambda w: w.astype(BF16)
    return cast(w_q), cast(w_kv), cast(w_z), cast(g), cast(w_misc), bf_row.astype(F32)


def _project(x3, scale3, shift3, tabs, weights, norm_w_l, tm):
    g, r, d = x3.shape
    w_q, w_kv, w_z, w_g, w_misc, bf_row = weights
    h = _norm_mod(x3, norm_w_l, scale3, shift3, tm).reshape(g * r, d)
    q_outs = _proj_call(
        _projq_kernel, h, w_q, tabs, (), [(MIX_WIDTH, BF16, True)] * 3 + [(MIX_WIDTH, F32, True)],
        tm, w_q.shape[1], "proj_q")
    kv, kvb = _proj_call(
        _projkv_kernel, h, w_kv, tabs, (), [(w_kv.shape[1], F32, True), (w_kv.shape[1], BF16, True)],
        tm, w_kv.shape[1], "proj_kv")
    (sz,) = _proj_call(_projz_kernel, h, w_z, (), (), [(w_z.shape[1], F32, True)], tm, w_z.shape[1], "proj_z")
    (sg,) = _proj_call(_projg_kernel, h, w_g, (), (), [(w_g.shape[1], F32, False)], tm, d, "proj_g")
    misc = _proj_call(
        _projmisc_kernel, h, w_misc, tabs, (bf_row,),
        [(MIX_WIDTH, BF16, True), (IDX_DIM, F32, True), (2 * IDX_DIM, BF16, True),
         (IDX_HEADS, F32, True), (N_HEADS, F32, True)],
        tm, MISC_N, "proj_misc")
    return q_outs, kv, kvb, sz, sg, misc


def _prompt_layer(x3, scale3, shift3, gate3, tabs, weights, norm_w_l, wb_l, wo_l, fw, *, tm, tm_out, tq, tk, final):
    b, t, _ = x3.shape
    (qa, qb, qc, qaf), kv, kvb, sz, sg, (qi, ki, ki2, wi, lf) = _project(
        x3, scale3, shift3, tabs, weights, norm_w_l, tm)
    s3 = lambda a: a.reshape(b, t, a.shape[-1])
    kv3, kvb3, lf3 = s3(kv), s3(kvb), s3(lf)
    km = _kmean_prompt(kv3)
    km = jnp.pad(km, ((0, 0), (0, KMEAN_ROWS - km.shape[1]), (0, 0)))
    cum = _cumsum(jnp.swapaxes(lf3, 1, 2))
    o_a = _moba_attn(s3(qaf), s3(qa), km, kvb3, tq)
    o_b = _fox_attn(s3(qb), cum, kvb3, tq, tk)
    o_c = _dsa_attn(s3(qi), s3(wi), s3(ki2), s3(qc), kvb3, tq, tk)
    x_new = _out_proj(o_a, o_b, o_c, sz, sg, x3, gate3, wb_l, wo_l, fw, tm_out, final)
    return x_new, kv3, lf3, s3(ki)


def _sample_layer(x3, scale3, shift3, gate3, tabs, weights, norm_w_l, wb_l, wo_l, fw, caches, page_table, layer,
                  *, nseq, final):
    _, m, _ = x3.shape
    t = m // nseq
    cache_t, cache_kxt, cache_lft = caches
    (qa, qb, qc, qaf), kv, kvb, sz, sg, (qi, ki, ki2, wi, lf) = _project(
        x3, scale3, shift3, tabs, weights, norm_w_l, m)
    s3 = lambda a: a.reshape(nseq, t, a.shape[-1])
    pad_slots = lambda a: jnp.pad(a, ((0, 0), (0, PAGE_SIZE - t), (0, 0)))
    lp = page_table.shape[1] * PAGE_SIZE + MOBA_BLOCK
    bias = _sample_index(cache_kxt, page_table, layer, s3(qi), s3(wi), pad_slots(s3(ki2)), lp)
    nlf = jnp.pad(jnp.swapaxes(s3(lf), 1, 2), ((0, 0), (0, 0), (0, PAGE_SIZE - t)))
    o_a, o_b, o_c = _sample_attn(cache_t, cache_lft, page_table, layer, bias, s3(qa), s3(qb), s3(qc), s3(qaf),
                                 pad_slots(s3(kvb)), nlf)
    x_new = _out_proj(o_a, o_b, o_c, sz, sg, x3, gate3, wb_l, wo_l, fw, m, final)
    return x_new, s3(kv), s3(lf), s3(ki)


def kernel(x_prompt, x_sample, cache_kv, cache_logf, cache_kidx, page_table, c_prompt, c_sample,
           norm_w, w_ada, b_ada, w_in, b_f, w_branch, w_out, final_norm_w):
    depth = norm_w.shape[0]
    bp, tp, d = x_prompt.shape
    bs, ts, _ = x_sample.shape
    n_pool = cache_kv.shape[1]
    n_pages = page_table.shape[1]
    past_len = n_pages * PAGE_SIZE
    assert ts == SUBLANES and n_pages % INDEX_PAGES_PER_STEP == 0 and past_len % MOBA_BLOCK == 0
    assert tp % MOBA_BLOCK == 0

    nc = bp + bs
    rc = -(-nc // SUBLANES) * SUBLANES
    c_all = jnp.concatenate([c_prompt, c_sample, jnp.zeros((rc - nc, d), F32)], axis=0)
    mod = _modulation(c_all, w_ada, b_ada)

    tabs_p = _rope_tables(jnp.arange(tp, dtype=jnp.int32))
    tabs_s = _rope_tables(jnp.tile(past_len + jnp.arange(ts, dtype=jnp.int32), bs))

    cache_t = jnp.transpose(cache_kv, (0, 1, 3, 4, 5, 6, 2)).reshape(depth, n_pool, N_MIXERS * MIXER_COLS, PAGE_SIZE)
    cache_kxt = jnp.swapaxes(cache_kidx, 2, 3)
    cache_lft = jnp.swapaxes(cache_logf, 2, 3)

    ms = bs * ts
    xp = x_prompt
    xs = x_sample.reshape(1, ms, d)
    new_p, new_s = [], []
    for l in range(depth):
        weights = _split_weights(w_in[l], b_f[l])
        wb_l = w_branch[l].astype(BF16)
        wo_l = w_out[l].astype(BF16)
        final = l == depth - 1
        shift, scale, gate = (mod[l, :, i * d:(i + 1) * d] for i in range(3))
        p3 = lambda a: a[:bp].reshape(bp, 1, d)
        s3 = lambda a: jnp.repeat(a[bp:nc], ts, axis=0).reshape(1, ms, d)

        xp, kv_p, lf_p, ki_p = _prompt_layer(
            xp, p3(scale), p3(shift), p3(gate), tabs_p, weights, norm_w[l], wb_l, wo_l, final_norm_w,
            tm=min(512, tp), tm_out=min(256, tp), tq=min(64, tp), tk=MOBA_BLOCK, final=final)
        xs, kv_s, lf_s, ki_s = _sample_layer(
            xs, s3(scale), s3(shift), s3(gate), tabs_s, weights, norm_w[l], wb_l, wo_l, final_norm_w,
            (cache_t, cache_kxt, cache_lft), page_table, l, nseq=bs, final=final)
        new_p.append((kv_p, lf_p, ki_p))
        new_s.append((kv_s, lf_s, ki_s))

    kv_shape = (N_MIXERS, 2, N_KV_HEADS, HEAD_DIM)
    stack = lambda items, i: jnp.stack([n[i] for n in items])
    kv_prompt = stack(new_p, 0).reshape(depth, bp, tp, *kv_shape)
    kv_sample = stack(new_s, 0).reshape(depth, bs, ts, *kv_shape)
    return (xp, xs.reshape(bs, ts, d), kv_prompt, stack(new_p, 1), stack(new_p, 2),
            kv_sample, stack(new_s, 1), stack(new_s, 2))
```

```python
import functools

import numpy as np
import jax
import jax.numpy as jnp
from jax import lax
from jax.experimental import pallas as pl
from jax.experimental.pallas import tpu as pltpu

F32 = jnp.float32
BF16 = jnp.bfloat16
HIGHEST = lax.Precision.HIGHEST

N_MIXERS = 3
N_HEADS = 8
N_KV_HEADS = 4
HEAD_DIM = 64
MIX_WIDTH = N_HEADS * HEAD_DIM
KV_WIDTH = N_KV_HEADS * HEAD_DIM
MIXER_COLS = 2 * KV_WIDTH
ROT_DIM = HEAD_DIM // 4
ROPE_THETA = 500000.0
ATTN_SCALE = HEAD_DIM ** -0.5
MOBA_BLOCK = 256
MOBA_TOPK = 3
IDX_HEADS = 8
IDX_DIM = 64
IDX_SCALE = (IDX_DIM * IDX_HEADS) ** -0.5
DSA_TOPK = 256
NORM_EPS = 1e-6
PAGE_SIZE = 128

LANES = 128
SUBLANES = 8
KMEAN_ROWS = 128
NEG = -1e30
F32_LOWEST = -3.0e38
VMEM_LIMIT = 56 * 1024 * 1024
PAGES_PER_STEP = 8
DSA_SWEEP = 1024

_NT = (((1,), (1,)), ((), ()))


def _cp(*sem):
    return pltpu.CompilerParams(dimension_semantics=sem, vmem_limit_bytes=VMEM_LIMIT)


def _sigmoid(x):
    return 1.0 / (1.0 + jnp.exp(-x))


def _mod_kernel(c_ref, w_ref, b_ref, o_ref):
    c = c_ref[...]
    sc = c * _sigmoid(c)
    o_ref[0] = jnp.dot(sc, w_ref[0], precision=HIGHEST, preferred_element_type=F32) + b_ref[0]


def _modulation(c_all, w_ada, b_ada):
    depth, d, d3 = w_ada.shape
    rc = c_all.shape[0]
    nj = d3 // d
    return pl.pallas_call(
        _mod_kernel,
        grid=(depth, nj),
        in_specs=[
            pl.BlockSpec((rc, d), lambda l, j: (0, 0)),
            pl.BlockSpec((1, d, d), lambda l, j: (l, 0, j)),
            pl.BlockSpec((1, 1, d), lambda l, j: (l, 0, j)),
        ],
        out_specs=pl.BlockSpec((1, rc, d), lambda l, j: (l, 0, j)),
        out_shape=jax.ShapeDtypeStruct((depth, rc, d3), F32),
        compiler_params=_cp("arbitrary", "arbitrary"),
        name="adaln_mod",
    )(c_all, w_ada, b_ada.reshape(depth, 1, d3))


def _norm_kernel(x_ref, w_ref, sc_ref, sh_ref, h_ref):
    x = x_ref[0]
    ms = jnp.mean(x * x, axis=-1, keepdims=True)
    y = x * lax.rsqrt(ms + NORM_EPS) * w_ref[...]
    h_ref[0] = (y * (1.0 + sc_ref[0]) + sh_ref[0]).astype(BF16)


def _norm_mod(x3, norm_w, scale3, shift3, tm):
    g, r, d = x3.shape
    rm = scale3.shape[1]
    tmod = 1 if rm == 1 else tm
    mod_map = (lambda b, i: (b, 0, 0)) if rm == 1 else (lambda b, i: (b, i, 0))
    return pl.pallas_call(
        _norm_kernel,
        grid=(g, r // tm),
        in_specs=[
            pl.BlockSpec((1, tm, d), lambda b, i: (b, i, 0)),
            pl.BlockSpec((1, d), lambda b, i: (0, 0)),
            pl.BlockSpec((1, tmod, d), mod_map),
            pl.BlockSpec((1, tmod, d), mod_map),
        ],
        out_specs=pl.BlockSpec((1, tm, d), lambda b, i: (b, i, 0)),
        out_shape=jax.ShapeDtypeStruct((g, r, d), BF16),
        compiler_params=_cp("arbitrary", "arbitrary"),
        name="norm_mod",
    )(x3, norm_w.reshape(1, d), scale3, shift3)


def _rope_chunk(x, a, bm, cm):
    return x * a + pltpu.roll(x, LANES - ROT_DIM // 2, 1) * bm + pltpu.roll(x, ROT_DIM // 2, 1) * cm


def _projq_kernel(h_ref, w_ref, ra_ref, rb_ref, rc_ref, qa_ref, qb_ref, qc_ref, qaf_ref):
    y = jnp.dot(h_ref[...], w_ref[...], preferred_element_type=F32)
    a, bm, cm = ra_ref[...], rb_ref[...], rc_ref[...]
    for m, o_ref in enumerate((qa_ref, qb_ref, qc_ref)):
        for c in range(MIX_WIDTH // LANES):
            x = y[:, m * MIX_WIDTH + c * LANES:m * MIX_WIDTH + (c + 1) * LANES]
            if m != 1:
                x = _rope_chunk(x, a, bm, cm)
            if m == 0:
                qaf_ref[:, c * LANES:(c + 1) * LANES] = x
            o_ref[:, c * LANES:(c + 1) * LANES] = (x * ATTN_SCALE).astype(BF16)


def _projkv_kernel(h_ref, w_ref, ra_ref, rb_ref, rc_ref, kv_ref, kvb_ref):
    y = jnp.dot(h_ref[...], w_ref[...], preferred_element_type=F32)
    a, bm, cm = ra_ref[...], rb_ref[...], rc_ref[...]
    per_mixer = MIXER_COLS // LANES
    for c in range(N_MIXERS * per_mixer):
        x = y[:, c * LANES:(c + 1) * LANES]
        mixer, within = divmod(c, per_mixer)
        if mixer != 1 and within < KV_WIDTH // LANES:
            x = _rope_chunk(x, a, bm, cm)
        kv_ref[:, c * LANES:(c + 1) * LANES] = x
        kvb_ref[:, c * LANES:(c + 1) * LANES] = x.astype(BF16)


def _projz_kernel(h_ref, w_ref, o_ref):
    y = jnp.dot(h_ref[...], w_ref[...], preferred_element_type=F32)
    o_ref[...] = y * _sigmoid(y)


def _projg_kernel(h_ref, w_ref, o_ref):
    y = jnp.dot(h_ref[...], w_ref[...], preferred_element_type=F32)
    o_ref[...] = _sigmoid(y)


MISC_KI = MIX_WIDTH
MISC_WI = MISC_KI + LANES
MISC_FB = MISC_WI + LANES
MISC_N = MISC_FB + LANES


def _projmisc_kernel(h_ref, w_ref, ra_ref, rb_ref, rc_ref, bf_ref,
                     qi_ref, ki_ref, ki2_ref, wi_ref, lf_ref):
    y = jnp.dot(h_ref[...], w_ref[...], preferred_element_type=F32)
    a, bm, cm = ra_ref[...], rb_ref[...], rc_ref[...]
    for c in range(MIX_WIDTH // LANES):
        x = _rope_chunk(y[:, c * LANES:(c + 1) * LANES], a, bm, cm)
        qi_ref[:, c * LANES:(c + 1) * LANES] = x.astype(BF16)
    ki = _rope_chunk(y[:, MISC_KI:MISC_KI + LANES], a, bm, cm)
    ki_ref[...] = ki[:, :IDX_DIM]
    ki2_ref[...] = (ki + pltpu.roll(ki, IDX_DIM, 1)).astype(BF16)
    wi_ref[...] = y[:, MISC_WI:MISC_WI + IDX_HEADS]
    f = y[:, MISC_FB:MISC_FB + LANES] + bf_ref[...]
    logf = jnp.minimum(f, 0.0) - jnp.log(1.0 + jnp.exp(-jnp.abs(f)))
    lf_ref[...] = logf[:, :N_HEADS]


def _proj_call(kernel, h, w, tabs, extra, outs, tm, tn, name):
    m, d = h.shape
    n = w.shape[1]
    nt = tabs[0].shape[0] // tm if tabs else 1
    in_specs = [pl.BlockSpec((tm, d), lambda j, i: (i, 0)),
                pl.BlockSpec((d, tn), lambda j, i: (0, j))]
    in_specs += [pl.BlockSpec((tm, LANES), lambda j, i: (i % nt, 0)) for _ in tabs]
    in_specs += [pl.BlockSpec(e.shape, lambda j, i: (0, 0)) for e in extra]
    out_specs = [pl.BlockSpec((tm, wd if full else tn), (lambda j, i: (i, 0)) if full else (lambda j, i: (i, j)))
                 for (wd, _, full) in outs]
    out_shape = [jax.ShapeDtypeStruct((m, wd), dt) for (wd, dt, _) in outs]
    return pl.pallas_call(
        kernel,
        grid=(n // tn, m // tm),
        in_specs=in_specs,
        out_specs=out_specs,
        out_shape=out_shape,
        compiler_params=_cp("arbitrary", "arbitrary"),
        name=name,
    )(h, w, *tabs, *extra)


def _kmean_kernel(k_ref, o_ref):
    o_ref[0, 0] = jnp.sum(k_ref[0], axis=0, keepdims=True) * (1.0 / MOBA_BLOCK)


def _kmean_prompt(kv3):
    b, t, _ = kv3.shape
    nb = t // MOBA_BLOCK
    out = pl.pallas_call(
        _kmean_kernel,
        grid=(b, nb),
        in_specs=[pl.BlockSpec((1, MOBA_BLOCK, KV_WIDTH), lambda i, n: (i, n, 0))],
        out_specs=pl.BlockSpec((1, 1, 1, KV_WIDTH), lambda i, n: (i, n, 0, 0)),
        out_shape=jax.ShapeDtypeStruct((b, nb, 1, KV_WIDTH), F32),
        compiler_params=_cp("arbitrary", "arbitrary"),
        name="moba_kmean",
    )(kv3)
    return out.reshape(b, nb, KV_WIDTH)


CUM_CHUNK = 256


def _cumsum_kernel(x_ref, o_ref):
    n = x_ref.shape[2] // CUM_CHUNK
    r = lax.broadcasted_iota(jnp.int32, (CUM_CHUNK, CUM_CHUNK), 0)
    c = lax.broadcasted_iota(jnp.int32, (CUM_CHUNK, CUM_CHUNK), 1)
    tri = (r <= c).astype(F32)

    def body(i, carry):
        st = pl.multiple_of(i * CUM_CHUNK, CUM_CHUNK)
        x = x_ref[0, :, pl.ds(st, CUM_CHUNK)]
        y = jnp.dot(x, tri, precision=HIGHEST, preferred_element_type=F32) + carry
        o_ref[0, :, pl.ds(st, CUM_CHUNK)] = y
        return y[:, CUM_CHUNK - 1:CUM_CHUNK]

    lax.fori_loop(0, n, body, jnp.zeros((x_ref.shape[1], 1), F32))


def _cumsum(lft):
    b, h, lp = lft.shape
    return pl.pallas_call(
        _cumsum_kernel,
        grid=(b,),
        in_specs=[pl.BlockSpec((1, h, lp), lambda i: (i, 0, 0))],
        out_specs=pl.BlockSpec((1, h, lp), lambda i: (i, 0, 0)),
        out_shape=jax.ShapeDtypeStruct((b, h, lp), F32),
        compiler_params=_cp("arbitrary"),
        name="fox_cumsum",
    )(lft)


def _stack4(blk0, blk1):
    lane = lax.broadcasted_iota(jnp.int32, blk0.shape, 1)
    lo = lane < HEAD_DIM
    return jnp.concatenate([
        jnp.where(lo, blk0, 0.0),
        jnp.where(lo, pltpu.roll(blk0, HEAD_DIM, 1), 0.0),
        jnp.where(lo, 0.0, pltpu.roll(blk1, HEAD_DIM, 1)),
        jnp.where(lo, 0.0, blk1)], axis=0)


def _unstack4(o, tq):
    lane = lax.broadcasted_iota(jnp.int32, (tq, LANES), 1)
    lo = lane < HEAD_DIM
    b0 = jnp.where(lo, o[:tq], pltpu.roll(o[tq:2 * tq], HEAD_DIM, 1))
    b1 = jnp.where(lo, pltpu.roll(o[2 * tq:3 * tq], HEAD_DIM, 1), o[3 * tq:])
    return b0, b1


def _stacked_queries(q_ref):
    return [_stack4(q_ref[0, :, (2 * c) * LANES:(2 * c + 1) * LANES].astype(F32),
                    q_ref[0, :, (2 * c + 1) * LANES:(2 * c + 2) * LANES].astype(F32)) for c in range(2)]


def _flash_scratch(tq, tk):
    rows = 4 * tq
    return [pltpu.VMEM((2, rows, tk), F32), pltpu.VMEM((2, rows, tk), BF16),
            pltpu.VMEM((2, rows, LANES), F32), pltpu.VMEM((2, rows, LANES), F32), pltpu.VMEM((2, rows, LANES), F32),
            pltpu.VMEM((2, rows, 2 * LANES), F32), pltpu.VMEM((2, rows, tk), F32)]


def _flash_pipeline(n_all, qs, k_ref, v_ref, add_bias, diag_mask, scratch, o_ref, tq, tk):
    s_ref, p_ref, al_ref, m_ref, mx_ref, acc_ref, mb_ref = scratch
    m_ref[...] = jnp.full(m_ref.shape, -jnp.inf, F32)
    acc_ref[...] = jnp.zeros(acc_ref.shape, F32)
    s_ref[...] = jnp.full(s_ref.shape, NEG, F32)
    mx_ref[...] = jnp.full(mx_ref.shape, NEG, F32)
    p_ref[...] = jnp.zeros(p_ref.shape, BF16)
    al_ref[...] = jnp.zeros(al_ref.shape, F32)
    if diag_mask is not None:
        mb_ref[0] = jnp.zeros(mb_ref.shape[1:], F32)
        mb_ref[1] = diag_mask
    last = n_all - 1
    ones = jnp.ones((tk, LANES), BF16)

    def step(i):
        jc = jnp.clip(i - 2, 0, last)
        stc = pl.multiple_of(jc * tk, tk)
        for c in range(2):
            vo = jnp.concatenate([v_ref[0, pl.ds(stc, tk), c * LANES:(c + 1) * LANES], ones], axis=1)
            al = al_ref[c]
            acc_ref[c] = (jnp.concatenate([al, al], axis=1) * acc_ref[c]
                          + jnp.dot(p_ref[c], vo, preferred_element_type=F32))
        for c in range(2):
            m_old = m_ref[c]
            m_new = jnp.maximum(m_old, jnp.max(mx_ref[c], axis=-1, keepdims=True))
            al_ref[c] = jnp.exp(m_old - m_new)
            m_ref[c] = m_new
            p_ref[c] = jnp.exp(s_ref[c] - jnp.concatenate([m_new] * (tk // LANES), axis=1)).astype(BF16)
        ja = jnp.minimum(i, last)
        sta = pl.multiple_of(ja * tk, tk)
        pen = jnp.where(i <= last, 0.0, NEG).astype(F32)
        for c in range(2):
            s = lax.dot_general(qs[c], k_ref[0, pl.ds(sta, tk), c * LANES:(c + 1) * LANES], _NT,
                                preferred_element_type=F32)
            s = add_bias(c, s, ja, pen)
            if diag_mask is not None:
                s = s + mb_ref[(ja == last).astype(jnp.int32)]
            s_ref[c] = s
            mx_ref[c] = functools.reduce(jnp.maximum, [s[:, k * LANES:(k + 1) * LANES] for k in range(tk // LANES)])

    def step2(ii, carry):
        step(2 * ii)
        step(2 * ii + 1)
        return carry

    lax.fori_loop(0, (n_all + 3) // 2, step2, 0)
    for c in range(2):
        b0, b1 = _unstack4(acc_ref[c, :, :LANES] / acc_ref[c, :, LANES:], tq)
        o_ref[0, :, (2 * c) * LANES:(2 * c + 1) * LANES] = b0
        o_ref[0, :, (2 * c + 1) * LANES:(2 * c + 2) * LANES] = b1


def _diag_mask(q_lo, n_all, tq, tk):
    r1 = lax.broadcasted_iota(jnp.int32, (tq, 1), 0) + q_lo
    qpos = jnp.concatenate([r1] * 4, axis=0)
    kpos = (n_all - 1) * tk + lax.broadcasted_iota(jnp.int32, (4 * tq, tk), 1)
    return jnp.where(kpos <= qpos, 0.0, NEG)


def _attn_call(kern, name, ins, in_specs, b, t, tq, scratch):
    return pl.pallas_call(
        kern,
        grid=(b, t // tq),
        in_specs=in_specs,
        out_specs=pl.BlockSpec((1, tq, MIX_WIDTH), lambda i, j: (i, j, 0)),
        out_shape=jax.ShapeDtypeStruct((b, t, MIX_WIDTH), F32),
        scratch_shapes=scratch,
        compiler_params=_cp("arbitrary", "arbitrary"),
        name=name,
    )(*ins)


_q_spec = lambda tq: pl.BlockSpec((1, tq, MIX_WIDTH), lambda i, j: (i, j, 0))
_kv_spec = lambda lp, col: pl.BlockSpec((1, lp, KV_WIDTH), lambda i, j: (i, 0, col))


def _moba_kernel(qf_ref, q_ref, km_ref, k_ref, v_ref, o_ref, *scratch, tq, topk):
    tk = MOBA_BLOCK
    q_lo = pl.program_id(1) * tq
    own = q_lo // MOBA_BLOCK
    rows = 4 * tq
    lane = lax.broadcasted_iota(jnp.int32, (rows, LANES), 1)
    lane_f = lane.astype(F32)
    qs = [q.astype(BF16) for q in _stacked_queries(q_ref)]
    selb = []
    for c, qf in enumerate(_stacked_queries(qf_ref)):
        gate = lax.dot_general(qf, km_ref[0, :, c * LANES:(c + 1) * LANES], _NT, precision=HIGHEST,
                               preferred_element_type=F32)
        gate = jnp.where(lane < own, gate, -jnp.inf)
        sb = jnp.where(lane == own, 0.0, NEG)
        for _ in range(topk):
            mx = jnp.max(gate, axis=-1, keepdims=True)
            is_max = (gate == mx) & (mx > -jnp.inf)
            first = jnp.min(jnp.where(is_max, lane_f, 1e9), axis=-1, keepdims=True)
            pick = lane_f == first
            sb = jnp.where(pick, 0.0, sb)
            gate = jnp.where(pick, -jnp.inf, gate)
        selb.append(sb)

    def add_bias(c, s, j, pen):
        return s + (jnp.sum(jnp.where(lane == j, selb[c], 0.0), axis=-1, keepdims=True) + pen)

    _flash_pipeline(own + 1, qs, k_ref, v_ref, add_bias, _diag_mask(q_lo, own + 1, tq, tk), scratch, o_ref, tq, tk)


def _moba_attn(qf, q, km, kvb, tq):
    b, t, _ = q.shape
    lp = kvb.shape[1]
    kern = functools.partial(_moba_kernel, tq=tq, topk=min(MOBA_TOPK, lp // MOBA_BLOCK))
    specs = [_q_spec(tq), _q_spec(tq), pl.BlockSpec((1, KMEAN_ROWS, KV_WIDTH), lambda i, j: (i, 0, 0)),
             _kv_spec(lp, 0), _kv_spec(lp, 1)]
    return _attn_call(kern, "moba_attn", (qf, q, km, kvb, kvb), specs, b, t, tq, _flash_scratch(tq, MOBA_BLOCK))


def _fox_kernel(q_ref, cum_ref, k_ref, v_ref, o_ref, *scratch, tq, tk):
    q_lo = pl.program_id(1) * tq
    n_all = (q_lo + tq + tk - 1) // tk
    qs = [q.astype(BF16) for q in _stacked_queries(q_ref)]

    def add_bias(c, s, j, pen):
        st = pl.multiple_of(j * tk, tk)
        parts = [s[hh * tq:(hh + 1) * tq] - (cum_ref[0, 4 * c + hh:4 * c + hh + 1, pl.ds(st, tk)] - pen)
                 for hh in range(4)]
        return jnp.concatenate(parts, axis=0)

    _flash_pipeline(n_all, qs, k_ref, v_ref, add_bias, _diag_mask(q_lo, n_all, tq, tk), scratch, o_ref, tq, tk)


def _fox_attn(q, cum, kvb, tq, tk):
    b, t, _ = q.shape
    lp = kvb.shape[1]
    kern = functools.partial(_fox_kernel, tq=tq, tk=tk)
    specs = [_q_spec(tq), pl.BlockSpec((1, N_HEADS, lp), lambda i, j: (i, 0, 0)), _kv_spec(lp, 2), _kv_spec(lp, 3)]
    return _attn_call(kern, "fox_attn", (q, cum, kvb, kvb), specs, b, t, tq, _flash_scratch(tq, tk))


def _key_to_float(key):
    bits = jnp.where(key < 0, key & jnp.int32(0x7FFFFFFF), ~key)
    return lax.bitcast_convert_type(bits, F32)


def _topk_threshold(count, n_rows, need, few, idx_bits):
    def bit_body(i, key):
        cand = key | lax.shift_left(jnp.int32(1), 31 - i)
        thr_c = _key_to_float(cand)
        return jnp.where(count(lambda x, idx: x >= thr_c) >= need, cand, key)

    key = lax.fori_loop(0, 32, bit_body, jnp.zeros((n_rows, 1), jnp.int32))
    thr = _key_to_float(key)
    cnt_ge = count(lambda x, idx: x >= thr)
    has_ties = jnp.max(jnp.where(few, 0.0, cnt_ge - need)) > 0.0

    def tie_cut(_):
        r = need - count(lambda x, idx: x > thr)

        def jb(i, cut):
            cand = cut | lax.shift_left(jnp.int32(1), idx_bits - 1 - i)
            cnt = count(lambda x, idx: (x == thr) & (idx < cand))
            return jnp.where(cnt < r, cand, cut)

        return lax.fori_loop(0, idx_bits, jb, jnp.zeros((n_rows, 1), jnp.int32))

    cut = lax.cond(has_ties, tie_cut, lambda _: jnp.full((n_rows, 1), 2 ** 30, jnp.int32), 0)
    return jnp.where(few, F32_LOWEST, thr), jnp.where(few, 2 ** 30, cut)


def _dsa_kernel(qi_ref, w_ref, kx_ref, q_ref, k_ref, v_ref, o_ref, isc_ref, *scratch, tq, tk, sweep, topk,
                idx_bits):
    q_lo = pl.program_id(1) * tq
    n_all = (q_lo + tq + tk - 1) // tk
    nl = tk // LANES
    lane = lax.broadcasted_iota(jnp.int32, (tq, LANES), 1)
    lo = lane < HEAD_DIM
    qpos1 = lax.broadcasted_iota(jnp.int32, (tq, 1), 0) + q_lo
    kiota1 = lax.broadcasted_iota(jnp.int32, (tq, tk), 1)

    pieces, wpieces = [], []
    for h in range(IDX_HEADS):
        chunk = qi_ref[0, :, (h // 2) * LANES:(h // 2 + 1) * LANES].astype(F32)
        pieces.append(jnp.where(lo, chunk, 0.0) if h % 2 == 0 else jnp.where(lo, 0.0, chunk))
        wpieces.append(jnp.broadcast_to(w_ref[0, :, h:h + 1], (tq, LANES)))
    qstack = jnp.concatenate(pieces, axis=0).astype(BF16)
    wfull = jnp.concatenate([jnp.concatenate(wpieces, axis=0)] * nl, axis=1)

    def idx_body(j, carry):
        start = pl.multiple_of(j * tk, tk)
        sc = lax.dot_general(qstack, kx_ref[0, pl.ds(start, tk), :], _NT, preferred_element_type=F32)
        contrib = jnp.maximum(sc, 0.0) * wfull
        isc = contrib[:tq]
        for h in range(1, IDX_HEADS):
            isc = isc + contrib[h * tq:(h + 1) * tq]
        isc_ref[:, pl.ds(start, tk)] = jnp.where(j * tk + kiota1 <= qpos1, isc * IDX_SCALE, -jnp.inf)
        return carry

    lax.fori_loop(0, n_all, idx_body, 0)

    tiles_per_sweep = sweep // tk
    n_sweeps = (n_all + tiles_per_sweep - 1) // tiles_per_sweep

    def pad_body(j, carry):
        isc_ref[:, pl.ds(pl.multiple_of(j * tk, tk), tk)] = jnp.full((tq, tk), -jnp.inf, F32)
        return carry

    lax.fori_loop(n_all, n_sweeps * tiles_per_sweep, pad_body, 0)

    def count(pred):
        def body(j, acc):
            start = pl.multiple_of(j * sweep, sweep)
            x = isc_ref[:, pl.ds(start, sweep)]
            for cc in range(sweep // LANES):
                idx = j * sweep + cc * LANES + lane
                acc = acc + jnp.where(pred(x[:, cc * LANES:(cc + 1) * LANES], idx), 1.0, 0.0)
            return acc
        acc = lax.fori_loop(0, n_sweeps, body, jnp.zeros((tq, LANES), F32))
        return jnp.sum(acc, axis=-1, keepdims=True)

    few = qpos1 + 1 <= topk
    thr, cut = _topk_threshold(count, tq, jnp.float32(topk), few, idx_bits)

    def bias_body(j, carry):
        start = pl.multiple_of(j * tk, tk)
        x = isc_ref[:, pl.ds(start, tk)]
        keep = (x > thr) | ((x == thr) & (j * tk + kiota1 <= cut))
        isc_ref[:, pl.ds(start, tk)] = jnp.where(keep, 0.0, NEG)
        return carry

    lax.fori_loop(0, n_all, bias_body, 0)

    qs = [q.astype(BF16) for q in _stacked_queries(q_ref)]

    def add_bias(c, s, j, pen):
        bias = isc_ref[:, pl.ds(pl.multiple_of(j * tk, tk), tk)] + pen
        return s + jnp.concatenate([bias] * 4, axis=0)

    _flash_pipeline(n_all, qs, k_ref, v_ref, add_bias, None, scratch, o_ref, tq, tk)


def _dsa_attn(qi, wi, kx2, q, kvb, tq, tk):
    b, t, _ = q.shape
    lp = kvb.shape[1]
    sweep = max(tk, min(DSA_SWEEP, lp))
    assert lp % sweep == 0 and sweep % tk == 0
    kern = functools.partial(_dsa_kernel, tq=tq, tk=tk, sweep=sweep, topk=min(DSA_TOPK, lp // 4),
                             idx_bits=max(1, int(lp).bit_length()))
    specs = [_q_spec(tq), pl.BlockSpec((1, tq, IDX_HEADS), lambda i, j: (i, j, 0)),
             pl.BlockSpec((1, lp, 2 * IDX_DIM), lambda i, j: (i, 0, 0)), _q_spec(tq), _kv_spec(lp, 4), _kv_spec(lp, 5)]
    scratch = [pltpu.VMEM((tq, lp), F32)] + _flash_scratch(tq, tk)
    return _attn_call(kern, "dsa_attn", (qi, wi, kx2, q, kvb, kvb), specs, b, t, tq, scratch)


def _sample_index_kernel(pt_ref, *refs, ppi, n_steps, topk, l_past, t_new, idx_bits):
    del pt_ref
    kx_refs = refs[:ppi]
    qi_ref, w_ref, nkx_ref, o_ref, isc_ref = refs[ppi:]
    j = pl.program_id(1)
    lp = isc_ref.shape[1]
    lane = lax.broadcasted_iota(jnp.int32, (t_new, LANES), 1)
    lo = lane < HEAD_DIM
    pieces, wpieces = [], []
    for h in range(IDX_HEADS):
        chunk = qi_ref[0, :, (h // 2) * LANES:(h // 2 + 1) * LANES].astype(F32)
        pieces.append(jnp.where(lo, chunk, 0.0) if h % 2 == 0 else jnp.where(lo, 0.0, chunk))
        wpieces.append(jnp.broadcast_to(w_ref[0, :, h:h + 1], (t_new, LANES)))
    qstack = jnp.concatenate(pieces, axis=0).astype(BF16)
    wb = jnp.concatenate(wpieces, axis=0)

    def head_sum(sc):
        contrib = jnp.maximum(sc, 0.0) * jnp.concatenate([wb] * (sc.shape[1] // LANES), axis=1)
        isc = contrib[:t_new]
        for h in range(1, IDX_HEADS):
            isc = isc + contrib[h * t_new:(h + 1) * t_new]
        return isc * IDX_SCALE

    kxt = jnp.concatenate([r[0, 0] for r in kx_refs], axis=1)
    rhs = jnp.concatenate([kxt, kxt], axis=0).astype(BF16)
    start = pl.multiple_of(j * (ppi * PAGE_SIZE), ppi * PAGE_SIZE)
    isc_ref[:, pl.ds(start, ppi * PAGE_SIZE)] = head_sum(jnp.dot(qstack, rhs, preferred_element_type=F32))

    @pl.when(j == n_steps - 1)
    def _():
        scn = lax.dot_general(qstack, nkx_ref[0], _NT, preferred_element_type=F32)
        row = lax.broadcasted_iota(jnp.int32, (t_new, LANES), 0)
        isc_ref[:, l_past:l_past + LANES] = jnp.where((lane <= row) & (lane < t_new), head_sum(scn), -jnp.inf)
        if lp > l_past + LANES:
            isc_ref[:, l_past + LANES:] = jnp.full((t_new, lp - l_past - LANES), -jnp.inf, F32)
        x = isc_ref[...]
        idx = lax.broadcasted_iota(jnp.int32, (t_new, lp), 1)

        def count(pred):
            return jnp.sum(jnp.where(pred(x, idx), 1.0, 0.0), axis=-1, keepdims=True)

        qpos1 = lax.broadcasted_iota(jnp.int32, (t_new, 1), 0) + l_past
        thr, cut = _topk_threshold(count, t_new, jnp.float32(topk), qpos1 + 1 <= topk, idx_bits)
        keep = (x > thr) | ((x == thr) & (idx <= cut))
        o_ref[0] = jnp.where(keep, 0.0, NEG)


def _sample_index(cache_kxt, page_table, layer, qi, wi, nkx2, lp):
    nseq, n_pages = page_table.shape
    t_new = qi.shape[1]
    ppi = PAGES_PER_STEP
    n_steps = n_pages // ppi
    l_past = n_pages * PAGE_SIZE
    pg = lambda pp: (lambda i, j, pt: (layer, pt[i, j * ppi + pp], 0, 0))
    grid_spec = pltpu.PrefetchScalarGridSpec(
        num_scalar_prefetch=1,
        grid=(nseq, n_steps),
        in_specs=[pl.BlockSpec((1, 1, IDX_DIM, PAGE_SIZE), pg(pp)) for pp in range(ppi)] + [
            pl.BlockSpec((1, t_new, MIX_WIDTH), lambda i, j, pt: (i, 0, 0)),
            pl.BlockSpec((1, t_new, IDX_HEADS), lambda i, j, pt: (i, 0, 0)),
            pl.BlockSpec((1, PAGE_SIZE, 2 * IDX_DIM), lambda i, j, pt: (i, 0, 0)),
        ],
        out_specs=pl.BlockSpec((1, t_new, lp), lambda i, j, pt: (i, 0, 0)),
        scratch_shapes=[pltpu.VMEM((t_new, lp), F32)],
    )
    kern = functools.partial(_sample_index_kernel, ppi=ppi, n_steps=n_steps, topk=min(DSA_TOPK, (l_past + t_new) // 4),
                             l_past=l_past, t_new=t_new, idx_bits=max(1, int(lp).bit_length()))
    return pl.pallas_call(
        kern,
        grid_spec=grid_spec,
        out_shape=jax.ShapeDtypeStruct((nseq, t_new, lp), F32),
        compiler_params=_cp("arbitrary", "arbitrary"),
        name="dsa_index_paged",
    )(page_table, *([cache_kxt] * ppi), qi, wi, nkx2)


def _block_diag_q(q):
    t = q.shape[0]
    lane = lax.broadcasted_iota(jnp.int32, (t, LANES), 1)
    lo = lane < HEAD_DIM
    zero = jnp.zeros((t, LANES), F32)
    out = []
    for h in range(N_HEADS):
        src = q[:, (h // 2) * LANES:(h // 2 + 1) * LANES]
        kvh = h // (N_HEADS // N_KV_HEADS)
        chunk, half = divmod(kvh, 2)
        piece = src if h % 2 == half else pltpu.roll(src, HEAD_DIM, 1)
        piece = jnp.where(lo, piece, 0.0) if half == 0 else jnp.where(lo, 0.0, piece)
        out.append(jnp.concatenate([piece, zero] if chunk == 0 else [zero, piece], axis=1))
    return jnp.concatenate(out, axis=0)


def _token_major(o, t):
    lane = lax.broadcasted_iota(jnp.int32, (t, LANES), 1)
    lo = lane < HEAD_DIM
    out = []
    for co in range(MIX_WIDTH // LANES):
        chunk, half = divmod(co, 2)
        x = o[(2 * co) * t:(2 * co + 1) * t, chunk * LANES:(chunk + 1) * LANES]
        y = o[(2 * co + 1) * t:(2 * co + 2) * t, chunk * LANES:(chunk + 1) * LANES]
        if half == 0:
            out.append(jnp.where(lo, x, pltpu.roll(y, HEAD_DIM, 1)))
        else:
            out.append(jnp.where(lo, pltpu.roll(x, HEAD_DIM, 1), y))
    return jnp.concatenate(out, axis=1)


def _lane_cumsum(x):
    lane = lax.broadcasted_iota(jnp.int32, x.shape, 1)
    d = 1
    while d < LANES:
        x = x + jnp.where(lane >= d, pltpu.roll(x, d, 1), 0.0)
        d *= 2
    return x


def _rows_from_heads(c, t):
    return jnp.concatenate([jnp.broadcast_to(c[h:h + 1], (t, c.shape[1])) for h in range(N_HEADS)], axis=0)


def _online_update(s, pv_fn, m_ref, l_ref, acc_ref):
    m_old = m_ref[...]
    m_new = jnp.maximum(m_old, jnp.max(s, axis=-1, keepdims=True))
    alpha = jnp.exp(m_old - m_new)
    p = jnp.exp(s - jnp.concatenate([m_new] * (s.shape[1] // LANES), axis=1))
    l_ref[...] = alpha * l_ref[...] + jnp.sum(p, axis=-1, keepdims=True)
    acc_ref[...] = jnp.concatenate([alpha] * (acc_ref.shape[1] // LANES), axis=1) * acc_ref[...] + pv_fn(p.astype(BF16))
    m_ref[...] = m_new


def _sample_attn_kernel(pt_ref, *refs, nb, pps, t_new, topk):
    del pt_ref
    pg_refs, lf_refs = refs[:pps], refs[pps:2 * pps]
    (bias_ref, qa_ref, qb_ref, qc_ref, qaf_ref, nkv_ref, nlf_ref, oa_ref, ob_ref, oc_ref,
     qbd_ref, qbdf_ref, km_ref, ms_ref, ls_ref, accs_ref,
     fm_ref, fl_ref, facc_ref, car_ref, dm_ref, dl_ref, dacc_ref) = refs[2 * pps:]
    j = pl.program_id(1)
    rows = N_HEADS * t_new
    ppb = MOBA_BLOCK // PAGE_SIZE
    bps = pps // ppb
    n_steps = nb // bps
    lane = lax.broadcasted_iota(jnp.int32, (rows, LANES), 1)

    @pl.when(j == 0)
    def _():
        for m, q_ref in enumerate((qa_ref, qb_ref, qc_ref)):
            qbd_ref[m] = _block_diag_q(q_ref[0].astype(F32)).astype(BF16)
        qbdf_ref[...] = _block_diag_q(qaf_ref[0])
        km_ref[...] = jnp.zeros(km_ref.shape, F32)
        ms_ref[...] = jnp.zeros(ms_ref.shape, F32)
        ls_ref[...] = jnp.zeros(ls_ref.shape, F32)
        car_ref[...] = jnp.zeros(car_ref.shape, F32)
        for m_ref, l_ref, acc_ref in ((fm_ref, fl_ref, facc_ref), (dm_ref, dl_ref, dacc_ref)):
            m_ref[...] = jnp.full(m_ref.shape, -jnp.inf, F32)
            l_ref[...] = jnp.zeros(l_ref.shape, F32)
            acc_ref[...] = jnp.zeros(acc_ref.shape, F32)

    @pl.when(j < n_steps)
    def _():
        def kt(m, off):
            lo_r = m * MIXER_COLS + off
            return jnp.concatenate([r[0, 0, lo_r:lo_r + KV_WIDTH, :] for r in pg_refs], axis=1)

        def pv_nt(m):
            vt = kt(m, KV_WIDTH).astype(BF16)
            return lambda p: lax.dot_general(p, vt, _NT, preferred_element_type=F32)

        k0 = kt(0, 0)
        v0 = kt(0, KV_WIDTH).astype(BF16)
        s = jnp.dot(qbd_ref[0], k0.astype(BF16), preferred_element_type=F32)
        blk_lane = lax.broadcasted_iota(jnp.int32, km_ref.shape, 1)
        km, ms, ls = km_ref[...], ms_ref[...], ls_ref[...]
        for b in range(bps):
            n = j * bps + b
            cols = slice(b * MOBA_BLOCK, (b + 1) * MOBA_BLOCK)
            km = jnp.where(blk_lane == n, jnp.sum(k0[:, cols], axis=1, keepdims=True) * (1.0 / MOBA_BLOCK), km)
            m_n = jnp.max(s[:, cols], axis=-1, keepdims=True)
            p = jnp.exp(s[:, cols] - m_n)
            ms = jnp.where(lane == n, m_n, ms)
            ls = jnp.where(lane == n, jnp.sum(p, axis=-1, keepdims=True), ls)
            accs_ref[n] = lax.dot_general(p.astype(BF16), v0[:, cols], _NT, preferred_element_type=F32)
        km_ref[...], ms_ref[...], ls_ref[...] = km, ms, ls

        carry, cums = car_ref[...][:, :1], []
        for r in lf_refs:
            cums.append(_lane_cumsum(r[0, 0]) + carry)
            carry = cums[-1][:, LANES - 1:LANES]
        car_ref[...] = jnp.broadcast_to(carry, car_ref.shape)
        s = jnp.dot(qbd_ref[1], kt(1, 0).astype(BF16), preferred_element_type=F32)
        s = s - _rows_from_heads(jnp.concatenate(cums, axis=1), t_new)
        _online_update(s, pv_nt(1), fm_ref, fl_ref, facc_ref)

        s = jnp.dot(qbd_ref[2], kt(2, 0).astype(BF16), preferred_element_type=F32)
        s = s + jnp.concatenate([bias_ref[0]] * N_HEADS, axis=0)
        _online_update(s, pv_nt(2), dm_ref, dl_ref, dacc_ref)

    @pl.when(j == n_steps)
    def _():
        nk = lambda m: nkv_ref[0, :, m * MIXER_COLS:m * MIXER_COLS + KV_WIDTH]
        nv = lambda m: nkv_ref[0, :, m * MIXER_COLS + KV_WIDTH:(m + 1) * MIXER_COLS]
        pv = lambda m: (lambda p: jnp.dot(p, nv(m), preferred_element_type=F32))
        row_t = lax.broadcasted_iota(jnp.int32, (rows, LANES), 0) & (t_new - 1)
        causal = (lane <= row_t) & (lane < t_new)

        s = lax.dot_general(qbd_ref[0], nk(0), _NT, preferred_element_type=F32)
        s = jnp.where(causal, s, NEG)
        m_o = jnp.max(s, axis=-1, keepdims=True)
        p = jnp.exp(s - m_o)
        l_o = jnp.sum(p, axis=-1, keepdims=True)
        acc_o = pv(0)(p.astype(BF16))
        gate = jnp.dot(qbdf_ref[...], km_ref[...], precision=HIGHEST, preferred_element_type=F32)
        gate = jnp.where(lane < nb, gate, -jnp.inf)
        lane_f = lane.astype(F32)
        sel = lane < 0
        for _ in range(topk):
            mx = jnp.max(gate, axis=-1, keepdims=True)
            is_max = (gate == mx) & (mx > -jnp.inf)
            first = jnp.min(jnp.where(is_max, lane_f, 1e9), axis=-1, keepdims=True)
            pick = lane_f == first
            sel = sel | pick
            gate = jnp.where(pick, -jnp.inf, gate)
        ms = ms_ref[...]
        m_all = jnp.maximum(m_o, jnp.max(jnp.where(sel, ms, -jnp.inf), axis=-1, keepdims=True))
        w = jnp.where(sel, jnp.exp(ms - m_all), 0.0)
        w_o = jnp.exp(m_o - m_all)
        l_all = jnp.sum(w * ls_ref[...], axis=-1, keepdims=True) + w_o * l_o

        def merge(n, o):
            wn = jnp.sum(jnp.where(lane == n, w, 0.0), axis=-1, keepdims=True)
            return o + wn * accs_ref[n]

        o_a = lax.fori_loop(0, nb, merge, w_o * acc_o) / l_all
        oa_ref[0] = _token_major(o_a, t_new)

        cn = _lane_cumsum(nlf_ref[0]) + car_ref[...]
        s = lax.dot_general(qbd_ref[1], nk(1), _NT, preferred_element_type=F32) - _rows_from_heads(cn, t_new)
        _online_update(jnp.where(causal, s, NEG), pv(1), fm_ref, fl_ref, facc_ref)
        wide = lambda l_ref: jnp.concatenate([l_ref[...]] * (KV_WIDTH // LANES), axis=1)
        ob_ref[0] = _token_major(facc_ref[...] / wide(fl_ref), t_new)

        s = lax.dot_general(qbd_ref[2], nk(2), _NT, preferred_element_type=F32)
        s = s + jnp.concatenate([bias_ref[0, :, :LANES]] * N_HEADS, axis=0)
        _online_update(s, pv(2), dm_ref, dl_ref, dacc_ref)
        oc_ref[0] = _token_major(dacc_ref[...] / wide(dl_ref), t_new)


def _sample_attn(cache_t, cache_lft, page_table, layer, bias, qa, qb, qc, qaf, nkv, nlf):
    nseq, n_pages = page_table.shape
    t_new = qa.shape[1]
    nb = n_pages // (MOBA_BLOCK // PAGE_SIZE)
    pps = PAGES_PER_STEP
    rows = N_HEADS * t_new
    width = cache_t.shape[2]
    last = n_pages - 1
    page = lambda k: (lambda i, j, pt: (layer, pt[i, jnp.minimum(pps * j + k, last)], 0, 0))
    seq_blk = lambda shape: pl.BlockSpec((1,) + shape, lambda i, j, pt: (i, 0, 0))
    grid_spec = pltpu.PrefetchScalarGridSpec(
        num_scalar_prefetch=1,
        grid=(nseq, n_pages // pps + 1),
        in_specs=[pl.BlockSpec((1, 1, width, PAGE_SIZE), page(k)) for k in range(pps)] + [
            pl.BlockSpec((1, 1, N_HEADS, PAGE_SIZE), page(k)) for k in range(pps)] + [
            pl.BlockSpec((1, t_new, pps * PAGE_SIZE), lambda i, j, pt: (i, 0, j)),
            seq_blk((t_new, MIX_WIDTH)), seq_blk((t_new, MIX_WIDTH)), seq_blk((t_new, MIX_WIDTH)),
            seq_blk((t_new, MIX_WIDTH)),
            seq_blk((PAGE_SIZE, width)), seq_blk((N_HEADS, PAGE_SIZE)),
        ],
        out_specs=[seq_blk((t_new, MIX_WIDTH))] * 3,
        scratch_shapes=[
            pltpu.VMEM((N_MIXERS, rows, KV_WIDTH), BF16), pltpu.VMEM((rows, KV_WIDTH), F32),
            pltpu.VMEM((KV_WIDTH, KMEAN_ROWS), F32), pltpu.VMEM((rows, LANES), F32), pltpu.VMEM((rows, LANES), F32),
            pltpu.VMEM((nb, rows, KV_WIDTH), F32),
            pltpu.VMEM((rows, LANES), F32), pltpu.VMEM((rows, LANES), F32), pltpu.VMEM((rows, KV_WIDTH), F32),
            pltpu.VMEM((N_HEADS, LANES), F32),
            pltpu.VMEM((rows, LANES), F32), pltpu.VMEM((rows, LANES), F32), pltpu.VMEM((rows, KV_WIDTH), F32),
        ],
    )
    kern = functools.partial(_sample_attn_kernel, nb=nb, pps=pps, t_new=t_new, topk=min(MOBA_TOPK, nb + 1))
    out = jax.ShapeDtypeStruct((nseq, t_new, MIX_WIDTH), F32)
    return pl.pallas_call(
        kern,
        grid_spec=grid_spec,
        out_shape=[out, out, out],
        compiler_params=_cp("arbitrary", "arbitrary"),
        name="sample_attn",
    )(page_table, *([cache_t] * pps), *([cache_lft] * pps), bias, qa, qb, qc, qaf, nkv, nlf)


def _out_kernel(oa_ref, ob_ref, oc_ref, sz_ref, sg_ref, x_ref, gate_ref, wb_ref, wo_ref, fw_ref, y_ref,
                *, final):
    d = x_ref.shape[2]
    merged = None
    for i, o_ref in enumerate((oa_ref, ob_ref, oc_ref)):
        t = (o_ref[0] * sz_ref[0, :, i * MIX_WIDTH:(i + 1) * MIX_WIDTH]).astype(BF16)
        br = sg_ref[0, :, i * d:(i + 1) * d] * jnp.dot(t, wb_ref[i], preferred_element_type=F32)
        merged = br if merged is None else merged + br
    y = x_ref[0] + gate_ref[0] * jnp.dot(merged.astype(BF16), wo_ref[...], preferred_element_type=F32)
    if final:
        ms = jnp.mean(y * y, axis=-1, keepdims=True)
        y = y * lax.rsqrt(ms + NORM_EPS) * fw_ref[...]
    y_ref[0] = y


def _out_proj(oa, ob, oc, sz, sg, x3, gate3, wb, wo, fw, tm, final):
    g, r, d = x3.shape
    rm = gate3.shape[1]
    tmod = 1 if rm == 1 else tm
    mod_map = (lambda b, i: (b, 0, 0)) if rm == 1 else (lambda b, i: (b, i, 0))
    row = lambda w: pl.BlockSpec((1, tm, w), lambda b, i: (b, i, 0))
    as3 = lambda a: a.reshape(g, r, a.shape[-1])
    return pl.pallas_call(
        functools.partial(_out_kernel, final=final),
        grid=(g, r // tm),
        in_specs=[
            row(MIX_WIDTH), row(MIX_WIDTH), row(MIX_WIDTH), row(N_MIXERS * MIX_WIDTH), row(N_MIXERS * d), row(d),
            pl.BlockSpec((1, tmod, d), mod_map),
            pl.BlockSpec(wb.shape, lambda b, i: (0, 0, 0)),
            pl.BlockSpec(wo.shape, lambda b, i: (0, 0)),
            pl.BlockSpec((1, d), lambda b, i: (0, 0)),
        ],
        out_specs=row(d),
        out_shape=jax.ShapeDtypeStruct((g, r, d), F32),
        compiler_params=_cp("arbitrary", "arbitrary"),
        name="out_proj",
    )(as3(oa), as3(ob), as3(oc), as3(sz), as3(sg), x3, gate3, wb, wo, fw.reshape(1, d))


def _rope_tables(pos):
    half = ROT_DIM // 2
    expo = jnp.arange(0, ROT_DIM, 2, dtype=F32) / ROT_DIM
    inv_freq = jnp.power(jnp.float32(ROPE_THETA), -expo)
    ang = pos.astype(F32)[:, None] * inv_freq[None, :]
    cos, sin = jnp.cos(ang), jnp.sin(ang)
    n = pos.shape[0]
    rest = HEAD_DIM - ROT_DIM
    a = jnp.concatenate([cos, cos, jnp.ones((n, rest), F32)], axis=1)
    bm = jnp.concatenate([-sin, jnp.zeros((n, half + rest), F32)], axis=1)
    cm = jnp.concatenate([jnp.zeros((n, half), F32), sin, jnp.zeros((n, rest), F32)], axis=1)
    rep = LANES // HEAD_DIM
    return tuple(jnp.tile(t, (1, rep)) for t in (a, bm, cm))


def _split_weights(w_in_l, b_f_l):
    o = np.cumsum([0, MIX_WIDTH, KV_WIDTH, KV_WIDTH, MIX_WIDTH,
                   MIX_WIDTH, KV_WIDTH, KV_WIDTH, N_HEADS, MIX_WIDTH,
                   MIX_WIDTH, KV_WIDTH, KV_WIDTH, IDX_HEADS * IDX_DIM, IDX_DIM, IDX_HEADS, MIX_WIDTH]).tolist()
    col = lambda i: w_in_l[:, o[i]:o[i + 1]]
    qa, ka, va, za, qb, kb, vb, fb, zb, qc, kc, vc, qi, ki, wi, zc = (col(i) for i in range(16))
    g = w_in_l[:, o[16]:]
    d = w_in_l.shape[0]
    zpad = lambda w: jnp.zeros((d, w), w_in_l.dtype)
    w_q = jnp.concatenate([qa, qb, qc], axis=1)
    w_kv = jnp.concatenate([ka, va, kb, vb, kc, vc], axis=1)
    w_z = jnp.concatenate([za, zb, zc], axis=1)
    w_misc = jnp.concatenate([qi, ki, zpad(LANES - IDX_DIM), wi, zpad(LANES - IDX_HEADS),
                              fb, zpad(LANES - N_HEADS)], axis=1)
    bf_row = jnp.concatenate([b_f_l, jnp.zeros((LANES - N_HEADS,), b_f_l.dtype)]).reshape(1, LANES)
    cast = lambda w: w.astype(BF16)
    return cast(w_q), cast(w_kv), cast(w_z), cast(g), cast(w_misc), bf_row.astype(F32)


def _project(x3, scale3, shift3, tabs, weights, norm_w_l, tm):
    g, r, d = x3.shape
    w_q, w_kv, w_z, w_g, w_misc, bf_row = weights
    h = _norm_mod(x3, norm_w_l, scale3, shift3, tm).reshape(g * r, d)
    q_outs = _proj_call(
        _projq_kernel, h, w_q, tabs, (), [(MIX_WIDTH, BF16, True)] * 3 + [(MIX_WIDTH, F32, True)],
        tm, w_q.shape[1], "proj_q")
    kv, kvb = _proj_call(
        _projkv_kernel, h, w_kv, tabs, (), [(w_kv.shape[1], F32, True), (w_kv.shape[1], BF16, True)],
        tm, w_kv.shape[1], "proj_kv")
    (sz,) = _proj_call(_projz_kernel, h, w_z, (), (), [(w_z.shape[1], F32, True)], tm, w_z.shape[1], "proj_z")
    (sg,) = _proj_call(_projg_kernel, h, w_g, (), (), [(w_g.shape[1], F32, False)], tm, d, "proj_g")
    misc = _proj_call(
        _projmisc_kernel, h, w_misc, tabs, (bf_row,),
        [(MIX_WIDTH, BF16, True), (IDX_DIM, F32, True), (2 * IDX_DIM, BF16, True),
         (IDX_HEADS, F32, True), (N_HEADS, F32, True)],
        tm, MISC_N, "proj_misc")
    return q_outs, kv, kvb, sz, sg, misc


def _prompt_layer(x3, scale3, shift3, gate3, tabs, weights, norm_w_l, wb_l, wo_l, fw, *, tm, tm_out, tq, tk, final):
    b, t, _ = x3.shape
    (qa, qb, qc, qaf), kv, kvb, sz, sg, (qi, ki, ki2, wi, lf) = _project(
        x3, scale3, shift3, tabs, weights, norm_w_l, tm)
    s3 = lambda a: a.reshape(b, t, a.shape[-1])
    kv3, kvb3, lf3 = s3(kv), s3(kvb), s3(lf)
    km = _kmean_prompt(kv3)
    km = jnp.pad(km, ((0, 0), (0, KMEAN_ROWS - km.shape[1]), (0, 0)))
    cum = _cumsum(jnp.swapaxes(lf3, 1, 2))
    o_a = _moba_attn(s3(qaf), s3(qa), km, kvb3, tq)
    o_b = _fox_attn(s3(qb), cum, kvb3, tq, tk)
    o_c = _dsa_attn(s3(qi), s3(wi), s3(ki2), s3(qc), kvb3, tq, tk)
    x_new = _out_proj(o_a, o_b, o_c, sz, sg, x3, gate3, wb_l, wo_l, fw, tm_out, final)
    return x_new, kv3, lf3, s3(ki)


def _sample_layer(x3, scale3, shift3, gate3, tabs, weights, norm_w_l, wb_l, wo_l, fw, caches, page_table, layer,
                  *, nseq, final):
    _, m, _ = x3.shape
    t = m // nseq
    cache_t, cache_kxt, cache_lft = caches
    (qa, qb, qc, qaf), kv, kvb, sz, sg, (qi, ki, ki2, wi, lf) = _project(
        x3, scale3, shift3, tabs, weights, norm_w_l, m)
    s3 = lambda a: a.reshape(nseq, t, a.shape[-1])
    pad_slots = lambda a: jnp.pad(a, ((0, 0), (0, PAGE_SIZE - t), (0, 0)))
    lp = (page_table.shape[1] + PAGES_PER_STEP) * PAGE_SIZE
    bias = _sample_index(cache_kxt, page_table, layer, s3(qi), s3(wi), pad_slots(s3(ki2)), lp)
    nlf = jnp.pad(jnp.swapaxes(s3(lf), 1, 2), ((0, 0), (0, 0), (0, PAGE_SIZE - t)))
    o_a, o_b, o_c = _sample_attn(cache_t, cache_lft, page_table, layer, bias, s3(qa), s3(qb), s3(qc), s3(qaf),
                                 pad_slots(s3(kvb)), nlf)
    x_new = _out_proj(o_a, o_b, o_c, sz, sg, x3, gate3, wb_l, wo_l, fw, m, final)
    return x_new, s3(kv), s3(lf), s3(ki)


def kernel(x_prompt, x_sample, cache_kv, cache_logf, cache_kidx, page_table, c_prompt, c_sample,
           norm_w, w_ada, b_ada, w_in, b_f, w_branch, w_out, final_norm_w):
    depth = norm_w.shape[0]
    bp, tp, d = x_prompt.shape
    bs, ts, _ = x_sample.shape
    n_pool = cache_kv.shape[1]
    n_pages = page_table.shape[1]
    past_len = n_pages * PAGE_SIZE
    assert ts == SUBLANES and n_pages % PAGES_PER_STEP == 0 and PAGES_PER_STEP % (MOBA_BLOCK // PAGE_SIZE) == 0
    assert tp % MOBA_BLOCK == 0 and tp % min(512, tp) == 0

    nc = bp + bs
    rc = -(-nc // SUBLANES) * SUBLANES
    c_all = jnp.concatenate([c_prompt, c_sample, jnp.zeros((rc - nc, d), F32)], axis=0)
    mod = _modulation(c_all, w_ada, b_ada)

    tabs_p = _rope_tables(jnp.arange(tp, dtype=jnp.int32))
    tabs_s = _rope_tables(jnp.tile(past_len + jnp.arange(ts, dtype=jnp.int32), bs))

    cache_t = jnp.transpose(cache_kv, (0, 1, 3, 4, 5, 6, 2)).reshape(depth, n_pool, N_MIXERS * MIXER_COLS, PAGE_SIZE)
    cache_kxt = jnp.swapaxes(cache_kidx, 2, 3)
    cache_lft = jnp.swapaxes(cache_logf, 2, 3)

    ms = bs * ts
    xp = x_prompt
    xs = x_sample.reshape(1, ms, d)
    new_p, new_s = [], []
    for l in range(depth):
        weights = _split_weights(w_in[l], b_f[l])
        wb_l = w_branch[l].astype(BF16)
        wo_l = w_out[l].astype(BF16)
        final = l == depth - 1
        shift, scale, gate = (mod[l, :, i * d:(i + 1) * d] for i in range(3))
        p3 = lambda a: a[:bp].reshape(bp, 1, d)
        s3 = lambda a: jnp.repeat(a[bp:nc], ts, axis=0).reshape(1, ms, d)

        xp, kv_p, lf_p, ki_p = _prompt_layer(
            xp, p3(scale), p3(shift), p3(gate), tabs_p, weights, norm_w[l], wb_l, wo_l, final_norm_w,
            tm=min(512, tp), tm_out=min(256, tp), tq=min(128, tp), tk=MOBA_BLOCK, final=final)
        xs, kv_s, lf_s, ki_s = _sample_layer(
            xs, s3(scale), s3(shift), s3(gate), tabs_s, weights, norm_w[l], wb_l, wo_l, final_norm_w,
            (cache_t, cache_kxt, cache_lft), page_table, l, nseq=bs, final=final)
        new_p.append((kv_p, lf_p, ki_p))
        new_s.append((kv_s, lf_s, ki_s))

    kv_shape = (N_MIXERS, 2, N_KV_HEADS, HEAD_DIM)
    stack = lambda items, i: jnp.stack([n[i] for n in items])
    kv_prompt = stack(new_p, 0).reshape(depth, bp, tp, *kv_shape)
    kv_sample = stack(new_s, 0).reshape(depth, bs, ts, *kv_shape)
    return (xp, xs.reshape(bs, ts, d), kv_prompt, stack(new_p, 1), stack(new_p, 2),
            kv_sample, stack(new_s, 1), stack(new_s, 2))
```

```python
import functools

import numpy as np
import jax
import jax.numpy as jnp
from jax import lax
from jax.experimental import pallas as pl
from jax.experimental.pallas import tpu as pltpu

F32 = jnp.float32
BF16 = jnp.bfloat16
HIGHEST = lax.Precision.HIGHEST

N_MIXERS = 3
N_HEADS = 8
N_KV_HEADS = 4
HEAD_DIM = 64
MIX_WIDTH = N_HEADS * HEAD_DIM
KV_WIDTH = N_KV_HEADS * HEAD_DIM
MIXER_COLS = 2 * KV_WIDTH
ROT_DIM = HEAD_DIM // 4
ROPE_THETA = 500000.0
ATTN_SCALE = HEAD_DIM ** -0.5
LOG2E = 1.4426950408889634
MOBA_BLOCK = 256
MOBA_TOPK = 3
IDX_HEADS = 8
IDX_DIM = 64
IDX_SCALE = (IDX_DIM * IDX_HEADS) ** -0.5
DSA_TOPK = 256
NORM_EPS = 1e-6
PAGE_SIZE = 128

LANES = 128
SUBLANES = 8
KMEAN_ROWS = 128
NEG = -1e30
F32_LOWEST = -3.0e38
VMEM_LIMIT = 56 * 1024 * 1024
PAGES_PER_STEP = 8
DSA_SWEEP = 1024

_NT = (((1,), (1,)), ((), ()))


def _cp(*sem):
    return pltpu.CompilerParams(dimension_semantics=sem, vmem_limit_bytes=VMEM_LIMIT)


def _sigmoid(x):
    return 1.0 / (1.0 + jnp.exp(-x))


def _mod_kernel(c_ref, w_ref, b_ref, o_ref):
    c = c_ref[...]
    sc = c * _sigmoid(c)
    o_ref[0] = jnp.dot(sc, w_ref[0], precision=HIGHEST, preferred_element_type=F32) + b_ref[0]


def _modulation(c_all, w_ada, b_ada):
    depth, d, d3 = w_ada.shape
    rc = c_all.shape[0]
    nj = d3 // d
    return pl.pallas_call(
        _mod_kernel,
        grid=(depth, nj),
        in_specs=[
            pl.BlockSpec((rc, d), lambda l, j: (0, 0)),
            pl.BlockSpec((1, d, d), lambda l, j: (l, 0, j)),
            pl.BlockSpec((1, 1, d), lambda l, j: (l, 0, j)),
        ],
        out_specs=pl.BlockSpec((1, rc, d), lambda l, j: (l, 0, j)),
        out_shape=jax.ShapeDtypeStruct((depth, rc, d3), F32),
        compiler_params=_cp("arbitrary", "arbitrary"),
        name="adaln_mod",
    )(c_all, w_ada, b_ada.reshape(depth, 1, d3))


def _norm_kernel(x_ref, w_ref, sc_ref, sh_ref, h_ref):
    x = x_ref[0]
    ms = jnp.mean(x * x, axis=-1, keepdims=True)
    y = x * lax.rsqrt(ms + NORM_EPS) * w_ref[...]
    h_ref[0] = (y * (1.0 + sc_ref[0]) + sh_ref[0]).astype(BF16)


def _norm_mod(x3, norm_w, scale3, shift3, tm):
    g, r, d = x3.shape
    rm = scale3.shape[1]
    tmod = 1 if rm == 1 else tm
    mod_map = (lambda b, i: (b, 0, 0)) if rm == 1 else (lambda b, i: (b, i, 0))
    return pl.pallas_call(
        _norm_kernel,
        grid=(g, r // tm),
        in_specs=[
            pl.BlockSpec((1, tm, d), lambda b, i: (b, i, 0)),
            pl.BlockSpec((1, d), lambda b, i: (0, 0)),
            pl.BlockSpec((1, tmod, d), mod_map),
            pl.BlockSpec((1, tmod, d), mod_map),
        ],
        out_specs=pl.BlockSpec((1, tm, d), lambda b, i: (b, i, 0)),
        out_shape=jax.ShapeDtypeStruct((g, r, d), BF16),
        compiler_params=_cp("arbitrary", "arbitrary"),
        name="norm_mod",
    )(x3, norm_w.reshape(1, d), scale3, shift3)


def _rope_chunk(x, a, bm, cm):
    return x * a + pltpu.roll(x, LANES - ROT_DIM // 2, 1) * bm + pltpu.roll(x, ROT_DIM // 2, 1) * cm


def _projq_kernel(h_ref, w_ref, ra_ref, rb_ref, rc_ref, qa_ref, qb_ref, qc_ref, qaf_ref):
    y = jnp.dot(h_ref[...], w_ref[...], preferred_element_type=F32)
    a, bm, cm = ra_ref[...], rb_ref[...], rc_ref[...]
    for m, o_ref in enumerate((qa_ref, qb_ref, qc_ref)):
        for c in range(MIX_WIDTH // LANES):
            x = y[:, m * MIX_WIDTH + c * LANES:m * MIX_WIDTH + (c + 1) * LANES]
            if m != 1:
                x = _rope_chunk(x, a, bm, cm)
            if m == 0:
                qaf_ref[:, c * LANES:(c + 1) * LANES] = x
            o_ref[:, c * LANES:(c + 1) * LANES] = (x * (ATTN_SCALE * LOG2E)).astype(BF16)


def _projkv_kernel(h_ref, w_ref, ra_ref, rb_ref, rc_ref, kv_ref, kvb_ref):
    y = jnp.dot(h_ref[...], w_ref[...], preferred_element_type=F32)
    a, bm, cm = ra_ref[...], rb_ref[...], rc_ref[...]
    per_mixer = MIXER_COLS // LANES
    for c in range(N_MIXERS * per_mixer):
        x = y[:, c * LANES:(c + 1) * LANES]
        mixer, within = divmod(c, per_mixer)
        if mixer != 1 and within < KV_WIDTH // LANES:
            x = _rope_chunk(x, a, bm, cm)
        kv_ref[:, c * LANES:(c + 1) * LANES] = x
        kvb_ref[:, c * LANES:(c + 1) * LANES] = x.astype(BF16)


def _projkv_prompt_kernel(h_ref, w_ref, ra_ref, rb_ref, rc_ref, *refs):
    kvt_ref, kvb_ref, km_ref = refs[-3:]
    y = jnp.dot(h_ref[...], w_ref[...], preferred_element_type=F32)
    a, bm, cm = ra_ref[...], rb_ref[...], rc_ref[...]
    tm = y.shape[0]
    per_mixer = MIXER_COLS // LANES
    moba_k = []
    for c in range(N_MIXERS * per_mixer):
        x = y[:, c * LANES:(c + 1) * LANES]
        mixer, within = divmod(c, per_mixer)
        if mixer != 1 and within < KV_WIDTH // LANES:
            x = _rope_chunk(x, a, bm, cm)
        if mixer == 0 and within < KV_WIDTH // LANES:
            moba_k.append(x)
        kvb_ref[:, c * LANES:(c + 1) * LANES] = x.astype(BF16)
        kvt_ref[0, 0, c * LANES:(c + 1) * LANES, :] = x.T
    means = [jnp.concatenate([jnp.sum(x[g * MOBA_BLOCK:(g + 1) * MOBA_BLOCK], axis=0, keepdims=True)
                              for x in moba_k], axis=1) * (1.0 / MOBA_BLOCK) for g in range(tm // MOBA_BLOCK)]
    km_ref[0] = jnp.concatenate(means + [jnp.zeros((SUBLANES - len(means), KV_WIDTH), F32)], axis=0)


def _proj_kv_prompt(h, w, tabs, kvt_prev, layer, depth, b, t, tm):
    m, d = h.shape
    n = w.shape[1]
    nt = t // tm
    assert tm % MOBA_BLOCK == 0 and tm // MOBA_BLOCK <= SUBLANES
    in_specs = [pl.BlockSpec((tm, d), lambda j, i: (i, 0)), pl.BlockSpec((d, n), lambda j, i: (0, 0))]
    in_specs += [pl.BlockSpec((tm, LANES), lambda j, i: (i % nt, 0)) for _ in tabs]
    args = [h, w, *tabs]
    aliases = {}
    if kvt_prev is not None:
        in_specs.append(pl.BlockSpec(memory_space=pl.ANY))
        aliases = {len(args): 0}
        args.append(kvt_prev)
    return pl.pallas_call(
        _projkv_prompt_kernel,
        grid=(1, m // tm),
        in_specs=in_specs,
        out_specs=[pl.BlockSpec((1, 1, n, tm), lambda j, i: (layer, i // nt, 0, i % nt)),
                   pl.BlockSpec((tm, n), lambda j, i: (i, 0)),
                   pl.BlockSpec((1, SUBLANES, KV_WIDTH), lambda j, i: (i, 0, 0))],
        out_shape=[jax.ShapeDtypeStruct((depth, b, n, t), F32), jax.ShapeDtypeStruct((m, n), BF16),
                   jax.ShapeDtypeStruct((m // tm, SUBLANES, KV_WIDTH), F32)],
        input_output_aliases=aliases,
        compiler_params=_cp("arbitrary", "arbitrary"),
        name="proj_kv_prompt",
    )(*args)


def _projz_kernel(h_ref, w_ref, o_ref):
    y = jnp.dot(h_ref[...], w_ref[...], preferred_element_type=F32)
    o_ref[...] = (y * _sigmoid(y)).astype(o_ref.dtype)


def _projg_kernel(h_ref, w_ref, o_ref):
    y = jnp.dot(h_ref[...], w_ref[...], preferred_element_type=F32)
    o_ref[...] = _sigmoid(y).astype(o_ref.dtype)


MISC_KI = MIX_WIDTH
MISC_WI = MISC_KI + LANES
MISC_FB = MISC_WI + LANES
MISC_N = MISC_FB + LANES


def _projmisc_kernel(h_ref, w_ref, ra_ref, rb_ref, rc_ref, bf_ref,
                     qi_ref, ki_ref, ki2_ref, wi_ref, lf_ref):
    y = jnp.dot(h_ref[...], w_ref[...], preferred_element_type=F32)
    a, bm, cm = ra_ref[...], rb_ref[...], rc_ref[...]
    for c in range(MIX_WIDTH // LANES):
        x = _rope_chunk(y[:, c * LANES:(c + 1) * LANES], a, bm, cm)
        qi_ref[:, c * LANES:(c + 1) * LANES] = x.astype(BF16)
    ki = _rope_chunk(y[:, MISC_KI:MISC_KI + LANES], a, bm, cm)
    ki_ref[...] = ki[:, :IDX_DIM]
    ki2_ref[...] = (ki + pltpu.roll(ki, IDX_DIM, 1)).astype(BF16)
    wi_ref[...] = y[:, MISC_WI:MISC_WI + IDX_HEADS]
    f = y[:, MISC_FB:MISC_FB + LANES] + bf_ref[...]
    logf = jnp.minimum(f, 0.0) - jnp.log(1.0 + jnp.exp(-jnp.abs(f)))
    lf_ref[...] = logf[:, :N_HEADS]


def _proj_call(kernel, h, w, tabs, extra, outs, tm, tn, name):
    m, d = h.shape
    n = w.shape[1]
    nt = tabs[0].shape[0] // tm if tabs else 1
    in_specs = [pl.BlockSpec((tm, d), lambda j, i: (i, 0)),
                pl.BlockSpec((d, tn), lambda j, i: (0, j))]
    in_specs += [pl.BlockSpec((tm, LANES), lambda j, i: (i % nt, 0)) for _ in tabs]
    in_specs += [pl.BlockSpec(e.shape, lambda j, i: (0, 0)) for e in extra]
    out_specs = [pl.BlockSpec((tm, wd if full else tn), (lambda j, i: (i, 0)) if full else (lambda j, i: (i, j)))
                 for (wd, _, full) in outs]
    out_shape = [jax.ShapeDtypeStruct((m, wd), dt) for (wd, dt, _) in outs]
    return pl.pallas_call(
        kernel,
        grid=(n // tn, m // tm),
        in_specs=in_specs,
        out_specs=out_specs,
        out_shape=out_shape,
        compiler_params=_cp("arbitrary", "arbitrary"),
        name=name,
    )(h, w, *tabs, *extra)


CUM_CHUNK = 256


def _cumsum_kernel(x_ref, o_ref):
    n = x_ref.shape[2] // CUM_CHUNK
    r = lax.broadcasted_iota(jnp.int32, (CUM_CHUNK, CUM_CHUNK), 0)
    c = lax.broadcasted_iota(jnp.int32, (CUM_CHUNK, CUM_CHUNK), 1)
    tri = (r <= c).astype(F32)

    def body(i, carry):
        st = pl.multiple_of(i * CUM_CHUNK, CUM_CHUNK)
        x = x_ref[0, :, pl.ds(st, CUM_CHUNK)]
        y = jnp.dot(x, tri, precision=HIGHEST, preferred_element_type=F32) + carry
        o_ref[0, :, pl.ds(st, CUM_CHUNK)] = y * LOG2E
        return y[:, CUM_CHUNK - 1:CUM_CHUNK]

    lax.fori_loop(0, n, body, jnp.zeros((x_ref.shape[1], 1), F32))


def _cumsum(lft):
    b, h, lp = lft.shape
    return pl.pallas_call(
        _cumsum_kernel,
        grid=(b,),
        in_specs=[pl.BlockSpec((1, h, lp), lambda i: (i, 0, 0))],
        out_specs=pl.BlockSpec((1, h, lp), lambda i: (i, 0, 0)),
        out_shape=jax.ShapeDtypeStruct((b, h, lp), F32),
        compiler_params=_cp("arbitrary"),
        name="fox_cumsum",
    )(lft)


def _stack4(blk0, blk1):
    lane = lax.broadcasted_iota(jnp.int32, blk0.shape, 1)
    lo = lane < HEAD_DIM
    return jnp.concatenate([
        jnp.where(lo, blk0, 0.0),
        jnp.where(lo, pltpu.roll(blk0, HEAD_DIM, 1), 0.0),
        jnp.where(lo, 0.0, pltpu.roll(blk1, HEAD_DIM, 1)),
        jnp.where(lo, 0.0, blk1)], axis=0)


def _unstack4(o, tq):
    lane = lax.broadcasted_iota(jnp.int32, (tq, LANES), 1)
    lo = lane < HEAD_DIM
    b0 = jnp.where(lo, o[:tq], pltpu.roll(o[tq:2 * tq], HEAD_DIM, 1))
    b1 = jnp.where(lo, pltpu.roll(o[2 * tq:3 * tq], HEAD_DIM, 1), o[3 * tq:])
    return b0, b1


def _stacked_queries(q_ref):
    return [_stack4(q_ref[0, :, (2 * c) * LANES:(2 * c + 1) * LANES].astype(F32),
                    q_ref[0, :, (2 * c + 1) * LANES:(2 * c + 2) * LANES].astype(F32)) for c in range(2)]


def _flash_scratch(tq, tk):
    rows = 4 * tq
    return [pltpu.VMEM((2, rows, tk), F32), pltpu.VMEM((2, rows, tk), BF16),
            pltpu.VMEM((2, rows, LANES), F32), pltpu.VMEM((2, rows, LANES), F32), pltpu.VMEM((2, rows, LANES), F32),
            pltpu.VMEM((2, rows, 2 * LANES), F32), pltpu.VMEM((2, rows, tk), F32)]


def _flash_pipeline(n_all, qs, k_ref, v_ref, add_bias, diag_mask, scratch, o_ref, tq, tk):
    s_ref, p_ref, al_ref, m_ref, mx_ref, acc_ref, mb_ref = scratch
    m_ref[...] = jnp.full(m_ref.shape, -jnp.inf, F32)
    acc_ref[...] = jnp.zeros(acc_ref.shape, F32)
    s_ref[...] = jnp.full(s_ref.shape, NEG, F32)
    mx_ref[...] = jnp.full(mx_ref.shape, NEG, F32)
    p_ref[...] = jnp.zeros(p_ref.shape, BF16)
    al_ref[...] = jnp.zeros(al_ref.shape, F32)
    if diag_mask is not None:
        mb_ref[0] = jnp.zeros(mb_ref.shape[1:], F32)
        mb_ref[1] = diag_mask
    last = n_all - 1
    ones = jnp.ones((tk, LANES), BF16)

    def step(i):
        jc = jnp.clip(i - 2, 0, last)
        stc = pl.multiple_of(jc * tk, tk)
        for c in range(2):
            vo = jnp.concatenate([v_ref[0, pl.ds(stc, tk), c * LANES:(c + 1) * LANES], ones], axis=1)
            al = al_ref[c]
            acc_ref[c] = (jnp.concatenate([al, al], axis=1) * acc_ref[c]
                          + jnp.dot(p_ref[c], vo, preferred_element_type=F32))
        for c in range(2):
            m_old = m_ref[c]
            m_new = jnp.maximum(m_old, jnp.max(mx_ref[c], axis=-1, keepdims=True))
            al_ref[c] = jnp.exp2(m_old - m_new)
            m_ref[c] = m_new
            p_ref[c] = jnp.exp2(s_ref[c] - jnp.concatenate([m_new] * (tk // LANES), axis=1)).astype(BF16)
        ja = jnp.minimum(i, last)
        sta = pl.multiple_of(ja * tk, tk)
        pen = jnp.where(i <= last, 0.0, NEG).astype(F32)
        for c in range(2):
            s = lax.dot_general(qs[c], k_ref[0, pl.ds(sta, tk), c * LANES:(c + 1) * LANES], _NT,
                                preferred_element_type=F32)
            s = add_bias(c, s, ja, pen)
            if diag_mask is not None:
                s = s + mb_ref[(ja == last).astype(jnp.int32)]
            s_ref[c] = s
            mx_ref[c] = functools.reduce(jnp.maximum, [s[:, k * LANES:(k + 1) * LANES] for k in range(tk // LANES)])

    def step2(ii, carry):
        step(2 * ii)
        step(2 * ii + 1)
        return carry

    lax.fori_loop(0, (n_all + 3) // 2, step2, 0)
    for c in range(2):
        b0, b1 = _unstack4(acc_ref[c, :, :LANES] / acc_ref[c, :, LANES:], tq)
        o_ref[0, :, (2 * c) * LANES:(2 * c + 1) * LANES] = b0.astype(o_ref.dtype)
        o_ref[0, :, (2 * c + 1) * LANES:(2 * c + 2) * LANES] = b1.astype(o_ref.dtype)


def _diag_mask(q_lo, n_all, tq, tk):
    r1 = lax.broadcasted_iota(jnp.int32, (tq, 1), 0) + q_lo
    qpos = jnp.concatenate([r1] * 4, axis=0)
    kpos = (n_all - 1) * tk + lax.broadcasted_iota(jnp.int32, (4 * tq, tk), 1)
    return jnp.where(kpos <= qpos, 0.0, NEG)


def _attn_call(kern, name, ins, in_specs, b, t, tq, scratch):
    return pl.pallas_call(
        kern,
        grid=(b, t // tq),
        in_specs=in_specs,
        out_specs=pl.BlockSpec((1, tq, MIX_WIDTH), lambda i, j: (i, j, 0)),
        out_shape=jax.ShapeDtypeStruct((b, t, MIX_WIDTH), BF16),
        scratch_shapes=scratch,
        compiler_params=_cp("arbitrary", "arbitrary"),
        name=name,
    )(*ins)


_q_spec = lambda tq: pl.BlockSpec((1, tq, MIX_WIDTH), lambda i, j: (i, j, 0))
_kv_spec = lambda lp, col: pl.BlockSpec((1, lp, KV_WIDTH), lambda i, j: (i, 0, col))


def _moba_kernel(qf_ref, q_ref, km_ref, k_ref, v_ref, o_ref, *scratch, tq, topk):
    tk = MOBA_BLOCK
    q_lo = pl.program_id(1) * tq
    own = q_lo // MOBA_BLOCK
    rows = 4 * tq
    lane = lax.broadcasted_iota(jnp.int32, (rows, LANES), 1)
    lane_f = lane.astype(F32)
    qs = [q.astype(BF16) for q in _stacked_queries(q_ref)]
    selb = []
    for c, qf in enumerate(_stacked_queries(qf_ref)):
        gate = lax.dot_general(qf, km_ref[0, :, c * LANES:(c + 1) * LANES], _NT, precision=HIGHEST,
                               preferred_element_type=F32)
        gate = jnp.where(lane < own, gate, -jnp.inf)
        sb = jnp.where(lane == own, 0.0, NEG)
        for _ in range(topk):
            mx = jnp.max(gate, axis=-1, keepdims=True)
            is_max = (gate == mx) & (mx > -jnp.inf)
            first = jnp.min(jnp.where(is_max, lane_f, 1e9), axis=-1, keepdims=True)
            pick = lane_f == first
            sb = jnp.where(pick, 0.0, sb)
            gate = jnp.where(pick, -jnp.inf, gate)
        selb.append(sb)

    def add_bias(c, s, j, pen):
        return s + (jnp.sum(jnp.where(lane == j, selb[c], 0.0), axis=-1, keepdims=True) + pen)

    _flash_pipeline(own + 1, qs, k_ref, v_ref, add_bias, _diag_mask(q_lo, own + 1, tq, tk), scratch, o_ref, tq, tk)


def _moba_attn(qf, q, km, kvb, tq):
    b, t, _ = q.shape
    lp = kvb.shape[1]
    kern = functools.partial(_moba_kernel, tq=tq, topk=min(MOBA_TOPK, lp // MOBA_BLOCK))
    specs = [_q_spec(tq), _q_spec(tq), pl.BlockSpec((1, KMEAN_ROWS, KV_WIDTH), lambda i, j: (i, 0, 0)),
             _kv_spec(lp, 0), _kv_spec(lp, 1)]
    return _attn_call(kern, "moba_attn", (qf, q, km, kvb, kvb), specs, b, t, tq, _flash_scratch(tq, MOBA_BLOCK))


def _fox_kernel(q_ref, cum_ref, k_ref, v_ref, o_ref, *scratch, tq, tk):
    q_lo = pl.program_id(1) * tq
    n_all = (q_lo + tq + tk - 1) // tk
    qs = [q.astype(BF16) for q in _stacked_queries(q_ref)]

    def add_bias(c, s, j, pen):
        st = pl.multiple_of(j * tk, tk)
        parts = [s[hh * tq:(hh + 1) * tq] - (cum_ref[0, 4 * c + hh:4 * c + hh + 1, pl.ds(st, tk)] - pen)
                 for hh in range(4)]
        return jnp.concatenate(parts, axis=0)

    _flash_pipeline(n_all, qs, k_ref, v_ref, add_bias, _diag_mask(q_lo, n_all, tq, tk), scratch, o_ref, tq, tk)


def _fox_attn(q, cum, kvb, tq, tk):
    b, t, _ = q.shape
    lp = kvb.shape[1]
    kern = functools.partial(_fox_kernel, tq=tq, tk=tk)
    specs = [_q_spec(tq), pl.BlockSpec((1, N_HEADS, lp), lambda i, j: (i, 0, 0)), _kv_spec(lp, 2), _kv_spec(lp, 3)]
    return _attn_call(kern, "fox_attn", (q, cum, kvb, kvb), specs, b, t, tq, _flash_scratch(tq, tk))


def _key_to_float(key):
    bits = jnp.where(key < 0, key & jnp.int32(0x7FFFFFFF), ~key)
    return lax.bitcast_convert_type(bits, F32)


def _topk_threshold(count, n_rows, need, few, idx_bits):
    def bit_body(i, key):
        cand = key | lax.shift_left(jnp.int32(1), 31 - i)
        thr_c = _key_to_float(cand)
        return jnp.where(count(lambda x, idx: x >= thr_c) >= need, cand, key)

    key = lax.fori_loop(0, 32, bit_body, jnp.zeros((n_rows, 1), jnp.int32))
    thr = _key_to_float(key)
    cnt_ge = count(lambda x, idx: x >= thr)
    has_ties = jnp.max(jnp.where(few, 0.0, cnt_ge - need)) > 0.0

    def tie_cut(_):
        r = need - count(lambda x, idx: x > thr)

        def jb(i, cut):
            cand = cut | lax.shift_left(jnp.int32(1), idx_bits - 1 - i)
            cnt = count(lambda x, idx: (x == thr) & (idx < cand))
            return jnp.where(cnt < r, cand, cut)

        return lax.fori_loop(0, idx_bits, jb, jnp.zeros((n_rows, 1), jnp.int32))

    cut = lax.cond(has_ties, tie_cut, lambda _: jnp.full((n_rows, 1), 2 ** 30, jnp.int32), 0)
    return jnp.where(few, F32_LOWEST, thr), jnp.where(few, 2 ** 30, cut)


def _dsa_kernel(qi_ref, w_ref, kx_ref, q_ref, k_ref, v_ref, o_ref, isc_ref, *scratch, tq, tk, sweep, topk,
                idx_bits):
    q_lo = pl.program_id(1) * tq
    n_all = (q_lo + tq + tk - 1) // tk
    nl = tk // LANES
    lane = lax.broadcasted_iota(jnp.int32, (tq, LANES), 1)
    lo = lane < HEAD_DIM
    qpos1 = lax.broadcasted_iota(jnp.int32, (tq, 1), 0) + q_lo
    kiota1 = lax.broadcasted_iota(jnp.int32, (tq, tk), 1)

    pieces, wpieces = [], []
    for h in range(IDX_HEADS):
        chunk = qi_ref[0, :, (h // 2) * LANES:(h // 2 + 1) * LANES].astype(F32)
        pieces.append(jnp.where(lo, chunk, 0.0) if h % 2 == 0 else jnp.where(lo, 0.0, chunk))
        wpieces.append(jnp.broadcast_to(w_ref[0, :, h:h + 1], (tq, LANES)))
    qstack = jnp.concatenate(pieces, axis=0).astype(BF16)
    wfull = jnp.concatenate([jnp.concatenate(wpieces, axis=0)] * nl, axis=1)

    def idx_body(j, carry):
        start = pl.multiple_of(j * tk, tk)
        sc = lax.dot_general(qstack, kx_ref[0, pl.ds(start, tk), :], _NT, preferred_element_type=F32)
        contrib = jnp.maximum(sc, 0.0) * wfull
        isc = contrib[:tq]
        for h in range(1, IDX_HEADS):
            isc = isc + contrib[h * tq:(h + 1) * tq]
        isc_ref[:, pl.ds(start, tk)] = jnp.where(j * tk + kiota1 <= qpos1, isc * IDX_SCALE, -jnp.inf)
        return carry

    lax.fori_loop(0, n_all, idx_body, 0)

    tiles_per_sweep = sweep // tk
    n_sweeps = (n_all + tiles_per_sweep - 1) // tiles_per_sweep

    def pad_body(j, carry):
        isc_ref[:, pl.ds(pl.multiple_of(j * tk, tk), tk)] = jnp.full((tq, tk), -jnp.inf, F32)
        return carry

    lax.fori_loop(n_all, n_sweeps * tiles_per_sweep, pad_body, 0)

    def count(pred):
        def body(j, acc):
            start = pl.multiple_of(j * sweep, sweep)
            x = isc_ref[:, pl.ds(start, sweep)]
            for cc in range(sweep // LANES):
                idx = j * sweep + cc * LANES + lane
                acc = acc + jnp.where(pred(x[:, cc * LANES:(cc + 1) * LANES], idx), 1.0, 0.0)
            return acc
        acc = lax.fori_loop(0, n_sweeps, body, jnp.zeros((tq, LANES), F32))
        return jnp.sum(acc, axis=-1, keepdims=True)

    few = qpos1 + 1 <= topk
    thr, cut = _topk_threshold(count, tq, jnp.float32(topk), few, idx_bits)

    def bias_body(j, carry):
        start = pl.multiple_of(j * tk, tk)
        x = isc_ref[:, pl.ds(start, tk)]
        keep = (x > thr) | ((x == thr) & (j * tk + kiota1 <= cut))
        isc_ref[:, pl.ds(start, tk)] = jnp.where(keep, 0.0, NEG)
        return carry

    lax.fori_loop(0, n_all, bias_body, 0)

    qs = [q.astype(BF16) for q in _stacked_queries(q_ref)]

    def add_bias(c, s, j, pen):
        bias = isc_ref[:, pl.ds(pl.multiple_of(j * tk, tk), tk)] + pen
        return s + jnp.concatenate([bias] * 4, axis=0)

    _flash_pipeline(n_all, qs, k_ref, v_ref, add_bias, None, scratch, o_ref, tq, tk)


def _dsa_attn(qi, wi, kx2, q, kvb, tq, tk):
    b, t, _ = q.shape
    lp = kvb.shape[1]
    sweep = max(tk, min(DSA_SWEEP, lp))
    assert lp % sweep == 0 and sweep % tk == 0
    kern = functools.partial(_dsa_kernel, tq=tq, tk=tk, sweep=sweep, topk=min(DSA_TOPK, lp // 4),
                             idx_bits=max(1, int(lp).bit_length()))
    specs = [_q_spec(tq), pl.BlockSpec((1, tq, IDX_HEADS), lambda i, j: (i, j, 0)),
             pl.BlockSpec((1, lp, 2 * IDX_DIM), lambda i, j: (i, 0, 0)), _q_spec(tq), _kv_spec(lp, 4), _kv_spec(lp, 5)]
    scratch = [pltpu.VMEM((tq, lp), F32)] + _flash_scratch(tq, tk)
    return _attn_call(kern, "dsa_attn", (qi, wi, kx2, q, kvb, kvb), specs, b, t, tq, scratch)


def _sample_index_kernel(pt_ref, *refs, ppi, n_steps, l_past, t_new):
    del pt_ref
    kx_refs = refs[:ppi]
    qi_ref, w_ref, nkx_ref, o_ref, isc_ref = refs[ppi:]
    j = pl.program_id(1)
    lp = isc_ref.shape[1]
    lane = lax.broadcasted_iota(jnp.int32, (t_new, LANES), 1)
    lo = lane < HEAD_DIM
    pieces, wpieces = [], []
    for h in range(IDX_HEADS):
        chunk = qi_ref[0, :, (h // 2) * LANES:(h // 2 + 1) * LANES].astype(F32)
        pieces.append(jnp.where(lo, chunk, 0.0) if h % 2 == 0 else jnp.where(lo, 0.0, chunk))
        wpieces.append(jnp.broadcast_to(w_ref[0, :, h:h + 1], (t_new, LANES)))
    qstack = jnp.concatenate(pieces, axis=0).astype(BF16)
    wb = jnp.concatenate(wpieces, axis=0)

    def head_sum(sc):
        contrib = jnp.maximum(sc, 0.0) * jnp.concatenate([wb] * (sc.shape[1] // LANES), axis=1)
        isc = contrib[:t_new]
        for h in range(1, IDX_HEADS):
            isc = isc + contrib[h * t_new:(h + 1) * t_new]
        return isc * IDX_SCALE

    kxt = jnp.concatenate([r[0, 0] for r in kx_refs], axis=1)
    rhs = jnp.concatenate([kxt, kxt], axis=0).astype(BF16)
    start = pl.multiple_of(j * (ppi * PAGE_SIZE), ppi * PAGE_SIZE)
    isc_ref[:, pl.ds(start, ppi * PAGE_SIZE)] = head_sum(jnp.dot(qstack, rhs, preferred_element_type=F32))

    @pl.when(j == n_steps - 1)
    def _():
        scn = lax.dot_general(qstack, nkx_ref[0], _NT, preferred_element_type=F32)
        row = lax.broadcasted_iota(jnp.int32, (t_new, LANES), 0)
        isc_ref[:, l_past:l_past + LANES] = jnp.where((lane <= row) & (lane < t_new), head_sum(scn), -jnp.inf)
        if lp > l_past + LANES:
            isc_ref[:, l_past + LANES:] = jnp.full((t_new, lp - l_past - LANES), -jnp.inf, F32)
        o_ref[0] = isc_ref[...]


def _topk_bias_kernel(x_ref, o_ref, *, sweep, topk, l_past, t_new, idx_bits):
    rows, lp = x_ref.shape
    lane = lax.broadcasted_iota(jnp.int32, (rows, LANES), 1)

    def count(pred):
        def body(j, acc):
            x = x_ref[:, pl.ds(pl.multiple_of(j * sweep, sweep), sweep)]
            for cc in range(sweep // LANES):
                idx = j * sweep + cc * LANES + lane
                acc = acc + jnp.where(pred(x[:, cc * LANES:(cc + 1) * LANES], idx), 1.0, 0.0)
            return acc
        acc = lax.fori_loop(0, lp // sweep, body, jnp.zeros((rows, LANES), F32))
        return jnp.sum(acc, axis=-1, keepdims=True)

    qpos1 = (lax.broadcasted_iota(jnp.int32, (rows, 1), 0) & (t_new - 1)) + l_past
    thr, cut = _topk_threshold(count, rows, jnp.float32(topk), qpos1 + 1 <= topk, idx_bits)
    kiota = lax.broadcasted_iota(jnp.int32, (rows, sweep), 1)

    def bias_body(j, carry):
        start = pl.multiple_of(j * sweep, sweep)
        x = x_ref[:, pl.ds(start, sweep)]
        keep = (x > thr) | ((x == thr) & (j * sweep + kiota <= cut))
        o_ref[:, pl.ds(start, sweep)] = jnp.where(keep, 0.0, NEG)
        return carry

    lax.fori_loop(0, lp // sweep, bias_body, 0)


def _topk_bias(isc, l_past, t_new, rows_per_step):
    rows, lp = isc.shape
    sweep = min(DSA_SWEEP, lp)
    assert lp % sweep == 0 and rows % rows_per_step == 0 and rows_per_step % t_new == 0
    kern = functools.partial(_topk_bias_kernel, sweep=sweep, topk=min(DSA_TOPK, (l_past + t_new) // 4),
                             l_past=l_past, t_new=t_new, idx_bits=max(1, int(lp).bit_length()))
    return pl.pallas_call(
        kern,
        grid=(rows // rows_per_step,),
        in_specs=[pl.BlockSpec((rows_per_step, lp), lambda i: (i, 0))],
        out_specs=pl.BlockSpec((rows_per_step, lp), lambda i: (i, 0)),
        out_shape=jax.ShapeDtypeStruct((rows, lp), F32),
        compiler_params=_cp("arbitrary"),
        name="dsa_topk_bias",
    )(isc)


def _sample_index(cache_kxt, page_table, layer, qi, wi, nkx2, lp):
    nseq, n_pages = page_table.shape
    t_new = qi.shape[1]
    ppi = PAGES_PER_STEP
    n_steps = n_pages // ppi
    l_past = n_pages * PAGE_SIZE
    pg = lambda pp: (lambda i, j, pt: (layer, pt[i, j * ppi + pp], 0, 0))
    grid_spec = pltpu.PrefetchScalarGridSpec(
        num_scalar_prefetch=1,
        grid=(nseq, n_steps),
        in_specs=[pl.BlockSpec((1, 1, IDX_DIM, PAGE_SIZE), pg(pp)) for pp in range(ppi)] + [
            pl.BlockSpec((1, t_new, MIX_WIDTH), lambda i, j, pt: (i, 0, 0)),
            pl.BlockSpec((1, t_new, IDX_HEADS), lambda i, j, pt: (i, 0, 0)),
            pl.BlockSpec((1, PAGE_SIZE, 2 * IDX_DIM), lambda i, j, pt: (i, 0, 0)),
        ],
        out_specs=pl.BlockSpec((1, t_new, lp), lambda i, j, pt: (i, 0, 0)),
        scratch_shapes=[pltpu.VMEM((t_new, lp), F32)],
    )
    kern = functools.partial(_sample_index_kernel, ppi=ppi, n_steps=n_steps, l_past=l_past, t_new=t_new)
    return pl.pallas_call(
        kern,
        grid_spec=grid_spec,
        out_shape=jax.ShapeDtypeStruct((nseq, t_new, lp), F32),
        compiler_params=_cp("arbitrary", "arbitrary"),
        name="dsa_index_paged",
    )(page_table, *([cache_kxt] * ppi), qi, wi, nkx2)


def _block_diag_q(q):
    t = q.shape[0]
    lane = lax.broadcasted_iota(jnp.int32, (t, LANES), 1)
    lo = lane < HEAD_DIM
    zero = jnp.zeros((t, LANES), F32)
    out = []
    for h in range(N_HEADS):
        src = q[:, (h // 2) * LANES:(h // 2 + 1) * LANES]
        kvh = h // (N_HEADS // N_KV_HEADS)
        chunk, half = divmod(kvh, 2)
        piece = src if h % 2 == half else pltpu.roll(src, HEAD_DIM, 1)
        piece = jnp.where(lo, piece, 0.0) if half == 0 else jnp.where(lo, 0.0, piece)
        out.append(jnp.concatenate([piece, zero] if chunk == 0 else [zero, piece], axis=1))
    return jnp.concatenate(out, axis=0)


def _token_major(o, t):
    lane = lax.broadcasted_iota(jnp.int32, (t, LANES), 1)
    lo = lane < HEAD_DIM
    out = []
    for co in range(MIX_WIDTH // LANES):
        chunk, half = divmod(co, 2)
        x = o[(2 * co) * t:(2 * co + 1) * t, chunk * LANES:(chunk + 1) * LANES]
        y = o[(2 * co + 1) * t:(2 * co + 2) * t, chunk * LANES:(chunk + 1) * LANES]
        if half == 0:
            out.append(jnp.where(lo, x, pltpu.roll(y, HEAD_DIM, 1)))
        else:
            out.append(jnp.where(lo, pltpu.roll(x, HEAD_DIM, 1), y))
    return jnp.concatenate(out, axis=1)


def _lane_cumsum(x):
    lane = lax.broadcasted_iota(jnp.int32, x.shape, 1)
    d = 1
    while d < LANES:
        x = x + jnp.where(lane >= d, pltpu.roll(x, d, 1), 0.0)
        d *= 2
    return x


def _rows_from_heads(c, t):
    return jnp.concatenate([jnp.broadcast_to(c[h:h + 1], (t, c.shape[1])) for h in range(N_HEADS)], axis=0)


def _online_update(s, pv_fn, m_ref, l_ref, acc_ref):
    m_old = m_ref[...]
    m_new = jnp.maximum(m_old, jnp.max(s, axis=-1, keepdims=True))
    alpha = jnp.exp2(m_old - m_new)
    p = jnp.exp2(s - jnp.concatenate([m_new] * (s.shape[1] // LANES), axis=1))
    l_ref[...] = alpha * l_ref[...] + jnp.sum(p, axis=-1, keepdims=True)
    acc_ref[...] = jnp.concatenate([alpha] * (acc_ref.shape[1] // LANES), axis=1) * acc_ref[...] + pv_fn(p.astype(BF16))
    m_ref[...] = m_new


def _sample_attn_kernel(pt_ref, *refs, nb, pps, t_new, topk):
    del pt_ref
    pg_refs, lf_refs = refs[:pps], refs[pps:2 * pps]
    (bias_ref, qa_ref, qb_ref, qc_ref, qaf_ref, nkv_ref, nlf_ref, oa_ref, ob_ref, oc_ref,
     qbd_ref, qbdf_ref, km_ref, ms_ref, ls_ref, accs_ref,
     fm_ref, fl_ref, facc_ref, car_ref, dm_ref, dl_ref, dacc_ref) = refs[2 * pps:]
    j = pl.program_id(1)
    rows = N_HEADS * t_new
    ppb = MOBA_BLOCK // PAGE_SIZE
    bps = pps // ppb
    n_steps = nb // bps
    lane = lax.broadcasted_iota(jnp.int32, (rows, LANES), 1)

    @pl.when(j == 0)
    def _():
        for m, q_ref in enumerate((qa_ref, qb_ref, qc_ref)):
            qbd_ref[m] = _block_diag_q(q_ref[0].astype(F32)).astype(BF16)
        qbdf_ref[...] = _block_diag_q(qaf_ref[0])
        km_ref[...] = jnp.zeros(km_ref.shape, F32)
        ms_ref[...] = jnp.zeros(ms_ref.shape, F32)
        ls_ref[...] = jnp.zeros(ls_ref.shape, F32)
        car_ref[...] = jnp.zeros(car_ref.shape, F32)
        for m_ref, l_ref, acc_ref in ((fm_ref, fl_ref, facc_ref), (dm_ref, dl_ref, dacc_ref)):
            m_ref[...] = jnp.full(m_ref.shape, -jnp.inf, F32)
            l_ref[...] = jnp.zeros(l_ref.shape, F32)
            acc_ref[...] = jnp.zeros(acc_ref.shape, F32)

    @pl.when(j < n_steps)
    def _():
        def kt(m, off):
            lo_r = m * MIXER_COLS + off
            return jnp.concatenate([r[0, 0, lo_r:lo_r + KV_WIDTH, :] for r in pg_refs], axis=1)

        def pv_nt(m):
            vt = kt(m, KV_WIDTH).astype(BF16)
            return lambda p: lax.dot_general(p, vt, _NT, preferred_element_type=F32)

        k0 = kt(0, 0)
        v0 = kt(0, KV_WIDTH).astype(BF16)
        s = jnp.dot(qbd_ref[0], k0.astype(BF16), preferred_element_type=F32)
        blk_lane = lax.broadcasted_iota(jnp.int32, km_ref.shape, 1)
        km, ms, ls = km_ref[...], ms_ref[...], ls_ref[...]
        for b in range(bps):
            n = j * bps + b
            cols = slice(b * MOBA_BLOCK, (b + 1) * MOBA_BLOCK)
            km = jnp.where(blk_lane == n, jnp.sum(k0[:, cols], axis=1, keepdims=True) * (1.0 / MOBA_BLOCK), km)
            m_n = jnp.max(s[:, cols], axis=-1, keepdims=True)
            p = jnp.exp2(s[:, cols] - m_n)
            ms = jnp.where(lane == n, m_n, ms)
            ls = jnp.where(lane == n, jnp.sum(p, axis=-1, keepdims=True), ls)
            accs_ref[n] = lax.dot_general(p.astype(BF16), v0[:, cols], _NT, preferred_element_type=F32)
        km_ref[...], ms_ref[...], ls_ref[...] = km, ms, ls

        carry, cums = car_ref[...][:, :1], []
        for r in lf_refs:
            cums.append(_lane_cumsum(r[0, 0]) + carry)
            carry = cums[-1][:, LANES - 1:LANES]
        car_ref[...] = jnp.broadcast_to(carry, car_ref.shape)
        s = jnp.dot(qbd_ref[1], kt(1, 0).astype(BF16), preferred_element_type=F32)
        s = s - _rows_from_heads(jnp.concatenate(cums, axis=1) * LOG2E, t_new)
        _online_update(s, pv_nt(1), fm_ref, fl_ref, facc_ref)

        s = jnp.dot(qbd_ref[2], kt(2, 0).astype(BF16), preferred_element_type=F32)
        s = s + jnp.concatenate([bias_ref[0]] * N_HEADS, axis=0)
        _online_update(s, pv_nt(2), dm_ref, dl_ref, dacc_ref)

    @pl.when(j == n_steps)
    def _():
        nk = lambda m: nkv_ref[0, :, m * MIXER_COLS:m * MIXER_COLS + KV_WIDTH]
        nv = lambda m: nkv_ref[0, :, m * MIXER_COLS + KV_WIDTH:(m + 1) * MIXER_COLS]
        pv = lambda m: (lambda p: jnp.dot(p, nv(m), preferred_element_type=F32))
        row_t = lax.broadcasted_iota(jnp.int32, (rows, LANES), 0) & (t_new - 1)
        causal = (lane <= row_t) & (lane < t_new)

        s = lax.dot_general(qbd_ref[0], nk(0), _NT, preferred_element_type=F32)
        s = jnp.where(causal, s, NEG)
        m_o = jnp.max(s, axis=-1, keepdims=True)
        p = jnp.exp2(s - m_o)
        l_o = jnp.sum(p, axis=-1, keepdims=True)
        acc_o = pv(0)(p.astype(BF16))
        gate = jnp.dot(qbdf_ref[...], km_ref[...], precision=HIGHEST, preferred_element_type=F32)
        gate = jnp.where(lane < nb, gate, -jnp.inf)
        lane_f = lane.astype(F32)
        sel = lane < 0
        for _ in range(topk):
            mx = jnp.max(gate, axis=-1, keepdims=True)
            is_max = (gate == mx) & (mx > -jnp.inf)
            first = jnp.min(jnp.where(is_max, lane_f, 1e9), axis=-1, keepdims=True)
            pick = lane_f == first
            sel = sel | pick
            gate = jnp.where(pick, -jnp.inf, gate)
        ms = ms_ref[...]
        m_all = jnp.maximum(m_o, jnp.max(jnp.where(sel, ms, -jnp.inf), axis=-1, keepdims=True))
        w = jnp.where(sel, jnp.exp2(ms - m_all), 0.0)
        w_o = jnp.exp2(m_o - m_all)
        l_all = jnp.sum(w * ls_ref[...], axis=-1, keepdims=True) + w_o * l_o

        def merge(n, o):
            wn = jnp.sum(jnp.where(lane == n, w, 0.0), axis=-1, keepdims=True)
            return o + wn * accs_ref[n]

        o_a = lax.fori_loop(0, nb, merge, w_o * acc_o) / l_all
        oa_ref[0] = _token_major(o_a, t_new)

        cn = _lane_cumsum(nlf_ref[0]) + car_ref[...]
        s = lax.dot_general(qbd_ref[1], nk(1), _NT, preferred_element_type=F32) - _rows_from_heads(cn * LOG2E, t_new)
        _online_update(jnp.where(causal, s, NEG), pv(1), fm_ref, fl_ref, facc_ref)
        wide = lambda l_ref: jnp.concatenate([l_ref[...]] * (KV_WIDTH // LANES), axis=1)
        ob_ref[0] = _token_major(facc_ref[...] / wide(fl_ref), t_new)

        s = lax.dot_general(qbd_ref[2], nk(2), _NT, preferred_element_type=F32)
        s = s + jnp.concatenate([bias_ref[0, :, :LANES]] * N_HEADS, axis=0)
        _online_update(s, pv(2), dm_ref, dl_ref, dacc_ref)
        oc_ref[0] = _token_major(dacc_ref[...] / wide(dl_ref), t_new)


def _sample_attn(cache_t, cache_lft, page_table, layer, bias, qa, qb, qc, qaf, nkv, nlf):
    nseq, n_pages = page_table.shape
    t_new = qa.shape[1]
    nb = n_pages // (MOBA_BLOCK // PAGE_SIZE)
    pps = PAGES_PER_STEP
    rows = N_HEADS * t_new
    width = cache_t.shape[2]
    last = n_pages - 1
    page = lambda k: (lambda i, j, pt: (layer, pt[i, jnp.minimum(pps * j + k, last)], 0, 0))
    seq_blk = lambda shape: pl.BlockSpec((1,) + shape, lambda i, j, pt: (i, 0, 0))
    grid_spec = pltpu.PrefetchScalarGridSpec(
        num_scalar_prefetch=1,
        grid=(nseq, n_pages // pps + 1),
        in_specs=[pl.BlockSpec((1, 1, width, PAGE_SIZE), page(k)) for k in range(pps)] + [
            pl.BlockSpec((1, 1, N_HEADS, PAGE_SIZE), page(k)) for k in range(pps)] + [
            pl.BlockSpec((1, t_new, pps * PAGE_SIZE), lambda i, j, pt: (i, 0, j)),
            seq_blk((t_new, MIX_WIDTH)), seq_blk((t_new, MIX_WIDTH)), seq_blk((t_new, MIX_WIDTH)),
            seq_blk((t_new, MIX_WIDTH)),
            seq_blk((PAGE_SIZE, width)), seq_blk((N_HEADS, PAGE_SIZE)),
        ],
        out_specs=[seq_blk((t_new, MIX_WIDTH))] * 3,
        scratch_shapes=[
            pltpu.VMEM((N_MIXERS, rows, KV_WIDTH), BF16), pltpu.VMEM((rows, KV_WIDTH), F32),
            pltpu.VMEM((KV_WIDTH, KMEAN_ROWS), F32), pltpu.VMEM((rows, LANES), F32), pltpu.VMEM((rows, LANES), F32),
            pltpu.VMEM((nb, rows, KV_WIDTH), F32),
            pltpu.VMEM((rows, LANES), F32), pltpu.VMEM((rows, LANES), F32), pltpu.VMEM((rows, KV_WIDTH), F32),
            pltpu.VMEM((N_HEADS, LANES), F32),
            pltpu.VMEM((rows, LANES), F32), pltpu.VMEM((rows, LANES), F32), pltpu.VMEM((rows, KV_WIDTH), F32),
        ],
    )
    kern = functools.partial(_sample_attn_kernel, nb=nb, pps=pps, t_new=t_new, topk=min(MOBA_TOPK, nb + 1))
    out = jax.ShapeDtypeStruct((nseq, t_new, MIX_WIDTH), F32)
    return pl.pallas_call(
        kern,
        grid_spec=grid_spec,
        out_shape=[out, out, out],
        compiler_params=_cp("arbitrary", "arbitrary"),
        name="sample_attn",
    )(page_table, *([cache_t] * pps), *([cache_lft] * pps), bias, qa, qb, qc, qaf, nkv, nlf)


def _out_kernel(oa_ref, ob_ref, oc_ref, sz_ref, sg_ref, x_ref, gate_ref, wb_ref, wo_ref, fw_ref, y_ref,
                *, final):
    d = x_ref.shape[2]
    merged = None
    for i, o_ref in enumerate((oa_ref, ob_ref, oc_ref)):
        t = (o_ref[0].astype(F32) * sz_ref[0, :, i * MIX_WIDTH:(i + 1) * MIX_WIDTH].astype(F32)).astype(BF16)
        br = sg_ref[0, :, i * d:(i + 1) * d].astype(F32) * jnp.dot(t, wb_ref[i], preferred_element_type=F32)
        merged = br if merged is None else merged + br
    y = x_ref[0] + gate_ref[0] * jnp.dot(merged.astype(BF16), wo_ref[...], preferred_element_type=F32)
    if final:
        ms = jnp.mean(y * y, axis=-1, keepdims=True)
        y = y * lax.rsqrt(ms + NORM_EPS) * fw_ref[...]
    y_ref[0] = y


def _out_proj(oa, ob, oc, sz, sg, x3, gate3, wb, wo, fw, tm, final):
    g, r, d = x3.shape
    rm = gate3.shape[1]
    tmod = 1 if rm == 1 else tm
    mod_map = (lambda b, i: (b, 0, 0)) if rm == 1 else (lambda b, i: (b, i, 0))
    row = lambda w: pl.BlockSpec((1, tm, w), lambda b, i: (b, i, 0))
    as3 = lambda a: a.reshape(g, r, a.shape[-1])
    return pl.pallas_call(
        functools.partial(_out_kernel, final=final),
        grid=(g, r // tm),
        in_specs=[
            row(MIX_WIDTH), row(MIX_WIDTH), row(MIX_WIDTH), row(N_MIXERS * MIX_WIDTH), row(N_MIXERS * d), row(d),
            pl.BlockSpec((1, tmod, d), mod_map),
            pl.BlockSpec(wb.shape, lambda b, i: (0, 0, 0)),
            pl.BlockSpec(wo.shape, lambda b, i: (0, 0)),
            pl.BlockSpec((1, d), lambda b, i: (0, 0)),
        ],
        out_specs=row(d),
        out_shape=jax.ShapeDtypeStruct((g, r, d), F32),
        compiler_params=_cp("arbitrary", "arbitrary"),
        name="out_proj",
    )(as3(oa), as3(ob), as3(oc), as3(sz), as3(sg), x3, gate3, wb, wo, fw.reshape(1, d))


def _rope_tables(pos):
    half = ROT_DIM // 2
    expo = jnp.arange(0, ROT_DIM, 2, dtype=F32) / ROT_DIM
    inv_freq = jnp.power(jnp.float32(ROPE_THETA), -expo)
    ang = pos.astype(F32)[:, None] * inv_freq[None, :]
    cos, sin = jnp.cos(ang), jnp.sin(ang)
    n = pos.shape[0]
    rest = HEAD_DIM - ROT_DIM
    a = jnp.concatenate([cos, cos, jnp.ones((n, rest), F32)], axis=1)
    bm = jnp.concatenate([-sin, jnp.zeros((n, half + rest), F32)], axis=1)
    cm = jnp.concatenate([jnp.zeros((n, half), F32), sin, jnp.zeros((n, rest), F32)], axis=1)
    rep = LANES // HEAD_DIM
    return tuple(jnp.tile(t, (1, rep)) for t in (a, bm, cm))


def _split_weights(w_in_l, b_f_l):
    o = np.cumsum([0, MIX_WIDTH, KV_WIDTH, KV_WIDTH, MIX_WIDTH,
                   MIX_WIDTH, KV_WIDTH, KV_WIDTH, N_HEADS, MIX_WIDTH,
                   MIX_WIDTH, KV_WIDTH, KV_WIDTH, IDX_HEADS * IDX_DIM, IDX_DIM, IDX_HEADS, MIX_WIDTH]).tolist()
    col = lambda i: w_in_l[:, o[i]:o[i + 1]]
    qa, ka, va, za, qb, kb, vb, fb, zb, qc, kc, vc, qi, ki, wi, zc = (col(i) for i in range(16))
    g = w_in_l[:, o[16]:]
    d = w_in_l.shape[0]
    zpad = lambda w: jnp.zeros((d, w), w_in_l.dtype)
    w_q = jnp.concatenate([qa, qb, qc], axis=1)
    w_kv = jnp.concatenate([ka, va, kb, vb, kc, vc], axis=1)
    w_z = jnp.concatenate([za, zb, zc], axis=1)
    w_misc = jnp.concatenate([qi, ki, zpad(LANES - IDX_DIM), wi, zpad(LANES - IDX_HEADS),
                              fb, zpad(LANES - N_HEADS)], axis=1)
    bf_row = jnp.concatenate([b_f_l, jnp.zeros((LANES - N_HEADS,), b_f_l.dtype)]).reshape(1, LANES)
    cast = lambda w: w.astype(BF16)
    return cast(w_q), cast(w_kv), cast(w_z), cast(g), cast(w_misc), bf_row.astype(F32)


def _project(x3, scale3, shift3, tabs, weights, norm_w_l, tm, kv_out=None):
    g, r, d = x3.shape
    w_q, w_kv, w_z, w_g, w_misc, bf_row = weights
    h = _norm_mod(x3, norm_w_l, scale3, shift3, tm).reshape(g * r, d)
    q_outs = _proj_call(
        _projq_kernel, h, w_q, tabs, (), [(MIX_WIDTH, BF16, True)] * 3 + [(MIX_WIDTH, F32, True)],
        tm, w_q.shape[1], "proj_q")
    if kv_out is None:
        kv, kvb = _proj_call(
            _projkv_kernel, h, w_kv, tabs, (), [(w_kv.shape[1], F32, True), (w_kv.shape[1], BF16, True)],
            tm, w_kv.shape[1], "proj_kv")
    else:
        kvt_prev, layer, depth = kv_out
        kvt, kvb, km = _proj_kv_prompt(h, w_kv, tabs, kvt_prev, layer, depth, g, r, tm)
        kv = (kvt, km)
    (sz,) = _proj_call(_projz_kernel, h, w_z, (), (), [(w_z.shape[1], BF16, True)], tm, w_z.shape[1], "proj_z")
    (sg,) = _proj_call(_projg_kernel, h, w_g, (), (), [(w_g.shape[1], BF16, False)], tm, d, "proj_g")
    misc = _proj_call(
        _projmisc_kernel, h, w_misc, tabs, (bf_row,),
        [(MIX_WIDTH, BF16, True), (IDX_DIM, F32, True), (2 * IDX_DIM, BF16, True),
         (IDX_HEADS, F32, True), (N_HEADS, F32, True)],
        tm, MISC_N, "proj_misc")
    return q_outs, kv, kvb, sz, sg, misc


def _prompt_layer(x3, scale3, shift3, gate3, tabs, weights, norm_w_l, wb_l, wo_l, fw, kvt_prev, layer, depth,
                  *, tm, tm_out, tq, tk, final):
    b, t, _ = x3.shape
    (qa, qb, qc, qaf), (kvt, km8), kvb, sz, sg, (qi, ki, ki2, wi, lf) = _project(
        x3, scale3, shift3, tabs, weights, norm_w_l, tm, kv_out=(kvt_prev, layer, depth))
    s3 = lambda a: a.reshape(b, t, a.shape[-1])
    kvb3, lf3 = s3(kvb), s3(lf)
    km = km8[:, :tm // MOBA_BLOCK].reshape(b, t // MOBA_BLOCK, KV_WIDTH)
    km = jnp.pad(km, ((0, 0), (0, KMEAN_ROWS - km.shape[1]), (0, 0)))
    cum = _cumsum(jnp.swapaxes(lf3, 1, 2))
    o_a = _moba_attn(s3(qaf), s3(qa), km, kvb3, tq)
    o_b = _fox_attn(s3(qb), cum, kvb3, tq, tk)
    o_c = _dsa_attn(s3(qi), s3(wi), s3(ki2), s3(qc), kvb3, tq, tk)
    x_new = _out_proj(o_a, o_b, o_c, sz, sg, x3, gate3, wb_l, wo_l, fw, tm_out, final)
    return x_new, kvt, lf3, s3(ki)


def _sample_layer(x3, scale3, shift3, gate3, tabs, weights, norm_w_l, wb_l, wo_l, fw, caches, page_table, layer,
                  *, nseq, final):
    _, m, _ = x3.shape
    t = m // nseq
    cache_t, cache_kxt, cache_lft = caches
    (qa, qb, qc, qaf), kv, kvb, sz, sg, (qi, ki, ki2, wi, lf) = _project(
        x3, scale3, shift3, tabs, weights, norm_w_l, m)
    s3 = lambda a: a.reshape(nseq, t, a.shape[-1])
    pad_slots = lambda a: jnp.pad(a, ((0, 0), (0, PAGE_SIZE - t), (0, 0)))
    lp = (page_table.shape[1] + PAGES_PER_STEP) * PAGE_SIZE
    isc = _sample_index(cache_kxt, page_table, layer, s3(qi), s3(wi), pad_slots(s3(ki2)), lp)
    bias = _topk_bias(isc.reshape(m, lp), page_table.shape[1] * PAGE_SIZE, t, min(m, 64)).reshape(nseq, t, lp)
    nlf = jnp.pad(jnp.swapaxes(s3(lf), 1, 2), ((0, 0), (0, 0), (0, PAGE_SIZE - t)))
    o_a, o_b, o_c = _sample_attn(cache_t, cache_lft, page_table, layer, bias, s3(qa), s3(qb), s3(qc), s3(qaf),
                                 pad_slots(s3(kvb)), nlf)
    x_new = _out_proj(o_a, o_b, o_c, sz, sg, x3, gate3, wb_l, wo_l, fw, m, final)
    return x_new, s3(kv), s3(lf), s3(ki)


def kernel(x_prompt, x_sample, cache_kv, cache_logf, cache_kidx, page_table, c_prompt, c_sample,
           norm_w, w_ada, b_ada, w_in, b_f, w_branch, w_out, final_norm_w):
    depth = norm_w.shape[0]
    bp, tp, d = x_prompt.shape
    bs, ts, _ = x_sample.shape
    n_pool = cache_kv.shape[1]
    n_pages = page_table.shape[1]
    past_len = n_pages * PAGE_SIZE
    assert ts == SUBLANES and n_pages % PAGES_PER_STEP == 0 and PAGES_PER_STEP % (MOBA_BLOCK // PAGE_SIZE) == 0
    assert tp % MOBA_BLOCK == 0 and tp % min(512, tp) == 0

    nc = bp + bs
    rc = -(-nc // SUBLANES) * SUBLANES
    c_all = jnp.concatenate([c_prompt, c_sample, jnp.zeros((rc - nc, d), F32)], axis=0)
    mod = _modulation(c_all, w_ada, b_ada)

    tabs_p = _rope_tables(jnp.arange(tp, dtype=jnp.int32))
    tabs_s = _rope_tables(jnp.tile(past_len + jnp.arange(ts, dtype=jnp.int32), bs))

    cache_t = jnp.transpose(cache_kv, (0, 1, 3, 4, 5, 6, 2)).reshape(depth, n_pool, N_MIXERS * MIXER_COLS, PAGE_SIZE)
    cache_kxt = jnp.swapaxes(cache_kidx, 2, 3)
    cache_lft = jnp.swapaxes(cache_logf, 2, 3)

    ms = bs * ts
    xp = x_prompt
    xs = x_sample.reshape(1, ms, d)
    new_p, new_s = [], []
    kvt = None
    for l in range(depth):
        weights = _split_weights(w_in[l], b_f[l])
        wb_l = w_branch[l].astype(BF16)
        wo_l = w_out[l].astype(BF16)
        final = l == depth - 1
        shift, scale, gate = (mod[l, :, i * d:(i + 1) * d] for i in range(3))
        p3 = lambda a: a[:bp].reshape(bp, 1, d)
        s3 = lambda a: jnp.repeat(a[bp:nc], ts, axis=0).reshape(1, ms, d)

        xp, kvt, lf_p, ki_p = _prompt_layer(
            xp, p3(scale), p3(shift), p3(gate), tabs_p, weights, norm_w[l], wb_l, wo_l, final_norm_w, kvt, l, depth,
            tm=min(512, tp), tm_out=min(512, tp), tq=min(128, tp), tk=MOBA_BLOCK, final=final)
        xs, kv_s, lf_s, ki_s = _sample_layer(
            xs, s3(scale), s3(shift), s3(gate), tabs_s, weights, norm_w[l], wb_l, wo_l, final_norm_w,
            (cache_t, cache_kxt, cache_lft), page_table, l, nseq=bs, final=final)
        new_p.append((None, lf_p, ki_p))
        new_s.append((kv_s, lf_s, ki_s))

    kv_shape = (N_MIXERS, 2, N_KV_HEADS, HEAD_DIM)
    stack = lambda items, i: jnp.stack([n[i] for n in items])
    kv_prompt = jnp.transpose(kvt.reshape(depth, bp, *kv_shape, tp), (0, 1, 6, 2, 3, 4, 5))
    kv_sample = stack(new_s, 0).reshape(depth, bs, ts, *kv_shape)
    return (xp, xs.reshape(bs, ts, d), kv_prompt, stack(new_p, 1), stack(new_p, 2),
            kv_sample, stack(new_s, 1), stack(new_s, 2))
```

```python
import functools

import numpy as np
import jax
import jax.numpy as jnp
from jax import lax
from jax.experimental import pallas as pl
from jax.experimental.pallas import tpu as pltpu

F32 = jnp.float32
BF16 = jnp.bfloat16
HIGHEST = lax.Precision.HIGHEST

N_MIXERS = 3
N_HEADS = 8
N_KV_HEADS = 4
HEAD_DIM = 64
MIX_WIDTH = N_HEADS * HEAD_DIM
KV_WIDTH = N_KV_HEADS * HEAD_DIM
MIXER_COLS = 2 * KV_WIDTH
ROT_DIM = HEAD_DIM // 4
ROPE_THETA = 500000.0
ATTN_SCALE = HEAD_DIM ** -0.5
LOG2E = 1.4426950408889634
MOBA_BLOCK = 256
MOBA_TOPK = 3
IDX_HEADS = 8
IDX_DIM = 64
IDX_SCALE = (IDX_DIM * IDX_HEADS) ** -0.5
DSA_TOPK = 256
NORM_EPS = 1e-6
PAGE_SIZE = 128

LANES = 128
SUBLANES = 8
KMEAN_ROWS = 128
NEG = -1e30
F32_LOWEST = -3.0e38
VMEM_LIMIT = 56 * 1024 * 1024
PAGES_PER_STEP = 8
DSA_SWEEP = 1024

_NT = (((1,), (1,)), ((), ()))


def _cp(*sem):
    return pltpu.CompilerParams(dimension_semantics=sem, vmem_limit_bytes=VMEM_LIMIT)


def _sigmoid(x):
    return 1.0 / (1.0 + jnp.exp(-x))


def _mod_kernel(c_ref, w_ref, b_ref, o_ref):
    c = c_ref[...]
    sc = c * _sigmoid(c)
    o_ref[0] = jnp.dot(sc, w_ref[0], precision=HIGHEST, preferred_element_type=F32) + b_ref[0]


def _modulation(c_all, w_ada, b_ada):
    depth, d, d3 = w_ada.shape
    rc = c_all.shape[0]
    nj = d3 // d
    return pl.pallas_call(
        _mod_kernel,
        grid=(depth, nj),
        in_specs=[
            pl.BlockSpec((rc, d), lambda l, j: (0, 0)),
            pl.BlockSpec((1, d, d), lambda l, j: (l, 0, j)),
            pl.BlockSpec((1, 1, d), lambda l, j: (l, 0, j)),
        ],
        out_specs=pl.BlockSpec((1, rc, d), lambda l, j: (l, 0, j)),
        out_shape=jax.ShapeDtypeStruct((depth, rc, d3), F32),
        compiler_params=_cp("arbitrary", "arbitrary"),
        name="adaln_mod",
    )(c_all, w_ada, b_ada.reshape(depth, 1, d3))


def _norm_kernel(x_ref, w_ref, sc_ref, sh_ref, h_ref):
    x = x_ref[0]
    ms = jnp.mean(x * x, axis=-1, keepdims=True)
    y = x * lax.rsqrt(ms + NORM_EPS) * w_ref[...]
    h_ref[0] = (y * (1.0 + sc_ref[0]) + sh_ref[0]).astype(BF16)


def _norm_mod(x3, norm_w, scale3, shift3, tm):
    g, r, d = x3.shape
    rm = scale3.shape[1]
    tmod = 1 if rm == 1 else tm
    mod_map = (lambda b, i: (b, 0, 0)) if rm == 1 else (lambda b, i: (b, i, 0))
    return pl.pallas_call(
        _norm_kernel,
        grid=(g, r // tm),
        in_specs=[
            pl.BlockSpec((1, tm, d), lambda b, i: (b, i, 0)),
            pl.BlockSpec((1, d), lambda b, i: (0, 0)),
            pl.BlockSpec((1, tmod, d), mod_map),
            pl.BlockSpec((1, tmod, d), mod_map),
        ],
        out_specs=pl.BlockSpec((1, tm, d), lambda b, i: (b, i, 0)),
        out_shape=jax.ShapeDtypeStruct((g, r, d), BF16),
        compiler_params=_cp("arbitrary", "arbitrary"),
        name="norm_mod",
    )(x3, norm_w.reshape(1, d), scale3, shift3)


def _rope_chunk(x, a, bm, cm):
    return x * a + pltpu.roll(x, LANES - ROT_DIM // 2, 1) * bm + pltpu.roll(x, ROT_DIM // 2, 1) * cm


def _projq_kernel(h_ref, w_ref, ra_ref, rb_ref, rc_ref, qa_ref, qb_ref, qc_ref, qaf_ref):
    y = jnp.dot(h_ref[...], w_ref[...], preferred_element_type=F32)
    a, bm, cm = ra_ref[...], rb_ref[...], rc_ref[...]
    for m, o_ref in enumerate((qa_ref, qb_ref, qc_ref)):
        for c in range(MIX_WIDTH // LANES):
            x = y[:, m * MIX_WIDTH + c * LANES:m * MIX_WIDTH + (c + 1) * LANES]
            if m != 1:
                x = _rope_chunk(x, a, bm, cm)
            if m == 0:
                qaf_ref[:, c * LANES:(c + 1) * LANES] = x
            o_ref[:, c * LANES:(c + 1) * LANES] = (x * (ATTN_SCALE * LOG2E)).astype(BF16)


def _projkv_kernel(h_ref, w_ref, ra_ref, rb_ref, rc_ref, kv_ref, kvb_ref):
    y = jnp.dot(h_ref[...], w_ref[...], preferred_element_type=F32)
    a, bm, cm = ra_ref[...], rb_ref[...], rc_ref[...]
    per_mixer = MIXER_COLS // LANES
    for c in range(N_MIXERS * per_mixer):
        x = y[:, c * LANES:(c + 1) * LANES]
        mixer, within = divmod(c, per_mixer)
        if mixer != 1 and within < KV_WIDTH // LANES:
            x = _rope_chunk(x, a, bm, cm)
        kv_ref[:, c * LANES:(c + 1) * LANES] = x
        kvb_ref[:, c * LANES:(c + 1) * LANES] = x.astype(BF16)


def _projkv_prompt_kernel(h_ref, w_ref, ra_ref, rb_ref, rc_ref, *refs):
    kvt_ref, kvb_ref, km_ref = refs[-3:]
    y = jnp.dot(h_ref[...], w_ref[...], preferred_element_type=F32)
    a, bm, cm = ra_ref[...], rb_ref[...], rc_ref[...]
    tm = y.shape[0]
    per_mixer = MIXER_COLS // LANES
    moba_k = []
    for c in range(N_MIXERS * per_mixer):
        x = y[:, c * LANES:(c + 1) * LANES]
        mixer, within = divmod(c, per_mixer)
        if mixer != 1 and within < KV_WIDTH // LANES:
            x = _rope_chunk(x, a, bm, cm)
        if mixer == 0 and within < KV_WIDTH // LANES:
            moba_k.append(x)
        kvb_ref[:, c * LANES:(c + 1) * LANES] = x.astype(BF16)
        kvt_ref[0, 0, c * LANES:(c + 1) * LANES, :] = x.T
    means = [jnp.concatenate([jnp.sum(x[g * MOBA_BLOCK:(g + 1) * MOBA_BLOCK], axis=0, keepdims=True)
                              for x in moba_k], axis=1) * (1.0 / MOBA_BLOCK) for g in range(tm // MOBA_BLOCK)]
    km_ref[0] = jnp.concatenate(means + [jnp.zeros((SUBLANES - len(means), KV_WIDTH), F32)], axis=0)


def _proj_kv_prompt(h, w, tabs, kvt_prev, layer, depth, b, t, tm):
    m, d = h.shape
    n = w.shape[1]
    nt = t // tm
    assert tm % MOBA_BLOCK == 0 and tm // MOBA_BLOCK <= SUBLANES
    in_specs = [pl.BlockSpec((tm, d), lambda j, i: (i, 0)), pl.BlockSpec((d, n), lambda j, i: (0, 0))]
    in_specs += [pl.BlockSpec((tm, LANES), lambda j, i: (i % nt, 0)) for _ in tabs]
    args = [h, w, *tabs]
    aliases = {}
    if kvt_prev is not None:
        in_specs.append(pl.BlockSpec(memory_space=pl.ANY))
        aliases = {len(args): 0}
        args.append(kvt_prev)
    return pl.pallas_call(
        _projkv_prompt_kernel,
        grid=(1, m // tm),
        in_specs=in_specs,
        out_specs=[pl.BlockSpec((1, 1, n, tm), lambda j, i: (layer, i // nt, 0, i % nt)),
                   pl.BlockSpec((tm, n), lambda j, i: (i, 0)),
                   pl.BlockSpec((1, SUBLANES, KV_WIDTH), lambda j, i: (i, 0, 0))],
        out_shape=[jax.ShapeDtypeStruct((depth, b, n, t), F32), jax.ShapeDtypeStruct((m, n), BF16),
                   jax.ShapeDtypeStruct((m // tm, SUBLANES, KV_WIDTH), F32)],
        input_output_aliases=aliases,
        compiler_params=_cp("arbitrary", "arbitrary"),
        name="proj_kv_prompt",
    )(*args)


def _projz_kernel(h_ref, w_ref, o_ref):
    y = jnp.dot(h_ref[...], w_ref[...], preferred_element_type=F32)
    o_ref[...] = (y * _sigmoid(y)).astype(o_ref.dtype)


def _projg_kernel(h_ref, w_ref, o_ref):
    y = jnp.dot(h_ref[...], w_ref[...], preferred_element_type=F32)
    o_ref[...] = _sigmoid(y).astype(o_ref.dtype)


MISC_KI = MIX_WIDTH
MISC_WI = MISC_KI + LANES
MISC_FB = MISC_WI + LANES
MISC_N = MISC_FB + LANES


def _projmisc_kernel(h_ref, w_ref, ra_ref, rb_ref, rc_ref, bf_ref,
                     qi_ref, ki_ref, ki2_ref, wi_ref, lf_ref):
    y = jnp.dot(h_ref[...], w_ref[...], preferred_element_type=F32)
    a, bm, cm = ra_ref[...], rb_ref[...], rc_ref[...]
    for c in range(MIX_WIDTH // LANES):
        x = _rope_chunk(y[:, c * LANES:(c + 1) * LANES], a, bm, cm)
        qi_ref[:, c * LANES:(c + 1) * LANES] = x.astype(BF16)
    ki = _rope_chunk(y[:, MISC_KI:MISC_KI + LANES], a, bm, cm)
    ki_ref[...] = ki[:, :IDX_DIM]
    ki2_ref[...] = (ki + pltpu.roll(ki, IDX_DIM, 1)).astype(BF16)
    wi_ref[...] = y[:, MISC_WI:MISC_WI + IDX_HEADS]
    f = y[:, MISC_FB:MISC_FB + LANES] + bf_ref[...]
    logf = jnp.minimum(f, 0.0) - jnp.log(1.0 + jnp.exp(-jnp.abs(f)))
    lf_ref[...] = logf[:, :N_HEADS]


def _proj_call(kernel, h, w, tabs, extra, outs, tm, tn, name):
    m, d = h.shape
    n = w.shape[1]
    nt = tabs[0].shape[0] // tm if tabs else 1
    in_specs = [pl.BlockSpec((tm, d), lambda j, i: (i, 0)),
                pl.BlockSpec((d, tn), lambda j, i: (0, j))]
    in_specs += [pl.BlockSpec((tm, LANES), lambda j, i: (i % nt, 0)) for _ in tabs]
    in_specs += [pl.BlockSpec(e.shape, lambda j, i: (0, 0)) for e in extra]
    out_specs = [pl.BlockSpec((tm, wd if full else tn), (lambda j, i: (i, 0)) if full else (lambda j, i: (i, j)))
                 for (wd, _, full) in outs]
    out_shape = [jax.ShapeDtypeStruct((m, wd), dt) for (wd, dt, _) in outs]
    return pl.pallas_call(
        kernel,
        grid=(n // tn, m // tm),
        in_specs=in_specs,
        out_specs=out_specs,
        out_shape=out_shape,
        compiler_params=_cp("arbitrary", "arbitrary"),
        name=name,
    )(h, w, *tabs, *extra)


CUM_CHUNK = 256


def _cumsum_kernel(x_ref, o_ref):
    n = x_ref.shape[2] // CUM_CHUNK
    r = lax.broadcasted_iota(jnp.int32, (CUM_CHUNK, CUM_CHUNK), 0)
    c = lax.broadcasted_iota(jnp.int32, (CUM_CHUNK, CUM_CHUNK), 1)
    tri = (r <= c).astype(F32)

    def body(i, carry):
        st = pl.multiple_of(i * CUM_CHUNK, CUM_CHUNK)
        x = x_ref[0, :, pl.ds(st, CUM_CHUNK)]
        y = jnp.dot(x, tri, precision=HIGHEST, preferred_element_type=F32) + carry
        o_ref[0, :, pl.ds(st, CUM_CHUNK)] = y * LOG2E
        return y[:, CUM_CHUNK - 1:CUM_CHUNK]

    lax.fori_loop(0, n, body, jnp.zeros((x_ref.shape[1], 1), F32))


def _cumsum(lft):
    b, h, lp = lft.shape
    return pl.pallas_call(
        _cumsum_kernel,
        grid=(b,),
        in_specs=[pl.BlockSpec((1, h, lp), lambda i: (i, 0, 0))],
        out_specs=pl.BlockSpec((1, h, lp), lambda i: (i, 0, 0)),
        out_shape=jax.ShapeDtypeStruct((b, h, lp), F32),
        compiler_params=_cp("arbitrary"),
        name="fox_cumsum",
    )(lft)


def _stack4(blk0, blk1):
    lane = lax.broadcasted_iota(jnp.int32, blk0.shape, 1)
    lo = lane < HEAD_DIM
    return jnp.concatenate([
        jnp.where(lo, blk0, 0.0),
        jnp.where(lo, pltpu.roll(blk0, HEAD_DIM, 1), 0.0),
        jnp.where(lo, 0.0, pltpu.roll(blk1, HEAD_DIM, 1)),
        jnp.where(lo, 0.0, blk1)], axis=0)


def _unstack4(o, tq):
    lane = lax.broadcasted_iota(jnp.int32, (tq, LANES), 1)
    lo = lane < HEAD_DIM
    b0 = jnp.where(lo, o[:tq], pltpu.roll(o[tq:2 * tq], HEAD_DIM, 1))
    b1 = jnp.where(lo, pltpu.roll(o[2 * tq:3 * tq], HEAD_DIM, 1), o[3 * tq:])
    return b0, b1


def _stacked_queries(q_ref):
    return [_stack4(q_ref[0, :, (2 * c) * LANES:(2 * c + 1) * LANES].astype(F32),
                    q_ref[0, :, (2 * c + 1) * LANES:(2 * c + 2) * LANES].astype(F32)) for c in range(2)]


def _flash_scratch(tq, tk):
    rows = 4 * tq
    return [pltpu.VMEM((2, rows, tk), F32), pltpu.VMEM((2, rows, tk), BF16),
            pltpu.VMEM((2, rows, LANES), F32), pltpu.VMEM((2, rows, LANES), F32), pltpu.VMEM((2, rows, LANES), F32),
            pltpu.VMEM((2, rows, 2 * LANES), F32), pltpu.VMEM((2, rows, tk), F32)]


def _flash_pipeline(n_all, qs, k_ref, v_ref, add_bias, diag_mask, scratch, o_ref, tq, tk):
    s_ref, p_ref, al_ref, m_ref, mx_ref, acc_ref, mb_ref = scratch
    m_ref[...] = jnp.full(m_ref.shape, -jnp.inf, F32)
    acc_ref[...] = jnp.zeros(acc_ref.shape, F32)
    if diag_mask is not None:
        @pl.when((pl.program_id(0) == 0) & (pl.program_id(1) == 0))
        def _():
            mb_ref[0] = jnp.zeros(mb_ref.shape[1:], F32)
        mb_ref[1] = diag_mask
    last = n_all - 1
    ones = jnp.ones((tk, LANES), BF16)

    def stage_pv(i):
        jc = jnp.clip(i - 2, 0, last)
        stc = pl.multiple_of(jc * tk, tk)
        for c in range(2):
            vo = jnp.concatenate([v_ref[0, pl.ds(stc, tk), c * LANES:(c + 1) * LANES], ones], axis=1)
            al = al_ref[c]
            acc_ref[c] = (jnp.concatenate([al, al], axis=1) * acc_ref[c]
                          + jnp.dot(p_ref[c], vo, preferred_element_type=F32))

    def stage_softmax():
        for c in range(2):
            m_old = m_ref[c]
            m_new = jnp.maximum(m_old, jnp.max(mx_ref[c], axis=-1, keepdims=True))
            al_ref[c] = jnp.exp2(m_old - m_new)
            m_ref[c] = m_new
            p_ref[c] = jnp.exp2(s_ref[c] - jnp.concatenate([m_new] * (tk // LANES), axis=1)).astype(BF16)

    def stage_scores(i):
        ja = jnp.minimum(i, last)
        sta = pl.multiple_of(ja * tk, tk)
        pen = jnp.where(i <= last, 0.0, NEG).astype(F32)
        for c in range(2):
            s = lax.dot_general(qs[c], k_ref[0, pl.ds(sta, tk), c * LANES:(c + 1) * LANES], _NT,
                                preferred_element_type=F32)
            s = add_bias(c, s, ja, pen)
            if diag_mask is not None:
                s = s + mb_ref[(ja == last).astype(jnp.int32)]
            s_ref[c] = s
            mx_ref[c] = functools.reduce(jnp.maximum, [s[:, k * LANES:(k + 1) * LANES] for k in range(tk // LANES)])

    def step(i):
        stage_pv(i)
        stage_softmax()
        stage_scores(i)

    def step2(ii, carry):
        step(2 * ii + 2)
        step(2 * ii + 3)
        return carry

    stage_scores(0)
    stage_softmax()
    stage_scores(1)
    lax.fori_loop(0, (n_all + 1) // 2, step2, 0)
    for c in range(2):
        b0, b1 = _unstack4(acc_ref[c, :, :LANES] / acc_ref[c, :, LANES:], tq)
        o_ref[0, :, (2 * c) * LANES:(2 * c + 1) * LANES] = b0.astype(o_ref.dtype)
        o_ref[0, :, (2 * c + 1) * LANES:(2 * c + 2) * LANES] = b1.astype(o_ref.dtype)


def _diag_mask(q_lo, n_all, tq, tk):
    r1 = lax.broadcasted_iota(jnp.int32, (tq, 1), 0) + q_lo
    qpos = jnp.concatenate([r1] * 4, axis=0)
    kpos = (n_all - 1) * tk + lax.broadcasted_iota(jnp.int32, (4 * tq, tk), 1)
    return jnp.where(kpos <= qpos, 0.0, NEG)


def _attn_call(kern, name, ins, in_specs, b, t, tq, scratch):
    return pl.pallas_call(
        kern,
        grid=(b, t // tq),
        in_specs=in_specs,
        out_specs=pl.BlockSpec((1, tq, MIX_WIDTH), lambda i, j: (i, j, 0)),
        out_shape=jax.ShapeDtypeStruct((b, t, MIX_WIDTH), BF16),
        scratch_shapes=scratch,
        compiler_params=_cp("arbitrary", "arbitrary"),
        name=name,
    )(*ins)


_q_spec = lambda tq: pl.BlockSpec((1, tq, MIX_WIDTH), lambda i, j: (i, j, 0))
_kv_spec = lambda lp, col: pl.BlockSpec((1, lp, KV_WIDTH), lambda i, j: (i, 0, col))


def _moba_kernel(qf_ref, q_ref, km_ref, k_ref, v_ref, o_ref, *scratch, tq, topk, nblk):
    tk = MOBA_BLOCK
    q_lo = pl.program_id(1) * tq
    own = q_lo // MOBA_BLOCK
    rows = 4 * tq
    lane = lax.broadcasted_iota(jnp.int32, (rows, LANES), 1)
    blk = lax.broadcasted_iota(jnp.int32, (nblk, rows), 0)
    blk_f = blk.astype(F32)
    qs = [q.astype(BF16) for q in _stacked_queries(q_ref)]
    selb = []
    for c, qf in enumerate(_stacked_queries(qf_ref)):
        gate = lax.dot_general(km_ref[0, :nblk, c * LANES:(c + 1) * LANES], qf, _NT, precision=HIGHEST,
                               preferred_element_type=F32)
        gate = jnp.where(blk < own, gate, -jnp.inf)
        sb = jnp.where(blk == own, 0.0, NEG)
        for _ in range(topk):
            mx = jnp.max(gate, axis=0, keepdims=True)
            is_max = (gate == mx) & (mx > -jnp.inf)
            first = jnp.min(jnp.where(is_max, blk_f, 1e9), axis=0, keepdims=True)
            pick = blk_f == first
            sb = jnp.where(pick, 0.0, sb)
            gate = jnp.where(pick, -jnp.inf, gate)
        sb = jnp.concatenate([sb, jnp.full((LANES - nblk, rows), NEG, F32)], axis=0)
        selb.append(sb.T)

    def add_bias(c, s, j, pen):
        return s + (jnp.sum(jnp.where(lane == j, selb[c], 0.0), axis=-1, keepdims=True) + pen)

    _flash_pipeline(own + 1, qs, k_ref, v_ref, add_bias, _diag_mask(q_lo, own + 1, tq, tk), scratch, o_ref, tq, tk)


def _moba_attn(qf, q, km, kvb, tq):
    b, t, _ = q.shape
    lp = kvb.shape[1]
    nblk = -(-(lp // MOBA_BLOCK) // SUBLANES) * SUBLANES
    assert nblk <= KMEAN_ROWS
    kern = functools.partial(_moba_kernel, tq=tq, topk=min(MOBA_TOPK, lp // MOBA_BLOCK), nblk=nblk)
    specs = [_q_spec(tq), _q_spec(tq), pl.BlockSpec((1, KMEAN_ROWS, KV_WIDTH), lambda i, j: (i, 0, 0)),
             _kv_spec(lp, 0), _kv_spec(lp, 1)]
    return _attn_call(kern, "moba_attn", (qf, q, km, kvb, kvb), specs, b, t, tq, _flash_scratch(tq, MOBA_BLOCK))


def _fox_kernel(q_ref, cum_ref, k_ref, v_ref, o_ref, *scratch, tq, tk):
    q_lo = pl.program_id(1) * tq
    n_all = (q_lo + tq + tk - 1) // tk
    qs = [q.astype(BF16) for q in _stacked_queries(q_ref)]

    def add_bias(c, s, j, pen):
        st = pl.multiple_of(j * tk, tk)
        parts = [s[hh * tq:(hh + 1) * tq] - (cum_ref[0, 4 * c + hh:4 * c + hh + 1, pl.ds(st, tk)] - pen)
                 for hh in range(4)]
        return jnp.concatenate(parts, axis=0)

    _flash_pipeline(n_all, qs, k_ref, v_ref, add_bias, _diag_mask(q_lo, n_all, tq, tk), scratch, o_ref, tq, tk)


def _fox_attn(q, cum, kvb, tq, tk):
    b, t, _ = q.shape
    lp = kvb.shape[1]
    kern = functools.partial(_fox_kernel, tq=tq, tk=tk)
    specs = [_q_spec(tq), pl.BlockSpec((1, N_HEADS, lp), lambda i, j: (i, 0, 0)), _kv_spec(lp, 2), _kv_spec(lp, 3)]
    return _attn_call(kern, "fox_attn", (q, cum, kvb, kvb), specs, b, t, tq, _flash_scratch(tq, tk))


def _key_to_float(key):
    bits = jnp.where(key < 0, key & jnp.int32(0x7FFFFFFF), ~key)
    return lax.bitcast_convert_type(bits, F32)


def _topk_threshold(count, vshape, need, few, idx_bits):
    def bit_body(i, key):
        cand = key | lax.shift_left(jnp.int32(1), 31 - i)
        thr_c = _key_to_float(cand)
        return jnp.where(count(lambda x, idx: x >= thr_c) >= need, cand, key)

    key = lax.fori_loop(0, 32, bit_body, jnp.zeros(vshape, jnp.int32))
    thr = _key_to_float(key)
    cnt_ge = count(lambda x, idx: x >= thr)
    has_ties = jnp.max(jnp.where(few, 0.0, cnt_ge - need)) > 0.0

    def tie_cut(_):
        r = need - count(lambda x, idx: x > thr)

        def jb(i, cut):
            cand = cut | lax.shift_left(jnp.int32(1), idx_bits - 1 - i)
            cnt = count(lambda x, idx: (x == thr) & (idx < cand))
            return jnp.where(cnt < r, cand, cut)

        return lax.fori_loop(0, idx_bits, jb, jnp.zeros(vshape, jnp.int32))

    cut = lax.cond(has_ties, tie_cut, lambda _: jnp.full(vshape, 2 ** 30, jnp.int32), 0)
    return jnp.where(few, F32_LOWEST, thr), jnp.where(few, 2 ** 30, cut)


def _dsa_kernel(qi_ref, w_ref, kx_ref, q_ref, k_ref, v_ref, o_ref, isc_ref, bias_ref, *scratch, tq, tk, sweep,
                topk, idx_bits):
    q_lo = pl.program_id(1) * tq
    n_all = (q_lo + tq + tk - 1) // tk
    lane = lax.broadcasted_iota(jnp.int32, (tq, LANES), 1)
    lo = lane < HEAD_DIM
    qpos = lax.broadcasted_iota(jnp.int32, (1, tq), 1) + q_lo
    krow = lax.broadcasted_iota(jnp.int32, (tk, 1), 0)

    pieces = []
    for h in range(IDX_HEADS):
        chunk = qi_ref[0, :, (h // 2) * LANES:(h // 2 + 1) * LANES].astype(F32)
        pieces.append(jnp.where(lo, chunk, 0.0) if h % 2 == 0 else jnp.where(lo, 0.0, chunk))
    qstack_t = jnp.concatenate(pieces, axis=0).T.astype(BF16)
    wrow = jnp.concatenate([w_ref[0, h:h + 1, :] for h in range(IDX_HEADS)], axis=1)

    def idx_body(j, carry):
        start = pl.multiple_of(j * tk, tk)
        sc = jnp.dot(kx_ref[0, pl.ds(start, tk), :], qstack_t, preferred_element_type=F32)
        contrib = jnp.maximum(sc, 0.0) * wrow
        isc = contrib[:, :tq]
        for h in range(1, IDX_HEADS):
            isc = isc + contrib[:, h * tq:(h + 1) * tq]
        isc_ref[pl.ds(start, tk), :] = jnp.where(j * tk + krow <= qpos, isc * IDX_SCALE, -jnp.inf)
        return carry

    lax.fori_loop(0, n_all, idx_body, 0)

    tiles_per_sweep = sweep // tk
    n_sweeps = (n_all + tiles_per_sweep - 1) // tiles_per_sweep

    def pad_body(j, carry):
        isc_ref[pl.ds(pl.multiple_of(j * tk, tk), tk), :] = jnp.full((tk, tq), -jnp.inf, F32)
        return carry

    lax.fori_loop(n_all, n_sweeps * tiles_per_sweep, pad_body, 0)
    grp = 8 * SUBLANES
    sub = lax.broadcasted_iota(jnp.int32, (grp, tq), 0)

    def count(pred):
        def body(j, acc):
            start = pl.multiple_of(j * sweep, sweep)
            x = isc_ref[pl.ds(start, sweep), :]
            for r in range(sweep // grp):
                idx = j * sweep + r * grp + sub
                acc = acc + jnp.where(pred(x[r * grp:(r + 1) * grp], idx), 1.0, 0.0)
            return acc
        acc = lax.fori_loop(0, n_sweeps, body, jnp.zeros((grp, tq), F32))
        return jnp.sum(acc, axis=0, keepdims=True)

    few = qpos + 1 <= topk
    thr, cut = _topk_threshold(count, (1, tq), jnp.float32(topk), few, idx_bits)

    def bias_body(j, carry):
        start = pl.multiple_of(j * tk, tk)
        x = isc_ref[pl.ds(start, tk), :]
        keep = (x > thr) | ((x == thr) & (j * tk + krow <= cut))
        bias_ref[:, pl.ds(start, tk)] = jnp.where(keep, 0.0, NEG).T
        return carry

    lax.fori_loop(0, n_all, bias_body, 0)

    qs = [q.astype(BF16) for q in _stacked_queries(q_ref)]

    def add_bias(c, s, j, pen):
        bias = bias_ref[:, pl.ds(pl.multiple_of(j * tk, tk), tk)] + pen
        return s + jnp.concatenate([bias] * 4, axis=0)

    _flash_pipeline(n_all, qs, k_ref, v_ref, add_bias, None, scratch, o_ref, tq, tk)


def _dsa_attn(qi, wi, kx2, q, kvb, tq, tk):
    b, t, _ = q.shape
    lp = kvb.shape[1]
    sweep = max(tk, min(DSA_SWEEP, lp))
    assert lp % sweep == 0 and sweep % tk == 0
    kern = functools.partial(_dsa_kernel, tq=tq, tk=tk, sweep=sweep, topk=min(DSA_TOPK, lp // 4),
                             idx_bits=max(1, int(lp).bit_length()))
    specs = [_q_spec(tq), pl.BlockSpec((1, IDX_HEADS, tq), lambda i, j: (i, 0, j)),
             pl.BlockSpec((1, lp, 2 * IDX_DIM), lambda i, j: (i, 0, 0)), _q_spec(tq), _kv_spec(lp, 4), _kv_spec(lp, 5)]
    scratch = [pltpu.VMEM((lp, tq), F32), pltpu.VMEM((tq, lp), F32)] + _flash_scratch(tq, tk)
    return _attn_call(kern, "dsa_attn", (qi, jnp.swapaxes(wi, 1, 2), kx2, q, kvb, kvb), specs, b, t, tq, scratch)


def _sample_index_kernel(pt_ref, *refs, ppi, n_steps, l_past, t_new):
    del pt_ref
    kx_refs = refs[:ppi]
    qi_ref, w_ref, nkx_ref, o_ref, isc_ref = refs[ppi:]
    j = pl.program_id(1)
    lp = isc_ref.shape[1]
    lane = lax.broadcasted_iota(jnp.int32, (t_new, LANES), 1)
    lo = lane < HEAD_DIM
    pieces, wpieces = [], []
    for h in range(IDX_HEADS):
        chunk = qi_ref[0, :, (h // 2) * LANES:(h // 2 + 1) * LANES].astype(F32)
        pieces.append(jnp.where(lo, chunk, 0.0) if h % 2 == 0 else jnp.where(lo, 0.0, chunk))
        wpieces.append(jnp.broadcast_to(w_ref[0, :, h:h + 1], (t_new, LANES)))
    qstack = jnp.concatenate(pieces, axis=0).astype(BF16)
    wb = jnp.concatenate(wpieces, axis=0)

    def head_sum(sc):
        contrib = jnp.maximum(sc, 0.0) * jnp.concatenate([wb] * (sc.shape[1] // LANES), axis=1)
        isc = contrib[:t_new]
        for h in range(1, IDX_HEADS):
            isc = isc + contrib[h * t_new:(h + 1) * t_new]
        return isc * IDX_SCALE

    kxt = jnp.concatenate([r[0, 0] for r in kx_refs], axis=1)
    rhs = jnp.concatenate([kxt, kxt], axis=0).astype(BF16)
    start = pl.multiple_of(j * (ppi * PAGE_SIZE), ppi * PAGE_SIZE)
    isc_ref[:, pl.ds(start, ppi * PAGE_SIZE)] = head_sum(jnp.dot(qstack, rhs, preferred_element_type=F32))

    @pl.when(j == n_steps - 1)
    def _():
        scn = lax.dot_general(qstack, nkx_ref[0], _NT, preferred_element_type=F32)
        row = lax.broadcasted_iota(jnp.int32, (t_new, LANES), 0)
        isc_ref[:, l_past:l_past + LANES] = jnp.where((lane <= row) & (lane < t_new), head_sum(scn), -jnp.inf)
        if lp > l_past + LANES:
            isc_ref[:, l_past + LANES:] = jnp.full((t_new, lp - l_past - LANES), -jnp.inf, F32)
        o_ref[0] = isc_ref[...]


def _topk_bias_kernel(x_ref, o_ref, *, sweep, topk, l_past, t_new, idx_bits):
    rows, lp = x_ref.shape
    lane = lax.broadcasted_iota(jnp.int32, (rows, LANES), 1)

    def count(pred):
        def body(j, acc):
            x = x_ref[:, pl.ds(pl.multiple_of(j * sweep, sweep), sweep)]
            for cc in range(sweep // LANES):
                idx = j * sweep + cc * LANES + lane
                acc = acc + jnp.where(pred(x[:, cc * LANES:(cc + 1) * LANES], idx), 1.0, 0.0)
            return acc
        acc = lax.fori_loop(0, lp // sweep, body, jnp.zeros((rows, LANES), F32))
        return jnp.sum(acc, axis=-1, keepdims=True)

    qpos1 = (lax.broadcasted_iota(jnp.int32, (rows, 1), 0) & (t_new - 1)) + l_past
    thr, cut = _topk_threshold(count, (rows, 1), jnp.float32(topk), qpos1 + 1 <= topk, idx_bits)
    kiota = lax.broadcasted_iota(jnp.int32, (rows, sweep), 1)

    def bias_body(j, carry):
        start = pl.multiple_of(j * sweep, sweep)
        x = x_ref[:, pl.ds(start, sweep)]
        keep = (x > thr) | ((x == thr) & (j * sweep + kiota <= cut))
        o_ref[:, pl.ds(start, sweep)] = jnp.where(keep, 0.0, NEG)
        return carry

    lax.fori_loop(0, lp // sweep, bias_body, 0)


def _topk_bias(isc, l_past, t_new, rows_per_step):
    rows, lp = isc.shape
    sweep = min(DSA_SWEEP, lp)
    assert lp % sweep == 0 and rows % rows_per_step == 0 and rows_per_step % t_new == 0
    kern = functools.partial(_topk_bias_kernel, sweep=sweep, topk=min(DSA_TOPK, (l_past + t_new) // 4),
                             l_past=l_past, t_new=t_new, idx_bits=max(1, int(lp).bit_length()))
    return pl.pallas_call(
        kern,
        grid=(rows // rows_per_step,),
        in_specs=[pl.BlockSpec((rows_per_step, lp), lambda i: (i, 0))],
        out_specs=pl.BlockSpec((rows_per_step, lp), lambda i: (i, 0)),
        out_shape=jax.ShapeDtypeStruct((rows, lp), F32),
        compiler_params=_cp("arbitrary"),
        name="dsa_topk_bias",
    )(isc)


def _sample_index(cache_kxt, page_table, layer, qi, wi, nkx2, lp):
    nseq, n_pages = page_table.shape
    t_new = qi.shape[1]
    ppi = PAGES_PER_STEP
    n_steps = n_pages // ppi
    l_past = n_pages * PAGE_SIZE
    pg = lambda pp: (lambda i, j, pt: (layer, pt[i, j * ppi + pp], 0, 0))
    grid_spec = pltpu.PrefetchScalarGridSpec(
        num_scalar_prefetch=1,
        grid=(nseq, n_steps),
        in_specs=[pl.BlockSpec((1, 1, IDX_DIM, PAGE_SIZE), pg(pp)) for pp in range(ppi)] + [
            pl.BlockSpec((1, t_new, MIX_WIDTH), lambda i, j, pt: (i, 0, 0)),
            pl.BlockSpec((1, t_new, IDX_HEADS), lambda i, j, pt: (i, 0, 0)),
            pl.BlockSpec((1, PAGE_SIZE, 2 * IDX_DIM), lambda i, j, pt: (i, 0, 0)),
        ],
        out_specs=pl.BlockSpec((1, t_new, lp), lambda i, j, pt: (i, 0, 0)),
        scratch_shapes=[pltpu.VMEM((t_new, lp), F32)],
    )
    kern = functools.partial(_sample_index_kernel, ppi=ppi, n_steps=n_steps, l_past=l_past, t_new=t_new)
    return pl.pallas_call(
        kern,
        grid_spec=grid_spec,
        out_shape=jax.ShapeDtypeStruct((nseq, t_new, lp), F32),
        compiler_params=_cp("arbitrary", "arbitrary"),
        name="dsa_index_paged",
    )(page_table, *([cache_kxt] * ppi), qi, wi, nkx2)


def _block_diag_q(q):
    t = q.shape[0]
    lane = lax.broadcasted_iota(jnp.int32, (t, LANES), 1)
    lo = lane < HEAD_DIM
    zero = jnp.zeros((t, LANES), F32)
    out = []
    for h in range(N_HEADS):
        src = q[:, (h // 2) * LANES:(h // 2 + 1) * LANES]
        kvh = h // (N_HEADS // N_KV_HEADS)
        chunk, half = divmod(kvh, 2)
        piece = src if h % 2 == half else pltpu.roll(src, HEAD_DIM, 1)
        piece = jnp.where(lo, piece, 0.0) if half == 0 else jnp.where(lo, 0.0, piece)
        out.append(jnp.concatenate([piece, zero] if chunk == 0 else [zero, piece], axis=1))
    return jnp.concatenate(out, axis=0)


def _token_major(o, t):
    lane = lax.broadcasted_iota(jnp.int32, (t, LANES), 1)
    lo = lane < HEAD_DIM
    out = []
    for co in range(MIX_WIDTH // LANES):
        chunk, half = divmod(co, 2)
        x = o[(2 * co) * t:(2 * co + 1) * t, chunk * LANES:(chunk + 1) * LANES]
        y = o[(2 * co + 1) * t:(2 * co + 2) * t, chunk * LANES:(chunk + 1) * LANES]
        if half == 0:
            out.append(jnp.where(lo, x, pltpu.roll(y, HEAD_DIM, 1)))
        else:
            out.append(jnp.where(lo, pltpu.roll(x, HEAD_DIM, 1), y))
    return jnp.concatenate(out, axis=1)


def _lane_cumsum(x):
    lane = lax.broadcasted_iota(jnp.int32, x.shape, 1)
    d = 1
    while d < LANES:
        x = x + jnp.where(lane >= d, pltpu.roll(x, d, 1), 0.0)
        d *= 2
    return x


def _rows_from_heads(c, t):
    return jnp.concatenate([jnp.broadcast_to(c[h:h + 1], (t, c.shape[1])) for h in range(N_HEADS)], axis=0)


def _online_update(s, pv_fn, m_ref, l_ref, acc_ref):
    m_old = m_ref[...]
    m_new = jnp.maximum(m_old, jnp.max(s, axis=-1, keepdims=True))
    alpha = jnp.exp2(m_old - m_new)
    p = jnp.exp2(s - jnp.concatenate([m_new] * (s.shape[1] // LANES), axis=1))
    l_ref[...] = alpha * l_ref[...] + jnp.sum(p, axis=-1, keepdims=True)
    acc_ref[...] = jnp.concatenate([alpha] * (acc_ref.shape[1] // LANES), axis=1) * acc_ref[...] + pv_fn(p.astype(BF16))
    m_ref[...] = m_new


def _sample_attn_kernel(pt_ref, *refs, nb, pps, t_new, topk):
    del pt_ref
    pg_refs, lf_refs = refs[:pps], refs[pps:2 * pps]
    (bias_ref, qa_ref, qb_ref, qc_ref, qaf_ref, nkv_ref, nlf_ref, oa_ref, ob_ref, oc_ref,
     qbd_ref, qbdf_ref, km_ref, ms_ref, ls_ref, accs_ref,
     fm_ref, fl_ref, facc_ref, car_ref, dm_ref, dl_ref, dacc_ref) = refs[2 * pps:]
    j = pl.program_id(1)
    rows = N_HEADS * t_new
    ppb = MOBA_BLOCK // PAGE_SIZE
    bps = pps // ppb
    n_steps = nb // bps
    lane = lax.broadcasted_iota(jnp.int32, (rows, LANES), 1)

    @pl.when(j == 0)
    def _():
        for m, q_ref in enumerate((qa_ref, qb_ref, qc_ref)):
            qbd_ref[m] = _block_diag_q(q_ref[0].astype(F32)).astype(BF16)
        qbdf_ref[...] = _block_diag_q(qaf_ref[0])
        km_ref[...] = jnp.zeros(km_ref.shape, F32)
        ms_ref[...] = jnp.zeros(ms_ref.shape, F32)
        ls_ref[...] = jnp.zeros(ls_ref.shape, F32)
        car_ref[...] = jnp.zeros(car_ref.shape, F32)
        for m_ref, l_ref, acc_ref in ((fm_ref, fl_ref, facc_ref), (dm_ref, dl_ref, dacc_ref)):
            m_ref[...] = jnp.full(m_ref.shape, -jnp.inf, F32)
            l_ref[...] = jnp.zeros(l_ref.shape, F32)
            acc_ref[...] = jnp.zeros(acc_ref.shape, F32)

    @pl.when(j < n_steps)
    def _():
        def kt(m, off):
            lo_r = m * MIXER_COLS + off
            return jnp.concatenate([r[0, 0, lo_r:lo_r + KV_WIDTH, :] for r in pg_refs], axis=1)

        def pv_nt(m):
            vt = kt(m, KV_WIDTH).astype(BF16)
            return lambda p: lax.dot_general(p, vt, _NT, preferred_element_type=F32)

        k0 = kt(0, 0)
        v0 = kt(0, KV_WIDTH).astype(BF16)
        s = jnp.dot(qbd_ref[0], k0.astype(BF16), preferred_element_type=F32)
        blk_lane = lax.broadcasted_iota(jnp.int32, km_ref.shape, 1)
        km, ms, ls = km_ref[...], ms_ref[...], ls_ref[...]
        for b in range(bps):
            n = j * bps + b
            cols = slice(b * MOBA_BLOCK, (b + 1) * MOBA_BLOCK)
            km = jnp.where(blk_lane == n, jnp.sum(k0[:, cols], axis=1, keepdims=True) * (1.0 / MOBA_BLOCK), km)
            m_n = jnp.max(s[:, cols], axis=-1, keepdims=True)
            p = jnp.exp2(s[:, cols] - m_n)
            ms = jnp.where(lane == n, m_n, ms)
            ls = jnp.where(lane == n, jnp.sum(p, axis=-1, keepdims=True), ls)
            accs_ref[n] = lax.dot_general(p.astype(BF16), v0[:, cols], _NT, preferred_element_type=F32)
        km_ref[...], ms_ref[...], ls_ref[...] = km, ms, ls

        carry, cums = car_ref[...][:, :1], []
        for r in lf_refs:
            cums.append(_lane_cumsum(r[0, 0]) + carry)
            carry = cums[-1][:, LANES - 1:LANES]
        car_ref[...] = jnp.broadcast_to(carry, car_ref.shape)
        s = jnp.dot(qbd_ref[1], kt(1, 0).astype(BF16), preferred_element_type=F32)
        s = s - _rows_from_heads(jnp.concatenate(cums, axis=1) * LOG2E, t_new)
        _online_update(s, pv_nt(1), fm_ref, fl_ref, facc_ref)

        s = jnp.dot(qbd_ref[2], kt(2, 0).astype(BF16), preferred_element_type=F32)
        s = s + jnp.concatenate([bias_ref[0]] * N_HEADS, axis=0)
        _online_update(s, pv_nt(2), dm_ref, dl_ref, dacc_ref)

    @pl.when(j == n_steps)
    def _():
        nk = lambda m: nkv_ref[0, :, m * MIXER_COLS:m * MIXER_COLS + KV_WIDTH]
        nv = lambda m: nkv_ref[0, :, m * MIXER_COLS + KV_WIDTH:(m + 1) * MIXER_COLS]
        pv = lambda m: (lambda p: jnp.dot(p, nv(m), preferred_element_type=F32))
        row_t = lax.broadcasted_iota(jnp.int32, (rows, LANES), 0) & (t_new - 1)
        causal = (lane <= row_t) & (lane < t_new)

        s = lax.dot_general(qbd_ref[0], nk(0), _NT, preferred_element_type=F32)
        s = jnp.where(causal, s, NEG)
        m_o = jnp.max(s, axis=-1, keepdims=True)
        p = jnp.exp2(s - m_o)
        l_o = jnp.sum(p, axis=-1, keepdims=True)
        acc_o = pv(0)(p.astype(BF16))
        gate = jnp.dot(qbdf_ref[...], km_ref[...], precision=HIGHEST, preferred_element_type=F32)
        gate = jnp.where(lane < nb, gate, -jnp.inf)
        lane_f = lane.astype(F32)
        sel = lane < 0
        for _ in range(topk):
            mx = jnp.max(gate, axis=-1, keepdims=True)
            is_max = (gate == mx) & (mx > -jnp.inf)
            first = jnp.min(jnp.where(is_max, lane_f, 1e9), axis=-1, keepdims=True)
            pick = lane_f == first
            sel = sel | pick
            gate = jnp.where(pick, -jnp.inf, gate)
        ms = ms_ref[...]
        m_all = jnp.maximum(m_o, jnp.max(jnp.where(sel, ms, -jnp.inf), axis=-1, keepdims=True))
        w = jnp.where(sel, jnp.exp2(ms - m_all), 0.0)
        w_o = jnp.exp2(m_o - m_all)
        l_all = jnp.sum(w * ls_ref[...], axis=-1, keepdims=True) + w_o * l_o

        def merge(n, o):
            wn = jnp.sum(jnp.where(lane == n, w, 0.0), axis=-1, keepdims=True)
            return o + wn * accs_ref[n]

        o_a = lax.fori_loop(0, nb, merge, w_o * acc_o) / l_all
        oa_ref[0] = _token_major(o_a, t_new)

        cn = _lane_cumsum(nlf_ref[0]) + car_ref[...]
        s = lax.dot_general(qbd_ref[1], nk(1), _NT, preferred_element_type=F32) - _rows_from_heads(cn * LOG2E, t_new)
        _online_update(jnp.where(causal, s, NEG), pv(1), fm_ref, fl_ref, facc_ref)
        wide = lambda l_ref: jnp.concatenate([l_ref[...]] * (KV_WIDTH // LANES), axis=1)
        ob_ref[0] = _token_major(facc_ref[...] / wide(fl_ref), t_new)

        s = lax.dot_general(qbd_ref[2], nk(2), _NT, preferred_element_type=F32)
        s = s + jnp.concatenate([bias_ref[0, :, :LANES]] * N_HEADS, axis=0)
        _online_update(s, pv(2), dm_ref, dl_ref, dacc_ref)
        oc_ref[0] = _token_major(dacc_ref[...] / wide(dl_ref), t_new)


def _sample_attn(cache_t, cache_lft, page_table, layer, bias, qa, qb, qc, qaf, nkv, nlf):
    nseq, n_pages = page_table.shape
    t_new = qa.shape[1]
    nb = n_pages // (MOBA_BLOCK // PAGE_SIZE)
    pps = PAGES_PER_STEP
    rows = N_HEADS * t_new
    width = cache_t.shape[2]
    last = n_pages - 1
    page = lambda k: (lambda i, j, pt: (layer, pt[i, jnp.minimum(pps * j + k, last)], 0, 0))
    seq_blk = lambda shape: pl.BlockSpec((1,) + shape, lambda i, j, pt: (i, 0, 0))
    grid_spec = pltpu.PrefetchScalarGridSpec(
        num_scalar_prefetch=1,
        grid=(nseq, n_pages // pps + 1),
        in_specs=[pl.BlockSpec((1, 1, width, PAGE_SIZE), page(k)) for k in range(pps)] + [
            pl.BlockSpec((1, 1, N_HEADS, PAGE_SIZE), page(k)) for k in range(pps)] + [
            pl.BlockSpec((1, t_new, pps * PAGE_SIZE), lambda i, j, pt: (i, 0, j)),
            seq_blk((t_new, MIX_WIDTH)), seq_blk((t_new, MIX_WIDTH)), seq_blk((t_new, MIX_WIDTH)),
            seq_blk((t_new, MIX_WIDTH)),
            seq_blk((PAGE_SIZE, width)), seq_blk((N_HEADS, PAGE_SIZE)),
        ],
        out_specs=[seq_blk((t_new, MIX_WIDTH))] * 3,
        scratch_shapes=[
            pltpu.VMEM((N_MIXERS, rows, KV_WIDTH), BF16), pltpu.VMEM((rows, KV_WIDTH), F32),
            pltpu.VMEM((KV_WIDTH, KMEAN_ROWS), F32), pltpu.VMEM((rows, LANES), F32), pltpu.VMEM((rows, LANES), F32),
            pltpu.VMEM((nb, rows, KV_WIDTH), F32),
            pltpu.VMEM((rows, LANES), F32), pltpu.VMEM((rows, LANES), F32), pltpu.VMEM((rows, KV_WIDTH), F32),
            pltpu.VMEM((N_HEADS, LANES), F32),
            pltpu.VMEM((rows, LANES), F32), pltpu.VMEM((rows, LANES), F32), pltpu.VMEM((rows, KV_WIDTH), F32),
        ],
    )
    kern = functools.partial(_sample_attn_kernel, nb=nb, pps=pps, t_new=t_new, topk=min(MOBA_TOPK, nb + 1))
    out = jax.ShapeDtypeStruct((nseq, t_new, MIX_WIDTH), F32)
    return pl.pallas_call(
        kern,
        grid_spec=grid_spec,
        out_shape=[out, out, out],
        compiler_params=_cp("arbitrary", "arbitrary"),
        name="sample_attn",
    )(page_table, *([cache_t] * pps), *([cache_lft] * pps), bias, qa, qb, qc, qaf, nkv, nlf)


def _out_kernel(oa_ref, ob_ref, oc_ref, sz_ref, sg_ref, x_ref, gate_ref, wb_ref, wo_ref, fw_ref, y_ref,
                *, final):
    d = x_ref.shape[2]
    merged = None
    for i, o_ref in enumerate((oa_ref, ob_ref, oc_ref)):
        t = (o_ref[0].astype(F32) * sz_ref[0, :, i * MIX_WIDTH:(i + 1) * MIX_WIDTH].astype(F32)).astype(BF16)
        br = sg_ref[0, :, i * d:(i + 1) * d].astype(F32) * jnp.dot(t, wb_ref[i], preferred_element_type=F32)
        merged = br if merged is None else merged + br
    y = x_ref[0] + gate_ref[0] * jnp.dot(merged.astype(BF16), wo_ref[...], preferred_element_type=F32)
    if final:
        ms = jnp.mean(y * y, axis=-1, keepdims=True)
        y = y * lax.rsqrt(ms + NORM_EPS) * fw_ref[...]
    y_ref[0] = y


def _out_proj(oa, ob, oc, sz, sg, x3, gate3, wb, wo, fw, tm, final):
    g, r, d = x3.shape
    rm = gate3.shape[1]
    tmod = 1 if rm == 1 else tm
    mod_map = (lambda b, i: (b, 0, 0)) if rm == 1 else (lambda b, i: (b, i, 0))
    row = lambda w: pl.BlockSpec((1, tm, w), lambda b, i: (b, i, 0))
    as3 = lambda a: a.reshape(g, r, a.shape[-1])
    return pl.pallas_call(
        functools.partial(_out_kernel, final=final),
        grid=(g, r // tm),
        in_specs=[
            row(MIX_WIDTH), row(MIX_WIDTH), row(MIX_WIDTH), row(N_MIXERS * MIX_WIDTH), row(N_MIXERS * d), row(d),
            pl.BlockSpec((1, tmod, d), mod_map),
            pl.BlockSpec(wb.shape, lambda b, i: (0, 0, 0)),
            pl.BlockSpec(wo.shape, lambda b, i: (0, 0)),
            pl.BlockSpec((1, d), lambda b, i: (0, 0)),
        ],
        out_specs=row(d),
        out_shape=jax.ShapeDtypeStruct((g, r, d), F32),
        compiler_params=_cp("arbitrary", "arbitrary"),
        name="out_proj",
    )(as3(oa), as3(ob), as3(oc), as3(sz), as3(sg), x3, gate3, wb, wo, fw.reshape(1, d))


def _rope_tables(pos):
    half = ROT_DIM // 2
    expo = jnp.arange(0, ROT_DIM, 2, dtype=F32) / ROT_DIM
    inv_freq = jnp.power(jnp.float32(ROPE_THETA), -expo)
    ang = pos.astype(F32)[:, None] * inv_freq[None, :]
    cos, sin = jnp.cos(ang), jnp.sin(ang)
    n = pos.shape[0]
    rest = HEAD_DIM - ROT_DIM
    a = jnp.concatenate([cos, cos, jnp.ones((n, rest), F32)], axis=1)
    bm = jnp.concatenate([-sin, jnp.zeros((n, half + rest), F32)], axis=1)
    cm = jnp.concatenate([jnp.zeros((n, half), F32), sin, jnp.zeros((n, rest), F32)], axis=1)
    rep = LANES // HEAD_DIM
    return tuple(jnp.tile(t, (1, rep)) for t in (a, bm, cm))


def _split_weights(w_in_l, b_f_l):
    o = np.cumsum([0, MIX_WIDTH, KV_WIDTH, KV_WIDTH, MIX_WIDTH,
                   MIX_WIDTH, KV_WIDTH, KV_WIDTH, N_HEADS, MIX_WIDTH,
                   MIX_WIDTH, KV_WIDTH, KV_WIDTH, IDX_HEADS * IDX_DIM, IDX_DIM, IDX_HEADS, MIX_WIDTH]).tolist()
    col = lambda i: w_in_l[:, o[i]:o[i + 1]]
    qa, ka, va, za, qb, kb, vb, fb, zb, qc, kc, vc, qi, ki, wi, zc = (col(i) for i in range(16))
    g = w_in_l[:, o[16]:]
    d = w_in_l.shape[0]
    zpad = lambda w: jnp.zeros((d, w), w_in_l.dtype)
    w_q = jnp.concatenate([qa, qb, qc], axis=1)
    w_kv = jnp.concatenate([ka, va, kb, vb, kc, vc], axis=1)
    w_z = jnp.concatenate([za, zb, zc], axis=1)
    w_misc = jnp.concatenate([qi, ki, zpad(LANES - IDX_DIM), wi, zpad(LANES - IDX_HEADS),
                              fb, zpad(LANES - N_HEADS)], axis=1)
    bf_row = jnp.concatenate([b_f_l, jnp.zeros((LANES - N_HEADS,), b_f_l.dtype)]).reshape(1, LANES)
    cast = lambda w: w.astype(BF16)
    return cast(w_q), cast(w_kv), cast(w_z), cast(g), cast(w_misc), bf_row.astype(F32)


def _project(x3, scale3, shift3, tabs, weights, norm_w_l, tm, kv_out=None):
    g, r, d = x3.shape
    w_q, w_kv, w_z, w_g, w_misc, bf_row = weights
    h = _norm_mod(x3, norm_w_l, scale3, shift3, tm).reshape(g * r, d)
    q_outs = _proj_call(
        _projq_kernel, h, w_q, tabs, (), [(MIX_WIDTH, BF16, True)] * 3 + [(MIX_WIDTH, F32, True)],
        tm, w_q.shape[1], "proj_q")
    if kv_out is None:
        kv, kvb = _proj_call(
            _projkv_kernel, h, w_kv, tabs, (), [(w_kv.shape[1], F32, True), (w_kv.shape[1], BF16, True)],
            tm, w_kv.shape[1], "proj_kv")
    else:
        kvt_prev, layer, depth = kv_out
        kvt, kvb, km = _proj_kv_prompt(h, w_kv, tabs, kvt_prev, layer, depth, g, r, tm)
        kv = (kvt, km)
    (sz,) = _proj_call(_projz_kernel, h, w_z, (), (), [(w_z.shape[1], BF16, True)], tm, w_z.shape[1], "proj_z")
    (sg,) = _proj_call(_projg_kernel, h, w_g, (), (), [(w_g.shape[1], BF16, False)], tm, d, "proj_g")
    misc = _proj_call(
        _projmisc_kernel, h, w_misc, tabs, (bf_row,),
        [(MIX_WIDTH, BF16, True), (IDX_DIM, F32, True), (2 * IDX_DIM, BF16, True),
         (IDX_HEADS, F32, True), (N_HEADS, F32, True)],
        tm, MISC_N, "proj_misc")
    return q_outs, kv, kvb, sz, sg, misc


def _prompt_layer(x3, scale3, shift3, gate3, tabs, weights, norm_w_l, wb_l, wo_l, fw, kvt_prev, layer, depth,
                  *, tm, tm_out, tq, tk, final):
    b, t, _ = x3.shape
    (qa, qb, qc, qaf), (kvt, km8), kvb, sz, sg, (qi, ki, ki2, wi, lf) = _project(
        x3, scale3, shift3, tabs, weights, norm_w_l, tm, kv_out=(kvt_prev, layer, depth))
    s3 = lambda a: a.reshape(b, t, a.shape[-1])
    kvb3, lf3 = s3(kvb), s3(lf)
    km = km8[:, :tm // MOBA_BLOCK].reshape(b, t // MOBA_BLOCK, KV_WIDTH)
    km = jnp.pad(km, ((0, 0), (0, KMEAN_ROWS - km.shape[1]), (0, 0)))
    cum = _cumsum(jnp.swapaxes(lf3, 1, 2))
    o_a = _moba_attn(s3(qaf), s3(qa), km, kvb3, tq)
    o_b = _fox_attn(s3(qb), cum, kvb3, tq, tk)
    o_c = _dsa_attn(s3(qi), s3(wi), s3(ki2), s3(qc), kvb3, tq, tk)
    x_new = _out_proj(o_a, o_b, o_c, sz, sg, x3, gate3, wb_l, wo_l, fw, tm_out, final)
    return x_new, kvt, lf3, s3(ki)


def _sample_layer(x3, scale3, shift3, gate3, tabs, weights, norm_w_l, wb_l, wo_l, fw, caches, page_table, layer,
                  *, nseq, final):
    _, m, _ = x3.shape
    t = m // nseq
    cache_t, cache_kxt, cache_lft = caches
    (qa, qb, qc, qaf), kv, kvb, sz, sg, (qi, ki, ki2, wi, lf) = _project(
        x3, scale3, shift3, tabs, weights, norm_w_l, m)
    s3 = lambda a: a.reshape(nseq, t, a.shape[-1])
    pad_slots = lambda a: jnp.pad(a, ((0, 0), (0, PAGE_SIZE - t), (0, 0)))
    lp = (page_table.shape[1] + PAGES_PER_STEP) * PAGE_SIZE
    isc = _sample_index(cache_kxt, page_table, layer, s3(qi), s3(wi), pad_slots(s3(ki2)), lp)
    bias = _topk_bias(isc.reshape(m, lp), page_table.shape[1] * PAGE_SIZE, t, min(m, 64)).reshape(nseq, t, lp)
    nlf = jnp.pad(jnp.swapaxes(s3(lf), 1, 2), ((0, 0), (0, 0), (0, PAGE_SIZE - t)))
    o_a, o_b, o_c = _sample_attn(cache_t, cache_lft, page_table, layer, bias, s3(qa), s3(qb), s3(qc), s3(qaf),
                                 pad_slots(s3(kvb)), nlf)
    x_new = _out_proj(o_a, o_b, o_c, sz, sg, x3, gate3, wb_l, wo_l, fw, m, final)
    return x_new, s3(kv), s3(lf), s3(ki)


def kernel(x_prompt, x_sample, cache_kv, cache_logf, cache_kidx, page_table, c_prompt, c_sample,
           norm_w, w_ada, b_ada, w_in, b_f, w_branch, w_out, final_norm_w):
    depth = norm_w.shape[0]
    bp, tp, d = x_prompt.shape
    bs, ts, _ = x_sample.shape
    n_pool = cache_kv.shape[1]
    n_pages = page_table.shape[1]
    past_len = n_pages * PAGE_SIZE
    assert ts == SUBLANES and n_pages % PAGES_PER_STEP == 0 and PAGES_PER_STEP % (MOBA_BLOCK // PAGE_SIZE) == 0
    assert tp % MOBA_BLOCK == 0 and tp % min(512, tp) == 0

    nc = bp + bs
    rc = -(-nc // SUBLANES) * SUBLANES
    c_all = jnp.concatenate([c_prompt, c_sample, jnp.zeros((rc - nc, d), F32)], axis=0)
    mod = _modulation(c_all, w_ada, b_ada)

    tabs_p = _rope_tables(jnp.arange(tp, dtype=jnp.int32))
    tabs_s = _rope_tables(jnp.tile(past_len + jnp.arange(ts, dtype=jnp.int32), bs))

    cache_t = jnp.transpose(cache_kv, (0, 1, 3, 4, 5, 6, 2)).reshape(depth, n_pool, N_MIXERS * MIXER_COLS, PAGE_SIZE)
    cache_kxt = jnp.swapaxes(cache_kidx, 2, 3)
    cache_lft = jnp.swapaxes(cache_logf, 2, 3)

    ms = bs * ts
    xp = x_prompt
    xs = x_sample.reshape(1, ms, d)
    new_p, new_s = [], []
    kvt = None
    for l in range(depth):
        weights = _split_weights(w_in[l], b_f[l])
        wb_l = w_branch[l].astype(BF16)
        wo_l = w_out[l].astype(BF16)
        final = l == depth - 1
        shift, scale, gate = (mod[l, :, i * d:(i + 1) * d] for i in range(3))
        p3 = lambda a: a[:bp].reshape(bp, 1, d)
        s3 = lambda a: jnp.repeat(a[bp:nc], ts, axis=0).reshape(1, ms, d)

        xp, kvt, lf_p, ki_p = _prompt_layer(
            xp, p3(scale), p3(shift), p3(gate), tabs_p, weights, norm_w[l], wb_l, wo_l, final_norm_w, kvt, l, depth,
            tm=min(512, tp), tm_out=min(512, tp), tq=min(128, tp), tk=MOBA_BLOCK, final=final)
        xs, kv_s, lf_s, ki_s = _sample_layer(
            xs, s3(scale), s3(shift), s3(gate), tabs_s, weights, norm_w[l], wb_l, wo_l, final_norm_w,
            (cache_t, cache_kxt, cache_lft), page_table, l, nseq=bs, final=final)
        new_p.append((None, lf_p, ki_p))
        new_s.append((kv_s, lf_s, ki_s))

    kv_shape = (N_MIXERS, 2, N_KV_HEADS, HEAD_DIM)
    stack = lambda items, i: jnp.stack([n[i] for n in items])
    kv_prompt = jnp.transpose(kvt.reshape(depth, bp, *kv_shape, tp), (0, 1, 6, 2, 3, 4, 5))
    kv_sample = stack(new_s, 0).reshape(depth, bs, ts, *kv_shape)
    return (xp, xs.reshape(bs, ts, d), kv_prompt, stack(new_p, 1), stack(new_p, 2),
            kv_sample, stack(new_s, 1), stack(new_s, 2))
```

```python
import functools

import numpy as np
import jax
import jax.numpy as jnp
from jax import lax
from jax.experimental import pallas as pl
from jax.experimental.pallas import tpu as pltpu

F32 = jnp.float32
BF16 = jnp.bfloat16
HIGHEST = lax.Precision.HIGHEST

N_MIXERS = 3
N_HEADS = 8
N_KV_HEADS = 4
HEAD_DIM = 64
MIX_WIDTH = N_HEADS * HEAD_DIM
KV_WIDTH = N_KV_HEADS * HEAD_DIM
MIXER_COLS = 2 * KV_WIDTH
ROT_DIM = HEAD_DIM // 4
ROPE_THETA = 500000.0
ATTN_SCALE = HEAD_DIM ** -0.5
LOG2E = 1.4426950408889634
MOBA_BLOCK = 256
MOBA_TOPK = 3
IDX_HEADS = 8
IDX_DIM = 64
IDX_SCALE = (IDX_DIM * IDX_HEADS) ** -0.5
DSA_TOPK = 256
NORM_EPS = 1e-6
PAGE_SIZE = 128

LANES = 128
SUBLANES = 8
KMEAN_ROWS = 128
NEG = -1e30
F32_LOWEST = -3.0e38
VMEM_LIMIT = 56 * 1024 * 1024
PAGES_PER_STEP = 8
INDEX_PAGES_PER_STEP = 16
DSA_SWEEP = 1024

_NT = (((1,), (1,)), ((), ()))


def _cp(*sem):
    return pltpu.CompilerParams(dimension_semantics=sem, vmem_limit_bytes=VMEM_LIMIT)


def _sigmoid(x):
    return 1.0 / (1.0 + jnp.exp(-x))


def _mod_kernel(c_ref, w_ref, b_ref, o_ref):
    c = c_ref[...]
    sc = c * _sigmoid(c)
    o_ref[0] = jnp.dot(sc, w_ref[0], precision=HIGHEST, preferred_element_type=F32) + b_ref[0]


def _modulation(c_all, w_ada, b_ada):
    depth, d, d3 = w_ada.shape
    rc = c_all.shape[0]
    nj = d3 // d
    return pl.pallas_call(
        _mod_kernel,
        grid=(depth, nj),
        in_specs=[
            pl.BlockSpec((rc, d), lambda l, j: (0, 0)),
            pl.BlockSpec((1, d, d), lambda l, j: (l, 0, j)),
            pl.BlockSpec((1, 1, d), lambda l, j: (l, 0, j)),
        ],
        out_specs=pl.BlockSpec((1, rc, d), lambda l, j: (l, 0, j)),
        out_shape=jax.ShapeDtypeStruct((depth, rc, d3), F32),
        compiler_params=_cp("arbitrary", "arbitrary"),
        name="adaln_mod",
    )(c_all, w_ada, b_ada.reshape(depth, 1, d3))


def _norm_kernel(x_ref, w_ref, sc_ref, sh_ref, h_ref):
    x = x_ref[0]
    ms = jnp.mean(x * x, axis=-1, keepdims=True)
    y = x * lax.rsqrt(ms + NORM_EPS) * w_ref[...]
    h_ref[0] = (y * (1.0 + sc_ref[0]) + sh_ref[0]).astype(BF16)


def _norm_mod(x3, norm_w, scale3, shift3, tm):
    g, r, d = x3.shape
    rm = scale3.shape[1]
    tmod = 1 if rm == 1 else tm
    mod_map = (lambda b, i: (b, 0, 0)) if rm == 1 else (lambda b, i: (b, i, 0))
    return pl.pallas_call(
        _norm_kernel,
        grid=(g, r // tm),
        in_specs=[
            pl.BlockSpec((1, tm, d), lambda b, i: (b, i, 0)),
            pl.BlockSpec((1, d), lambda b, i: (0, 0)),
            pl.BlockSpec((1, tmod, d), mod_map),
            pl.BlockSpec((1, tmod, d), mod_map),
        ],
        out_specs=pl.BlockSpec((1, tm, d), lambda b, i: (b, i, 0)),
        out_shape=jax.ShapeDtypeStruct((g, r, d), BF16),
        compiler_params=_cp("arbitrary", "arbitrary"),
        name="norm_mod",
    )(x3, norm_w.reshape(1, d), scale3, shift3)


def _rope_chunk(x, a, bm, cm):
    return x * a + pltpu.roll(x, LANES - ROT_DIM // 2, 1) * bm + pltpu.roll(x, ROT_DIM // 2, 1) * cm


def _projq_kernel(h_ref, w_ref, ra_ref, rb_ref, rc_ref, qa_ref, qb_ref, qc_ref, qaf_ref):
    y = jnp.dot(h_ref[...], w_ref[...], preferred_element_type=F32)
    a, bm, cm = ra_ref[...], rb_ref[...], rc_ref[...]
    for m, o_ref in enumerate((qa_ref, qb_ref, qc_ref)):
        for c in range(MIX_WIDTH // LANES):
            x = y[:, m * MIX_WIDTH + c * LANES:m * MIX_WIDTH + (c + 1) * LANES]
            if m != 1:
                x = _rope_chunk(x, a, bm, cm)
            if m == 0:
                qaf_ref[:, c * LANES:(c + 1) * LANES] = x
            o_ref[:, c * LANES:(c + 1) * LANES] = (x * (ATTN_SCALE * LOG2E)).astype(BF16)


def _projkv_kernel(h_ref, w_ref, ra_ref, rb_ref, rc_ref, kv_ref, kvb_ref):
    y = jnp.dot(h_ref[...], w_ref[...], preferred_element_type=F32)
    a, bm, cm = ra_ref[...], rb_ref[...], rc_ref[...]
    per_mixer = MIXER_COLS // LANES
    for c in range(N_MIXERS * per_mixer):
        x = y[:, c * LANES:(c + 1) * LANES]
        mixer, within = divmod(c, per_mixer)
        if mixer != 1 and within < KV_WIDTH // LANES:
            x = _rope_chunk(x, a, bm, cm)
        kv_ref[:, c * LANES:(c + 1) * LANES] = x
        kvb_ref[:, c * LANES:(c + 1) * LANES] = x.astype(BF16)


def _projkv_prompt_kernel(h_ref, w_ref, ra_ref, rb_ref, rc_ref, *refs):
    kvt_ref, kvb_ref, km_ref = refs[-3:]
    y = jnp.dot(h_ref[...], w_ref[...], preferred_element_type=F32)
    a, bm, cm = ra_ref[...], rb_ref[...], rc_ref[...]
    tm = y.shape[0]
    per_mixer = MIXER_COLS // LANES
    moba_k = []
    for c in range(N_MIXERS * per_mixer):
        x = y[:, c * LANES:(c + 1) * LANES]
        mixer, within = divmod(c, per_mixer)
        if mixer != 1 and within < KV_WIDTH // LANES:
            x = _rope_chunk(x, a, bm, cm)
        if mixer == 0 and within < KV_WIDTH // LANES:
            moba_k.append(x)
        kvb_ref[:, c * LANES:(c + 1) * LANES] = x.astype(BF16)
        kvt_ref[0, 0, c * LANES:(c + 1) * LANES, :] = x.T
    means = [jnp.concatenate([jnp.sum(x[g * MOBA_BLOCK:(g + 1) * MOBA_BLOCK], axis=0, keepdims=True)
                              for x in moba_k], axis=1) * (1.0 / MOBA_BLOCK) for g in range(tm // MOBA_BLOCK)]
    km_ref[0] = jnp.concatenate(means + [jnp.zeros((SUBLANES - len(means), KV_WIDTH), F32)], axis=0)


def _proj_kv_prompt(h, w, tabs, kvt_prev, layer, depth, b, t, tm):
    m, d = h.shape
    n = w.shape[1]
    nt = t // tm
    assert tm % MOBA_BLOCK == 0 and tm // MOBA_BLOCK <= SUBLANES
    in_specs = [pl.BlockSpec((tm, d), lambda j, i: (i, 0)), pl.BlockSpec((d, n), lambda j, i: (0, 0))]
    in_specs += [pl.BlockSpec((tm, LANES), lambda j, i: (i % nt, 0)) for _ in tabs]
    in_specs.append(pl.BlockSpec(memory_space=pl.ANY))
    args = [h, w, *tabs, kvt_prev]
    aliases = {len(args) - 1: 0}
    return pl.pallas_call(
        _projkv_prompt_kernel,
        grid=(1, m // tm),
        in_specs=in_specs,
        out_specs=[pl.BlockSpec((1, 1, n, tm), lambda j, i: (layer, i // nt, 0, i % nt)),
                   pl.BlockSpec((tm, n), lambda j, i: (i, 0)),
                   pl.BlockSpec((1, SUBLANES, KV_WIDTH), lambda j, i: (i, 0, 0))],
        out_shape=[jax.ShapeDtypeStruct((depth, b, n, t), F32), jax.ShapeDtypeStruct((m, n), BF16),
                   jax.ShapeDtypeStruct((m // tm, SUBLANES, KV_WIDTH), F32)],
        input_output_aliases=aliases,
        compiler_params=_cp("arbitrary", "arbitrary"),
        name="proj_kv_prompt",
    )(*args)


def _projz_kernel(h_ref, w_ref, o_ref):
    y = jnp.dot(h_ref[...], w_ref[...], preferred_element_type=F32)
    o_ref[...] = (y * _sigmoid(y)).astype(o_ref.dtype)


def _projg_kernel(h_ref, w_ref, o_ref):
    y = jnp.dot(h_ref[...], w_ref[...], preferred_element_type=F32)
    o_ref[...] = _sigmoid(y).astype(o_ref.dtype)


MISC_KI = MIX_WIDTH
MISC_WI = MISC_KI + LANES
MISC_FB = MISC_WI + LANES
MISC_N = MISC_FB + LANES


def _projmisc_kernel(h_ref, w_ref, ra_ref, rb_ref, rc_ref, bf_ref,
                     qi_ref, ki_ref, ki2_ref, wi_ref, lf_ref):
    y = jnp.dot(h_ref[...], w_ref[...], preferred_element_type=F32)
    a, bm, cm = ra_ref[...], rb_ref[...], rc_ref[...]
    for c in range(MIX_WIDTH // LANES):
        x = _rope_chunk(y[:, c * LANES:(c + 1) * LANES], a, bm, cm)
        qi_ref[:, c * LANES:(c + 1) * LANES] = x.astype(BF16)
    ki = _rope_chunk(y[:, MISC_KI:MISC_KI + LANES], a, bm, cm)
    ki_ref[...] = ki[:, :IDX_DIM]
    ki2_ref[...] = (ki + pltpu.roll(ki, IDX_DIM, 1)).astype(BF16)
    wi_ref[...] = y[:, MISC_WI:MISC_WI + IDX_HEADS]
    f = y[:, MISC_FB:MISC_FB + LANES] + bf_ref[...]
    logf = jnp.minimum(f, 0.0) - jnp.log(1.0 + jnp.exp(-jnp.abs(f)))
    lf_ref[...] = logf[:, :N_HEADS]


def _proj_call(kernel, h, w, tabs, extra, outs, tm, tn, name):
    m, d = h.shape
    n = w.shape[1]
    nt = tabs[0].shape[0] // tm if tabs else 1
    in_specs = [pl.BlockSpec((tm, d), lambda j, i: (i, 0)),
                pl.BlockSpec((d, tn), lambda j, i: (0, j))]
    in_specs += [pl.BlockSpec((tm, LANES), lambda j, i: (i % nt, 0)) for _ in tabs]
    in_specs += [pl.BlockSpec(e.shape, lambda j, i: (0, 0)) for e in extra]
    out_specs = [pl.BlockSpec((tm, wd if full else tn), (lambda j, i: (i, 0)) if full else (lambda j, i: (i, j)))
                 for (wd, _, full) in outs]
    out_shape = [jax.ShapeDtypeStruct((m, wd), dt) for (wd, dt, _) in outs]
    return pl.pallas_call(
        kernel,
        grid=(n // tn, m // tm),
        in_specs=in_specs,
        out_specs=out_specs,
        out_shape=out_shape,
        compiler_params=_cp("arbitrary", "arbitrary"),
        name=name,
    )(h, w, *tabs, *extra)


CUM_CHUNK = 256


def _cumsum_kernel(x_ref, o_ref):
    n = x_ref.shape[2] // CUM_CHUNK
    r = lax.broadcasted_iota(jnp.int32, (CUM_CHUNK, CUM_CHUNK), 0)
    c = lax.broadcasted_iota(jnp.int32, (CUM_CHUNK, CUM_CHUNK), 1)
    tri = (r <= c).astype(F32)

    def body(i, carry):
        st = pl.multiple_of(i * CUM_CHUNK, CUM_CHUNK)
        x = x_ref[0, :, pl.ds(st, CUM_CHUNK)]
        y = jnp.dot(x, tri, precision=HIGHEST, preferred_element_type=F32) + carry
        o_ref[0, :, pl.ds(st, CUM_CHUNK)] = y * LOG2E
        return y[:, CUM_CHUNK - 1:CUM_CHUNK]

    lax.fori_loop(0, n, body, jnp.zeros((x_ref.shape[1], 1), F32))


def _cumsum(lft):
    b, h, lp = lft.shape
    return pl.pallas_call(
        _cumsum_kernel,
        grid=(b,),
        in_specs=[pl.BlockSpec((1, h, lp), lambda i: (i, 0, 0))],
        out_specs=pl.BlockSpec((1, h, lp), lambda i: (i, 0, 0)),
        out_shape=jax.ShapeDtypeStruct((b, h, lp), F32),
        compiler_params=_cp("arbitrary"),
        name="fox_cumsum",
    )(lft)


def _stack4(blk0, blk1):
    lane = lax.broadcasted_iota(jnp.int32, blk0.shape, 1)
    lo = lane < HEAD_DIM
    return jnp.concatenate([
        jnp.where(lo, blk0, 0.0),
        jnp.where(lo, pltpu.roll(blk0, HEAD_DIM, 1), 0.0),
        jnp.where(lo, 0.0, pltpu.roll(blk1, HEAD_DIM, 1)),
        jnp.where(lo, 0.0, blk1)], axis=0)


def _unstack4(o, tq):
    lane = lax.broadcasted_iota(jnp.int32, (tq, LANES), 1)
    lo = lane < HEAD_DIM
    b0 = jnp.where(lo, o[:tq], pltpu.roll(o[tq:2 * tq], HEAD_DIM, 1))
    b1 = jnp.where(lo, pltpu.roll(o[2 * tq:3 * tq], HEAD_DIM, 1), o[3 * tq:])
    return b0, b1


def _stacked_queries(q_ref):
    return [_stack4(q_ref[0, :, (2 * c) * LANES:(2 * c + 1) * LANES].astype(F32),
                    q_ref[0, :, (2 * c + 1) * LANES:(2 * c + 2) * LANES].astype(F32)) for c in range(2)]


def _flash_scratch(tq, tk):
    rows = 4 * tq
    return [pltpu.VMEM((2, rows, tk), F32), pltpu.VMEM((2, rows, tk), BF16),
            pltpu.VMEM((2, rows, LANES), F32), pltpu.VMEM((2, rows, LANES), F32), pltpu.VMEM((2, rows, LANES), F32),
            pltpu.VMEM((2, rows, 2 * LANES), F32), pltpu.VMEM((2, rows, tk), F32)]


def _flash_pipeline(n_all, qs, k_ref, v_ref, add_bias, diag_mask, scratch, o_ref, tq, tk, k_aug=None):
    s_ref, p_ref, m0_ref, m1_ref, mx_ref, acc_ref, mb_ref = scratch
    m_slots = (m0_ref, m1_ref)
    m0_ref[...] = jnp.full(m0_ref.shape, -jnp.inf, F32)
    acc_ref[...] = jnp.zeros(acc_ref.shape, F32)
    if diag_mask is not None:
        @pl.when((pl.program_id(0) == 0) & (pl.program_id(1) == 0))
        def _():
            mb_ref[0] = jnp.zeros(mb_ref.shape[1:], F32)
        mb_ref[1] = diag_mask
    last = n_all - 1
    ones = jnp.ones((tk, LANES), BF16)

    def stage_pv(i, par):
        jc = jnp.clip(i - 2, 0, last)
        stc = pl.multiple_of(jc * tk, tk)
        for c in range(2):
            vo = jnp.concatenate([v_ref[0, pl.ds(stc, tk), c * LANES:(c + 1) * LANES], ones], axis=1)
            al = jnp.exp2(m_slots[par][c] - m_slots[1 - par][c])
            acc_ref[c] = (jnp.concatenate([al, al], axis=1) * acc_ref[c]
                          + jnp.dot(p_ref[c], vo, preferred_element_type=F32))

    def stage_softmax(par):
        for c in range(2):
            m_new = jnp.maximum(m_slots[1 - par][c], jnp.max(mx_ref[c], axis=-1, keepdims=True))
            m_slots[par][c] = m_new
            p_ref[c] = jnp.exp2(s_ref[c] - jnp.concatenate([m_new] * (tk // LANES), axis=1)).astype(BF16)

    def stage_scores(i):
        ja = jnp.minimum(i, last)
        sta = pl.multiple_of(ja * tk, tk)
        pen = jnp.where(i <= last, 0.0, NEG).astype(F32)
        for c in range(2):
            kt = k_ref[0, pl.ds(sta, tk), c * LANES:(c + 1) * LANES]
            if k_aug is not None:
                kt = jnp.concatenate([kt, k_aug(c, ja, i <= last)], axis=1)
            s = lax.dot_general(qs[c], kt, _NT, preferred_element_type=F32)
            s = add_bias(c, s, ja, pen)
            if diag_mask is not None:
                s = s + mb_ref[(ja == last).astype(jnp.int32)]
            s_ref[c] = s
            mx_ref[c] = functools.reduce(jnp.maximum, [s[:, k * LANES:(k + 1) * LANES] for k in range(tk // LANES)])

    def step(i, par):
        stage_pv(i, par)
        stage_softmax(par)
        stage_scores(i)

    def step2(ii, carry):
        step(2 * ii + 2, 0)
        step(2 * ii + 3, 1)
        return carry

    stage_scores(0)
    stage_softmax(1)
    stage_scores(1)
    lax.fori_loop(0, (n_all + 1) // 2, step2, 0)
    for c in range(2):
        b0, b1 = _unstack4(acc_ref[c, :, :LANES] / acc_ref[c, :, LANES:], tq)
        o_ref[0, :, (2 * c) * LANES:(2 * c + 1) * LANES] = b0.astype(o_ref.dtype)
        o_ref[0, :, (2 * c + 1) * LANES:(2 * c + 2) * LANES] = b1.astype(o_ref.dtype)


def _diag_mask(q_lo, n_all, tq, tk):
    r1 = lax.broadcasted_iota(jnp.int32, (tq, 1), 0) + q_lo
    qpos = jnp.concatenate([r1] * 4, axis=0)
    kpos = (n_all - 1) * tk + lax.broadcasted_iota(jnp.int32, (4 * tq, tk), 1)
    return jnp.where(kpos <= qpos, 0.0, NEG)


def _attn_call(kern, name, ins, in_specs, b, t, tq, scratch):
    return pl.pallas_call(
        kern,
        grid=(b, t // tq),
        in_specs=in_specs,
        out_specs=pl.BlockSpec((1, tq, MIX_WIDTH), lambda i, j: (i, j, 0)),
        out_shape=jax.ShapeDtypeStruct((b, t, MIX_WIDTH), BF16),
        scratch_shapes=scratch,
        compiler_params=_cp("arbitrary", "arbitrary"),
        name=name,
    )(*ins)


_q_spec = lambda tq: pl.BlockSpec((1, tq, MIX_WIDTH), lambda i, j: (i, j, 0))
_kv_spec = lambda lp, col: pl.BlockSpec((1, lp, KV_WIDTH), lambda i, j: (i, 0, col))


def _moba_kernel(qf_ref, q_ref, km_ref, k_ref, v_ref, o_ref, *scratch, tq, topk, nblk):
    tk = MOBA_BLOCK
    q_lo = pl.program_id(1) * tq
    own = q_lo // MOBA_BLOCK
    rows = 4 * tq
    blk = lax.broadcasted_iota(jnp.int32, (nblk, rows), 0)
    blk_f = blk.astype(F32)
    q_st = _stacked_queries(q_ref)
    qs = []
    for c, qf in enumerate(_stacked_queries(qf_ref)):
        gate = lax.dot_general(km_ref[0, :nblk, c * LANES:(c + 1) * LANES], qf, _NT, precision=HIGHEST,
                               preferred_element_type=F32)
        gate = jnp.where(blk < own, gate, -jnp.inf)
        sb = jnp.where(blk == own, 0.0, NEG)
        for _ in range(topk):
            mx = jnp.max(gate, axis=0, keepdims=True)
            is_max = (gate == mx) & (mx > -jnp.inf)
            first = jnp.min(jnp.where(is_max, blk_f, 1e9), axis=0, keepdims=True)
            pick = blk_f == first
            sb = jnp.where(pick, 0.0, sb)
            gate = jnp.where(pick, -jnp.inf, gate)
        sb = jnp.concatenate([sb, jnp.full((LANES - nblk, rows), NEG, F32)], axis=0)
        qs.append(jnp.concatenate([q_st[c], sb.T], axis=1).astype(BF16))

    klane = lax.broadcasted_iota(jnp.int32, (tk, LANES), 1)

    def k_aug(c, j, valid):
        return jnp.where(klane == jnp.where(valid, j, LANES - 1), 1.0, 0.0).astype(BF16)

    _flash_pipeline(own + 1, qs, k_ref, v_ref, lambda c, s, j, pen: s, _diag_mask(q_lo, own + 1, tq, tk), scratch,
                    o_ref, tq, tk, k_aug=k_aug)


def _moba_attn(qf, q, km, kvb, tq):
    b, t, _ = q.shape
    lp = kvb.shape[1]
    nblk = -(-(lp // MOBA_BLOCK) // SUBLANES) * SUBLANES
    assert nblk < LANES
    kern = functools.partial(_moba_kernel, tq=tq, topk=min(MOBA_TOPK, lp // MOBA_BLOCK), nblk=nblk)
    specs = [_q_spec(tq), _q_spec(tq), pl.BlockSpec((1, KMEAN_ROWS, KV_WIDTH), lambda i, j: (i, 0, 0)),
             _kv_spec(lp, 0), _kv_spec(lp, 1)]
    return _attn_call(kern, "moba_attn", (qf, q, km, kvb, kvb), specs, b, t, tq, _flash_scratch(tq, MOBA_BLOCK))


def _fox_kernel(q_ref, cum_ref, k_ref, v_ref, o_ref, *scratch, tq, tk):
    q_lo = pl.program_id(1) * tq
    n_all = (q_lo + tq + tk - 1) // tk
    qs = [q.astype(BF16) for q in _stacked_queries(q_ref)]

    def add_bias(c, s, j, pen):
        st = pl.multiple_of(j * tk, tk)
        parts = [s[hh * tq:(hh + 1) * tq] - (cum_ref[0, 4 * c + hh:4 * c + hh + 1, pl.ds(st, tk)] - pen)
                 for hh in range(4)]
        return jnp.concatenate(parts, axis=0)

    _flash_pipeline(n_all, qs, k_ref, v_ref, add_bias, _diag_mask(q_lo, n_all, tq, tk), scratch, o_ref, tq, tk)


def _fox_attn(q, cum, kvb, tq, tk):
    b, t, _ = q.shape
    lp = kvb.shape[1]
    kern = functools.partial(_fox_kernel, tq=tq, tk=tk)
    specs = [_q_spec(tq), pl.BlockSpec((1, N_HEADS, lp), lambda i, j: (i, 0, 0)), _kv_spec(lp, 2), _kv_spec(lp, 3)]
    return _attn_call(kern, "fox_attn", (q, cum, kvb, kvb), specs, b, t, tq, _flash_scratch(tq, tk))


def _key_to_float(key):
    bits = jnp.where(key < 0, key & jnp.int32(0x7FFFFFFF), ~key)
    return lax.bitcast_convert_type(bits, F32)


def _topk_threshold(count, vshape, need, few, idx_bits):
    def bit_body(i, key):
        cand = key | lax.shift_left(jnp.int32(1), 31 - i)
        thr_c = _key_to_float(cand)
        return jnp.where(count(lambda x, idx: x >= thr_c) >= need, cand, key)

    key = lax.fori_loop(0, 32, bit_body, jnp.zeros(vshape, jnp.int32))
    thr = _key_to_float(key)
    cnt_ge = count(lambda x, idx: x >= thr)
    has_ties = jnp.max(jnp.where(few, 0.0, cnt_ge - need)) > 0.0

    def tie_cut(_):
        r = need - count(lambda x, idx: x > thr)

        def jb(i, cut):
            cand = cut | lax.shift_left(jnp.int32(1), idx_bits - 1 - i)
            cnt = count(lambda x, idx: (x == thr) & (idx < cand))
            return jnp.where(cnt < r, cand, cut)

        return lax.fori_loop(0, idx_bits, jb, jnp.zeros(vshape, jnp.int32))

    cut = lax.cond(has_ties, tie_cut, lambda _: jnp.full(vshape, 2 ** 30, jnp.int32), 0)
    return jnp.where(few, F32_LOWEST, thr), jnp.where(few, 2 ** 30, cut)


def _dsa_kernel(qi_ref, w_ref, kx_ref, q_ref, k_ref, v_ref, o_ref, isc_ref, bias_ref, *scratch, tq, tk, sweep,
                topk, idx_bits):
    q_lo = pl.program_id(1) * tq
    n_all = (q_lo + tq + tk - 1) // tk
    lane = lax.broadcasted_iota(jnp.int32, (tq, LANES), 1)
    lo = lane < HEAD_DIM
    qpos = lax.broadcasted_iota(jnp.int32, (1, tq), 1) + q_lo
    krow = lax.broadcasted_iota(jnp.int32, (tk, 1), 0)

    pieces = []
    for h in range(IDX_HEADS):
        chunk = qi_ref[0, :, (h // 2) * LANES:(h // 2 + 1) * LANES].astype(F32)
        pieces.append(jnp.where(lo, chunk, 0.0) if h % 2 == 0 else jnp.where(lo, 0.0, chunk))
    qstack_t = jnp.concatenate(pieces, axis=0).T.astype(BF16)
    wrow = jnp.concatenate([w_ref[0, h:h + 1, :] for h in range(IDX_HEADS)], axis=1)

    def idx_tile(j):
        start = pl.multiple_of(j * tk, tk)
        sc = jnp.dot(kx_ref[0, pl.ds(start, tk), :], qstack_t, preferred_element_type=F32)
        contrib = jnp.maximum(sc, 0.0) * wrow
        isc = contrib[:, :tq]
        for h in range(1, IDX_HEADS):
            isc = isc + contrib[:, h * tq:(h + 1) * tq]
        isc_ref[pl.ds(start, tk), :] = jnp.where(j * tk + krow <= qpos, isc * IDX_SCALE, -jnp.inf)

    def idx_body(jj, carry):
        idx_tile(2 * jj)
        idx_tile(2 * jj + 1)
        return carry

    n_idx = (n_all + 1) // 2 * 2
    lax.fori_loop(0, n_idx // 2, idx_body, 0)

    tiles_per_sweep = sweep // tk
    n_sweeps = (n_all + tiles_per_sweep - 1) // tiles_per_sweep

    def pad_body(j, carry):
        isc_ref[pl.ds(pl.multiple_of(j * tk, tk), tk), :] = jnp.full((tk, tq), -jnp.inf, F32)
        return carry

    lax.fori_loop(n_idx, n_sweeps * tiles_per_sweep, pad_body, 0)
    grp = 8 * SUBLANES
    sub = lax.broadcasted_iota(jnp.int32, (grp, tq), 0)

    def count(pred):
        def body(j, acc):
            start = pl.multiple_of(j * sweep, sweep)
            x = isc_ref[pl.ds(start, sweep), :]
            for r in range(sweep // grp):
                idx = j * sweep + r * grp + sub
                acc = acc + jnp.where(pred(x[r * grp:(r + 1) * grp], idx), 1.0, 0.0)
            return acc
        acc = lax.fori_loop(0, n_sweeps, body, jnp.zeros((grp, tq), F32))
        return jnp.sum(acc, axis=0, keepdims=True)

    few = qpos + 1 <= topk
    thr, cut = _topk_threshold(count, (1, tq), jnp.float32(topk), few, idx_bits)

    def bias_body(j, carry):
        start = pl.multiple_of(j * tk, tk)
        x = isc_ref[pl.ds(start, tk), :]
        keep = (x > thr) | ((x == thr) & (j * tk + krow <= cut))
        bias_ref[:, pl.ds(start, tk)] = jnp.where(keep, 0.0, NEG).T
        return carry

    lax.fori_loop(0, n_all, bias_body, 0)

    qs = [q.astype(BF16) for q in _stacked_queries(q_ref)]

    def add_bias(c, s, j, pen):
        bias = bias_ref[:, pl.ds(pl.multiple_of(j * tk, tk), tk)] + pen
        return s + jnp.concatenate([bias] * 4, axis=0)

    _flash_pipeline(n_all, qs, k_ref, v_ref, add_bias, None, scratch, o_ref, tq, tk)


def _dsa_attn(qi, wi, kx2, q, kvb, tq, tk):
    b, t, _ = q.shape
    lp = kvb.shape[1]
    sweep = max(tk, min(DSA_SWEEP, lp))
    assert lp % sweep == 0 and sweep % tk == 0 and lp % (2 * tk) == 0
    kern = functools.partial(_dsa_kernel, tq=tq, tk=tk, sweep=sweep, topk=min(DSA_TOPK, lp // 4),
                             idx_bits=max(1, int(lp).bit_length()))
    specs = [_q_spec(tq), pl.BlockSpec((1, IDX_HEADS, tq), lambda i, j: (i, 0, j)),
             pl.BlockSpec((1, lp, 2 * IDX_DIM), lambda i, j: (i, 0, 0)), _q_spec(tq), _kv_spec(lp, 4), _kv_spec(lp, 5)]
    scratch = [pltpu.VMEM((lp, tq), F32), pltpu.VMEM((tq, lp), F32)] + _flash_scratch(tq, tk)
    return _attn_call(kern, "dsa_attn", (qi, jnp.swapaxes(wi, 1, 2), kx2, q, kvb, kvb), specs, b, t, tq, scratch)


def _sample_index_kernel(pt_ref, *refs, ppi, n_steps, l_past, t_new):
    del pt_ref
    kx_refs = refs[:ppi]
    qi_ref, w_ref, nkx_ref, o_ref, isc_ref = refs[ppi:]
    j = pl.program_id(1)
    lp = isc_ref.shape[1]
    lane = lax.broadcasted_iota(jnp.int32, (t_new, LANES), 1)
    lo = lane < HEAD_DIM
    pieces, wpieces = [], []
    for h in range(IDX_HEADS):
        chunk = qi_ref[0, :, (h // 2) * LANES:(h // 2 + 1) * LANES].astype(F32)
        pieces.append(jnp.where(lo, chunk, 0.0) if h % 2 == 0 else jnp.where(lo, 0.0, chunk))
        wpieces.append(jnp.broadcast_to(w_ref[0, :, h:h + 1], (t_new, LANES)))
    qstack = jnp.concatenate(pieces, axis=0).astype(BF16)
    wb = jnp.concatenate(wpieces, axis=0)

    def head_sum(sc):
        contrib = jnp.maximum(sc, 0.0) * jnp.concatenate([wb] * (sc.shape[1] // LANES), axis=1)
        isc = contrib[:t_new]
        for h in range(1, IDX_HEADS):
            isc = isc + contrib[h * t_new:(h + 1) * t_new]
        return isc * IDX_SCALE

    kxt = jnp.concatenate([r[0, 0] for r in kx_refs], axis=1)
    rhs = jnp.concatenate([kxt, kxt], axis=0).astype(BF16)
    start = pl.multiple_of(j * (ppi * PAGE_SIZE), ppi * PAGE_SIZE)
    isc_ref[:, pl.ds(start, ppi * PAGE_SIZE)] = head_sum(jnp.dot(qstack, rhs, preferred_element_type=F32))

    @pl.when(j == n_steps - 1)
    def _():
        scn = lax.dot_general(qstack, nkx_ref[0], _NT, preferred_element_type=F32)
        row = lax.broadcasted_iota(jnp.int32, (t_new, LANES), 0)
        isc_ref[:, l_past:l_past + LANES] = jnp.where((lane <= row) & (lane < t_new), head_sum(scn), -jnp.inf)
        if lp > l_past + LANES:
            isc_ref[:, l_past + LANES:] = jnp.full((t_new, lp - l_past - LANES), -jnp.inf, F32)
        o_ref[0] = isc_ref[...]


def _topk_bias_kernel(x_ref, o_ref, *, sweep, topk, l_past, t_new, idx_bits):
    rows, lp = x_ref.shape
    lane = lax.broadcasted_iota(jnp.int32, (rows, LANES), 1)

    def count(pred):
        def body(j, acc):
            x = x_ref[:, pl.ds(pl.multiple_of(j * sweep, sweep), sweep)]
            for cc in range(sweep // LANES):
                idx = j * sweep + cc * LANES + lane
                acc = acc + jnp.where(pred(x[:, cc * LANES:(cc + 1) * LANES], idx), 1.0, 0.0)
            return acc
        acc = lax.fori_loop(0, lp // sweep, body, jnp.zeros((rows, LANES), F32))
        return jnp.sum(acc, axis=-1, keepdims=True)

    qpos1 = (lax.broadcasted_iota(jnp.int32, (rows, 1), 0) & (t_new - 1)) + l_past
    thr, cut = _topk_threshold(count, (rows, 1), jnp.float32(topk), qpos1 + 1 <= topk, idx_bits)
    kiota = lax.broadcasted_iota(jnp.int32, (rows, sweep), 1)

    def bias_body(j, carry):
        start = pl.multiple_of(j * sweep, sweep)
        x = x_ref[:, pl.ds(start, sweep)]
        keep = (x > thr) | ((x == thr) & (j * sweep + kiota <= cut))
        o_ref[:, pl.ds(start, sweep)] = jnp.where(keep, 0.0, NEG)
        return carry

    lax.fori_loop(0, lp // sweep, bias_body, 0)


def _topk_bias(isc, l_past, t_new, rows_per_step):
    rows, lp = isc.shape
    sweep = min(DSA_SWEEP, lp)
    assert lp % sweep == 0 and rows % rows_per_step == 0 and rows_per_step % t_new == 0
    kern = functools.partial(_topk_bias_kernel, sweep=sweep, topk=min(DSA_TOPK, (l_past + t_new) // 4),
                             l_past=l_past, t_new=t_new, idx_bits=max(1, int(lp).bit_length()))
    return pl.pallas_call(
        kern,
        grid=(rows // rows_per_step,),
        in_specs=[pl.BlockSpec((rows_per_step, lp), lambda i: (i, 0))],
        out_specs=pl.BlockSpec((rows_per_step, lp), lambda i: (i, 0)),
        out_shape=jax.ShapeDtypeStruct((rows, lp), F32),
        compiler_params=_cp("arbitrary"),
        name="dsa_topk_bias",
    )(isc)


def _sample_index(cache_kxt, page_table, layer, qi, wi, nkx2, lp):
    nseq, n_pages = page_table.shape
    t_new = qi.shape[1]
    ppi = INDEX_PAGES_PER_STEP if n_pages % INDEX_PAGES_PER_STEP == 0 else PAGES_PER_STEP
    n_steps = n_pages // ppi
    l_past = n_pages * PAGE_SIZE
    pg = lambda pp: (lambda i, j, pt: (layer, pt[i, j * ppi + pp], 0, 0))
    grid_spec = pltpu.PrefetchScalarGridSpec(
        num_scalar_prefetch=1,
        grid=(nseq, n_steps),
        in_specs=[pl.BlockSpec((1, 1, IDX_DIM, PAGE_SIZE), pg(pp)) for pp in range(ppi)] + [
            pl.BlockSpec((1, t_new, MIX_WIDTH), lambda i, j, pt: (i, 0, 0)),
            pl.BlockSpec((1, t_new, IDX_HEADS), lambda i, j, pt: (i, 0, 0)),
            pl.BlockSpec((1, PAGE_SIZE, 2 * IDX_DIM), lambda i, j, pt: (i, 0, 0)),
        ],
        out_specs=pl.BlockSpec((1, t_new, lp), lambda i, j, pt: (i, 0, 0)),
        scratch_shapes=[pltpu.VMEM((t_new, lp), F32)],
    )
    kern = functools.partial(_sample_index_kernel, ppi=ppi, n_steps=n_steps, l_past=l_past, t_new=t_new)
    return pl.pallas_call(
        kern,
        grid_spec=grid_spec,
        out_shape=jax.ShapeDtypeStruct((nseq, t_new, lp), F32),
        compiler_params=_cp("arbitrary", "arbitrary"),
        name="dsa_index_paged",
    )(page_table, *([cache_kxt] * ppi), qi, wi, nkx2)


def _block_diag_q(q):
    t = q.shape[0]
    lane = lax.broadcasted_iota(jnp.int32, (t, LANES), 1)
    lo = lane < HEAD_DIM
    zero = jnp.zeros((t, LANES), F32)
    out = []
    for h in range(N_HEADS):
        src = q[:, (h // 2) * LANES:(h // 2 + 1) * LANES]
        kvh = h // (N_HEADS // N_KV_HEADS)
        chunk, half = divmod(kvh, 2)
        piece = src if h % 2 == half else pltpu.roll(src, HEAD_DIM, 1)
        piece = jnp.where(lo, piece, 0.0) if half == 0 else jnp.where(lo, 0.0, piece)
        out.append(jnp.concatenate([piece, zero] if chunk == 0 else [zero, piece], axis=1))
    return jnp.concatenate(out, axis=0)


def _token_major(o, t):
    lane = lax.broadcasted_iota(jnp.int32, (t, LANES), 1)
    lo = lane < HEAD_DIM
    out = []
    for co in range(MIX_WIDTH // LANES):
        chunk, half = divmod(co, 2)
        x = o[(2 * co) * t:(2 * co + 1) * t, chunk * LANES:(chunk + 1) * LANES]
        y = o[(2 * co + 1) * t:(2 * co + 2) * t, chunk * LANES:(chunk + 1) * LANES]
        if half == 0:
            out.append(jnp.where(lo, x, pltpu.roll(y, HEAD_DIM, 1)))
        else:
            out.append(jnp.where(lo, pltpu.roll(x, HEAD_DIM, 1), y))
    return jnp.concatenate(out, axis=1)


def _lane_cumsum(x):
    lane = lax.broadcasted_iota(jnp.int32, x.shape, 1)
    d = 1
    while d < LANES:
        x = x + jnp.where(lane >= d, pltpu.roll(x, d, 1), 0.0)
        d *= 2
    return x


def _rows_from_heads(c, t):
    return jnp.concatenate([jnp.broadcast_to(c[h:h + 1], (t, c.shape[1])) for h in range(N_HEADS)], axis=0)


def _online_update(s, pv_fn, m_ref, l_ref, acc_ref):
    m_old = m_ref[...]
    m_new = jnp.maximum(m_old, jnp.max(s, axis=-1, keepdims=True))
    alpha = jnp.exp2(m_old - m_new)
    p = jnp.exp2(s - jnp.concatenate([m_new] * (s.shape[1] // LANES), axis=1))
    l_ref[...] = alpha * l_ref[...] + jnp.sum(p, axis=-1, keepdims=True)
    acc_ref[...] = jnp.concatenate([alpha] * (acc_ref.shape[1] // LANES), axis=1) * acc_ref[...] + pv_fn(p.astype(BF16))
    m_ref[...] = m_new


def _sample_attn_kernel(pt_ref, *refs, nb, pps, t_new, topk):
    del pt_ref
    pg_refs, lf_refs = refs[:pps], refs[pps:2 * pps]
    (bias_ref, qa_ref, qb_ref, qc_ref, qaf_ref, nkv_ref, nlf_ref, oa_ref, ob_ref, oc_ref,
     qbd_ref, qbdf_ref, km_ref, ms_ref, ls_ref, accs_ref,
     fm_ref, fl_ref, facc_ref, car_ref, dm_ref, dl_ref, dacc_ref) = refs[2 * pps:]
    j = pl.program_id(1)
    rows = N_HEADS * t_new
    ppb = MOBA_BLOCK // PAGE_SIZE
    bps = pps // ppb
    n_steps = nb // bps
    lane = lax.broadcasted_iota(jnp.int32, (rows, LANES), 1)

    @pl.when(j == 0)
    def _():
        for m, q_ref in enumerate((qa_ref, qb_ref, qc_ref)):
            qbd_ref[m] = _block_diag_q(q_ref[0].astype(F32)).astype(BF16)
        qbdf_ref[...] = _block_diag_q(qaf_ref[0])
        km_ref[...] = jnp.zeros(km_ref.shape, F32)
        ms_ref[...] = jnp.zeros(ms_ref.shape, F32)
        ls_ref[...] = jnp.zeros(ls_ref.shape, F32)
        car_ref[...] = jnp.zeros(car_ref.shape, F32)
        for m_ref, l_ref, acc_ref in ((fm_ref, fl_ref, facc_ref), (dm_ref, dl_ref, dacc_ref)):
            m_ref[...] = jnp.full(m_ref.shape, -jnp.inf, F32)
            l_ref[...] = jnp.zeros(l_ref.shape, F32)
            acc_ref[...] = jnp.zeros(acc_ref.shape, F32)

    @pl.when(j < n_steps)
    def _():
        def kt(m, off):
            lo_r = m * MIXER_COLS + off
            return jnp.concatenate([r[0, 0, lo_r:lo_r + KV_WIDTH, :] for r in pg_refs], axis=1)

        def pv_nt(m):
            vt = kt(m, KV_WIDTH).astype(BF16)
            return lambda p: lax.dot_general(p, vt, _NT, preferred_element_type=F32)

        k0 = kt(0, 0)
        v0 = kt(0, KV_WIDTH).astype(BF16)
        s = jnp.dot(qbd_ref[0], k0.astype(BF16), preferred_element_type=F32)
        blk_lane = lax.broadcasted_iota(jnp.int32, km_ref.shape, 1)
        km, ms, ls = km_ref[...], ms_ref[...], ls_ref[...]
        for b in range(bps):
            n = j * bps + b
            cols = slice(b * MOBA_BLOCK, (b + 1) * MOBA_BLOCK)
            km = jnp.where(blk_lane == n, jnp.sum(k0[:, cols], axis=1, keepdims=True) * (1.0 / MOBA_BLOCK), km)
            m_n = jnp.max(s[:, cols], axis=-1, keepdims=True)
            p = jnp.exp2(s[:, cols] - m_n)
            ms = jnp.where(lane == n, m_n, ms)
            ls = jnp.where(lane == n, jnp.sum(p, axis=-1, keepdims=True), ls)
            accs_ref[n] = lax.dot_general(p.astype(BF16), v0[:, cols], _NT, preferred_element_type=F32)
        km_ref[...], ms_ref[...], ls_ref[...] = km, ms, ls

        carry, cums = car_ref[...][:, :1], []
        for r in lf_refs:
            cums.append(_lane_cumsum(r[0, 0]) + carry)
            carry = cums[-1][:, LANES - 1:LANES]
        car_ref[...] = jnp.broadcast_to(carry, car_ref.shape)
        s = jnp.dot(qbd_ref[1], kt(1, 0).astype(BF16), preferred_element_type=F32)
        s = s - _rows_from_heads(jnp.concatenate(cums, axis=1) * LOG2E, t_new)
        _online_update(s, pv_nt(1), fm_ref, fl_ref, facc_ref)

        s = jnp.dot(qbd_ref[2], kt(2, 0).astype(BF16), preferred_element_type=F32)
        s = s + jnp.concatenate([bias_ref[0]] * N_HEADS, axis=0)
        _online_update(s, pv_nt(2), dm_ref, dl_ref, dacc_ref)

    @pl.when(j == n_steps)
    def _():
        nk = lambda m: nkv_ref[0, :, m * MIXER_COLS:m * MIXER_COLS + KV_WIDTH]
        nv = lambda m: nkv_ref[0, :, m * MIXER_COLS + KV_WIDTH:(m + 1) * MIXER_COLS]
        pv = lambda m: (lambda p: jnp.dot(p, nv(m), preferred_element_type=F32))
        row_t = lax.broadcasted_iota(jnp.int32, (rows, LANES), 0) & (t_new - 1)
        causal = (lane <= row_t) & (lane < t_new)

        s = lax.dot_general(qbd_ref[0], nk(0), _NT, preferred_element_type=F32)
        s = jnp.where(causal, s, NEG)
        m_o = jnp.max(s, axis=-1, keepdims=True)
        p = jnp.exp2(s - m_o)
        l_o = jnp.sum(p, axis=-1, keepdims=True)
        acc_o = pv(0)(p.astype(BF16))
        gate = jnp.dot(qbdf_ref[...], km_ref[...], precision=HIGHEST, preferred_element_type=F32)
        gate = jnp.where(lane < nb, gate, -jnp.inf)
        lane_f = lane.astype(F32)
        sel = lane < 0
        for _ in range(topk):
            mx = jnp.max(gate, axis=-1, keepdims=True)
            is_max = (gate == mx) & (mx > -jnp.inf)
            first = jnp.min(jnp.where(is_max, lane_f, 1e9), axis=-1, keepdims=True)
            pick = lane_f == first
            sel = sel | pick
            gate = jnp.where(pick, -jnp.inf, gate)
        ms = ms_ref[...]
        m_all = jnp.maximum(m_o, jnp.max(jnp.where(sel, ms, -jnp.inf), axis=-1, keepdims=True))
        w = jnp.where(sel, jnp.exp2(ms - m_all), 0.0)
        w_o = jnp.exp2(m_o - m_all)
        l_all = jnp.sum(w * ls_ref[...], axis=-1, keepdims=True) + w_o * l_o

        def merge(n, o):
            wn = jnp.sum(jnp.where(lane == n, w, 0.0), axis=-1, keepdims=True)
            return o + wn * accs_ref[n]

        o_a = lax.fori_loop(0, nb, merge, w_o * acc_o) / l_all
        oa_ref[0] = _token_major(o_a, t_new)

        cn = _lane_cumsum(nlf_ref[0]) + car_ref[...]
        s = lax.dot_general(qbd_ref[1], nk(1), _NT, preferred_element_type=F32) - _rows_from_heads(cn * LOG2E, t_new)
        _online_update(jnp.where(causal, s, NEG), pv(1), fm_ref, fl_ref, facc_ref)
        wide = lambda l_ref: jnp.concatenate([l_ref[...]] * (KV_WIDTH // LANES), axis=1)
        ob_ref[0] = _token_major(facc_ref[...] / wide(fl_ref), t_new)

        s = lax.dot_general(qbd_ref[2], nk(2), _NT, preferred_element_type=F32)
        s = s + jnp.concatenate([bias_ref[0, :, :LANES]] * N_HEADS, axis=0)
        _online_update(s, pv(2), dm_ref, dl_ref, dacc_ref)
        oc_ref[0] = _token_major(dacc_ref[...] / wide(dl_ref), t_new)


def _sample_attn(cache_t, cache_lft, page_table, layer, bias, qa, qb, qc, qaf, nkv, nlf):
    nseq, n_pages = page_table.shape
    t_new = qa.shape[1]
    nb = n_pages // (MOBA_BLOCK // PAGE_SIZE)
    pps = PAGES_PER_STEP
    rows = N_HEADS * t_new
    width = cache_t.shape[2]
    last = n_pages - 1
    page = lambda k: (lambda i, j, pt: (layer, pt[i, jnp.minimum(pps * j + k, last)], 0, 0))
    seq_blk = lambda shape: pl.BlockSpec((1,) + shape, lambda i, j, pt: (i, 0, 0))
    grid_spec = pltpu.PrefetchScalarGridSpec(
        num_scalar_prefetch=1,
        grid=(nseq, n_pages // pps + 1),
        in_specs=[pl.BlockSpec((1, 1, width, PAGE_SIZE), page(k)) for k in range(pps)] + [
            pl.BlockSpec((1, 1, N_HEADS, PAGE_SIZE), page(k)) for k in range(pps)] + [
            pl.BlockSpec((1, t_new, pps * PAGE_SIZE), lambda i, j, pt: (i, 0, j)),
            seq_blk((t_new, MIX_WIDTH)), seq_blk((t_new, MIX_WIDTH)), seq_blk((t_new, MIX_WIDTH)),
            seq_blk((t_new, MIX_WIDTH)),
            seq_blk((PAGE_SIZE, width)), seq_blk((N_HEADS, PAGE_SIZE)),
        ],
        out_specs=[seq_blk((t_new, MIX_WIDTH))] * 3,
        scratch_shapes=[
            pltpu.VMEM((N_MIXERS, rows, KV_WIDTH), BF16), pltpu.VMEM((rows, KV_WIDTH), F32),
            pltpu.VMEM((KV_WIDTH, KMEAN_ROWS), F32), pltpu.VMEM((rows, LANES), F32), pltpu.VMEM((rows, LANES), F32),
            pltpu.VMEM((nb, rows, KV_WIDTH), F32),
            pltpu.VMEM((rows, LANES), F32), pltpu.VMEM((rows, LANES), F32), pltpu.VMEM((rows, KV_WIDTH), F32),
            pltpu.VMEM((N_HEADS, LANES), F32),
            pltpu.VMEM((rows, LANES), F32), pltpu.VMEM((rows, LANES), F32), pltpu.VMEM((rows, KV_WIDTH), F32),
        ],
    )
    kern = functools.partial(_sample_attn_kernel, nb=nb, pps=pps, t_new=t_new, topk=min(MOBA_TOPK, nb + 1))
    out = jax.ShapeDtypeStruct((nseq, t_new, MIX_WIDTH), F32)
    return pl.pallas_call(
        kern,
        grid_spec=grid_spec,
        out_shape=[out, out, out],
        compiler_params=_cp("arbitrary", "arbitrary"),
        name="sample_attn",
    )(page_table, *([cache_t] * pps), *([cache_lft] * pps), bias, qa, qb, qc, qaf, nkv, nlf)


def _out_kernel(oa_ref, ob_ref, oc_ref, sz_ref, sg_ref, x_ref, gate_ref, wb_ref, wo_ref, fw_ref, y_ref,
                *, final):
    d = x_ref.shape[2]
    merged = None
    for i, o_ref in enumerate((oa_ref, ob_ref, oc_ref)):
        t = (o_ref[0].astype(F32) * sz_ref[0, :, i * MIX_WIDTH:(i + 1) * MIX_WIDTH].astype(F32)).astype(BF16)
        br = sg_ref[0, :, i * d:(i + 1) * d].astype(F32) * jnp.dot(t, wb_ref[i], preferred_element_type=F32)
        merged = br if merged is None else merged + br
    y = x_ref[0] + gate_ref[0] * jnp.dot(merged.astype(BF16), wo_ref[...], preferred_element_type=F32)
    if final:
        ms = jnp.mean(y * y, axis=-1, keepdims=True)
        y = y * lax.rsqrt(ms + NORM_EPS) * fw_ref[...]
    y_ref[0] = y


def _out_proj(oa, ob, oc, sz, sg, x3, gate3, wb, wo, fw, tm, final):
    g, r, d = x3.shape
    rm = gate3.shape[1]
    tmod = 1 if rm == 1 else tm
    mod_map = (lambda b, i: (b, 0, 0)) if rm == 1 else (lambda b, i: (b, i, 0))
    row = lambda w: pl.BlockSpec((1, tm, w), lambda b, i: (b, i, 0))
    as3 = lambda a: a.reshape(g, r, a.shape[-1])
    return pl.pallas_call(
        functools.partial(_out_kernel, final=final),
        grid=(g, r // tm),
        in_specs=[
            row(MIX_WIDTH), row(MIX_WIDTH), row(MIX_WIDTH), row(N_MIXERS * MIX_WIDTH), row(N_MIXERS * d), row(d),
            pl.BlockSpec((1, tmod, d), mod_map),
            pl.BlockSpec(wb.shape, lambda b, i: (0, 0, 0)),
            pl.BlockSpec(wo.shape, lambda b, i: (0, 0)),
            pl.BlockSpec((1, d), lambda b, i: (0, 0)),
        ],
        out_specs=row(d),
        out_shape=jax.ShapeDtypeStruct((g, r, d), F32),
        compiler_params=_cp("arbitrary", "arbitrary"),
        name="out_proj",
    )(as3(oa), as3(ob), as3(oc), as3(sz), as3(sg), x3, gate3, wb, wo, fw.reshape(1, d))


def _rope_tables(pos):
    half = ROT_DIM // 2
    expo = jnp.arange(0, ROT_DIM, 2, dtype=F32) / ROT_DIM
    inv_freq = jnp.power(jnp.float32(ROPE_THETA), -expo)
    ang = pos.astype(F32)[:, None] * inv_freq[None, :]
    cos, sin = jnp.cos(ang), jnp.sin(ang)
    n = pos.shape[0]
    rest = HEAD_DIM - ROT_DIM
    a = jnp.concatenate([cos, cos, jnp.ones((n, rest), F32)], axis=1)
    bm = jnp.concatenate([-sin, jnp.zeros((n, half + rest), F32)], axis=1)
    cm = jnp.concatenate([jnp.zeros((n, half), F32), sin, jnp.zeros((n, rest), F32)], axis=1)
    rep = LANES // HEAD_DIM
    return tuple(jnp.tile(t, (1, rep)) for t in (a, bm, cm))


def _split_weights(w_in_l, b_f_l):
    o = np.cumsum([0, MIX_WIDTH, KV_WIDTH, KV_WIDTH, MIX_WIDTH,
                   MIX_WIDTH, KV_WIDTH, KV_WIDTH, N_HEADS, MIX_WIDTH,
                   MIX_WIDTH, KV_WIDTH, KV_WIDTH, IDX_HEADS * IDX_DIM, IDX_DIM, IDX_HEADS, MIX_WIDTH]).tolist()
    col = lambda i: w_in_l[:, o[i]:o[i + 1]]
    qa, ka, va, za, qb, kb, vb, fb, zb, qc, kc, vc, qi, ki, wi, zc = (col(i) for i in range(16))
    g = w_in_l[:, o[16]:]
    d = w_in_l.shape[0]
    zpad = lambda w: jnp.zeros((d, w), w_in_l.dtype)
    w_q = jnp.concatenate([qa, qb, qc], axis=1)
    w_kv = jnp.concatenate([ka, va, kb, vb, kc, vc], axis=1)
    w_z = jnp.concatenate([za, zb, zc], axis=1)
    w_misc = jnp.concatenate([qi, ki, zpad(LANES - IDX_DIM), wi, zpad(LANES - IDX_HEADS),
                              fb, zpad(LANES - N_HEADS)], axis=1)
    bf_row = jnp.concatenate([b_f_l, jnp.zeros((LANES - N_HEADS,), b_f_l.dtype)]).reshape(1, LANES)
    cast = lambda w: w.astype(BF16)
    return cast(w_q), cast(w_kv), cast(w_z), cast(g), cast(w_misc), bf_row.astype(F32)


def _project(x3, scale3, shift3, tabs, weights, norm_w_l, tm, kv_out=None):
    g, r, d = x3.shape
    w_q, w_kv, w_z, w_g, w_misc, bf_row = weights
    h = _norm_mod(x3, norm_w_l, scale3, shift3, tm).reshape(g * r, d)
    q_outs = _proj_call(
        _projq_kernel, h, w_q, tabs, (), [(MIX_WIDTH, BF16, True)] * 3 + [(MIX_WIDTH, F32, True)],
        tm, w_q.shape[1], "proj_q")
    if kv_out is None:
        kv, kvb = _proj_call(
            _projkv_kernel, h, w_kv, tabs, (), [(w_kv.shape[1], F32, True), (w_kv.shape[1], BF16, True)],
            tm, w_kv.shape[1], "proj_kv")
    else:
        kvt_prev, layer, depth = kv_out
        kvt, kvb, km = _proj_kv_prompt(h, w_kv, tabs, kvt_prev, layer, depth, g, r, tm)
        kv = (kvt, km)
    (sz,) = _proj_call(_projz_kernel, h, w_z, (), (), [(w_z.shape[1], BF16, True)], tm, w_z.shape[1], "proj_z")
    (sg,) = _proj_call(_projg_kernel, h, w_g, (), (), [(w_g.shape[1], BF16, False)], tm, d, "proj_g")
    misc = _proj_call(
        _projmisc_kernel, h, w_misc, tabs, (bf_row,),
        [(MIX_WIDTH, BF16, True), (IDX_DIM, F32, True), (2 * IDX_DIM, BF16, True),
         (IDX_HEADS, F32, True), (N_HEADS, F32, True)],
        tm, MISC_N, "proj_misc")
    return q_outs, kv, kvb, sz, sg, misc


def _prompt_layer(x3, scale3, shift3, gate3, tabs, weights, norm_w_l, wb_l, wo_l, fw, kvt_prev, layer, depth,
                  *, tm, tm_out, tq, tk, final):
    b, t, _ = x3.shape
    (qa, qb, qc, qaf), (kvt, km8), kvb, sz, sg, (qi, ki, ki2, wi, lf) = _project(
        x3, scale3, shift3, tabs, weights, norm_w_l, tm, kv_out=(kvt_prev, layer, depth))
    s3 = lambda a: a.reshape(b, t, a.shape[-1])
    kvb3, lf3 = s3(kvb), s3(lf)
    km = km8[:, :tm // MOBA_BLOCK].reshape(b, t // MOBA_BLOCK, KV_WIDTH)
    km = jnp.pad(km, ((0, 0), (0, KMEAN_ROWS - km.shape[1]), (0, 0)))
    cum = _cumsum(jnp.swapaxes(lf3, 1, 2))
    o_a = _moba_attn(s3(qaf), s3(qa), km, kvb3, tq)
    o_b = _fox_attn(s3(qb), cum, kvb3, tq, tk)
    o_c = _dsa_attn(s3(qi), s3(wi), s3(ki2), s3(qc), kvb3, tq, tk)
    x_new = _out_proj(o_a, o_b, o_c, sz, sg, x3, gate3, wb_l, wo_l, fw, tm_out, final)
    return x_new, kvt, lf3, s3(ki)


def _sample_layer(x3, scale3, shift3, gate3, tabs, weights, norm_w_l, wb_l, wo_l, fw, caches, page_table, layer,
                  *, nseq, final):
    _, m, _ = x3.shape
    t = m // nseq
    cache_t, cache_kxt, cache_lft = caches
    (qa, qb, qc, qaf), kv, kvb, sz, sg, (qi, ki, ki2, wi, lf) = _project(
        x3, scale3, shift3, tabs, weights, norm_w_l, m)
    s3 = lambda a: a.reshape(nseq, t, a.shape[-1])
    pad_slots = lambda a: jnp.pad(a, ((0, 0), (0, PAGE_SIZE - t), (0, 0)))
    lp = (page_table.shape[1] + PAGES_PER_STEP) * PAGE_SIZE
    isc = _sample_index(cache_kxt, page_table, layer, s3(qi), s3(wi), pad_slots(s3(ki2)), lp)
    bias = _topk_bias(isc.reshape(m, lp), page_table.shape[1] * PAGE_SIZE, t, min(m, 64)).reshape(nseq, t, lp)
    nlf = jnp.pad(jnp.swapaxes(s3(lf), 1, 2), ((0, 0), (0, 0), (0, PAGE_SIZE - t)))
    o_a, o_b, o_c = _sample_attn(cache_t, cache_lft, page_table, layer, bias, s3(qa), s3(qb), s3(qc), s3(qaf),
                                 pad_slots(s3(kvb)), nlf)
    x_new = _out_proj(o_a, o_b, o_c, sz, sg, x3, gate3, wb_l, wo_l, fw, m, final)
    return x_new, s3(kv), s3(lf), s3(ki)


def kernel(x_prompt, x_sample, cache_kv, cache_logf, cache_kidx, page_table, c_prompt, c_sample,
           norm_w, w_ada, b_ada, w_in, b_f, w_branch, w_out, final_norm_w):
    depth = norm_w.shape[0]
    bp, tp, d = x_prompt.shape
    bs, ts, _ = x_sample.shape
    n_pool = cache_kv.shape[1]
    n_pages = page_table.shape[1]
    past_len = n_pages * PAGE_SIZE
    assert ts == SUBLANES and n_pages % PAGES_PER_STEP == 0 and PAGES_PER_STEP % (MOBA_BLOCK // PAGE_SIZE) == 0
    assert tp % MOBA_BLOCK == 0 and tp % min(512, tp) == 0

    nc = bp + bs
    rc = -(-nc // SUBLANES) * SUBLANES
    c_all = jnp.concatenate([c_prompt, c_sample, jnp.zeros((rc - nc, d), F32)], axis=0)
    mod = _modulation(c_all, w_ada, b_ada)

    tabs_p = _rope_tables(jnp.arange(tp, dtype=jnp.int32))
    tabs_s = _rope_tables(jnp.tile(past_len + jnp.arange(ts, dtype=jnp.int32), bs))

    cache_t = jnp.transpose(cache_kv, (0, 1, 3, 4, 5, 6, 2)).reshape(depth, n_pool, N_MIXERS * MIXER_COLS, PAGE_SIZE)
    cache_kxt = jnp.swapaxes(cache_kidx, 2, 3)
    cache_lft = jnp.swapaxes(cache_logf, 2, 3)

    ms = bs * ts
    xp = x_prompt
    xs = x_sample.reshape(1, ms, d)
    new_p, new_s = [], []
    kvt = jnp.zeros((depth, bp, N_MIXERS * MIXER_COLS, tp), F32)
    for l in range(depth):
        weights = _split_weights(w_in[l], b_f[l])
        wb_l = w_branch[l].astype(BF16)
        wo_l = w_out[l].astype(BF16)
        final = l == depth - 1
        shift, scale, gate = (mod[l, :, i * d:(i + 1) * d] for i in range(3))
        p3 = lambda a: a[:bp].reshape(bp, 1, d)
        s3 = lambda a: jnp.repeat(a[bp:nc], ts, axis=0).reshape(1, ms, d)

        xp, kvt, lf_p, ki_p = _prompt_layer(
            xp, p3(scale), p3(shift), p3(gate), tabs_p, weights, norm_w[l], wb_l, wo_l, final_norm_w, kvt, l, depth,
            tm=min(512, tp), tm_out=min(512, tp), tq=min(128, tp), tk=MOBA_BLOCK, final=final)
        xs, kv_s, lf_s, ki_s = _sample_layer(
            xs, s3(scale), s3(shift), s3(gate), tabs_s, weights, norm_w[l], wb_l, wo_l, final_norm_w,
            (cache_t, cache_kxt, cache_lft), page_table, l, nseq=bs, final=final)
        new_p.append((None, lf_p, ki_p))
        new_s.append((kv_s, lf_s, ki_s))

    kv_shape = (N_MIXERS, 2, N_KV_HEADS, HEAD_DIM)
    stack = lambda items, i: jnp.stack([n[i] for n in items])
    kv_prompt = jnp.transpose(kvt.reshape(depth, bp, *kv_shape, tp), (0, 1, 6, 2, 3, 4, 5))
    kv_sample = stack(new_s, 0).reshape(depth, bs, ts, *kv_shape)
    return (xp, xs.reshape(bs, ts, d), kv_prompt, stack(new_p, 1), stack(new_p, 2),
            kv_sample, stack(new_s, 1), stack(new_s, 2))
```

```python
import functools

import numpy as np
import jax
import jax.numpy as jnp
from jax import lax
from jax.experimental import pallas as pl
from jax.experimental.pallas import tpu as pltpu

F32 = jnp.float32
BF16 = jnp.bfloat16
HIGHEST = lax.Precision.HIGHEST

N_MIXERS = 3
N_HEADS = 8
N_KV_HEADS = 4
HEAD_DIM = 64
MIX_WIDTH = N_HEADS * HEAD_DIM
KV_WIDTH = N_KV_HEADS * HEAD_DIM
MIXER_COLS = 2 * KV_WIDTH
ROT_DIM = HEAD_DIM // 4
ROPE_THETA = 500000.0
ATTN_SCALE = HEAD_DIM ** -0.5
LOG2E = 1.4426950408889634
MOBA_BLOCK = 256
MOBA_TOPK = 3
IDX_HEADS = 8
IDX_DIM = 64
IDX_SCALE = (IDX_DIM * IDX_HEADS) ** -0.5
DSA_TOPK = 256
NORM_EPS = 1e-6
PAGE_SIZE = 128

LANES = 128
SUBLANES = 8
KMEAN_ROWS = 128
NEG = -1e30
F32_LOWEST = -3.0e38
VMEM_LIMIT = 56 * 1024 * 1024
PAGES_PER_STEP = 8
INDEX_PAGES_PER_STEP = 16
DSA_SWEEP = 1024

_NT = (((1,), (1,)), ((), ()))


def _cp(*sem):
    return pltpu.CompilerParams(dimension_semantics=sem, vmem_limit_bytes=VMEM_LIMIT)


def _sigmoid(x):
    return 1.0 / (1.0 + jnp.exp(-x))


def _mod_kernel(c_ref, w_ref, b_ref, o_ref):
    c = c_ref[...]
    sc = c * _sigmoid(c)
    o_ref[0] = jnp.dot(sc, w_ref[0], precision=HIGHEST, preferred_element_type=F32) + b_ref[0]


def _modulation(c_all, w_ada, b_ada):
    depth, d, d3 = w_ada.shape
    rc = c_all.shape[0]
    nj = d3 // d
    return pl.pallas_call(
        _mod_kernel,
        grid=(depth, nj),
        in_specs=[
            pl.BlockSpec((rc, d), lambda l, j: (0, 0)),
            pl.BlockSpec((1, d, d), lambda l, j: (l, 0, j)),
            pl.BlockSpec((1, 1, d), lambda l, j: (l, 0, j)),
        ],
        out_specs=pl.BlockSpec((1, rc, d), lambda l, j: (l, 0, j)),
        out_shape=jax.ShapeDtypeStruct((depth, rc, d3), F32),
        compiler_params=_cp("arbitrary", "arbitrary"),
        name="adaln_mod",
    )(c_all, w_ada, b_ada.reshape(depth, 1, d3))


def _norm_kernel(x_ref, w_ref, sc_ref, sh_ref, h_ref):
    x = x_ref[0]
    ms = jnp.mean(x * x, axis=-1, keepdims=True)
    y = x * lax.rsqrt(ms + NORM_EPS) * w_ref[...]
    h_ref[0] = (y * (1.0 + sc_ref[0]) + sh_ref[0]).astype(BF16)


def _norm_mod(x3, norm_w, scale3, shift3, tm):
    g, r, d = x3.shape
    rm = scale3.shape[1]
    tmod = 1 if rm == 1 else tm
    mod_map = (lambda b, i: (b, 0, 0)) if rm == 1 else (lambda b, i: (b, i, 0))
    return pl.pallas_call(
        _norm_kernel,
        grid=(g, r // tm),
        in_specs=[
            pl.BlockSpec((1, tm, d), lambda b, i: (b, i, 0)),
            pl.BlockSpec((1, d), lambda b, i: (0, 0)),
            pl.BlockSpec((1, tmod, d), mod_map),
            pl.BlockSpec((1, tmod, d), mod_map),
        ],
        out_specs=pl.BlockSpec((1, tm, d), lambda b, i: (b, i, 0)),
        out_shape=jax.ShapeDtypeStruct((g, r, d), BF16),
        compiler_params=_cp("arbitrary", "arbitrary"),
        name="norm_mod",
    )(x3, norm_w.reshape(1, d), scale3, shift3)


def _rope_chunk(x, a, bm, cm):
    return x * a + pltpu.roll(x, LANES - ROT_DIM // 2, 1) * bm + pltpu.roll(x, ROT_DIM // 2, 1) * cm


def _projq_kernel(h_ref, w_ref, ra_ref, rb_ref, rc_ref, qa_ref, qb_ref, qc_ref, qaf_ref):
    y = jnp.dot(h_ref[...], w_ref[...], preferred_element_type=F32)
    a, bm, cm = ra_ref[...], rb_ref[...], rc_ref[...]
    for m, o_ref in enumerate((qa_ref, qb_ref, qc_ref)):
        for c in range(MIX_WIDTH // LANES):
            x = y[:, m * MIX_WIDTH + c * LANES:m * MIX_WIDTH + (c + 1) * LANES]
            if m != 1:
                x = _rope_chunk(x, a, bm, cm)
            if m == 0:
                qaf_ref[:, c * LANES:(c + 1) * LANES] = x
            o_ref[:, c * LANES:(c + 1) * LANES] = (x * (ATTN_SCALE * LOG2E)).astype(BF16)


def _projkv_kernel(h_ref, w_ref, ra_ref, rb_ref, rc_ref, kv_ref, kvb_ref):
    y = jnp.dot(h_ref[...], w_ref[...], preferred_element_type=F32)
    a, bm, cm = ra_ref[...], rb_ref[...], rc_ref[...]
    per_mixer = MIXER_COLS // LANES
    for c in range(N_MIXERS * per_mixer):
        x = y[:, c * LANES:(c + 1) * LANES]
        mixer, within = divmod(c, per_mixer)
        if mixer != 1 and within < KV_WIDTH // LANES:
            x = _rope_chunk(x, a, bm, cm)
        kv_ref[:, c * LANES:(c + 1) * LANES] = x
        kvb_ref[:, c * LANES:(c + 1) * LANES] = x.astype(BF16)


def _projkv_prompt_kernel(h_ref, w_ref, ra_ref, rb_ref, rc_ref, *refs):
    kvt_ref, kvb_ref, km_ref = refs[-3:]
    y = jnp.dot(h_ref[...], w_ref[...], preferred_element_type=F32)
    a, bm, cm = ra_ref[...], rb_ref[...], rc_ref[...]
    tm = y.shape[0]
    per_mixer = MIXER_COLS // LANES
    moba_k = []
    for c in range(N_MIXERS * per_mixer):
        x = y[:, c * LANES:(c + 1) * LANES]
        mixer, within = divmod(c, per_mixer)
        if mixer != 1 and within < KV_WIDTH // LANES:
            x = _rope_chunk(x, a, bm, cm)
        if mixer == 0 and within < KV_WIDTH // LANES:
            moba_k.append(x)
        kvb_ref[:, c * LANES:(c + 1) * LANES] = x.astype(BF16)
        kvt_ref[0, 0, c * LANES:(c + 1) * LANES, :] = x.T
    means = [jnp.concatenate([jnp.sum(x[g * MOBA_BLOCK:(g + 1) * MOBA_BLOCK], axis=0, keepdims=True)
                              for x in moba_k], axis=1) * (1.0 / MOBA_BLOCK) for g in range(tm // MOBA_BLOCK)]
    km_ref[0] = jnp.concatenate(means + [jnp.zeros((SUBLANES - len(means), KV_WIDTH), F32)], axis=0)


def _proj_kv_prompt(h, w, tabs, kvt_prev, layer, depth, b, t, tm):
    m, d = h.shape
    n = w.shape[1]
    nt = t // tm
    assert tm % MOBA_BLOCK == 0 and tm // MOBA_BLOCK <= SUBLANES
    in_specs = [pl.BlockSpec((tm, d), lambda j, i: (i, 0)), pl.BlockSpec((d, n), lambda j, i: (0, 0))]
    in_specs += [pl.BlockSpec((tm, LANES), lambda j, i: (i % nt, 0)) for _ in tabs]
    in_specs.append(pl.BlockSpec(memory_space=pl.ANY))
    args = [h, w, *tabs, kvt_prev]
    aliases = {len(args) - 1: 0}
    return pl.pallas_call(
        _projkv_prompt_kernel,
        grid=(1, m // tm),
        in_specs=in_specs,
        out_specs=[pl.BlockSpec((1, 1, n, tm), lambda j, i: (layer, i // nt, 0, i % nt)),
                   pl.BlockSpec((tm, n), lambda j, i: (i, 0)),
                   pl.BlockSpec((1, SUBLANES, KV_WIDTH), lambda j, i: (i, 0, 0))],
        out_shape=[jax.ShapeDtypeStruct((depth, b, n, t), F32), jax.ShapeDtypeStruct((m, n), BF16),
                   jax.ShapeDtypeStruct((m // tm, SUBLANES, KV_WIDTH), F32)],
        input_output_aliases=aliases,
        compiler_params=_cp("arbitrary", "arbitrary"),
        name="proj_kv_prompt",
    )(*args)


def _projz_kernel(h_ref, w_ref, o_ref):
    y = jnp.dot(h_ref[...], w_ref[...], preferred_element_type=F32)
    o_ref[...] = (y * _sigmoid(y)).astype(o_ref.dtype)


def _projg_kernel(h_ref, w_ref, o_ref):
    y = jnp.dot(h_ref[...], w_ref[...], preferred_element_type=F32)
    o_ref[...] = _sigmoid(y).astype(o_ref.dtype)


MISC_KI = MIX_WIDTH
MISC_WI = MISC_KI + LANES
MISC_FB = MISC_WI + LANES
MISC_N = MISC_FB + LANES


def _projmisc_kernel(h_ref, w_ref, ra_ref, rb_ref, rc_ref, bf_ref,
                     qi_ref, ki_ref, ki2_ref, wi_ref, lf_ref):
    y = jnp.dot(h_ref[...], w_ref[...], preferred_element_type=F32)
    a, bm, cm = ra_ref[...], rb_ref[...], rc_ref[...]
    for c in range(MIX_WIDTH // LANES):
        x = _rope_chunk(y[:, c * LANES:(c + 1) * LANES], a, bm, cm)
        qi_ref[:, c * LANES:(c + 1) * LANES] = x.astype(BF16)
    ki = _rope_chunk(y[:, MISC_KI:MISC_KI + LANES], a, bm, cm)
    ki_ref[...] = ki[:, :IDX_DIM]
    ki2_ref[...] = (ki + pltpu.roll(ki, IDX_DIM, 1)).astype(BF16)
    wi_ref[...] = y[:, MISC_WI:MISC_WI + IDX_HEADS]
    f = y[:, MISC_FB:MISC_FB + LANES] + bf_ref[...]
    logf = jnp.minimum(f, 0.0) - jnp.log(1.0 + jnp.exp(-jnp.abs(f)))
    lf_ref[...] = logf[:, :N_HEADS]


def _proj_call(kernel, h, w, tabs, extra, outs, tm, tn, name):
    m, d = h.shape
    n = w.shape[1]
    nt = tabs[0].shape[0] // tm if tabs else 1
    in_specs = [pl.BlockSpec((tm, d), lambda j, i: (i, 0)),
                pl.BlockSpec((d, tn), lambda j, i: (0, j))]
    in_specs += [pl.BlockSpec((tm, LANES), lambda j, i: (i % nt, 0)) for _ in tabs]
    in_specs += [pl.BlockSpec(e.shape, lambda j, i: (0, 0)) for e in extra]
    out_specs = [pl.BlockSpec((tm, wd if full else tn), (lambda j, i: (i, 0)) if full else (lambda j, i: (i, j)))
                 for (wd, _, full) in outs]
    out_shape = [jax.ShapeDtypeStruct((m, wd), dt) for (wd, dt, _) in outs]
    return pl.pallas_call(
        kernel,
        grid=(n // tn, m // tm),
        in_specs=in_specs,
        out_specs=out_specs,
        out_shape=out_shape,
        compiler_params=_cp("arbitrary", "arbitrary"),
        name=name,
    )(h, w, *tabs, *extra)


CUM_CHUNK = 256


def _cumsum_kernel(x_ref, o_ref):
    n = x_ref.shape[2] // CUM_CHUNK
    r = lax.broadcasted_iota(jnp.int32, (CUM_CHUNK, CUM_CHUNK), 0)
    c = lax.broadcasted_iota(jnp.int32, (CUM_CHUNK, CUM_CHUNK), 1)
    tri = (r <= c).astype(F32)

    def body(i, carry):
        st = pl.multiple_of(i * CUM_CHUNK, CUM_CHUNK)
        x = x_ref[0, :, pl.ds(st, CUM_CHUNK)]
        y = jnp.dot(x, tri, precision=HIGHEST, preferred_element_type=F32) + carry
        o_ref[0, :, pl.ds(st, CUM_CHUNK)] = y * LOG2E
        return y[:, CUM_CHUNK - 1:CUM_CHUNK]

    lax.fori_loop(0, n, body, jnp.zeros((x_ref.shape[1], 1), F32))


def _cumsum(lft):
    b, h, lp = lft.shape
    return pl.pallas_call(
        _cumsum_kernel,
        grid=(b,),
        in_specs=[pl.BlockSpec((1, h, lp), lambda i: (i, 0, 0))],
        out_specs=pl.BlockSpec((1, h, lp), lambda i: (i, 0, 0)),
        out_shape=jax.ShapeDtypeStruct((b, h, lp), F32),
        compiler_params=_cp("arbitrary"),
        name="fox_cumsum",
    )(lft)


def _stack4(blk0, blk1):
    lane = lax.broadcasted_iota(jnp.int32, blk0.shape, 1)
    lo = lane < HEAD_DIM
    return jnp.concatenate([
        jnp.where(lo, blk0, 0.0),
        jnp.where(lo, pltpu.roll(blk0, HEAD_DIM, 1), 0.0),
        jnp.where(lo, 0.0, pltpu.roll(blk1, HEAD_DIM, 1)),
        jnp.where(lo, 0.0, blk1)], axis=0)


def _unstack4(o, tq):
    lane = lax.broadcasted_iota(jnp.int32, (tq, LANES), 1)
    lo = lane < HEAD_DIM
    b0 = jnp.where(lo, o[:tq], pltpu.roll(o[tq:2 * tq], HEAD_DIM, 1))
    b1 = jnp.where(lo, pltpu.roll(o[2 * tq:3 * tq], HEAD_DIM, 1), o[3 * tq:])
    return b0, b1


def _stacked_queries(q_ref):
    return [_stack4(q_ref[0, :, (2 * c) * LANES:(2 * c + 1) * LANES].astype(F32),
                    q_ref[0, :, (2 * c + 1) * LANES:(2 * c + 2) * LANES].astype(F32)) for c in range(2)]


def _flash_scratch(tq, tk):
    rows = 4 * tq
    return [pltpu.VMEM((2, rows, tk), F32), pltpu.VMEM((2, rows, tk), BF16),
            pltpu.VMEM((2, rows, LANES), F32), pltpu.VMEM((2, rows, LANES), F32), pltpu.VMEM((2, rows, LANES), F32),
            pltpu.VMEM((2, rows, 2 * LANES), F32), pltpu.VMEM((2, rows, tk), F32)]


def _flash_pipeline(n_all, qs, k_ref, v_ref, add_bias, diag_mask, scratch, o_ref, tq, tk, k_aug=None,
                    tile_bias=None):
    s_ref, p_ref, m0_ref, m1_ref, mx_ref, acc_ref, mb_ref = scratch
    m_slots = (m0_ref, m1_ref)
    m0_ref[...] = jnp.full(m0_ref.shape, -jnp.inf, F32)
    acc_ref[...] = jnp.zeros(acc_ref.shape, F32)
    if diag_mask is not None:
        @pl.when((pl.program_id(0) == 0) & (pl.program_id(1) == 0))
        def _():
            mb_ref[0] = jnp.zeros(mb_ref.shape[1:], F32)
        mb_ref[1] = diag_mask
    last = n_all - 1
    ones = jnp.ones((tk, LANES), BF16)

    def stage_pv(i, par):
        jc = jnp.clip(i - 2, 0, last)
        stc = pl.multiple_of(jc * tk, tk)
        for c in range(2):
            vo = jnp.concatenate([v_ref[0, pl.ds(stc, tk), c * LANES:(c + 1) * LANES], ones], axis=1)
            al = jnp.exp2(m_slots[par][c] - m_slots[1 - par][c])
            acc_ref[c] = (jnp.concatenate([al, al], axis=1) * acc_ref[c]
                          + jnp.dot(p_ref[c], vo, preferred_element_type=F32))

    def stage_softmax(par):
        for c in range(2):
            m_new = jnp.maximum(m_slots[1 - par][c], jnp.max(mx_ref[c], axis=-1, keepdims=True))
            m_slots[par][c] = m_new
            p_ref[c] = jnp.exp2(s_ref[c] - jnp.concatenate([m_new] * (tk // LANES), axis=1)).astype(BF16)

    def stage_scores(i):
        ja = jnp.minimum(i, last)
        sta = pl.multiple_of(ja * tk, tk)
        pen = jnp.where(i <= last, 0.0, NEG).astype(F32)
        shared = None if tile_bias is None else tile_bias(ja, pen)
        for c in range(2):
            kt = k_ref[0, pl.ds(sta, tk), c * LANES:(c + 1) * LANES]
            if k_aug is not None:
                kt = jnp.concatenate([kt, k_aug(c, ja, i <= last)], axis=1)
            s = lax.dot_general(qs[c], kt, _NT, preferred_element_type=F32)
            s = add_bias(c, s, ja, pen)
            if shared is not None:
                s = s + jnp.concatenate([shared] * 4, axis=0)
            if diag_mask is not None:
                s = s + mb_ref[(ja == last).astype(jnp.int32)]
            s_ref[c] = s
            mx_ref[c] = functools.reduce(jnp.maximum, [s[:, k * LANES:(k + 1) * LANES] for k in range(tk // LANES)])

    def step(i, par):
        stage_pv(i, par)
        stage_softmax(par)
        stage_scores(i)

    def step2(ii, carry):
        step(2 * ii + 2, 0)
        step(2 * ii + 3, 1)
        return carry

    stage_scores(0)
    stage_softmax(1)
    stage_scores(1)
    lax.fori_loop(0, (n_all + 1) // 2, step2, 0)
    for c in range(2):
        b0, b1 = _unstack4(acc_ref[c, :, :LANES] / acc_ref[c, :, LANES:], tq)
        o_ref[0, :, (2 * c) * LANES:(2 * c + 1) * LANES] = b0.astype(o_ref.dtype)
        o_ref[0, :, (2 * c + 1) * LANES:(2 * c + 2) * LANES] = b1.astype(o_ref.dtype)


def _diag_mask(q_lo, n_all, tq, tk):
    r1 = lax.broadcasted_iota(jnp.int32, (tq, 1), 0) + q_lo
    qpos = jnp.concatenate([r1] * 4, axis=0)
    kpos = (n_all - 1) * tk + lax.broadcasted_iota(jnp.int32, (4 * tq, tk), 1)
    return jnp.where(kpos <= qpos, 0.0, NEG)


def _attn_call(kern, name, ins, in_specs, b, t, tq, scratch):
    return pl.pallas_call(
        kern,
        grid=(b, t // tq),
        in_specs=in_specs,
        out_specs=pl.BlockSpec((1, tq, MIX_WIDTH), lambda i, j: (i, j, 0)),
        out_shape=jax.ShapeDtypeStruct((b, t, MIX_WIDTH), BF16),
        scratch_shapes=scratch,
        compiler_params=_cp("arbitrary", "arbitrary"),
        name=name,
    )(*ins)


_q_spec = lambda tq: pl.BlockSpec((1, tq, MIX_WIDTH), lambda i, j: (i, j, 0))
_kv_spec = lambda lp, col: pl.BlockSpec((1, lp, KV_WIDTH), lambda i, j: (i, 0, col))


def _moba_kernel(qf_ref, q_ref, km_ref, k_ref, v_ref, o_ref, *scratch, tq, topk, nblk):
    tk = MOBA_BLOCK
    q_lo = pl.program_id(1) * tq
    own = q_lo // MOBA_BLOCK
    rows = 4 * tq
    blk = lax.broadcasted_iota(jnp.int32, (nblk, rows), 0)
    blk_f = blk.astype(F32)
    q_st = _stacked_queries(q_ref)
    qs = []
    for c, qf in enumerate(_stacked_queries(qf_ref)):
        gate = lax.dot_general(km_ref[0, :nblk, c * LANES:(c + 1) * LANES], qf, _NT, precision=HIGHEST,
                               preferred_element_type=F32)
        gate = jnp.where(blk < own, gate, -jnp.inf)
        sb = jnp.where(blk == own, 0.0, NEG)
        for _ in range(topk):
            mx = jnp.max(gate, axis=0, keepdims=True)
            is_max = (gate == mx) & (mx > -jnp.inf)
            first = jnp.min(jnp.where(is_max, blk_f, 1e9), axis=0, keepdims=True)
            pick = blk_f == first
            sb = jnp.where(pick, 0.0, sb)
            gate = jnp.where(pick, -jnp.inf, gate)
        sb = jnp.concatenate([sb, jnp.full((LANES - nblk, rows), NEG, F32)], axis=0)
        qs.append(jnp.concatenate([q_st[c], sb.T], axis=1).astype(BF16))

    klane = lax.broadcasted_iota(jnp.int32, (tk, LANES), 1)

    def k_aug(c, j, valid):
        return jnp.where(klane == jnp.where(valid, j, LANES - 1), 1.0, 0.0).astype(BF16)

    _flash_pipeline(own + 1, qs, k_ref, v_ref, lambda c, s, j, pen: s, _diag_mask(q_lo, own + 1, tq, tk), scratch,
                    o_ref, tq, tk, k_aug=k_aug)


def _moba_attn(qf, q, km, kvb, tq):
    b, t, _ = q.shape
    lp = kvb.shape[1]
    nblk = -(-(lp // MOBA_BLOCK) // SUBLANES) * SUBLANES
    assert nblk < LANES
    kern = functools.partial(_moba_kernel, tq=tq, topk=min(MOBA_TOPK, lp // MOBA_BLOCK), nblk=nblk)
    specs = [_q_spec(tq), _q_spec(tq), pl.BlockSpec((1, KMEAN_ROWS, KV_WIDTH), lambda i, j: (i, 0, 0)),
             _kv_spec(lp, 0), _kv_spec(lp, 1)]
    return _attn_call(kern, "moba_attn", (qf, q, km, kvb, kvb), specs, b, t, tq, _flash_scratch(tq, MOBA_BLOCK))


def _fox_kernel(q_ref, cum_ref, k_ref, v_ref, o_ref, *scratch, tq, tk):
    q_lo = pl.program_id(1) * tq
    n_all = (q_lo + tq + tk - 1) // tk
    qs = [q.astype(BF16) for q in _stacked_queries(q_ref)]

    def add_bias(c, s, j, pen):
        st = pl.multiple_of(j * tk, tk)
        parts = [s[hh * tq:(hh + 1) * tq] - (cum_ref[0, 4 * c + hh:4 * c + hh + 1, pl.ds(st, tk)] - pen)
                 for hh in range(4)]
        return jnp.concatenate(parts, axis=0)

    _flash_pipeline(n_all, qs, k_ref, v_ref, add_bias, _diag_mask(q_lo, n_all, tq, tk), scratch, o_ref, tq, tk)


def _fox_attn(q, cum, kvb, tq, tk):
    b, t, _ = q.shape
    lp = kvb.shape[1]
    kern = functools.partial(_fox_kernel, tq=tq, tk=tk)
    specs = [_q_spec(tq), pl.BlockSpec((1, N_HEADS, lp), lambda i, j: (i, 0, 0)), _kv_spec(lp, 2), _kv_spec(lp, 3)]
    return _attn_call(kern, "fox_attn", (q, cum, kvb, kvb), specs, b, t, tq, _flash_scratch(tq, tk))


def _key_to_float(key):
    bits = jnp.where(key < 0, key & jnp.int32(0x7FFFFFFF), ~key)
    return lax.bitcast_convert_type(bits, F32)


def _topk_threshold(count, vshape, need, few, idx_bits):
    def bit_body(i, key):
        cand = key | lax.shift_left(jnp.int32(1), 31 - i)
        thr_c = _key_to_float(cand)
        return jnp.where(count(lambda x, idx: x >= thr_c) >= need, cand, key)

    key = lax.fori_loop(0, 32, bit_body, jnp.zeros(vshape, jnp.int32))
    thr = _key_to_float(key)
    cnt_ge = count(lambda x, idx: x >= thr)
    has_ties = jnp.max(jnp.where(few, 0.0, cnt_ge - need)) > 0.0

    def tie_cut(_):
        r = need - count(lambda x, idx: x > thr)

        def jb(i, cut):
            cand = cut | lax.shift_left(jnp.int32(1), idx_bits - 1 - i)
            cnt = count(lambda x, idx: (x == thr) & (idx < cand))
            return jnp.where(cnt < r, cand, cut)

        return lax.fori_loop(0, idx_bits, jb, jnp.zeros(vshape, jnp.int32))

    cut = lax.cond(has_ties, tie_cut, lambda _: jnp.full(vshape, 2 ** 30, jnp.int32), 0)
    return jnp.where(few, F32_LOWEST, thr), jnp.where(few, 2 ** 30, cut)


def _dsa_kernel(qi_ref, w_ref, kx_ref, q_ref, k_ref, v_ref, o_ref, isc_ref, *scratch, tq, tk, sweep,
                topk, idx_bits):
    q_lo = pl.program_id(1) * tq
    n_all = (q_lo + tq + tk - 1) // tk
    lane = lax.broadcasted_iota(jnp.int32, (tq, LANES), 1)
    lo = lane < HEAD_DIM
    qpos = lax.broadcasted_iota(jnp.int32, (1, tq), 1) + q_lo
    krow = lax.broadcasted_iota(jnp.int32, (tk, 1), 0)

    pieces = []
    for h in range(IDX_HEADS):
        chunk = qi_ref[0, :, (h // 2) * LANES:(h // 2 + 1) * LANES].astype(F32)
        pieces.append(jnp.where(lo, chunk, 0.0) if h % 2 == 0 else jnp.where(lo, 0.0, chunk))
    qstack_t = jnp.concatenate(pieces, axis=0).T.astype(BF16)
    wrow = jnp.concatenate([w_ref[0, h:h + 1, :] for h in range(IDX_HEADS)], axis=1)

    def idx_tile(j):
        start = pl.multiple_of(j * tk, tk)
        sc = jnp.dot(kx_ref[0, pl.ds(start, tk), :], qstack_t, preferred_element_type=F32)
        contrib = jnp.maximum(sc, 0.0) * wrow
        isc = contrib[:, :tq]
        for h in range(1, IDX_HEADS):
            isc = isc + contrib[:, h * tq:(h + 1) * tq]
        isc_ref[pl.ds(start, tk), :] = jnp.where(j * tk + krow <= qpos, isc * IDX_SCALE, -jnp.inf)

    def idx_body(jj, carry):
        idx_tile(2 * jj)
        idx_tile(2 * jj + 1)
        return carry

    n_idx = (n_all + 1) // 2 * 2
    lax.fori_loop(0, n_idx // 2, idx_body, 0)

    tiles_per_sweep = sweep // tk
    n_sweeps = (n_all + tiles_per_sweep - 1) // tiles_per_sweep

    def pad_body(j, carry):
        isc_ref[pl.ds(pl.multiple_of(j * tk, tk), tk), :] = jnp.full((tk, tq), -jnp.inf, F32)
        return carry

    lax.fori_loop(n_idx, n_sweeps * tiles_per_sweep, pad_body, 0)
    grp = 8 * SUBLANES
    sub = lax.broadcasted_iota(jnp.int32, (grp, tq), 0)

    def count(pred):
        def body(j, acc):
            start = pl.multiple_of(j * sweep, sweep)
            x = isc_ref[pl.ds(start, sweep), :]
            for r in range(sweep // grp):
                idx = j * sweep + r * grp + sub
                acc = acc + jnp.where(pred(x[r * grp:(r + 1) * grp], idx), 1.0, 0.0)
            return acc
        acc = lax.fori_loop(0, n_sweeps, body, jnp.zeros((grp, tq), F32))
        return jnp.sum(acc, axis=0, keepdims=True)

    few = qpos + 1 <= topk
    thr, cut = _topk_threshold(count, (1, tq), jnp.float32(topk), few, idx_bits)

    qs = [q.astype(BF16) for q in _stacked_queries(q_ref)]

    def tile_bias(j, pen):
        x = isc_ref[pl.ds(pl.multiple_of(j * tk, tk), tk), :]
        keep = (x > thr) | ((x == thr) & (j * tk + krow <= cut))
        return jnp.where(keep, 0.0, NEG).T + pen

    _flash_pipeline(n_all, qs, k_ref, v_ref, lambda c, s, j, pen: s, None, scratch, o_ref, tq, tk,
                    tile_bias=tile_bias)


def _dsa_attn(qi, wi, kx2, q, kvb, tq, tk):
    b, t, _ = q.shape
    lp = kvb.shape[1]
    sweep = max(tk, min(DSA_SWEEP, lp))
    assert lp % sweep == 0 and sweep % tk == 0 and lp % (2 * tk) == 0
    kern = functools.partial(_dsa_kernel, tq=tq, tk=tk, sweep=sweep, topk=min(DSA_TOPK, lp // 4),
                             idx_bits=max(1, int(lp).bit_length()))
    specs = [_q_spec(tq), pl.BlockSpec((1, IDX_HEADS, tq), lambda i, j: (i, 0, j)),
             pl.BlockSpec((1, lp, 2 * IDX_DIM), lambda i, j: (i, 0, 0)), _q_spec(tq), _kv_spec(lp, 4), _kv_spec(lp, 5)]
    scratch = [pltpu.VMEM((lp, tq), F32)] + _flash_scratch(tq, tk)
    return _attn_call(kern, "dsa_attn", (qi, jnp.swapaxes(wi, 1, 2), kx2, q, kvb, kvb), specs, b, t, tq, scratch)


def _sample_index_kernel(pt_ref, *refs, ppi, n_steps, l_past, t_new):
    del pt_ref
    kx_refs = refs[:ppi]
    qi_ref, w_ref, nkx_ref, o_ref, isc_ref = refs[ppi:]
    j = pl.program_id(1)
    lp = isc_ref.shape[1]
    lane = lax.broadcasted_iota(jnp.int32, (t_new, LANES), 1)
    lo = lane < HEAD_DIM
    pieces, wpieces = [], []
    for h in range(IDX_HEADS):
        chunk = qi_ref[0, :, (h // 2) * LANES:(h // 2 + 1) * LANES].astype(F32)
        pieces.append(jnp.where(lo, chunk, 0.0) if h % 2 == 0 else jnp.where(lo, 0.0, chunk))
        wpieces.append(jnp.broadcast_to(w_ref[0, :, h:h + 1], (t_new, LANES)))
    qstack = jnp.concatenate(pieces, axis=0).astype(BF16)
    wb = jnp.concatenate(wpieces, axis=0)

    def head_sum(sc):
        contrib = jnp.maximum(sc, 0.0) * jnp.concatenate([wb] * (sc.shape[1] // LANES), axis=1)
        isc = contrib[:t_new]
        for h in range(1, IDX_HEADS):
            isc = isc + contrib[h * t_new:(h + 1) * t_new]
        return isc * IDX_SCALE

    kxt = jnp.concatenate([r[0, 0] for r in kx_refs], axis=1)
    rhs = jnp.concatenate([kxt, kxt], axis=0).astype(BF16)
    start = pl.multiple_of(j * (ppi * PAGE_SIZE), ppi * PAGE_SIZE)
    isc_ref[:, pl.ds(start, ppi * PAGE_SIZE)] = head_sum(jnp.dot(qstack, rhs, preferred_element_type=F32))

    @pl.when(j == n_steps - 1)
    def _():
        scn = lax.dot_general(qstack, nkx_ref[0], _NT, preferred_element_type=F32)
        row = lax.broadcasted_iota(jnp.int32, (t_new, LANES), 0)
        isc_ref[:, l_past:l_past + LANES] = jnp.where((lane <= row) & (lane < t_new), head_sum(scn), -jnp.inf)
        if lp > l_past + LANES:
            isc_ref[:, l_past + LANES:] = jnp.full((t_new, lp - l_past - LANES), -jnp.inf, F32)
        o_ref[0] = isc_ref[...]


def _topk_bias_kernel(x_ref, o_ref, *, sweep, topk, l_past, t_new, idx_bits):
    rows, lp = x_ref.shape
    lane = lax.broadcasted_iota(jnp.int32, (rows, LANES), 1)

    def count(pred):
        def body(j, acc):
            x = x_ref[:, pl.ds(pl.multiple_of(j * sweep, sweep), sweep)]
            for cc in range(sweep // LANES):
                idx = j * sweep + cc * LANES + lane
                acc = acc + jnp.where(pred(x[:, cc * LANES:(cc + 1) * LANES], idx), 1.0, 0.0)
            return acc
        acc = lax.fori_loop(0, lp // sweep, body, jnp.zeros((rows, LANES), F32))
        return jnp.sum(acc, axis=-1, keepdims=True)

    qpos1 = (lax.broadcasted_iota(jnp.int32, (rows, 1), 0) & (t_new - 1)) + l_past
    thr, cut = _topk_threshold(count, (rows, 1), jnp.float32(topk), qpos1 + 1 <= topk, idx_bits)
    kiota = lax.broadcasted_iota(jnp.int32, (rows, sweep), 1)

    def bias_body(j, carry):
        start = pl.multiple_of(j * sweep, sweep)
        x = x_ref[:, pl.ds(start, sweep)]
        keep = (x > thr) | ((x == thr) & (j * sweep + kiota <= cut))
        o_ref[:, pl.ds(start, sweep)] = jnp.where(keep, 0.0, NEG)
        return carry

    lax.fori_loop(0, lp // sweep, bias_body, 0)


def _topk_bias(isc, l_past, t_new, rows_per_step):
    rows, lp = isc.shape
    sweep = min(DSA_SWEEP, lp)
    assert lp % sweep == 0 and rows % rows_per_step == 0 and rows_per_step % t_new == 0
    kern = functools.partial(_topk_bias_kernel, sweep=sweep, topk=min(DSA_TOPK, (l_past + t_new) // 4),
                             l_past=l_past, t_new=t_new, idx_bits=max(1, int(lp).bit_length()))
    return pl.pallas_call(
        kern,
        grid=(rows // rows_per_step,),
        in_specs=[pl.BlockSpec((rows_per_step, lp), lambda i: (i, 0))],
        out_specs=pl.BlockSpec((rows_per_step, lp), lambda i: (i, 0)),
        out_shape=jax.ShapeDtypeStruct((rows, lp), F32),
        compiler_params=_cp("arbitrary"),
        name="dsa_topk_bias",
    )(isc)


def _sample_index(cache_kxt, page_table, layer, qi, wi, nkx2, lp):
    nseq, n_pages = page_table.shape
    t_new = qi.shape[1]
    ppi = INDEX_PAGES_PER_STEP if n_pages % INDEX_PAGES_PER_STEP == 0 else PAGES_PER_STEP
    n_steps = n_pages // ppi
    l_past = n_pages * PAGE_SIZE
    pg = lambda pp: (lambda i, j, pt: (layer, pt[i, j * ppi + pp], 0, 0))
    grid_spec = pltpu.PrefetchScalarGridSpec(
        num_scalar_prefetch=1,
        grid=(nseq, n_steps),
        in_specs=[pl.BlockSpec((1, 1, IDX_DIM, PAGE_SIZE), pg(pp)) for pp in range(ppi)] + [
            pl.BlockSpec((1, t_new, MIX_WIDTH), lambda i, j, pt: (i, 0, 0)),
            pl.BlockSpec((1, t_new, IDX_HEADS), lambda i, j, pt: (i, 0, 0)),
            pl.BlockSpec((1, PAGE_SIZE, 2 * IDX_DIM), lambda i, j, pt: (i, 0, 0)),
        ],
        out_specs=pl.BlockSpec((1, t_new, lp), lambda i, j, pt: (i, 0, 0)),
        scratch_shapes=[pltpu.VMEM((t_new, lp), F32)],
    )
    kern = functools.partial(_sample_index_kernel, ppi=ppi, n_steps=n_steps, l_past=l_past, t_new=t_new)
    return pl.pallas_call(
        kern,
        grid_spec=grid_spec,
        out_shape=jax.ShapeDtypeStruct((nseq, t_new, lp), F32),
        compiler_params=_cp("arbitrary", "arbitrary"),
        name="dsa_index_paged",
    )(page_table, *([cache_kxt] * ppi), qi, wi, nkx2)


def _block_diag_q(q):
    t = q.shape[0]
    lane = lax.broadcasted_iota(jnp.int32, (t, LANES), 1)
    lo = lane < HEAD_DIM
    zero = jnp.zeros((t, LANES), F32)
    out = []
    for h in range(N_HEADS):
        src = q[:, (h // 2) * LANES:(h // 2 + 1) * LANES]
        kvh = h // (N_HEADS // N_KV_HEADS)
        chunk, half = divmod(kvh, 2)
        piece = src if h % 2 == half else pltpu.roll(src, HEAD_DIM, 1)
        piece = jnp.where(lo, piece, 0.0) if half == 0 else jnp.where(lo, 0.0, piece)
        out.append(jnp.concatenate([piece, zero] if chunk == 0 else [zero, piece], axis=1))
    return jnp.concatenate(out, axis=0)


def _token_major(o, t):
    lane = lax.broadcasted_iota(jnp.int32, (t, LANES), 1)
    lo = lane < HEAD_DIM
    out = []
    for co in range(MIX_WIDTH // LANES):
        chunk, half = divmod(co, 2)
        x = o[(2 * co) * t:(2 * co + 1) * t, chunk * LANES:(chunk + 1) * LANES]
        y = o[(2 * co + 1) * t:(2 * co + 2) * t, chunk * LANES:(chunk + 1) * LANES]
        if half == 0:
            out.append(jnp.where(lo, x, pltpu.roll(y, HEAD_DIM, 1)))
        else:
            out.append(jnp.where(lo, pltpu.roll(x, HEAD_DIM, 1), y))
    return jnp.concatenate(out, axis=1)


def _lane_cumsum(x):
    lane = lax.broadcasted_iota(jnp.int32, x.shape, 1)
    d = 1
    while d < LANES:
        x = x + jnp.where(lane >= d, pltpu.roll(x, d, 1), 0.0)
        d *= 2
    return x


def _rows_from_heads(c, t):
    return jnp.concatenate([jnp.broadcast_to(c[h:h + 1], (t, c.shape[1])) for h in range(N_HEADS)], axis=0)


def _online_update(s, pv_fn, m_ref, l_ref, acc_ref):
    m_old = m_ref[...]
    m_new = jnp.maximum(m_old, jnp.max(s, axis=-1, keepdims=True))
    alpha = jnp.exp2(m_old - m_new)
    p = jnp.exp2(s - jnp.concatenate([m_new] * (s.shape[1] // LANES), axis=1))
    l_ref[...] = alpha * l_ref[...] + jnp.sum(p, axis=-1, keepdims=True)
    acc_ref[...] = jnp.concatenate([alpha] * (acc_ref.shape[1] // LANES), axis=1) * acc_ref[...] + pv_fn(p.astype(BF16))
    m_ref[...] = m_new


def _sample_attn_kernel(pt_ref, *refs, nb, pps, t_new, topk):
    del pt_ref
    pg_refs, lf_refs = refs[:pps], refs[pps:2 * pps]
    (bias_ref, qa_ref, qb_ref, qc_ref, qaf_ref, nkv_ref, nlf_ref, oa_ref, ob_ref, oc_ref,
     qbd_ref, qbdf_ref, km_ref, ms_ref, ls_ref, accs_ref,
     fm_ref, fl_ref, facc_ref, car_ref, dm_ref, dl_ref, dacc_ref, s_ref, vt_ref) = refs[2 * pps:]
    j = pl.program_id(1)
    rows = N_HEADS * t_new
    ppb = MOBA_BLOCK // PAGE_SIZE
    bps = pps // ppb
    n_steps = nb // bps
    lane = lax.broadcasted_iota(jnp.int32, (rows, LANES), 1)

    def init():
        for m, q_ref in enumerate((qa_ref, qb_ref, qc_ref)):
            qbd_ref[m] = _block_diag_q(q_ref[0].astype(F32)).astype(BF16)
        qbdf_ref[...] = _block_diag_q(qaf_ref[0])
        km_ref[...] = jnp.zeros(km_ref.shape, F32)
        ms_ref[...] = jnp.zeros(ms_ref.shape, F32)
        ls_ref[...] = jnp.zeros(ls_ref.shape, F32)
        car_ref[...] = jnp.zeros(car_ref.shape, F32)
        for m_ref, l_ref, acc_ref in ((fm_ref, fl_ref, facc_ref), (dm_ref, dl_ref, dacc_ref)):
            m_ref[...] = jnp.full(m_ref.shape, -jnp.inf, F32)
            l_ref[...] = jnp.zeros(l_ref.shape, F32)
            acc_ref[...] = jnp.zeros(acc_ref.shape, F32)

    def kt(m, off):
        lo_r = m * MIXER_COLS + off
        return jnp.concatenate([r[0, 0, lo_r:lo_r + KV_WIDTH, :] for r in pg_refs], axis=1)

    def score_stage():
        k0 = kt(0, 0)
        blk_lane = lax.broadcasted_iota(jnp.int32, km_ref.shape, 1)
        km = km_ref[...]
        for b in range(bps):
            cols = slice(b * MOBA_BLOCK, (b + 1) * MOBA_BLOCK)
            km = jnp.where(blk_lane == j * bps + b,
                           jnp.sum(k0[:, cols], axis=1, keepdims=True) * (1.0 / MOBA_BLOCK), km)
        km_ref[...] = km
        s_ref[0] = jnp.dot(qbd_ref[0], k0.astype(BF16), preferred_element_type=F32)
        carry, cums = car_ref[...][:, :1], []
        for r in lf_refs:
            cums.append(_lane_cumsum(r[0, 0]) + carry)
            carry = cums[-1][:, LANES - 1:LANES]
        car_ref[...] = jnp.broadcast_to(carry, car_ref.shape)
        s = jnp.dot(qbd_ref[1], kt(1, 0).astype(BF16), preferred_element_type=F32)
        s_ref[1] = s - _rows_from_heads(jnp.concatenate(cums, axis=1) * LOG2E, t_new)
        s = jnp.dot(qbd_ref[2], kt(2, 0).astype(BF16), preferred_element_type=F32)
        s_ref[2] = s + jnp.concatenate([bias_ref[0]] * N_HEADS, axis=0)
        for m in range(N_MIXERS):
            vt_ref[m] = kt(m, KV_WIDTH).astype(BF16)

    def update_stage(jp):
        s, v0 = s_ref[0], vt_ref[0]
        ms, ls = ms_ref[...], ls_ref[...]
        for b in range(bps):
            n = jp * bps + b
            cols = slice(b * MOBA_BLOCK, (b + 1) * MOBA_BLOCK)
            m_n = jnp.max(s[:, cols], axis=-1, keepdims=True)
            p = jnp.exp2(s[:, cols] - m_n)
            ms = jnp.where(lane == n, m_n, ms)
            ls = jnp.where(lane == n, jnp.sum(p, axis=-1, keepdims=True), ls)
            accs_ref[n] = lax.dot_general(p.astype(BF16), v0[:, cols], _NT, preferred_element_type=F32)
        ms_ref[...], ls_ref[...] = ms, ls
        pv_nt = lambda m: (lambda p: lax.dot_general(p, vt_ref[m], _NT, preferred_element_type=F32))
        _online_update(s_ref[1], pv_nt(1), fm_ref, fl_ref, facc_ref)
        _online_update(s_ref[2], pv_nt(2), dm_ref, dl_ref, dacc_ref)

    @pl.when(j == 0)
    def _():
        init()
        score_stage()

    @pl.when((j > 0) & (j < n_steps))
    def _():
        update_stage(j - 1)
        score_stage()

    @pl.when(j == n_steps)
    def _():
        update_stage(j - 1)
        nk = lambda m: nkv_ref[0, :, m * MIXER_COLS:m * MIXER_COLS + KV_WIDTH]
        nv = lambda m: nkv_ref[0, :, m * MIXER_COLS + KV_WIDTH:(m + 1) * MIXER_COLS]
        pv = lambda m: (lambda p: jnp.dot(p, nv(m), preferred_element_type=F32))
        row_t = lax.broadcasted_iota(jnp.int32, (rows, LANES), 0) & (t_new - 1)
        causal = (lane <= row_t) & (lane < t_new)

        s = lax.dot_general(qbd_ref[0], nk(0), _NT, preferred_element_type=F32)
        s = jnp.where(causal, s, NEG)
        m_o = jnp.max(s, axis=-1, keepdims=True)
        p = jnp.exp2(s - m_o)
        l_o = jnp.sum(p, axis=-1, keepdims=True)
        acc_o = pv(0)(p.astype(BF16))
        gate = jnp.dot(qbdf_ref[...], km_ref[...], precision=HIGHEST, preferred_element_type=F32)
        gate = jnp.where(lane < nb, gate, -jnp.inf)
        lane_f = lane.astype(F32)
        sel = lane < 0
        for _ in range(topk):
            mx = jnp.max(gate, axis=-1, keepdims=True)
            is_max = (gate == mx) & (mx > -jnp.inf)
            first = jnp.min(jnp.where(is_max, lane_f, 1e9), axis=-1, keepdims=True)
            pick = lane_f == first
            sel = sel | pick
            gate = jnp.where(pick, -jnp.inf, gate)
        ms = ms_ref[...]
        m_all = jnp.maximum(m_o, jnp.max(jnp.where(sel, ms, -jnp.inf), axis=-1, keepdims=True))
        w = jnp.where(sel, jnp.exp2(ms - m_all), 0.0)
        w_o = jnp.exp2(m_o - m_all)
        l_all = jnp.sum(w * ls_ref[...], axis=-1, keepdims=True) + w_o * l_o

        def merge(n, o):
            wn = jnp.sum(jnp.where(lane == n, w, 0.0), axis=-1, keepdims=True)
            return o + wn * accs_ref[n]

        o_a = lax.fori_loop(0, nb, merge, w_o * acc_o) / l_all
        oa_ref[0] = _token_major(o_a, t_new)

        cn = _lane_cumsum(nlf_ref[0]) + car_ref[...]
        s = lax.dot_general(qbd_ref[1], nk(1), _NT, preferred_element_type=F32) - _rows_from_heads(cn * LOG2E, t_new)
        _online_update(jnp.where(causal, s, NEG), pv(1), fm_ref, fl_ref, facc_ref)
        wide = lambda l_ref: jnp.concatenate([l_ref[...]] * (KV_WIDTH // LANES), axis=1)
        ob_ref[0] = _token_major(facc_ref[...] / wide(fl_ref), t_new)

        s = lax.dot_general(qbd_ref[2], nk(2), _NT, preferred_element_type=F32)
        s = s + jnp.concatenate([bias_ref[0, :, :LANES]] * N_HEADS, axis=0)
        _online_update(s, pv(2), dm_ref, dl_ref, dacc_ref)
        oc_ref[0] = _token_major(dacc_ref[...] / wide(dl_ref), t_new)


def _sample_attn(cache_t, cache_lft, page_table, layer, bias, qa, qb, qc, qaf, nkv, nlf):
    nseq, n_pages = page_table.shape
    t_new = qa.shape[1]
    nb = n_pages // (MOBA_BLOCK // PAGE_SIZE)
    pps = PAGES_PER_STEP
    rows = N_HEADS * t_new
    width = cache_t.shape[2]
    last = n_pages - 1
    page = lambda k: (lambda i, j, pt: (layer, pt[i, jnp.minimum(pps * j + k, last)], 0, 0))
    seq_blk = lambda shape: pl.BlockSpec((1,) + shape, lambda i, j, pt: (i, 0, 0))
    grid_spec = pltpu.PrefetchScalarGridSpec(
        num_scalar_prefetch=1,
        grid=(nseq, n_pages // pps + 1),
        in_specs=[pl.BlockSpec((1, 1, width, PAGE_SIZE), page(k)) for k in range(pps)] + [
            pl.BlockSpec((1, 1, N_HEADS, PAGE_SIZE), page(k)) for k in range(pps)] + [
            pl.BlockSpec((1, t_new, pps * PAGE_SIZE), lambda i, j, pt: (i, 0, j)),
            seq_blk((t_new, MIX_WIDTH)), seq_blk((t_new, MIX_WIDTH)), seq_blk((t_new, MIX_WIDTH)),
            seq_blk((t_new, MIX_WIDTH)),
            seq_blk((PAGE_SIZE, width)), seq_blk((N_HEADS, PAGE_SIZE)),
        ],
        out_specs=[seq_blk((t_new, MIX_WIDTH))] * 3,
        scratch_shapes=[
            pltpu.VMEM((N_MIXERS, rows, KV_WIDTH), BF16), pltpu.VMEM((rows, KV_WIDTH), F32),
            pltpu.VMEM((KV_WIDTH, KMEAN_ROWS), F32), pltpu.VMEM((rows, LANES), F32), pltpu.VMEM((rows, LANES), F32),
            pltpu.VMEM((nb, rows, KV_WIDTH), F32),
            pltpu.VMEM((rows, LANES), F32), pltpu.VMEM((rows, LANES), F32), pltpu.VMEM((rows, KV_WIDTH), F32),
            pltpu.VMEM((N_HEADS, LANES), F32),
            pltpu.VMEM((rows, LANES), F32), pltpu.VMEM((rows, LANES), F32), pltpu.VMEM((rows, KV_WIDTH), F32),
            pltpu.VMEM((N_MIXERS, rows, pps * PAGE_SIZE), F32), pltpu.VMEM((N_MIXERS, KV_WIDTH, pps * PAGE_SIZE), BF16),
        ],
    )
    kern = functools.partial(_sample_attn_kernel, nb=nb, pps=pps, t_new=t_new, topk=min(MOBA_TOPK, nb + 1))
    out = jax.ShapeDtypeStruct((nseq, t_new, MIX_WIDTH), F32)
    return pl.pallas_call(
        kern,
        grid_spec=grid_spec,
        out_shape=[out, out, out],
        compiler_params=_cp("arbitrary", "arbitrary"),
        name="sample_attn",
    )(page_table, *([cache_t] * pps), *([cache_lft] * pps), bias, qa, qb, qc, qaf, nkv, nlf)


def _out_kernel(oa_ref, ob_ref, oc_ref, sz_ref, sg_ref, x_ref, gate_ref, wb_ref, wo_ref, fw_ref, y_ref,
                *, final):
    d = x_ref.shape[2]
    merged = None
    for i, o_ref in enumerate((oa_ref, ob_ref, oc_ref)):
        t = (o_ref[0].astype(F32) * sz_ref[0, :, i * MIX_WIDTH:(i + 1) * MIX_WIDTH].astype(F32)).astype(BF16)
        br = sg_ref[0, :, i * d:(i + 1) * d].astype(F32) * jnp.dot(t, wb_ref[i], preferred_element_type=F32)
        merged = br if merged is None else merged + br
    y = x_ref[0] + gate_ref[0] * jnp.dot(merged.astype(BF16), wo_ref[...], preferred_element_type=F32)
    if final:
        ms = jnp.mean(y * y, axis=-1, keepdims=True)
        y = y * lax.rsqrt(ms + NORM_EPS) * fw_ref[...]
    y_ref[0] = y


def _out_proj(oa, ob, oc, sz, sg, x3, gate3, wb, wo, fw, tm, final):
    g, r, d = x3.shape
    rm = gate3.shape[1]
    tmod = 1 if rm == 1 else tm
    mod_map = (lambda b, i: (b, 0, 0)) if rm == 1 else (lambda b, i: (b, i, 0))
    row = lambda w: pl.BlockSpec((1, tm, w), lambda b, i: (b, i, 0))
    as3 = lambda a: a.reshape(g, r, a.shape[-1])
    return pl.pallas_call(
        functools.partial(_out_kernel, final=final),
        grid=(g, r // tm),
        in_specs=[
            row(MIX_WIDTH), row(MIX_WIDTH), row(MIX_WIDTH), row(N_MIXERS * MIX_WIDTH), row(N_MIXERS * d), row(d),
            pl.BlockSpec((1, tmod, d), mod_map),
            pl.BlockSpec(wb.shape, lambda b, i: (0, 0, 0)),
            pl.BlockSpec(wo.shape, lambda b, i: (0, 0)),
            pl.BlockSpec((1, d), lambda b, i: (0, 0)),
        ],
        out_specs=row(d),
        out_shape=jax.ShapeDtypeStruct((g, r, d), F32),
        compiler_params=_cp("arbitrary", "arbitrary"),
        name="out_proj",
    )(as3(oa), as3(ob), as3(oc), as3(sz), as3(sg), x3, gate3, wb, wo, fw.reshape(1, d))


def _rope_tables(pos):
    half = ROT_DIM // 2
    expo = jnp.arange(0, ROT_DIM, 2, dtype=F32) / ROT_DIM
    inv_freq = jnp.power(jnp.float32(ROPE_THETA), -expo)
    ang = pos.astype(F32)[:, None] * inv_freq[None, :]
    cos, sin = jnp.cos(ang), jnp.sin(ang)
    n = pos.shape[0]
    rest = HEAD_DIM - ROT_DIM
    a = jnp.concatenate([cos, cos, jnp.ones((n, rest), F32)], axis=1)
    bm = jnp.concatenate([-sin, jnp.zeros((n, half + rest), F32)], axis=1)
    cm = jnp.concatenate([jnp.zeros((n, half), F32), sin, jnp.zeros((n, rest), F32)], axis=1)
    rep = LANES // HEAD_DIM
    return tuple(jnp.tile(t, (1, rep)) for t in (a, bm, cm))


def _split_weights(w_in_l, b_f_l):
    o = np.cumsum([0, MIX_WIDTH, KV_WIDTH, KV_WIDTH, MIX_WIDTH,
                   MIX_WIDTH, KV_WIDTH, KV_WIDTH, N_HEADS, MIX_WIDTH,
                   MIX_WIDTH, KV_WIDTH, KV_WIDTH, IDX_HEADS * IDX_DIM, IDX_DIM, IDX_HEADS, MIX_WIDTH]).tolist()
    col = lambda i: w_in_l[:, o[i]:o[i + 1]]
    qa, ka, va, za, qb, kb, vb, fb, zb, qc, kc, vc, qi, ki, wi, zc = (col(i) for i in range(16))
    g = w_in_l[:, o[16]:]
    d = w_in_l.shape[0]
    zpad = lambda w: jnp.zeros((d, w), w_in_l.dtype)
    w_q = jnp.concatenate([qa, qb, qc], axis=1)
    w_kv = jnp.concatenate([ka, va, kb, vb, kc, vc], axis=1)
    w_z = jnp.concatenate([za, zb, zc], axis=1)
    w_misc = jnp.concatenate([qi, ki, zpad(LANES - IDX_DIM), wi, zpad(LANES - IDX_HEADS),
                              fb, zpad(LANES - N_HEADS)], axis=1)
    bf_row = jnp.concatenate([b_f_l, jnp.zeros((LANES - N_HEADS,), b_f_l.dtype)]).reshape(1, LANES)
    cast = lambda w: w.astype(BF16)
    return cast(w_q), cast(w_kv), cast(w_z), cast(g), cast(w_misc), bf_row.astype(F32)


def _project(x3, scale3, shift3, tabs, weights, norm_w_l, tm, kv_out=None):
    g, r, d = x3.shape
    w_q, w_kv, w_z, w_g, w_misc, bf_row = weights
    h = _norm_mod(x3, norm_w_l, scale3, shift3, tm).reshape(g * r, d)
    q_outs = _proj_call(
        _projq_kernel, h, w_q, tabs, (), [(MIX_WIDTH, BF16, True)] * 3 + [(MIX_WIDTH, F32, True)],
        tm, w_q.shape[1], "proj_q")
    if kv_out is None:
        kv, kvb = _proj_call(
            _projkv_kernel, h, w_kv, tabs, (), [(w_kv.shape[1], F32, True), (w_kv.shape[1], BF16, True)],
            tm, w_kv.shape[1], "proj_kv")
    else:
        kvt_prev, layer, depth = kv_out
        kvt, kvb, km = _proj_kv_prompt(h, w_kv, tabs, kvt_prev, layer, depth, g, r, tm)
        kv = (kvt, km)
    (sz,) = _proj_call(_projz_kernel, h, w_z, (), (), [(w_z.shape[1], BF16, True)], tm, w_z.shape[1], "proj_z")
    (sg,) = _proj_call(_projg_kernel, h, w_g, (), (), [(w_g.shape[1], BF16, False)], tm, d, "proj_g")
    misc = _proj_call(
        _projmisc_kernel, h, w_misc, tabs, (bf_row,),
        [(MIX_WIDTH, BF16, True), (IDX_DIM, F32, True), (2 * IDX_DIM, BF16, True),
         (IDX_HEADS, F32, True), (N_HEADS, F32, True)],
        tm, MISC_N, "proj_misc")
    return q_outs, kv, kvb, sz, sg, misc


def _prompt_layer(x3, scale3, shift3, gate3, tabs, weights, norm_w_l, wb_l, wo_l, fw, kvt_prev, layer, depth,
                  *, tm, tm_out, tq, tk, final):
    b, t, _ = x3.shape
    (qa, qb, qc, qaf), (kvt, km8), kvb, sz, sg, (qi, ki, ki2, wi, lf) = _project(
        x3, scale3, shift3, tabs, weights, norm_w_l, tm, kv_out=(kvt_prev, layer, depth))
    s3 = lambda a: a.reshape(b, t, a.shape[-1])
    kvb3, lf3 = s3(kvb), s3(lf)
    km = km8[:, :tm // MOBA_BLOCK].reshape(b, t // MOBA_BLOCK, KV_WIDTH)
    km = jnp.pad(km, ((0, 0), (0, KMEAN_ROWS - km.shape[1]), (0, 0)))
    cum = _cumsum(jnp.swapaxes(lf3, 1, 2))
    o_a = _moba_attn(s3(qaf), s3(qa), km, kvb3, tq)
    o_b = _fox_attn(s3(qb), cum, kvb3, tq, tk)
    o_c = _dsa_attn(s3(qi), s3(wi), s3(ki2), s3(qc), kvb3, tq, tk)
    x_new = _out_proj(o_a, o_b, o_c, sz, sg, x3, gate3, wb_l, wo_l, fw, tm_out, final)
    return x_new, kvt, lf3, s3(ki)


def _sample_layer(x3, scale3, shift3, gate3, tabs, weights, norm_w_l, wb_l, wo_l, fw, caches, page_table, layer,
                  *, nseq, final):
    _, m, _ = x3.shape
    t = m // nseq
    cache_t, cache_kxt, cache_lft = caches
    (qa, qb, qc, qaf), kv, kvb, sz, sg, (qi, ki, ki2, wi, lf) = _project(
        x3, scale3, shift3, tabs, weights, norm_w_l, m)
    s3 = lambda a: a.reshape(nseq, t, a.shape[-1])
    pad_slots = lambda a: jnp.pad(a, ((0, 0), (0, PAGE_SIZE - t), (0, 0)))
    lp = (page_table.shape[1] + PAGES_PER_STEP) * PAGE_SIZE
    isc = _sample_index(cache_kxt, page_table, layer, s3(qi), s3(wi), pad_slots(s3(ki2)), lp)
    bias = _topk_bias(isc.reshape(m, lp), page_table.shape[1] * PAGE_SIZE, t, min(m, 64)).reshape(nseq, t, lp)
    nlf = jnp.pad(jnp.swapaxes(s3(lf), 1, 2), ((0, 0), (0, 0), (0, PAGE_SIZE - t)))
    o_a, o_b, o_c = _sample_attn(cache_t, cache_lft, page_table, layer, bias, s3(qa), s3(qb), s3(qc), s3(qaf),
                                 pad_slots(s3(kvb)), nlf)
    x_new = _out_proj(o_a, o_b, o_c, sz, sg, x3, gate3, wb_l, wo_l, fw, m, final)
    return x_new, s3(kv), s3(lf), s3(ki)


def kernel(x_prompt, x_sample, cache_kv, cache_logf, cache_kidx, page_table, c_prompt, c_sample,
           norm_w, w_ada, b_ada, w_in, b_f, w_branch, w_out, final_norm_w):
    depth = norm_w.shape[0]
    bp, tp, d = x_prompt.shape
    bs, ts, _ = x_sample.shape
    n_pool = cache_kv.shape[1]
    n_pages = page_table.shape[1]
    past_len = n_pages * PAGE_SIZE
    assert ts == SUBLANES and n_pages % PAGES_PER_STEP == 0 and PAGES_PER_STEP % (MOBA_BLOCK // PAGE_SIZE) == 0
    assert tp % MOBA_BLOCK == 0 and tp % min(512, tp) == 0

    nc = bp + bs
    rc = -(-nc // SUBLANES) * SUBLANES
    c_all = jnp.concatenate([c_prompt, c_sample, jnp.zeros((rc - nc, d), F32)], axis=0)
    mod = _modulation(c_all, w_ada, b_ada)

    tabs_p = _rope_tables(jnp.arange(tp, dtype=jnp.int32))
    tabs_s = _rope_tables(jnp.tile(past_len + jnp.arange(ts, dtype=jnp.int32), bs))

    cache_t = jnp.transpose(cache_kv, (0, 1, 3, 4, 5, 6, 2)).reshape(depth, n_pool, N_MIXERS * MIXER_COLS, PAGE_SIZE)
    cache_kxt = jnp.swapaxes(cache_kidx, 2, 3)
    cache_lft = jnp.swapaxes(cache_logf, 2, 3)

    ms = bs * ts
    xp = x_prompt
    xs = x_sample.reshape(1, ms, d)
    new_p, new_s = [], []
    kvt = jnp.zeros((depth, bp, N_MIXERS * MIXER_COLS, tp), F32)
    for l in range(depth):
        weights = _split_weights(w_in[l], b_f[l])
        wb_l = w_branch[l].astype(BF16)
        wo_l = w_out[l].astype(BF16)
        final = l == depth - 1
        shift, scale, gate = (mod[l, :, i * d:(i + 1) * d] for i in range(3))
        p3 = lambda a: a[:bp].reshape(bp, 1, d)
        s3 = lambda a: jnp.repeat(a[bp:nc], ts, axis=0).reshape(1, ms, d)

        xp, kvt, lf_p, ki_p = _prompt_layer(
            xp, p3(scale), p3(shift), p3(gate), tabs_p, weights, norm_w[l], wb_l, wo_l, final_norm_w, kvt, l, depth,
            tm=min(512, tp), tm_out=min(512, tp), tq=min(128, tp), tk=MOBA_BLOCK, final=final)
        xs, kv_s, lf_s, ki_s = _sample_layer(
            xs, s3(scale), s3(shift), s3(gate), tabs_s, weights, norm_w[l], wb_l, wo_l, final_norm_w,
            (cache_t, cache_kxt, cache_lft), page_table, l, nseq=bs, final=final)
        new_p.append((None, lf_p, ki_p))
        new_s.append((kv_s, lf_s, ki_s))

    kv_shape = (N_MIXERS, 2, N_KV_HEADS, HEAD_DIM)
    stack = lambda items, i: jnp.stack([n[i] for n in items])
    kv_prompt = jnp.transpose(kvt.reshape(depth, bp, *kv_shape, tp), (0, 1, 6, 2, 3, 4, 5))
    kv_sample = stack(new_s, 0).reshape(depth, bs, ts, *kv_shape)
    return (xp, xs.reshape(bs, ts, d), kv_prompt, stack(new_p, 1), stack(new_p, 2),
            kv_sample, stack(new_s, 1), stack(new_s, 2))
```

```python
import functools

import numpy as np
import jax
import jax.numpy as jnp
from jax import lax
from jax.experimental import pallas as pl
from jax.experimental.pallas import tpu as pltpu

F32 = jnp.float32
BF16 = jnp.bfloat16
HIGHEST = lax.Precision.HIGHEST

N_MIXERS = 3
N_HEADS = 8
N_KV_HEADS = 4
HEAD_DIM = 64
MIX_WIDTH = N_HEADS * HEAD_DIM
KV_WIDTH = N_KV_HEADS * HEAD_DIM
MIXER_COLS = 2 * KV_WIDTH
ROT_DIM = HEAD_DIM // 4
ROPE_THETA = 500000.0
ATTN_SCALE = HEAD_DIM ** -0.5
LOG2E = 1.4426950408889634
MOBA_BLOCK = 256
MOBA_TOPK = 3
IDX_HEADS = 8
IDX_DIM = 64
IDX_SCALE = (IDX_DIM * IDX_HEADS) ** -0.5
DSA_TOPK = 256
NORM_EPS = 1e-6
PAGE_SIZE = 128

LANES = 128
SUBLANES = 8
KMEAN_ROWS = 128
NEG = -1e30
F32_LOWEST = -3.0e38
VMEM_LIMIT = 56 * 1024 * 1024
PAGES_PER_STEP = 16
INDEX_PAGES_PER_STEP = 16
DSA_SWEEP = 1024
PROJ_ROWS = 1024
TOPK_COARSE_BITS = 28

_NT = (((1,), (1,)), ((), ()))


def _cp(*sem):
    return pltpu.CompilerParams(dimension_semantics=sem, vmem_limit_bytes=VMEM_LIMIT)


def _sigmoid(x):
    return 1.0 / (1.0 + jnp.exp(-x))


def _mod_kernel(c_ref, w_ref, b_ref, o_ref):
    c = c_ref[...]
    sc = c * _sigmoid(c)
    o_ref[0] = jnp.dot(sc, w_ref[0], precision=HIGHEST, preferred_element_type=F32) + b_ref[0]


def _modulation(c_all, w_ada, b_ada):
    depth, d, d3 = w_ada.shape
    rc = c_all.shape[0]
    nj = d3 // d
    return pl.pallas_call(
        _mod_kernel,
        grid=(depth, nj),
        in_specs=[
            pl.BlockSpec((rc, d), lambda l, j: (0, 0)),
            pl.BlockSpec((1, d, d), lambda l, j: (l, 0, j)),
            pl.BlockSpec((1, 1, d), lambda l, j: (l, 0, j)),
        ],
        out_specs=pl.BlockSpec((1, rc, d), lambda l, j: (l, 0, j)),
        out_shape=jax.ShapeDtypeStruct((depth, rc, d3), F32),
        compiler_params=_cp("arbitrary", "arbitrary"),
        name="adaln_mod",
    )(c_all, w_ada, b_ada.reshape(depth, 1, d3))


def _norm_kernel(x_ref, w_ref, sc_ref, sh_ref, h_ref):
    x = x_ref[0]
    ms = jnp.mean(x * x, axis=-1, keepdims=True)
    y = x * lax.rsqrt(ms + NORM_EPS) * w_ref[...]
    h_ref[0] = (y * (1.0 + sc_ref[0]) + sh_ref[0]).astype(BF16)


def _norm_mod(x3, norm_w, scale3, shift3, tm):
    g, r, d = x3.shape
    rm = scale3.shape[1]
    tmod = 1 if rm == 1 else tm
    mod_map = (lambda b, i: (b, 0, 0)) if rm == 1 else (lambda b, i: (b, i, 0))
    return pl.pallas_call(
        _norm_kernel,
        grid=(g, r // tm),
        in_specs=[
            pl.BlockSpec((1, tm, d), lambda b, i: (b, i, 0)),
            pl.BlockSpec((1, d), lambda b, i: (0, 0)),
            pl.BlockSpec((1, tmod, d), mod_map),
            pl.BlockSpec((1, tmod, d), mod_map),
        ],
        out_specs=pl.BlockSpec((1, tm, d), lambda b, i: (b, i, 0)),
        out_shape=jax.ShapeDtypeStruct((g, r, d), BF16),
        compiler_params=_cp("arbitrary", "arbitrary"),
        name="norm_mod",
    )(x3, norm_w.reshape(1, d), scale3, shift3)


def _rope_chunk(x, a, bm, cm):
    return x * a + pltpu.roll(x, LANES - ROT_DIM // 2, 1) * bm + pltpu.roll(x, ROT_DIM // 2, 1) * cm


def _projq_kernel(h_ref, w_ref, ra_ref, rb_ref, rc_ref, qa_ref, qb_ref, qc_ref, qaf_ref):
    y = jnp.dot(h_ref[...], w_ref[...], preferred_element_type=F32)
    a, bm, cm = ra_ref[...], rb_ref[...], rc_ref[...]
    for m, o_ref in enumerate((qa_ref, qb_ref, qc_ref)):
        for c in range(MIX_WIDTH // LANES):
            x = y[:, m * MIX_WIDTH + c * LANES:m * MIX_WIDTH + (c + 1) * LANES]
            if m != 1:
                x = _rope_chunk(x, a, bm, cm)
            if m == 0:
                qaf_ref[:, c * LANES:(c + 1) * LANES] = x
            o_ref[:, c * LANES:(c + 1) * LANES] = (x * (ATTN_SCALE * LOG2E)).astype(BF16)


def _projkv_kernel(h_ref, w_ref, ra_ref, rb_ref, rc_ref, kv_ref, kvb_ref):
    y = jnp.dot(h_ref[...], w_ref[...], preferred_element_type=F32)
    a, bm, cm = ra_ref[...], rb_ref[...], rc_ref[...]
    per_mixer = MIXER_COLS // LANES
    for c in range(N_MIXERS * per_mixer):
        x = y[:, c * LANES:(c + 1) * LANES]
        mixer, within = divmod(c, per_mixer)
        if mixer != 1 and within < KV_WIDTH // LANES:
            x = _rope_chunk(x, a, bm, cm)
        kv_ref[:, c * LANES:(c + 1) * LANES] = x
        kvb_ref[:, c * LANES:(c + 1) * LANES] = x.astype(BF16)


def _projkv_prompt_kernel(h_ref, w_ref, ra_ref, rb_ref, rc_ref, *refs):
    kvt_ref, kvb_ref, km_ref = refs[-3:]
    y = jnp.dot(h_ref[...], w_ref[...], preferred_element_type=F32)
    a, bm, cm = ra_ref[...], rb_ref[...], rc_ref[...]
    tm = y.shape[0]
    per_mixer = MIXER_COLS // LANES
    moba_k = []
    for c in range(N_MIXERS * per_mixer):
        x = y[:, c * LANES:(c + 1) * LANES]
        mixer, within = divmod(c, per_mixer)
        if mixer != 1 and within < KV_WIDTH // LANES:
            x = _rope_chunk(x, a, bm, cm)
        if mixer == 0 and within < KV_WIDTH // LANES:
            moba_k.append(x)
        kvb_ref[:, c * LANES:(c + 1) * LANES] = x.astype(BF16)
        kvt_ref[0, 0, c * LANES:(c + 1) * LANES, :] = x.T
    means = [jnp.concatenate([jnp.sum(x[g * MOBA_BLOCK:(g + 1) * MOBA_BLOCK], axis=0, keepdims=True)
                              for x in moba_k], axis=1) * (1.0 / MOBA_BLOCK) for g in range(tm // MOBA_BLOCK)]
    km_ref[0] = jnp.concatenate(means + [jnp.zeros((SUBLANES - len(means), KV_WIDTH), F32)], axis=0)


def _proj_kv_prompt(h, w, tabs, kvt_prev, layer, depth, b, t, tm):
    m, d = h.shape
    n = w.shape[1]
    nt = t // tm
    assert tm % MOBA_BLOCK == 0 and tm // MOBA_BLOCK <= SUBLANES
    in_specs = [pl.BlockSpec((tm, d), lambda j, i: (i, 0)), pl.BlockSpec((d, n), lambda j, i: (0, 0))]
    in_specs += [pl.BlockSpec((tm, LANES), lambda j, i: (i % nt, 0)) for _ in tabs]
    in_specs.append(pl.BlockSpec(memory_space=pl.ANY))
    args = [h, w, *tabs, kvt_prev]
    aliases = {len(args) - 1: 0}
    return pl.pallas_call(
        _projkv_prompt_kernel,
        grid=(1, m // tm),
        in_specs=in_specs,
        out_specs=[pl.BlockSpec((1, 1, n, tm), lambda j, i: (layer, i // nt, 0, i % nt)),
                   pl.BlockSpec((tm, n), lambda j, i: (i, 0)),
                   pl.BlockSpec((1, SUBLANES, KV_WIDTH), lambda j, i: (i, 0, 0))],
        out_shape=[jax.ShapeDtypeStruct((depth, b, n, t), F32), jax.ShapeDtypeStruct((m, n), BF16),
                   jax.ShapeDtypeStruct((m // tm, SUBLANES, KV_WIDTH), F32)],
        input_output_aliases=aliases,
        compiler_params=_cp("arbitrary", "arbitrary"),
        name="proj_kv_prompt",
    )(*args)


def _projz_kernel(h_ref, w_ref, o_ref):
    y = jnp.dot(h_ref[...], w_ref[...], preferred_element_type=F32)
    o_ref[...] = (y * _sigmoid(y)).astype(o_ref.dtype)


def _projg_kernel(h_ref, w_ref, o_ref):
    y = jnp.dot(h_ref[...], w_ref[...], preferred_element_type=F32)
    o_ref[...] = _sigmoid(y).astype(o_ref.dtype)


MISC_KI = MIX_WIDTH
MISC_WI = MISC_KI + LANES
MISC_FB = MISC_WI + LANES
MISC_N = MISC_FB + LANES


def _projmisc_kernel(h_ref, w_ref, ra_ref, rb_ref, rc_ref, bf_ref,
                     qi_ref, ki_ref, ki2_ref, wi_ref, lf_ref):
    y = jnp.dot(h_ref[...], w_ref[...], preferred_element_type=F32)
    a, bm, cm = ra_ref[...], rb_ref[...], rc_ref[...]
    for c in range(MIX_WIDTH // LANES):
        x = _rope_chunk(y[:, c * LANES:(c + 1) * LANES], a, bm, cm)
        qi_ref[:, c * LANES:(c + 1) * LANES] = x.astype(BF16)
    ki = _rope_chunk(y[:, MISC_KI:MISC_KI + LANES], a, bm, cm)
    ki_ref[...] = ki[:, :IDX_DIM]
    ki2_ref[...] = (ki + pltpu.roll(ki, IDX_DIM, 1)).astype(BF16)
    wi_ref[...] = y[:, MISC_WI:MISC_WI + IDX_HEADS]
    f = y[:, MISC_FB:MISC_FB + LANES] + bf_ref[...]
    logf = jnp.minimum(f, 0.0) - jnp.log(1.0 + jnp.exp(-jnp.abs(f)))
    lf_ref[...] = logf[:, :N_HEADS]


def _proj_call(kernel, h, w, tabs, extra, outs, tm, tn, name):
    m, d = h.shape
    n = w.shape[1]
    nt = tabs[0].shape[0] // tm if tabs else 1
    in_specs = [pl.BlockSpec((tm, d), lambda j, i: (i, 0)),
                pl.BlockSpec((d, tn), lambda j, i: (0, j))]
    in_specs += [pl.BlockSpec((tm, LANES), lambda j, i: (i % nt, 0)) for _ in tabs]
    in_specs += [pl.BlockSpec(e.shape, lambda j, i: (0, 0)) for e in extra]
    out_specs = [pl.BlockSpec((tm, wd if full else tn), (lambda j, i: (i, 0)) if full else (lambda j, i: (i, j)))
                 for (wd, _, full) in outs]
    out_shape = [jax.ShapeDtypeStruct((m, wd), dt) for (wd, dt, _) in outs]
    return pl.pallas_call(
        kernel,
        grid=(n // tn, m // tm),
        in_specs=in_specs,
        out_specs=out_specs,
        out_shape=out_shape,
        compiler_params=_cp("arbitrary", "arbitrary"),
        name=name,
    )(h, w, *tabs, *extra)


CUM_CHUNK = 256


def _cumsum_kernel(x_ref, o_ref):
    n = x_ref.shape[2] // CUM_CHUNK
    r = lax.broadcasted_iota(jnp.int32, (CUM_CHUNK, CUM_CHUNK), 0)
    c = lax.broadcasted_iota(jnp.int32, (CUM_CHUNK, CUM_CHUNK), 1)
    tri = (r <= c).astype(F32)

    def body(i, carry):
        st = pl.multiple_of(i * CUM_CHUNK, CUM_CHUNK)
        x = x_ref[0, :, pl.ds(st, CUM_CHUNK)]
        y = jnp.dot(x, tri, precision=HIGHEST, preferred_element_type=F32) + carry
        o_ref[0, :, pl.ds(st, CUM_CHUNK)] = y * LOG2E
        return y[:, CUM_CHUNK - 1:CUM_CHUNK]

    lax.fori_loop(0, n, body, jnp.zeros((x_ref.shape[1], 1), F32))


def _cumsum(lft):
    b, h, lp = lft.shape
    return pl.pallas_call(
        _cumsum_kernel,
        grid=(b,),
        in_specs=[pl.BlockSpec((1, h, lp), lambda i: (i, 0, 0))],
        out_specs=pl.BlockSpec((1, h, lp), lambda i: (i, 0, 0)),
        out_shape=jax.ShapeDtypeStruct((b, h, lp), F32),
        compiler_params=_cp("arbitrary"),
        name="fox_cumsum",
    )(lft)


def _stack4(blk0, blk1):
    lane = lax.broadcasted_iota(jnp.int32, blk0.shape, 1)
    lo = lane < HEAD_DIM
    return jnp.concatenate([
        jnp.where(lo, blk0, 0.0),
        jnp.where(lo, pltpu.roll(blk0, HEAD_DIM, 1), 0.0),
        jnp.where(lo, 0.0, pltpu.roll(blk1, HEAD_DIM, 1)),
        jnp.where(lo, 0.0, blk1)], axis=0)


def _unstack4(o, tq):
    lane = lax.broadcasted_iota(jnp.int32, (tq, LANES), 1)
    lo = lane < HEAD_DIM
    b0 = jnp.where(lo, o[:tq], pltpu.roll(o[tq:2 * tq], HEAD_DIM, 1))
    b1 = jnp.where(lo, pltpu.roll(o[2 * tq:3 * tq], HEAD_DIM, 1), o[3 * tq:])
    return b0, b1


def _stacked_queries(q_ref):
    return [_stack4(q_ref[0, :, (2 * c) * LANES:(2 * c + 1) * LANES].astype(F32),
                    q_ref[0, :, (2 * c + 1) * LANES:(2 * c + 2) * LANES].astype(F32)) for c in range(2)]


def _flash_scratch(tq, tk):
    rows = 4 * tq
    return [pltpu.VMEM((2, rows, tk), F32), pltpu.VMEM((2, rows, tk), BF16),
            pltpu.VMEM((2, rows, LANES), F32), pltpu.VMEM((2, rows, LANES), F32), pltpu.VMEM((2, rows, LANES), F32),
            pltpu.VMEM((2, rows, 2 * LANES), F32), pltpu.VMEM((2, rows, tk), F32)]


def _flash_pipeline(n_all, qs, k_ref, v_ref, add_bias, diag_mask, scratch, o_ref, tq, tk, k_aug=None,
                    tile_bias=None):
    s_ref, p_ref, m0_ref, m1_ref, mx_ref, acc_ref, mb_ref = scratch
    m_slots = (m0_ref, m1_ref)
    m0_ref[...] = jnp.full(m0_ref.shape, -jnp.inf, F32)
    acc_ref[...] = jnp.zeros(acc_ref.shape, F32)
    if diag_mask is not None:
        @pl.when((pl.program_id(0) == 0) & (pl.program_id(1) == 0))
        def _():
            mb_ref[0] = jnp.zeros(mb_ref.shape[1:], F32)
        mb_ref[1] = diag_mask
    last = n_all - 1
    ones = jnp.ones((tk, LANES), BF16)

    def stage_pv(i, par):
        jc = jnp.clip(i - 2, 0, last)
        stc = pl.multiple_of(jc * tk, tk)
        for c in range(2):
            vo = jnp.concatenate([v_ref[0, pl.ds(stc, tk), c * LANES:(c + 1) * LANES], ones], axis=1)
            al = jnp.exp2(m_slots[par][c] - m_slots[1 - par][c])
            acc_ref[c] = (jnp.concatenate([al, al], axis=1) * acc_ref[c]
                          + jnp.dot(p_ref[c], vo, preferred_element_type=F32))

    def stage_softmax(par):
        for c in range(2):
            m_new = jnp.maximum(m_slots[1 - par][c], jnp.max(mx_ref[c], axis=-1, keepdims=True))
            m_slots[par][c] = m_new
            p_ref[c] = jnp.exp2(s_ref[c] - jnp.concatenate([m_new] * (tk // LANES), axis=1)).astype(BF16)

    def stage_scores(i):
        ja = jnp.minimum(i, last)
        sta = pl.multiple_of(ja * tk, tk)
        pen = jnp.where(i <= last, 0.0, NEG).astype(F32)
        shared = None if tile_bias is None else tile_bias(ja, pen)
        for c in range(2):
            kt = k_ref[0, pl.ds(sta, tk), c * LANES:(c + 1) * LANES]
            if k_aug is not None:
                kt = jnp.concatenate([kt, k_aug(c, ja, i <= last)], axis=1)
            s = lax.dot_general(qs[c], kt, _NT, preferred_element_type=F32)
            s = add_bias(c, s, ja, pen)
            if shared is not None:
                s = s + jnp.concatenate([shared] * 4, axis=0)
            if diag_mask is not None:
                s = s + mb_ref[(ja == last).astype(jnp.int32)]
            s_ref[c] = s
            mx_ref[c] = functools.reduce(jnp.maximum, [s[:, k * LANES:(k + 1) * LANES] for k in range(tk // LANES)])

    def step(i, par):
        stage_pv(i, par)
        stage_softmax(par)
        stage_scores(i)

    def step2(ii, carry):
        step(2 * ii + 2, 0)
        step(2 * ii + 3, 1)
        return carry

    stage_scores(0)
    stage_softmax(1)
    stage_scores(1)
    lax.fori_loop(0, (n_all + 1) // 2, step2, 0)
    for c in range(2):
        b0, b1 = _unstack4(acc_ref[c, :, :LANES] / acc_ref[c, :, LANES:], tq)
        o_ref[0, :, (2 * c) * LANES:(2 * c + 1) * LANES] = b0.astype(o_ref.dtype)
        o_ref[0, :, (2 * c + 1) * LANES:(2 * c + 2) * LANES] = b1.astype(o_ref.dtype)


def _diag_mask(q_lo, n_all, tq, tk):
    r1 = lax.broadcasted_iota(jnp.int32, (tq, 1), 0) + q_lo
    qpos = jnp.concatenate([r1] * 4, axis=0)
    kpos = (n_all - 1) * tk + lax.broadcasted_iota(jnp.int32, (4 * tq, tk), 1)
    return jnp.where(kpos <= qpos, 0.0, NEG)


def _attn_call(kern, name, ins, in_specs, b, t, tq, scratch):
    return pl.pallas_call(
        kern,
        grid=(b, t // tq),
        in_specs=in_specs,
        out_specs=pl.BlockSpec((1, tq, MIX_WIDTH), lambda i, j: (i, j, 0)),
        out_shape=jax.ShapeDtypeStruct((b, t, MIX_WIDTH), BF16),
        scratch_shapes=scratch,
        compiler_params=_cp("arbitrary", "arbitrary"),
        name=name,
    )(*ins)


_q_spec = lambda tq: pl.BlockSpec((1, tq, MIX_WIDTH), lambda i, j: (i, j, 0))
_kv_spec = lambda lp, col: pl.BlockSpec((1, lp, KV_WIDTH), lambda i, j: (i, 0, col))


def _moba_kernel(qf_ref, q_ref, km_ref, k_ref, v_ref, o_ref, *scratch, tq, topk, nblk):
    tk = MOBA_BLOCK
    q_lo = pl.program_id(1) * tq
    own = q_lo // MOBA_BLOCK
    rows = 4 * tq
    blk = lax.broadcasted_iota(jnp.int32, (nblk, rows), 0)
    blk_f = blk.astype(F32)
    q_st = _stacked_queries(q_ref)
    qs = []
    for c, qf in enumerate(_stacked_queries(qf_ref)):
        gate = lax.dot_general(km_ref[0, :nblk, c * LANES:(c + 1) * LANES], qf, _NT, precision=HIGHEST,
                               preferred_element_type=F32)
        gate = jnp.where(blk < own, gate, -jnp.inf)
        sb = jnp.where(blk == own, 0.0, NEG)
        for _ in range(topk):
            mx = jnp.max(gate, axis=0, keepdims=True)
            is_max = (gate == mx) & (mx > -jnp.inf)
            first = jnp.min(jnp.where(is_max, blk_f, 1e9), axis=0, keepdims=True)
            pick = blk_f == first
            sb = jnp.where(pick, 0.0, sb)
            gate = jnp.where(pick, -jnp.inf, gate)
        sb = jnp.concatenate([sb, jnp.full((LANES - nblk, rows), NEG, F32)], axis=0)
        qs.append(jnp.concatenate([q_st[c], sb.T], axis=1).astype(BF16))

    klane = lax.broadcasted_iota(jnp.int32, (tk, LANES), 1)

    def k_aug(c, j, valid):
        return jnp.where(klane == jnp.where(valid, j, LANES - 1), 1.0, 0.0).astype(BF16)

    _flash_pipeline(own + 1, qs, k_ref, v_ref, lambda c, s, j, pen: s, _diag_mask(q_lo, own + 1, tq, tk), scratch,
                    o_ref, tq, tk, k_aug=k_aug)


def _moba_attn(qf, q, km, kvb, tq):
    b, t, _ = q.shape
    lp = kvb.shape[1]
    nblk = -(-(lp // MOBA_BLOCK) // SUBLANES) * SUBLANES
    assert nblk < LANES
    kern = functools.partial(_moba_kernel, tq=tq, topk=min(MOBA_TOPK, lp // MOBA_BLOCK), nblk=nblk)
    specs = [_q_spec(tq), _q_spec(tq), pl.BlockSpec((1, KMEAN_ROWS, KV_WIDTH), lambda i, j: (i, 0, 0)),
             _kv_spec(lp, 0), _kv_spec(lp, 1)]
    return _attn_call(kern, "moba_attn", (qf, q, km, kvb, kvb), specs, b, t, tq, _flash_scratch(tq, MOBA_BLOCK))


def _fox_kernel(q_ref, cum_ref, k_ref, v_ref, o_ref, *scratch, tq, tk):
    q_lo = pl.program_id(1) * tq
    n_all = (q_lo + tq + tk - 1) // tk
    qs = [q.astype(BF16) for q in _stacked_queries(q_ref)]

    def add_bias(c, s, j, pen):
        st = pl.multiple_of(j * tk, tk)
        parts = [s[hh * tq:(hh + 1) * tq] - (cum_ref[0, 4 * c + hh:4 * c + hh + 1, pl.ds(st, tk)] - pen)
                 for hh in range(4)]
        return jnp.concatenate(parts, axis=0)

    _flash_pipeline(n_all, qs, k_ref, v_ref, add_bias, _diag_mask(q_lo, n_all, tq, tk), scratch, o_ref, tq, tk)


def _fox_attn(q, cum, kvb, tq, tk):
    b, t, _ = q.shape
    lp = kvb.shape[1]
    kern = functools.partial(_fox_kernel, tq=tq, tk=tk)
    specs = [_q_spec(tq), pl.BlockSpec((1, N_HEADS, lp), lambda i, j: (i, 0, 0)), _kv_spec(lp, 2), _kv_spec(lp, 3)]
    return _attn_call(kern, "fox_attn", (q, cum, kvb, kvb), specs, b, t, tq, _flash_scratch(tq, tk))


def _key_to_float(key):
    bits = jnp.where(key < 0, key & jnp.int32(0x7FFFFFFF), ~key)
    return lax.bitcast_convert_type(bits, F32)


def _topk_threshold(count, vshape, need, few, idx_bits):
    def bit_body(i, key):
        cand = key | lax.shift_left(jnp.int32(1), 31 - i)
        thr_c = _key_to_float(cand)
        return jnp.where(count(lambda x, idx: x >= thr_c) >= need, cand, key)

    no_cut = jnp.full(vshape, 2 ** 30, jnp.int32)

    def too_many(key):
        thr = _key_to_float(key)
        return jnp.max(jnp.where(few, 0.0, count(lambda x, idx: x >= thr) - need)) > 0.0

    def tie_cut(thr):
        r = need - count(lambda x, idx: x > thr)

        def jb(i, cut):
            cand = cut | lax.shift_left(jnp.int32(1), idx_bits - 1 - i)
            cnt = count(lambda x, idx: (x == thr) & (idx < cand))
            return jnp.where(cnt < r, cand, cut)

        return lax.fori_loop(0, idx_bits, jb, jnp.zeros(vshape, jnp.int32))

    def refine(key):
        key = lax.fori_loop(TOPK_COARSE_BITS, 32, bit_body, key)
        return key, lax.cond(too_many(key), lambda k: tie_cut(_key_to_float(k)), lambda k: no_cut, key)

    key = lax.fori_loop(0, TOPK_COARSE_BITS, bit_body, jnp.zeros(vshape, jnp.int32))
    key, cut = lax.cond(too_many(key), refine, lambda k: (k, no_cut), key)
    return jnp.where(few, F32_LOWEST, _key_to_float(key)), jnp.where(few, 2 ** 30, cut)


def _dsa_kernel(qi_ref, w_ref, kx_ref, q_ref, k_ref, v_ref, o_ref, isc_ref, *scratch, tq, tk, sweep,
                topk, idx_bits):
    q_lo = pl.program_id(1) * tq
    n_all = (q_lo + tq + tk - 1) // tk
    lane = lax.broadcasted_iota(jnp.int32, (tq, LANES), 1)
    lo = lane < HEAD_DIM
    qpos = lax.broadcasted_iota(jnp.int32, (1, tq), 1) + q_lo
    krow = lax.broadcasted_iota(jnp.int32, (tk, 1), 0)

    pieces = []
    for h in range(IDX_HEADS):
        chunk = qi_ref[0, :, (h // 2) * LANES:(h // 2 + 1) * LANES].astype(F32)
        pieces.append(jnp.where(lo, chunk, 0.0) if h % 2 == 0 else jnp.where(lo, 0.0, chunk))
    qstack_t = jnp.concatenate(pieces, axis=0).T.astype(BF16)
    wrow = jnp.concatenate([w_ref[0, h:h + 1, :] for h in range(IDX_HEADS)], axis=1)

    def idx_tile(j):
        start = pl.multiple_of(j * tk, tk)
        sc = jnp.dot(kx_ref[0, pl.ds(start, tk), :], qstack_t, preferred_element_type=F32)
        contrib = jnp.maximum(sc, 0.0) * wrow
        isc = contrib[:, :tq]
        for h in range(1, IDX_HEADS):
            isc = isc + contrib[:, h * tq:(h + 1) * tq]
        isc_ref[pl.ds(start, tk), :] = jnp.where(j * tk + krow <= qpos, isc * IDX_SCALE, -jnp.inf)

    def idx_body(jj, carry):
        idx_tile(2 * jj)
        idx_tile(2 * jj + 1)
        return carry

    n_idx = (n_all + 1) // 2 * 2
    lax.fori_loop(0, n_idx // 2, idx_body, 0)

    tiles_per_sweep = sweep // tk
    n_sweeps = (n_all + tiles_per_sweep - 1) // tiles_per_sweep

    def pad_body(j, carry):
        isc_ref[pl.ds(pl.multiple_of(j * tk, tk), tk), :] = jnp.full((tk, tq), -jnp.inf, F32)
        return carry

    lax.fori_loop(n_idx, n_sweeps * tiles_per_sweep, pad_body, 0)
    grp = 8 * SUBLANES
    sub = lax.broadcasted_iota(jnp.int32, (grp, tq), 0)

    def count(pred):
        def body(j, acc):
            start = pl.multiple_of(j * sweep, sweep)
            x = isc_ref[pl.ds(start, sweep), :]
            for r in range(sweep // grp):
                idx = j * sweep + r * grp + sub
                acc = acc + jnp.where(pred(x[r * grp:(r + 1) * grp], idx), 1.0, 0.0)
            return acc
        acc = lax.fori_loop(0, n_sweeps, body, jnp.zeros((grp, tq), F32))
        return jnp.sum(acc, axis=0, keepdims=True)

    few = qpos + 1 <= topk
    thr, cut = _topk_threshold(count, (1, tq), jnp.float32(topk), few, idx_bits)

    qs = [q.astype(BF16) for q in _stacked_queries(q_ref)]

    def tile_bias(j, pen):
        x = isc_ref[pl.ds(pl.multiple_of(j * tk, tk), tk), :]
        keep = (x > thr) | ((x == thr) & (j * tk + krow <= cut))
        return jnp.where(keep, 0.0, NEG).T + pen

    _flash_pipeline(n_all, qs, k_ref, v_ref, lambda c, s, j, pen: s, None, scratch, o_ref, tq, tk,
                    tile_bias=tile_bias)


def _dsa_attn(qi, wi, kx2, q, kvb, tq, tk):
    b, t, _ = q.shape
    lp = kvb.shape[1]
    sweep = max(tk, min(DSA_SWEEP, lp))
    assert lp % sweep == 0 and sweep % tk == 0 and lp % (2 * tk) == 0
    kern = functools.partial(_dsa_kernel, tq=tq, tk=tk, sweep=sweep, topk=min(DSA_TOPK, lp // 4),
                             idx_bits=max(1, int(lp).bit_length()))
    specs = [_q_spec(tq), pl.BlockSpec((1, IDX_HEADS, tq), lambda i, j: (i, 0, j)),
             pl.BlockSpec((1, lp, 2 * IDX_DIM), lambda i, j: (i, 0, 0)), _q_spec(tq), _kv_spec(lp, 4), _kv_spec(lp, 5)]
    scratch = [pltpu.VMEM((lp, tq), F32)] + _flash_scratch(tq, tk)
    return _attn_call(kern, "dsa_attn", (qi, jnp.swapaxes(wi, 1, 2), kx2, q, kvb, kvb), specs, b, t, tq, scratch)


def _sample_index_kernel(pt_ref, *refs, ppi, n_steps, l_past, t_new):
    del pt_ref
    kx_refs = refs[:ppi]
    qi_ref, w_ref, nkx_ref, o_ref, isc_ref = refs[ppi:]
    j = pl.program_id(1)
    lp = isc_ref.shape[1]
    lane = lax.broadcasted_iota(jnp.int32, (t_new, LANES), 1)
    lo = lane < HEAD_DIM
    pieces, wpieces = [], []
    for h in range(IDX_HEADS):
        chunk = qi_ref[0, :, (h // 2) * LANES:(h // 2 + 1) * LANES].astype(F32)
        pieces.append(jnp.where(lo, chunk, 0.0) if h % 2 == 0 else jnp.where(lo, 0.0, chunk))
        wpieces.append(jnp.broadcast_to(w_ref[0, :, h:h + 1], (t_new, LANES)))
    qstack = jnp.concatenate(pieces, axis=0).astype(BF16)
    wb = jnp.concatenate(wpieces, axis=0)

    def head_sum(sc):
        contrib = jnp.maximum(sc, 0.0) * jnp.concatenate([wb] * (sc.shape[1] // LANES), axis=1)
        isc = contrib[:t_new]
        for h in range(1, IDX_HEADS):
            isc = isc + contrib[h * t_new:(h + 1) * t_new]
        return isc * IDX_SCALE

    kxt = jnp.concatenate([r[0, 0] for r in kx_refs], axis=1)
    rhs = jnp.concatenate([kxt, kxt], axis=0).astype(BF16)
    start = pl.multiple_of(j * (ppi * PAGE_SIZE), ppi * PAGE_SIZE)
    isc_ref[:, pl.ds(start, ppi * PAGE_SIZE)] = head_sum(jnp.dot(qstack, rhs, preferred_element_type=F32))

    @pl.when(j == n_steps - 1)
    def _():
        scn = lax.dot_general(qstack, nkx_ref[0], _NT, preferred_element_type=F32)
        row = lax.broadcasted_iota(jnp.int32, (t_new, LANES), 0)
        isc_ref[:, l_past:l_past + LANES] = jnp.where((lane <= row) & (lane < t_new), head_sum(scn), -jnp.inf)
        if lp > l_past + LANES:
            isc_ref[:, l_past + LANES:] = jnp.full((t_new, lp - l_past - LANES), -jnp.inf, F32)
        o_ref[0] = isc_ref[...]


def _topk_bias_kernel(x_ref, o_ref, *, sweep, topk, l_past, t_new, idx_bits):
    rows, lp = x_ref.shape
    lane = lax.broadcasted_iota(jnp.int32, (rows, LANES), 1)

    def count(pred):
        def body(j, acc):
            x = x_ref[:, pl.ds(pl.multiple_of(j * sweep, sweep), sweep)]
            for cc in range(sweep // LANES):
                idx = j * sweep + cc * LANES + lane
                acc = acc + jnp.where(pred(x[:, cc * LANES:(cc + 1) * LANES], idx), 1.0, 0.0)
            return acc
        acc = lax.fori_loop(0, lp // sweep, body, jnp.zeros((rows, LANES), F32))
        return jnp.sum(acc, axis=-1, keepdims=True)

    qpos1 = (lax.broadcasted_iota(jnp.int32, (rows, 1), 0) & (t_new - 1)) + l_past
    thr, cut = _topk_threshold(count, (rows, 1), jnp.float32(topk), qpos1 + 1 <= topk, idx_bits)
    kiota = lax.broadcasted_iota(jnp.int32, (rows, sweep), 1)

    def bias_body(j, carry):
        start = pl.multiple_of(j * sweep, sweep)
        x = x_ref[:, pl.ds(start, sweep)]
        keep = (x > thr) | ((x == thr) & (j * sweep + kiota <= cut))
        o_ref[:, pl.ds(start, sweep)] = jnp.where(keep, 0.0, NEG)
        return carry

    lax.fori_loop(0, lp // sweep, bias_body, 0)


def _topk_bias(isc, l_past, t_new, rows_per_step):
    rows, lp = isc.shape
    sweep = min(DSA_SWEEP, lp)
    assert lp % sweep == 0 and rows % rows_per_step == 0 and rows_per_step % t_new == 0
    kern = functools.partial(_topk_bias_kernel, sweep=sweep, topk=min(DSA_TOPK, (l_past + t_new) // 4),
                             l_past=l_past, t_new=t_new, idx_bits=max(1, int(lp).bit_length()))
    return pl.pallas_call(
        kern,
        grid=(rows // rows_per_step,),
        in_specs=[pl.BlockSpec((rows_per_step, lp), lambda i: (i, 0))],
        out_specs=pl.BlockSpec((rows_per_step, lp), lambda i: (i, 0)),
        out_shape=jax.ShapeDtypeStruct((rows, lp), F32),
        compiler_params=_cp("arbitrary"),
        name="dsa_topk_bias",
    )(isc)


def _sample_index(cache_kxt, page_table, layer, qi, wi, nkx2, lp):
    nseq, n_pages = page_table.shape
    t_new = qi.shape[1]
    ppi = INDEX_PAGES_PER_STEP if n_pages % INDEX_PAGES_PER_STEP == 0 else PAGES_PER_STEP
    n_steps = n_pages // ppi
    l_past = n_pages * PAGE_SIZE
    pg = lambda pp: (lambda i, j, pt: (layer, pt[i, j * ppi + pp], 0, 0))
    grid_spec = pltpu.PrefetchScalarGridSpec(
        num_scalar_prefetch=1,
        grid=(nseq, n_steps),
        in_specs=[pl.BlockSpec((1, 1, IDX_DIM, PAGE_SIZE), pg(pp)) for pp in range(ppi)] + [
            pl.BlockSpec((1, t_new, MIX_WIDTH), lambda i, j, pt: (i, 0, 0)),
            pl.BlockSpec((1, t_new, IDX_HEADS), lambda i, j, pt: (i, 0, 0)),
            pl.BlockSpec((1, PAGE_SIZE, 2 * IDX_DIM), lambda i, j, pt: (i, 0, 0)),
        ],
        out_specs=pl.BlockSpec((1, t_new, lp), lambda i, j, pt: (i, 0, 0)),
        scratch_shapes=[pltpu.VMEM((t_new, lp), F32)],
    )
    kern = functools.partial(_sample_index_kernel, ppi=ppi, n_steps=n_steps, l_past=l_past, t_new=t_new)
    return pl.pallas_call(
        kern,
        grid_spec=grid_spec,
        out_shape=jax.ShapeDtypeStruct((nseq, t_new, lp), F32),
        compiler_params=_cp("arbitrary", "arbitrary"),
        name="dsa_index_paged",
    )(page_table, *([cache_kxt] * ppi), qi, wi, nkx2)


def _block_diag_q(q):
    t = q.shape[0]
    lane = lax.broadcasted_iota(jnp.int32, (t, LANES), 1)
    lo = lane < HEAD_DIM
    zero = jnp.zeros((t, LANES), F32)
    out = []
    for h in range(N_HEADS):
        src = q[:, (h // 2) * LANES:(h // 2 + 1) * LANES]
        kvh = h // (N_HEADS // N_KV_HEADS)
        chunk, half = divmod(kvh, 2)
        piece = src if h % 2 == half else pltpu.roll(src, HEAD_DIM, 1)
        piece = jnp.where(lo, piece, 0.0) if half == 0 else jnp.where(lo, 0.0, piece)
        out.append(jnp.concatenate([piece, zero] if chunk == 0 else [zero, piece], axis=1))
    return jnp.concatenate(out, axis=0)


def _token_major(o, t):
    lane = lax.broadcasted_iota(jnp.int32, (t, LANES), 1)
    lo = lane < HEAD_DIM
    out = []
    for co in range(MIX_WIDTH // LANES):
        chunk, half = divmod(co, 2)
        x = o[(2 * co) * t:(2 * co + 1) * t, chunk * LANES:(chunk + 1) * LANES]
        y = o[(2 * co + 1) * t:(2 * co + 2) * t, chunk * LANES:(chunk + 1) * LANES]
        if half == 0:
            out.append(jnp.where(lo, x, pltpu.roll(y, HEAD_DIM, 1)))
        else:
            out.append(jnp.where(lo, pltpu.roll(x, HEAD_DIM, 1), y))
    return jnp.concatenate(out, axis=1)


def _lane_cumsum(x):
    lane = lax.broadcasted_iota(jnp.int32, x.shape, 1)
    d = 1
    while d < LANES:
        x = x + jnp.where(lane >= d, pltpu.roll(x, d, 1), 0.0)
        d *= 2
    return x


def _rows_from_heads(c, t):
    return jnp.concatenate([jnp.broadcast_to(c[h:h + 1], (t, c.shape[1])) for h in range(N_HEADS)], axis=0)


def _online_update(s, pv_fn, m_ref, l_ref, acc_ref):
    m_old = m_ref[...]
    m_new = jnp.maximum(m_old, jnp.max(s, axis=-1, keepdims=True))
    alpha = jnp.exp2(m_old - m_new)
    p = jnp.exp2(s - jnp.concatenate([m_new] * (s.shape[1] // LANES), axis=1))
    l_ref[...] = alpha * l_ref[...] + jnp.sum(p, axis=-1, keepdims=True)
    acc_ref[...] = jnp.concatenate([alpha] * (acc_ref.shape[1] // LANES), axis=1) * acc_ref[...] + pv_fn(p.astype(BF16))
    m_ref[...] = m_new


def _sample_attn_kernel(pt_ref, *refs, nb, pps, t_new, topk):
    del pt_ref
    pg_refs, lf_refs = refs[:pps], refs[pps:2 * pps]
    (bias_ref, qa_ref, qb_ref, qc_ref, qaf_ref, nkv_ref, nlf_ref, oa_ref, ob_ref, oc_ref,
     qbd_ref, qbdf_ref, km_ref, ms_ref, ls_ref, accs_ref,
     fm_ref, fl_ref, facc_ref, car_ref, dm_ref, dl_ref, dacc_ref, s_ref, vt_ref) = refs[2 * pps:]
    j = pl.program_id(1)
    rows = N_HEADS * t_new
    ppb = MOBA_BLOCK // PAGE_SIZE
    bps = pps // ppb
    n_steps = nb // bps
    lane = lax.broadcasted_iota(jnp.int32, (rows, LANES), 1)

    def init():
        for m, q_ref in enumerate((qa_ref, qb_ref, qc_ref)):
            qbd_ref[m] = _block_diag_q(q_ref[0].astype(F32)).astype(BF16)
        qbdf_ref[...] = _block_diag_q(qaf_ref[0])
        km_ref[...] = jnp.zeros(km_ref.shape, F32)
        ms_ref[...] = jnp.zeros(ms_ref.shape, F32)
        ls_ref[...] = jnp.zeros(ls_ref.shape, F32)
        car_ref[...] = jnp.zeros(car_ref.shape, F32)
        for m_ref, l_ref, acc_ref in ((fm_ref, fl_ref, facc_ref), (dm_ref, dl_ref, dacc_ref)):
            m_ref[...] = jnp.full(m_ref.shape, -jnp.inf, F32)
            l_ref[...] = jnp.zeros(l_ref.shape, F32)
            acc_ref[...] = jnp.zeros(acc_ref.shape, F32)

    def kt(m, off):
        lo_r = m * MIXER_COLS + off
        return jnp.concatenate([r[0, 0, lo_r:lo_r + KV_WIDTH, :] for r in pg_refs], axis=1)

    def score_stage():
        k0 = kt(0, 0)
        blk_lane = lax.broadcasted_iota(jnp.int32, km_ref.shape, 1)
        km = km_ref[...]
        for b in range(bps):
            cols = slice(b * MOBA_BLOCK, (b + 1) * MOBA_BLOCK)
            km = jnp.where(blk_lane == j * bps + b,
                           jnp.sum(k0[:, cols], axis=1, keepdims=True) * (1.0 / MOBA_BLOCK), km)
        km_ref[...] = km
        s_ref[0] = jnp.dot(qbd_ref[0], k0.astype(BF16), preferred_element_type=F32)
        carry, cums = car_ref[...][:, :1], []
        for r in lf_refs:
            cums.append(_lane_cumsum(r[0, 0]) + carry)
            carry = cums[-1][:, LANES - 1:LANES]
        car_ref[...] = jnp.broadcast_to(carry, car_ref.shape)
        s = jnp.dot(qbd_ref[1], kt(1, 0).astype(BF16), preferred_element_type=F32)
        s_ref[1] = s - _rows_from_heads(jnp.concatenate(cums, axis=1) * LOG2E, t_new)
        s = jnp.dot(qbd_ref[2], kt(2, 0).astype(BF16), preferred_element_type=F32)
        s_ref[2] = s + jnp.concatenate([bias_ref[0]] * N_HEADS, axis=0)
        for m in range(N_MIXERS):
            vt_ref[m] = kt(m, KV_WIDTH).astype(BF16)

    def update_stage(jp):
        s, v0 = s_ref[0], vt_ref[0]
        ms, ls = ms_ref[...], ls_ref[...]
        for b in range(bps):
            n = jp * bps + b
            cols = slice(b * MOBA_BLOCK, (b + 1) * MOBA_BLOCK)
            m_n = jnp.max(s[:, cols], axis=-1, keepdims=True)
            p = jnp.exp2(s[:, cols] - m_n)
            ms = jnp.where(lane == n, m_n, ms)
            ls = jnp.where(lane == n, jnp.sum(p, axis=-1, keepdims=True), ls)
            accs_ref[n] = lax.dot_general(p.astype(BF16), v0[:, cols], _NT, preferred_element_type=F32)
        ms_ref[...], ls_ref[...] = ms, ls
        pv_nt = lambda m: (lambda p: lax.dot_general(p, vt_ref[m], _NT, preferred_element_type=F32))
        _online_update(s_ref[1], pv_nt(1), fm_ref, fl_ref, facc_ref)
        _online_update(s_ref[2], pv_nt(2), dm_ref, dl_ref, dacc_ref)

    @pl.when(j == 0)
    def _():
        init()
        score_stage()

    @pl.when((j > 0) & (j < n_steps))
    def _():
        update_stage(j - 1)
        score_stage()

    @pl.when(j == n_steps)
    def _():
        update_stage(j - 1)
        nk = lambda m: nkv_ref[0, :, m * MIXER_COLS:m * MIXER_COLS + KV_WIDTH]
        nv = lambda m: nkv_ref[0, :, m * MIXER_COLS + KV_WIDTH:(m + 1) * MIXER_COLS]
        pv = lambda m: (lambda p: jnp.dot(p, nv(m), preferred_element_type=F32))
        row_t = lax.broadcasted_iota(jnp.int32, (rows, LANES), 0) & (t_new - 1)
        causal = (lane <= row_t) & (lane < t_new)

        s = lax.dot_general(qbd_ref[0], nk(0), _NT, preferred_element_type=F32)
        s = jnp.where(causal, s, NEG)
        m_o = jnp.max(s, axis=-1, keepdims=True)
        p = jnp.exp2(s - m_o)
        l_o = jnp.sum(p, axis=-1, keepdims=True)
        acc_o = pv(0)(p.astype(BF16))
        gate = jnp.dot(qbdf_ref[...], km_ref[...], precision=HIGHEST, preferred_element_type=F32)
        gate = jnp.where(lane < nb, gate, -jnp.inf)
        lane_f = lane.astype(F32)
        sel = lane < 0
        for _ in range(topk):
            mx = jnp.max(gate, axis=-1, keepdims=True)
            is_max = (gate == mx) & (mx > -jnp.inf)
            first = jnp.min(jnp.where(is_max, lane_f, 1e9), axis=-1, keepdims=True)
            pick = lane_f == first
            sel = sel | pick
            gate = jnp.where(pick, -jnp.inf, gate)
        ms = ms_ref[...]
        m_all = jnp.maximum(m_o, jnp.max(jnp.where(sel, ms, -jnp.inf), axis=-1, keepdims=True))
        w = jnp.where(sel, jnp.exp2(ms - m_all), 0.0)
        w_o = jnp.exp2(m_o - m_all)
        l_all = jnp.sum(w * ls_ref[...], axis=-1, keepdims=True) + w_o * l_o

        def merge(n, o):
            wn = jnp.sum(jnp.where(lane == n, w, 0.0), axis=-1, keepdims=True)
            return o + wn * accs_ref[n]

        o_a = lax.fori_loop(0, nb, merge, w_o * acc_o) / l_all
        oa_ref[0] = _token_major(o_a, t_new)

        cn = _lane_cumsum(nlf_ref[0]) + car_ref[...]
        s = lax.dot_general(qbd_ref[1], nk(1), _NT, preferred_element_type=F32) - _rows_from_heads(cn * LOG2E, t_new)
        _online_update(jnp.where(causal, s, NEG), pv(1), fm_ref, fl_ref, facc_ref)
        wide = lambda l_ref: jnp.concatenate([l_ref[...]] * (KV_WIDTH // LANES), axis=1)
        ob_ref[0] = _token_major(facc_ref[...] / wide(fl_ref), t_new)

        s = lax.dot_general(qbd_ref[2], nk(2), _NT, preferred_element_type=F32)
        s = s + jnp.concatenate([bias_ref[0, :, :LANES]] * N_HEADS, axis=0)
        _online_update(s, pv(2), dm_ref, dl_ref, dacc_ref)
        oc_ref[0] = _token_major(dacc_ref[...] / wide(dl_ref), t_new)


def _sample_attn(cache_t, cache_lft, page_table, layer, bias, qa, qb, qc, qaf, nkv, nlf):
    nseq, n_pages = page_table.shape
    t_new = qa.shape[1]
    nb = n_pages // (MOBA_BLOCK // PAGE_SIZE)
    pps = PAGES_PER_STEP
    rows = N_HEADS * t_new
    width = cache_t.shape[2]
    last = n_pages - 1
    page = lambda k: (lambda i, j, pt: (layer, pt[i, jnp.minimum(pps * j + k, last)], 0, 0))
    seq_blk = lambda shape: pl.BlockSpec((1,) + shape, lambda i, j, pt: (i, 0, 0))
    grid_spec = pltpu.PrefetchScalarGridSpec(
        num_scalar_prefetch=1,
        grid=(nseq, n_pages // pps + 1),
        in_specs=[pl.BlockSpec((1, 1, width, PAGE_SIZE), page(k)) for k in range(pps)] + [
            pl.BlockSpec((1, 1, N_HEADS, PAGE_SIZE), page(k)) for k in range(pps)] + [
            pl.BlockSpec((1, t_new, pps * PAGE_SIZE), lambda i, j, pt: (i, 0, j)),
            seq_blk((t_new, MIX_WIDTH)), seq_blk((t_new, MIX_WIDTH)), seq_blk((t_new, MIX_WIDTH)),
            seq_blk((t_new, MIX_WIDTH)),
            seq_blk((PAGE_SIZE, width)), seq_blk((N_HEADS, PAGE_SIZE)),
        ],
        out_specs=[seq_blk((t_new, MIX_WIDTH))] * 3,
        scratch_shapes=[
            pltpu.VMEM((N_MIXERS, rows, KV_WIDTH), BF16), pltpu.VMEM((rows, KV_WIDTH), F32),
            pltpu.VMEM((KV_WIDTH, KMEAN_ROWS), F32), pltpu.VMEM((rows, LANES), F32), pltpu.VMEM((rows, LANES), F32),
            pltpu.VMEM((nb, rows, KV_WIDTH), F32),
            pltpu.VMEM((rows, LANES), F32), pltpu.VMEM((rows, LANES), F32), pltpu.VMEM((rows, KV_WIDTH), F32),
            pltpu.VMEM((N_HEADS, LANES), F32),
            pltpu.VMEM((rows, LANES), F32), pltpu.VMEM((rows, LANES), F32), pltpu.VMEM((rows, KV_WIDTH), F32),
            pltpu.VMEM((N_MIXERS, rows, pps * PAGE_SIZE), F32), pltpu.VMEM((N_MIXERS, KV_WIDTH, pps * PAGE_SIZE), BF16),
        ],
    )
    kern = functools.partial(_sample_attn_kernel, nb=nb, pps=pps, t_new=t_new, topk=min(MOBA_TOPK, nb + 1))
    out = jax.ShapeDtypeStruct((nseq, t_new, MIX_WIDTH), F32)
    return pl.pallas_call(
        kern,
        grid_spec=grid_spec,
        out_shape=[out, out, out],
        compiler_params=_cp("arbitrary", "arbitrary"),
        name="sample_attn",
    )(page_table, *([cache_t] * pps), *([cache_lft] * pps), bias, qa, qb, qc, qaf, nkv, nlf)


def _out_kernel(oa_ref, ob_ref, oc_ref, sz_ref, sg_ref, x_ref, gate_ref, wb_ref, wo_ref, fw_ref, y_ref,
                *, final):
    d = x_ref.shape[2]
    merged = None
    for i, o_ref in enumerate((oa_ref, ob_ref, oc_ref)):
        t = (o_ref[0].astype(F32) * sz_ref[0, :, i * MIX_WIDTH:(i + 1) * MIX_WIDTH].astype(F32)).astype(BF16)
        br = sg_ref[0, :, i * d:(i + 1) * d].astype(F32) * jnp.dot(t, wb_ref[i], preferred_element_type=F32)
        merged = br if merged is None else merged + br
    y = x_ref[0] + gate_ref[0] * jnp.dot(merged.astype(BF16), wo_ref[...], preferred_element_type=F32)
    if final:
        ms = jnp.mean(y * y, axis=-1, keepdims=True)
        y = y * lax.rsqrt(ms + NORM_EPS) * fw_ref[...]
    y_ref[0] = y


def _out_proj(oa, ob, oc, sz, sg, x3, gate3, wb, wo, fw, tm, final):
    g, r, d = x3.shape
    rm = gate3.shape[1]
    tmod = 1 if rm == 1 else tm
    mod_map = (lambda b, i: (b, 0, 0)) if rm == 1 else (lambda b, i: (b, i, 0))
    row = lambda w: pl.BlockSpec((1, tm, w), lambda b, i: (b, i, 0))
    as3 = lambda a: a.reshape(g, r, a.shape[-1])
    return pl.pallas_call(
        functools.partial(_out_kernel, final=final),
        grid=(g, r // tm),
        in_specs=[
            row(MIX_WIDTH), row(MIX_WIDTH), row(MIX_WIDTH), row(N_MIXERS * MIX_WIDTH), row(N_MIXERS * d), row(d),
            pl.BlockSpec((1, tmod, d), mod_map),
            pl.BlockSpec(wb.shape, lambda b, i: (0, 0, 0)),
            pl.BlockSpec(wo.shape, lambda b, i: (0, 0)),
            pl.BlockSpec((1, d), lambda b, i: (0, 0)),
        ],
        out_specs=row(d),
        out_shape=jax.ShapeDtypeStruct((g, r, d), F32),
        compiler_params=_cp("arbitrary", "arbitrary"),
        name="out_proj",
    )(as3(oa), as3(ob), as3(oc), as3(sz), as3(sg), x3, gate3, wb, wo, fw.reshape(1, d))


def _rope_tables(pos):
    half = ROT_DIM // 2
    expo = jnp.arange(0, ROT_DIM, 2, dtype=F32) / ROT_DIM
    inv_freq = jnp.power(jnp.float32(ROPE_THETA), -expo)
    ang = pos.astype(F32)[:, None] * inv_freq[None, :]
    cos, sin = jnp.cos(ang), jnp.sin(ang)
    n = pos.shape[0]
    rest = HEAD_DIM - ROT_DIM
    a = jnp.concatenate([cos, cos, jnp.ones((n, rest), F32)], axis=1)
    bm = jnp.concatenate([-sin, jnp.zeros((n, half + rest), F32)], axis=1)
    cm = jnp.concatenate([jnp.zeros((n, half), F32), sin, jnp.zeros((n, rest), F32)], axis=1)
    rep = LANES // HEAD_DIM
    return tuple(jnp.tile(t, (1, rep)) for t in (a, bm, cm))


def _split_weights(w_in_l, b_f_l):
    o = np.cumsum([0, MIX_WIDTH, KV_WIDTH, KV_WIDTH, MIX_WIDTH,
                   MIX_WIDTH, KV_WIDTH, KV_WIDTH, N_HEADS, MIX_WIDTH,
                   MIX_WIDTH, KV_WIDTH, KV_WIDTH, IDX_HEADS * IDX_DIM, IDX_DIM, IDX_HEADS, MIX_WIDTH]).tolist()
    col = lambda i: w_in_l[:, o[i]:o[i + 1]]
    qa, ka, va, za, qb, kb, vb, fb, zb, qc, kc, vc, qi, ki, wi, zc = (col(i) for i in range(16))
    g = w_in_l[:, o[16]:]
    d = w_in_l.shape[0]
    zpad = lambda w: jnp.zeros((d, w), w_in_l.dtype)
    w_q = jnp.concatenate([qa, qb, qc], axis=1)
    w_kv = jnp.concatenate([ka, va, kb, vb, kc, vc], axis=1)
    w_z = jnp.concatenate([za, zb, zc], axis=1)
    w_misc = jnp.concatenate([qi, ki, zpad(LANES - IDX_DIM), wi, zpad(LANES - IDX_HEADS),
                              fb, zpad(LANES - N_HEADS)], axis=1)
    bf_row = jnp.concatenate([b_f_l, jnp.zeros((LANES - N_HEADS,), b_f_l.dtype)]).reshape(1, LANES)
    cast = lambda w: w.astype(BF16)
    return cast(w_q), cast(w_kv), cast(w_z), cast(g), cast(w_misc), bf_row.astype(F32)


def _project(x3, scale3, shift3, tabs, weights, norm_w_l, tm, kv_out=None):
    g, r, d = x3.shape
    w_q, w_kv, w_z, w_g, w_misc, bf_row = weights
    h = _norm_mod(x3, norm_w_l, scale3, shift3, tm).reshape(g * r, d)
    q_outs = _proj_call(
        _projq_kernel, h, w_q, tabs, (), [(MIX_WIDTH, BF16, True)] * 3 + [(MIX_WIDTH, F32, True)],
        tm, w_q.shape[1], "proj_q")
    if kv_out is None:
        kv, kvb = _proj_call(
            _projkv_kernel, h, w_kv, tabs, (), [(w_kv.shape[1], F32, True), (w_kv.shape[1], BF16, True)],
            tm, w_kv.shape[1], "proj_kv")
    else:
        kvt_prev, layer, depth = kv_out
        kvt, kvb, km = _proj_kv_prompt(h, w_kv, tabs, kvt_prev, layer, depth, g, r, tm)
        kv = (kvt, km)
    (sz,) = _proj_call(_projz_kernel, h, w_z, (), (), [(w_z.shape[1], BF16, True)], tm, w_z.shape[1], "proj_z")
    (sg,) = _proj_call(_projg_kernel, h, w_g, (), (), [(w_g.shape[1], BF16, False)], tm, d, "proj_g")
    misc = _proj_call(
        _projmisc_kernel, h, w_misc, tabs, (bf_row,),
        [(MIX_WIDTH, BF16, True), (IDX_DIM, F32, True), (2 * IDX_DIM, BF16, True),
         (IDX_HEADS, F32, True), (N_HEADS, F32, True)],
        tm, MISC_N, "proj_misc")
    return q_outs, kv, kvb, sz, sg, misc


def _prompt_layer(x3, scale3, shift3, gate3, tabs, weights, norm_w_l, wb_l, wo_l, fw, kvt_prev, layer, depth,
                  *, tm, tm_out, tq, tk, final):
    b, t, _ = x3.shape
    (qa, qb, qc, qaf), (kvt, km8), kvb, sz, sg, (qi, ki, ki2, wi, lf) = _project(
        x3, scale3, shift3, tabs, weights, norm_w_l, tm, kv_out=(kvt_prev, layer, depth))
    s3 = lambda a: a.reshape(b, t, a.shape[-1])
    kvb3, lf3 = s3(kvb), s3(lf)
    km = km8[:, :tm // MOBA_BLOCK].reshape(b, t // MOBA_BLOCK, KV_WIDTH)
    km = jnp.pad(km, ((0, 0), (0, KMEAN_ROWS - km.shape[1]), (0, 0)))
    cum = _cumsum(jnp.swapaxes(lf3, 1, 2))
    o_a = _moba_attn(s3(qaf), s3(qa), km, kvb3, tq)
    o_b = _fox_attn(s3(qb), cum, kvb3, tq, tk)
    o_c = _dsa_attn(s3(qi), s3(wi), s3(ki2), s3(qc), kvb3, tq, tk)
    x_new = _out_proj(o_a, o_b, o_c, sz, sg, x3, gate3, wb_l, wo_l, fw, tm_out, final)
    return x_new, kvt, lf3, s3(ki)


def _sample_layer(x3, scale3, shift3, gate3, tabs, weights, norm_w_l, wb_l, wo_l, fw, caches, page_table, layer,
                  *, nseq, final):
    _, m, _ = x3.shape
    t = m // nseq
    cache_t, cache_kxt, cache_lft = caches
    (qa, qb, qc, qaf), kv, kvb, sz, sg, (qi, ki, ki2, wi, lf) = _project(
        x3, scale3, shift3, tabs, weights, norm_w_l, m)
    s3 = lambda a: a.reshape(nseq, t, a.shape[-1])
    pad_slots = lambda a: jnp.pad(a, ((0, 0), (0, PAGE_SIZE - t), (0, 0)))
    lp = (page_table.shape[1] + PAGES_PER_STEP) * PAGE_SIZE
    isc = _sample_index(cache_kxt, page_table, layer, s3(qi), s3(wi), pad_slots(s3(ki2)), lp)
    bias = _topk_bias(isc.reshape(m, lp), page_table.shape[1] * PAGE_SIZE, t, min(m, 64)).reshape(nseq, t, lp)
    nlf = jnp.pad(jnp.swapaxes(s3(lf), 1, 2), ((0, 0), (0, 0), (0, PAGE_SIZE - t)))
    o_a, o_b, o_c = _sample_attn(cache_t, cache_lft, page_table, layer, bias, s3(qa), s3(qb), s3(qc), s3(qaf),
                                 pad_slots(s3(kvb)), nlf)
    x_new = _out_proj(o_a, o_b, o_c, sz, sg, x3, gate3, wb_l, wo_l, fw, m, final)
    return x_new, s3(kv), s3(lf), s3(ki)


def kernel(x_prompt, x_sample, cache_kv, cache_logf, cache_kidx, page_table, c_prompt, c_sample,
           norm_w, w_ada, b_ada, w_in, b_f, w_branch, w_out, final_norm_w):
    depth = norm_w.shape[0]
    bp, tp, d = x_prompt.shape
    bs, ts, _ = x_sample.shape
    n_pool = cache_kv.shape[1]
    n_pages = page_table.shape[1]
    past_len = n_pages * PAGE_SIZE
    assert ts == SUBLANES and n_pages % PAGES_PER_STEP == 0 and PAGES_PER_STEP % (MOBA_BLOCK // PAGE_SIZE) == 0
    assert tp % MOBA_BLOCK == 0 and tp % min(PROJ_ROWS, tp) == 0 and tp % min(512, tp) == 0

    nc = bp + bs
    rc = -(-nc // SUBLANES) * SUBLANES
    c_all = jnp.concatenate([c_prompt, c_sample, jnp.zeros((rc - nc, d), F32)], axis=0)
    mod = _modulation(c_all, w_ada, b_ada)

    tabs_p = _rope_tables(jnp.arange(tp, dtype=jnp.int32))
    tabs_s = _rope_tables(jnp.tile(past_len + jnp.arange(ts, dtype=jnp.int32), bs))

    cache_t = jnp.transpose(cache_kv, (0, 1, 3, 4, 5, 6, 2)).reshape(depth, n_pool, N_MIXERS * MIXER_COLS, PAGE_SIZE)
    cache_kxt = jnp.swapaxes(cache_kidx, 2, 3)
    cache_lft = jnp.swapaxes(cache_logf, 2, 3)

    ms = bs * ts
    xp = x_prompt
    xs = x_sample.reshape(1, ms, d)
    new_p, new_s = [], []
    kvt = jnp.zeros((depth, bp, N_MIXERS * MIXER_COLS, tp), F32)
    for l in range(depth):
        weights = _split_weights(w_in[l], b_f[l])
        wb_l = w_branch[l].astype(BF16)
        wo_l = w_out[l].astype(BF16)
        final = l == depth - 1
        shift, scale, gate = (mod[l, :, i * d:(i + 1) * d] for i in range(3))
        p3 = lambda a: a[:bp].reshape(bp, 1, d)
        s3 = lambda a: jnp.repeat(a[bp:nc], ts, axis=0).reshape(1, ms, d)

        xp, kvt, lf_p, ki_p = _prompt_layer(
            xp, p3(scale), p3(shift), p3(gate), tabs_p, weights, norm_w[l], wb_l, wo_l, final_norm_w, kvt, l, depth,
            tm=min(PROJ_ROWS, tp), tm_out=min(512, tp), tq=min(128, tp), tk=MOBA_BLOCK, final=final)
        xs, kv_s, lf_s, ki_s = _sample_layer(
            xs, s3(scale), s3(shift), s3(gate), tabs_s, weights, norm_w[l], wb_l, wo_l, final_norm_w,
            (cache_t, cache_kxt, cache_lft), page_table, l, nseq=bs, final=final)
        new_p.append((None, lf_p, ki_p))
        new_s.append((kv_s, lf_s, ki_s))

    kv_shape = (N_MIXERS, 2, N_KV_HEADS, HEAD_DIM)
    stack = lambda items, i: jnp.stack([n[i] for n in items])
    kv_prompt = jnp.transpose(kvt.reshape(depth, bp, *kv_shape, tp), (0, 1, 6, 2, 3, 4, 5))
    kv_sample = stack(new_s, 0).reshape(depth, bs, ts, *kv_shape)
    return (xp, xs.reshape(bs, ts, d), kv_prompt, stack(new_p, 1), stack(new_p, 2),
            kv_sample, stack(new_s, 1), stack(new_s, 2))
```

```python
import functools

import numpy as np
import jax
import jax.numpy as jnp
from jax import lax
from jax.experimental import pallas as pl
from jax.experimental.pallas import tpu as pltpu

F32 = jnp.float32
BF16 = jnp.bfloat16
HIGHEST = lax.Precision.HIGHEST

N_MIXERS = 3
N_HEADS = 8
N_KV_HEADS = 4
HEAD_DIM = 64
MIX_WIDTH = N_HEADS * HEAD_DIM
KV_WIDTH = N_KV_HEADS * HEAD_DIM
MIXER_COLS = 2 * KV_WIDTH
ROT_DIM = HEAD_DIM // 4
ROPE_THETA = 500000.0
ATTN_SCALE = HEAD_DIM ** -0.5
LOG2E = 1.4426950408889634
MOBA_BLOCK = 256
MOBA_TOPK = 3
IDX_HEADS = 8
IDX_DIM = 64
IDX_SCALE = (IDX_DIM * IDX_HEADS) ** -0.5
DSA_TOPK = 256
NORM_EPS = 1e-6
PAGE_SIZE = 128

LANES = 128
SUBLANES = 8
KMEAN_ROWS = 128
NEG = -1e30
F32_LOWEST = -3.0e38
VMEM_LIMIT = 56 * 1024 * 1024
PAGES_PER_STEP = 8
INDEX_PAGES_PER_STEP = 16
DSA_SWEEP = 1024
PROJ_ROWS = 1024
TOPK_COARSE_BITS = 28

_NT = (((1,), (1,)), ((), ()))


def _cp(*sem):
    return pltpu.CompilerParams(dimension_semantics=sem, vmem_limit_bytes=VMEM_LIMIT)


def _sigmoid(x):
    return 1.0 / (1.0 + jnp.exp(-x))


def _mod_kernel(c_ref, w_ref, b_ref, o_ref):
    c = c_ref[...]
    sc = c * _sigmoid(c)
    o_ref[0] = jnp.dot(sc, w_ref[0], precision=HIGHEST, preferred_element_type=F32) + b_ref[0]


def _modulation(c_all, w_ada, b_ada):
    depth, d, d3 = w_ada.shape
    rc = c_all.shape[0]
    nj = d3 // d
    return pl.pallas_call(
        _mod_kernel,
        grid=(depth, nj),
        in_specs=[
            pl.BlockSpec((rc, d), lambda l, j: (0, 0)),
            pl.BlockSpec((1, d, d), lambda l, j: (l, 0, j)),
            pl.BlockSpec((1, 1, d), lambda l, j: (l, 0, j)),
        ],
        out_specs=pl.BlockSpec((1, rc, d), lambda l, j: (l, 0, j)),
        out_shape=jax.ShapeDtypeStruct((depth, rc, d3), F32),
        compiler_params=_cp("arbitrary", "arbitrary"),
        name="adaln_mod",
    )(c_all, w_ada, b_ada.reshape(depth, 1, d3))


def _norm_kernel(x_ref, w_ref, sc_ref, sh_ref, h_ref):
    x = x_ref[0]
    ms = jnp.mean(x * x, axis=-1, keepdims=True)
    y = x * lax.rsqrt(ms + NORM_EPS) * w_ref[...]
    h_ref[0] = (y * (1.0 + sc_ref[0]) + sh_ref[0]).astype(BF16)


def _norm_mod(x3, norm_w, scale3, shift3, tm):
    g, r, d = x3.shape
    rm = scale3.shape[1]
    tmod = 1 if rm == 1 else tm
    mod_map = (lambda b, i: (b, 0, 0)) if rm == 1 else (lambda b, i: (b, i, 0))
    return pl.pallas_call(
        _norm_kernel,
        grid=(g, r // tm),
        in_specs=[
            pl.BlockSpec((1, tm, d), lambda b, i: (b, i, 0)),
            pl.BlockSpec((1, d), lambda b, i: (0, 0)),
            pl.BlockSpec((1, tmod, d), mod_map),
            pl.BlockSpec((1, tmod, d), mod_map),
        ],
        out_specs=pl.BlockSpec((1, tm, d), lambda b, i: (b, i, 0)),
        out_shape=jax.ShapeDtypeStruct((g, r, d), BF16),
        compiler_params=_cp("arbitrary", "arbitrary"),
        name="norm_mod",
    )(x3, norm_w.reshape(1, d), scale3, shift3)


def _rope_chunk(x, a, bm, cm):
    return x * a + pltpu.roll(x, LANES - ROT_DIM // 2, 1) * bm + pltpu.roll(x, ROT_DIM // 2, 1) * cm


def _projq_kernel(h_ref, w_ref, ra_ref, rb_ref, rc_ref, qa_ref, qb_ref, qc_ref, qaf_ref):
    y = jnp.dot(h_ref[...], w_ref[...], preferred_element_type=F32)
    a, bm, cm = ra_ref[...], rb_ref[...], rc_ref[...]
    for m, o_ref in enumerate((qa_ref, qb_ref, qc_ref)):
        for c in range(MIX_WIDTH // LANES):
            x = y[:, m * MIX_WIDTH + c * LANES:m * MIX_WIDTH + (c + 1) * LANES]
            if m != 1:
                x = _rope_chunk(x, a, bm, cm)
            if m == 0:
                qaf_ref[:, c * LANES:(c + 1) * LANES] = x
            o_ref[:, c * LANES:(c + 1) * LANES] = (x * (ATTN_SCALE * LOG2E)).astype(BF16)


def _projkv_kernel(h_ref, w_ref, ra_ref, rb_ref, rc_ref, kv_ref, kvb_ref):
    y = jnp.dot(h_ref[...], w_ref[...], preferred_element_type=F32)
    a, bm, cm = ra_ref[...], rb_ref[...], rc_ref[...]
    per_mixer = MIXER_COLS // LANES
    for c in range(N_MIXERS * per_mixer):
        x = y[:, c * LANES:(c + 1) * LANES]
        mixer, within = divmod(c, per_mixer)
        if mixer != 1 and within < KV_WIDTH // LANES:
            x = _rope_chunk(x, a, bm, cm)
        kv_ref[:, c * LANES:(c + 1) * LANES] = x
        kvb_ref[:, c * LANES:(c + 1) * LANES] = x.astype(BF16)


def _projkv_prompt_kernel(h_ref, w_ref, ra_ref, rb_ref, rc_ref, *refs):
    kvt_ref, kvb_ref, km_ref = refs[-3:]
    y = jnp.dot(h_ref[...], w_ref[...], preferred_element_type=F32)
    a, bm, cm = ra_ref[...], rb_ref[...], rc_ref[...]
    tm = y.shape[0]
    per_mixer = MIXER_COLS // LANES
    moba_k = []
    for c in range(N_MIXERS * per_mixer):
        x = y[:, c * LANES:(c + 1) * LANES]
        mixer, within = divmod(c, per_mixer)
        if mixer != 1 and within < KV_WIDTH // LANES:
            x = _rope_chunk(x, a, bm, cm)
        if mixer == 0 and within < KV_WIDTH // LANES:
            moba_k.append(x)
        kvb_ref[:, c * LANES:(c + 1) * LANES] = x.astype(BF16)
        kvt_ref[0, 0, c * LANES:(c + 1) * LANES, :] = x.T
    means = [jnp.concatenate([jnp.sum(x[g * MOBA_BLOCK:(g + 1) * MOBA_BLOCK], axis=0, keepdims=True)
                              for x in moba_k], axis=1) * (1.0 / MOBA_BLOCK) for g in range(tm // MOBA_BLOCK)]
    km_ref[0] = jnp.concatenate(means + [jnp.zeros((SUBLANES - len(means), KV_WIDTH), F32)], axis=0)


def _proj_kv_prompt(h, w, tabs, kvt_prev, layer, depth, b, t, tm):
    m, d = h.shape
    n = w.shape[1]
    nt = t // tm
    assert tm % MOBA_BLOCK == 0 and tm // MOBA_BLOCK <= SUBLANES
    in_specs = [pl.BlockSpec((tm, d), lambda j, i: (i, 0)), pl.BlockSpec((d, n), lambda j, i: (0, 0))]
    in_specs += [pl.BlockSpec((tm, LANES), lambda j, i: (i % nt, 0)) for _ in tabs]
    in_specs.append(pl.BlockSpec(memory_space=pl.ANY))
    args = [h, w, *tabs, kvt_prev]
    aliases = {len(args) - 1: 0}
    return pl.pallas_call(
        _projkv_prompt_kernel,
        grid=(1, m // tm),
        in_specs=in_specs,
        out_specs=[pl.BlockSpec((1, 1, n, tm), lambda j, i: (layer, i // nt, 0, i % nt)),
                   pl.BlockSpec((tm, n), lambda j, i: (i, 0)),
                   pl.BlockSpec((1, SUBLANES, KV_WIDTH), lambda j, i: (i, 0, 0))],
        out_shape=[jax.ShapeDtypeStruct((depth, b, n, t), F32), jax.ShapeDtypeStruct((m, n), BF16),
                   jax.ShapeDtypeStruct((m // tm, SUBLANES, KV_WIDTH), F32)],
        input_output_aliases=aliases,
        compiler_params=_cp("arbitrary", "arbitrary"),
        name="proj_kv_prompt",
    )(*args)


def _projz_kernel(h_ref, w_ref, o_ref):
    y = jnp.dot(h_ref[...], w_ref[...], preferred_element_type=F32)
    o_ref[...] = (y * _sigmoid(y)).astype(o_ref.dtype)


def _projg_kernel(h_ref, w_ref, o_ref):
    y = jnp.dot(h_ref[...], w_ref[...], preferred_element_type=F32)
    o_ref[...] = _sigmoid(y).astype(o_ref.dtype)


MISC_KI = MIX_WIDTH
MISC_WI = MISC_KI + LANES
MISC_FB = MISC_WI + LANES
MISC_N = MISC_FB + LANES


def _projmisc_kernel(h_ref, w_ref, ra_ref, rb_ref, rc_ref, bf_ref,
                     qi_ref, ki_ref, ki2_ref, wi_ref, lf_ref):
    y = jnp.dot(h_ref[...], w_ref[...], preferred_element_type=F32)
    a, bm, cm = ra_ref[...], rb_ref[...], rc_ref[...]
    for c in range(MIX_WIDTH // LANES):
        x = _rope_chunk(y[:, c * LANES:(c + 1) * LANES], a, bm, cm)
        qi_ref[:, c * LANES:(c + 1) * LANES] = x.astype(BF16)
    ki = _rope_chunk(y[:, MISC_KI:MISC_KI + LANES], a, bm, cm)
    ki_ref[...] = ki[:, :IDX_DIM]
    ki2_ref[...] = (ki + pltpu.roll(ki, IDX_DIM, 1)).astype(BF16)
    wi_ref[...] = y[:, MISC_WI:MISC_WI + IDX_HEADS]
    f = y[:, MISC_FB:MISC_FB + LANES] + bf_ref[...]
    logf = jnp.minimum(f, 0.0) - jnp.log(1.0 + jnp.exp(-jnp.abs(f)))
    lf_ref[...] = logf[:, :N_HEADS]


def _proj_call(kernel, h, w, tabs, extra, outs, tm, tn, name):
    m, d = h.shape
    n = w.shape[1]
    nt = tabs[0].shape[0] // tm if tabs else 1
    in_specs = [pl.BlockSpec((tm, d), lambda j, i: (i, 0)),
                pl.BlockSpec((d, tn), lambda j, i: (0, j))]
    in_specs += [pl.BlockSpec((tm, LANES), lambda j, i: (i % nt, 0)) for _ in tabs]
    in_specs += [pl.BlockSpec(e.shape, lambda j, i: (0, 0)) for e in extra]
    out_specs = [pl.BlockSpec((tm, wd if full else tn), (lambda j, i: (i, 0)) if full else (lambda j, i: (i, j)))
                 for (wd, _, full) in outs]
    out_shape = [jax.ShapeDtypeStruct((m, wd), dt) for (wd, dt, _) in outs]
    return pl.pallas_call(
        kernel,
        grid=(n // tn, m // tm),
        in_specs=in_specs,
        out_specs=out_specs,
        out_shape=out_shape,
        compiler_params=_cp("arbitrary", "arbitrary"),
        name=name,
    )(h, w, *tabs, *extra)


CUM_CHUNK = 256


def _cumsum_kernel(x_ref, o_ref):
    n = x_ref.shape[2] // CUM_CHUNK
    r = lax.broadcasted_iota(jnp.int32, (CUM_CHUNK, CUM_CHUNK), 0)
    c = lax.broadcasted_iota(jnp.int32, (CUM_CHUNK, CUM_CHUNK), 1)
    tri = (r <= c).astype(F32)

    def body(i, carry):
        st = pl.multiple_of(i * CUM_CHUNK, CUM_CHUNK)
        x = x_ref[0, :, pl.ds(st, CUM_CHUNK)]
        y = jnp.dot(x, tri, precision=HIGHEST, preferred_element_type=F32) + carry
        o_ref[0, :, pl.ds(st, CUM_CHUNK)] = y * LOG2E
        return y[:, CUM_CHUNK - 1:CUM_CHUNK]

    lax.fori_loop(0, n, body, jnp.zeros((x_ref.shape[1], 1), F32))


def _cumsum(lft):
    b, h, lp = lft.shape
    return pl.pallas_call(
        _cumsum_kernel,
        grid=(b,),
        in_specs=[pl.BlockSpec((1, h, lp), lambda i: (i, 0, 0))],
        out_specs=pl.BlockSpec((1, h, lp), lambda i: (i, 0, 0)),
        out_shape=jax.ShapeDtypeStruct((b, h, lp), F32),
        compiler_params=_cp("arbitrary"),
        name="fox_cumsum",
    )(lft)


def _stack4(blk0, blk1):
    lane = lax.broadcasted_iota(jnp.int32, blk0.shape, 1)
    lo = lane < HEAD_DIM
    return jnp.concatenate([
        jnp.where(lo, blk0, 0.0),
        jnp.where(lo, pltpu.roll(blk0, HEAD_DIM, 1), 0.0),
        jnp.where(lo, 0.0, pltpu.roll(blk1, HEAD_DIM, 1)),
        jnp.where(lo, 0.0, blk1)], axis=0)


def _unstack4(o, tq):
    lane = lax.broadcasted_iota(jnp.int32, (tq, LANES), 1)
    lo = lane < HEAD_DIM
    b0 = jnp.where(lo, o[:tq], pltpu.roll(o[tq:2 * tq], HEAD_DIM, 1))
    b1 = jnp.where(lo, pltpu.roll(o[2 * tq:3 * tq], HEAD_DIM, 1), o[3 * tq:])
    return b0, b1


def _stacked_queries(q_ref):
    return [_stack4(q_ref[0, :, (2 * c) * LANES:(2 * c + 1) * LANES].astype(F32),
                    q_ref[0, :, (2 * c + 1) * LANES:(2 * c + 2) * LANES].astype(F32)) for c in range(2)]


def _flash_scratch(tq, tk):
    rows = 4 * tq
    return [pltpu.VMEM((2, rows, tk), F32), pltpu.VMEM((2, rows, tk), BF16),
            pltpu.VMEM((2, rows, LANES), F32), pltpu.VMEM((2, rows, LANES), F32), pltpu.VMEM((2, rows, LANES), F32),
            pltpu.VMEM((2, rows, 2 * LANES), F32), pltpu.VMEM((2, rows, tk), F32)]


def _flash_pipeline(n_all, qs, k_ref, v_ref, add_bias, diag_mask, scratch, o_ref, tq, tk, k_aug=None,
                    tile_bias=None):
    s_ref, p_ref, m0_ref, m1_ref, mx_ref, acc_ref, mb_ref = scratch
    m_slots = (m0_ref, m1_ref)
    m0_ref[...] = jnp.full(m0_ref.shape, -jnp.inf, F32)
    acc_ref[...] = jnp.zeros(acc_ref.shape, F32)
    if diag_mask is not None:
        @pl.when((pl.program_id(0) == 0) & (pl.program_id(1) == 0))
        def _():
            mb_ref[0] = jnp.zeros(mb_ref.shape[1:], F32)
        mb_ref[1] = diag_mask
    last = n_all - 1
    ones = jnp.ones((tk, LANES), BF16)

    def stage_pv(i, par):
        jc = jnp.clip(i - 2, 0, last)
        stc = pl.multiple_of(jc * tk, tk)
        for c in range(2):
            vo = jnp.concatenate([v_ref[0, pl.ds(stc, tk), c * LANES:(c + 1) * LANES], ones], axis=1)
            al = jnp.exp2(m_slots[par][c] - m_slots[1 - par][c])
            acc_ref[c] = (jnp.concatenate([al, al], axis=1) * acc_ref[c]
                          + jnp.dot(p_ref[c], vo, preferred_element_type=F32))

    def stage_softmax(par):
        for c in range(2):
            m_new = jnp.maximum(m_slots[1 - par][c], jnp.max(mx_ref[c], axis=-1, keepdims=True))
            m_slots[par][c] = m_new
            p_ref[c] = jnp.exp2(s_ref[c] - jnp.concatenate([m_new] * (tk // LANES), axis=1)).astype(BF16)

    def stage_scores(i):
        ja = jnp.minimum(i, last)
        sta = pl.multiple_of(ja * tk, tk)
        pen = jnp.where(i <= last, 0.0, NEG).astype(F32)
        shared = None if tile_bias is None else tile_bias(ja, pen)
        for c in range(2):
            kt = k_ref[0, pl.ds(sta, tk), c * LANES:(c + 1) * LANES]
            if k_aug is not None:
                kt = jnp.concatenate([kt, k_aug(c, ja, i <= last)], axis=1)
            s = lax.dot_general(qs[c], kt, _NT, preferred_element_type=F32)
            s = add_bias(c, s, ja, pen)
            if shared is not None:
                s = s + jnp.concatenate([shared] * 4, axis=0)
            if diag_mask is not None:
                s = s + mb_ref[(ja == last).astype(jnp.int32)]
            s_ref[c] = s
            mx_ref[c] = functools.reduce(jnp.maximum, [s[:, k * LANES:(k + 1) * LANES] for k in range(tk // LANES)])

    def step(i, par):
        stage_pv(i, par)
        stage_softmax(par)
        stage_scores(i)

    def step2(ii, carry):
        step(2 * ii + 2, 0)
        step(2 * ii + 3, 1)
        return carry

    stage_scores(0)
    stage_softmax(1)
    stage_scores(1)
    lax.fori_loop(0, (n_all + 1) // 2, step2, 0)
    for c in range(2):
        b0, b1 = _unstack4(acc_ref[c, :, :LANES] / acc_ref[c, :, LANES:], tq)
        o_ref[0, :, (2 * c) * LANES:(2 * c + 1) * LANES] = b0.astype(o_ref.dtype)
        o_ref[0, :, (2 * c + 1) * LANES:(2 * c + 2) * LANES] = b1.astype(o_ref.dtype)


def _diag_mask(q_lo, n_all, tq, tk):
    r1 = lax.broadcasted_iota(jnp.int32, (tq, 1), 0) + q_lo
    qpos = jnp.concatenate([r1] * 4, axis=0)
    kpos = (n_all - 1) * tk + lax.broadcasted_iota(jnp.int32, (4 * tq, tk), 1)
    return jnp.where(kpos <= qpos, 0.0, NEG)


def _attn_call(kern, name, ins, in_specs, b, t, tq, scratch):
    return pl.pallas_call(
        kern,
        grid=(b, t // tq),
        in_specs=in_specs,
        out_specs=pl.BlockSpec((1, tq, MIX_WIDTH), lambda i, j: (i, j, 0)),
        out_shape=jax.ShapeDtypeStruct((b, t, MIX_WIDTH), BF16),
        scratch_shapes=scratch,
        compiler_params=_cp("arbitrary", "arbitrary"),
        name=name,
    )(*ins)


_q_spec = lambda tq: pl.BlockSpec((1, tq, MIX_WIDTH), lambda i, j: (i, j, 0))
_kv_spec = lambda lp, col: pl.BlockSpec((1, lp, KV_WIDTH), lambda i, j: (i, 0, col))


def _moba_kernel(qf_ref, q_ref, km_ref, k_ref, v_ref, o_ref, *scratch, tq, topk, nblk):
    tk = MOBA_BLOCK
    q_lo = pl.program_id(1) * tq
    own = q_lo // MOBA_BLOCK
    rows = 4 * tq
    blk = lax.broadcasted_iota(jnp.int32, (nblk, rows), 0)
    blk_f = blk.astype(F32)
    q_st = _stacked_queries(q_ref)
    qs = []
    for c, qf in enumerate(_stacked_queries(qf_ref)):
        gate = lax.dot_general(km_ref[0, :nblk, c * LANES:(c + 1) * LANES], qf, _NT, precision=HIGHEST,
                               preferred_element_type=F32)
        gate = jnp.where(blk < own, gate, -jnp.inf)
        sb = jnp.where(blk == own, 0.0, NEG)
        for _ in range(topk):
            mx = jnp.max(gate, axis=0, keepdims=True)
            is_max = (gate == mx) & (mx > -jnp.inf)
            first = jnp.min(jnp.where(is_max, blk_f, 1e9), axis=0, keepdims=True)
            pick = blk_f == first
            sb = jnp.where(pick, 0.0, sb)
            gate = jnp.where(pick, -jnp.inf, gate)
        sb = jnp.concatenate([sb, jnp.full((LANES - nblk, rows), NEG, F32)], axis=0)
        qs.append(jnp.concatenate([q_st[c], sb.T], axis=1).astype(BF16))

    klane = lax.broadcasted_iota(jnp.int32, (tk, LANES), 1)

    def k_aug(c, j, valid):
        return jnp.where(klane == jnp.where(valid, j, LANES - 1), 1.0, 0.0).astype(BF16)

    _flash_pipeline(own + 1, qs, k_ref, v_ref, lambda c, s, j, pen: s, _diag_mask(q_lo, own + 1, tq, tk), scratch,
                    o_ref, tq, tk, k_aug=k_aug)


def _moba_attn(qf, q, km, kvb, tq):
    b, t, _ = q.shape
    lp = kvb.shape[1]
    nblk = -(-(lp // MOBA_BLOCK) // SUBLANES) * SUBLANES
    assert nblk < LANES
    kern = functools.partial(_moba_kernel, tq=tq, topk=min(MOBA_TOPK, lp // MOBA_BLOCK), nblk=nblk)
    specs = [_q_spec(tq), _q_spec(tq), pl.BlockSpec((1, KMEAN_ROWS, KV_WIDTH), lambda i, j: (i, 0, 0)),
             _kv_spec(lp, 0), _kv_spec(lp, 1)]
    return _attn_call(kern, "moba_attn", (qf, q, km, kvb, kvb), specs, b, t, tq, _flash_scratch(tq, MOBA_BLOCK))


def _fox_kernel(q_ref, cum_ref, k_ref, v_ref, o_ref, *scratch, tq, tk):
    q_lo = pl.program_id(1) * tq
    n_all = (q_lo + tq + tk - 1) // tk
    qs = [q.astype(BF16) for q in _stacked_queries(q_ref)]

    def add_bias(c, s, j, pen):
        st = pl.multiple_of(j * tk, tk)
        parts = [s[hh * tq:(hh + 1) * tq] - (cum_ref[0, 4 * c + hh:4 * c + hh + 1, pl.ds(st, tk)] - pen)
                 for hh in range(4)]
        return jnp.concatenate(parts, axis=0)

    _flash_pipeline(n_all, qs, k_ref, v_ref, add_bias, _diag_mask(q_lo, n_all, tq, tk), scratch, o_ref, tq, tk)


def _fox_attn(q, cum, kvb, tq, tk):
    b, t, _ = q.shape
    lp = kvb.shape[1]
    kern = functools.partial(_fox_kernel, tq=tq, tk=tk)
    specs = [_q_spec(tq), pl.BlockSpec((1, N_HEADS, lp), lambda i, j: (i, 0, 0)), _kv_spec(lp, 2), _kv_spec(lp, 3)]
    return _attn_call(kern, "fox_attn", (q, cum, kvb, kvb), specs, b, t, tq, _flash_scratch(tq, tk))


def _key_to_float(key):
    bits = jnp.where(key < 0, key & jnp.int32(0x7FFFFFFF), ~key)
    return lax.bitcast_convert_type(bits, F32)


def _topk_threshold(count, vshape, need, few, idx_bits):
    def bit_body(i, key):
        cand = key | lax.shift_left(jnp.int32(1), 31 - i)
        thr_c = _key_to_float(cand)
        return jnp.where(count(lambda x, idx: x >= thr_c) >= need, cand, key)

    no_cut = jnp.full(vshape, 2 ** 30, jnp.int32)

    def too_many(key):
        thr = _key_to_float(key)
        return jnp.max(jnp.where(few, 0.0, count(lambda x, idx: x >= thr) - need)) > 0.0

    def tie_cut(thr):
        r = need - count(lambda x, idx: x > thr)

        def jb(i, cut):
            cand = cut | lax.shift_left(jnp.int32(1), idx_bits - 1 - i)
            cnt = count(lambda x, idx: (x == thr) & (idx < cand))
            return jnp.where(cnt < r, cand, cut)

        return lax.fori_loop(0, idx_bits, jb, jnp.zeros(vshape, jnp.int32))

    def refine(key):
        key = lax.fori_loop(TOPK_COARSE_BITS, 32, bit_body, key)
        return key, lax.cond(too_many(key), lambda k: tie_cut(_key_to_float(k)), lambda k: no_cut, key)

    key = lax.fori_loop(0, TOPK_COARSE_BITS, bit_body, jnp.zeros(vshape, jnp.int32))
    key, cut = lax.cond(too_many(key), refine, lambda k: (k, no_cut), key)
    return jnp.where(few, F32_LOWEST, _key_to_float(key)), jnp.where(few, 2 ** 30, cut)


def _dsa_kernel(qi_ref, w_ref, kx_ref, q_ref, k_ref, v_ref, o_ref, isc_ref, *scratch, tq, tk, sweep,
                topk, idx_bits):
    q_lo = pl.program_id(1) * tq
    n_all = (q_lo + tq + tk - 1) // tk
    lane = lax.broadcasted_iota(jnp.int32, (tq, LANES), 1)
    lo = lane < HEAD_DIM
    qpos = lax.broadcasted_iota(jnp.int32, (1, tq), 1) + q_lo
    krow = lax.broadcasted_iota(jnp.int32, (tk, 1), 0)

    pieces = []
    for h in range(IDX_HEADS):
        chunk = qi_ref[0, :, (h // 2) * LANES:(h // 2 + 1) * LANES].astype(F32)
        pieces.append(jnp.where(lo, chunk, 0.0) if h % 2 == 0 else jnp.where(lo, 0.0, chunk))
    qstack_t = jnp.concatenate(pieces, axis=0).T.astype(BF16)
    wrow = jnp.concatenate([w_ref[0, h:h + 1, :] for h in range(IDX_HEADS)], axis=1)

    def idx_tile(j):
        start = pl.multiple_of(j * tk, tk)
        sc = jnp.dot(kx_ref[0, pl.ds(start, tk), :], qstack_t, preferred_element_type=F32)
        contrib = jnp.maximum(sc, 0.0) * wrow
        isc = contrib[:, :tq]
        for h in range(1, IDX_HEADS):
            isc = isc + contrib[:, h * tq:(h + 1) * tq]
        isc_ref[pl.ds(start, tk), :] = jnp.where(j * tk + krow <= qpos, isc * IDX_SCALE, -jnp.inf)

    def idx_body(jj, carry):
        idx_tile(2 * jj)
        idx_tile(2 * jj + 1)
        return carry

    n_idx = (n_all + 1) // 2 * 2
    lax.fori_loop(0, n_idx // 2, idx_body, 0)

    tiles_per_sweep = sweep // tk
    n_sweeps = (n_all + tiles_per_sweep - 1) // tiles_per_sweep

    def pad_body(j, carry):
        isc_ref[pl.ds(pl.multiple_of(j * tk, tk), tk), :] = jnp.full((tk, tq), -jnp.inf, F32)
        return carry

    lax.fori_loop(n_idx, n_sweeps * tiles_per_sweep, pad_body, 0)
    grp = 8 * SUBLANES
    sub = lax.broadcasted_iota(jnp.int32, (grp, tq), 0)

    def count(pred):
        def body(j, acc):
            start = pl.multiple_of(j * sweep, sweep)
            x = isc_ref[pl.ds(start, sweep), :]
            for r in range(sweep // grp):
                idx = j * sweep + r * grp + sub
                acc = acc + jnp.where(pred(x[r * grp:(r + 1) * grp], idx), 1.0, 0.0)
            return acc
        acc = lax.fori_loop(0, n_sweeps, body, jnp.zeros((grp, tq), F32))
        return jnp.sum(acc, axis=0, keepdims=True)

    few = qpos + 1 <= topk
    thr, cut = _topk_threshold(count, (1, tq), jnp.float32(topk), few, idx_bits)

    qs = [q.astype(BF16) for q in _stacked_queries(q_ref)]

    def tile_bias(j, pen):
        x = isc_ref[pl.ds(pl.multiple_of(j * tk, tk), tk), :]
        keep = (x > thr) | ((x == thr) & (j * tk + krow <= cut))
        return jnp.where(keep, 0.0, NEG).T + pen

    _flash_pipeline(n_all, qs, k_ref, v_ref, lambda c, s, j, pen: s, None, scratch, o_ref, tq, tk,
                    tile_bias=tile_bias)


def _dsa_attn(qi, wi, kx2, q, kvb, tq, tk):
    b, t, _ = q.shape
    lp = kvb.shape[1]
    sweep = max(tk, min(DSA_SWEEP, lp))
    assert lp % sweep == 0 and sweep % tk == 0 and lp % (2 * tk) == 0
    kern = functools.partial(_dsa_kernel, tq=tq, tk=tk, sweep=sweep, topk=min(DSA_TOPK, lp // 4),
                             idx_bits=max(1, int(lp).bit_length()))
    specs = [_q_spec(tq), pl.BlockSpec((1, IDX_HEADS, tq), lambda i, j: (i, 0, j)),
             pl.BlockSpec((1, lp, 2 * IDX_DIM), lambda i, j: (i, 0, 0)), _q_spec(tq), _kv_spec(lp, 4), _kv_spec(lp, 5)]
    scratch = [pltpu.VMEM((lp, tq), F32)] + _flash_scratch(tq, tk)
    return _attn_call(kern, "dsa_attn", (qi, jnp.swapaxes(wi, 1, 2), kx2, q, kvb, kvb), specs, b, t, tq, scratch)


def _sample_index_kernel(pt_ref, *refs, ppi, n_steps, l_past, t_new):
    del pt_ref
    kx_refs = refs[:ppi]
    qi_ref, w_ref, nkx_ref, o_ref, isc_ref = refs[ppi:]
    j = pl.program_id(1)
    lp = isc_ref.shape[1]
    lane = lax.broadcasted_iota(jnp.int32, (t_new, LANES), 1)
    lo = lane < HEAD_DIM
    pieces, wpieces = [], []
    for h in range(IDX_HEADS):
        chunk = qi_ref[0, :, (h // 2) * LANES:(h // 2 + 1) * LANES].astype(F32)
        pieces.append(jnp.where(lo, chunk, 0.0) if h % 2 == 0 else jnp.where(lo, 0.0, chunk))
        wpieces.append(jnp.broadcast_to(w_ref[0, :, h:h + 1], (t_new, LANES)))
    qstack = jnp.concatenate(pieces, axis=0).astype(BF16)
    wb = jnp.concatenate(wpieces, axis=0)

    def head_sum(sc):
        contrib = jnp.maximum(sc, 0.0) * jnp.concatenate([wb] * (sc.shape[1] // LANES), axis=1)
        isc = contrib[:t_new]
        for h in range(1, IDX_HEADS):
            isc = isc + contrib[h * t_new:(h + 1) * t_new]
        return isc * IDX_SCALE

    kxt = jnp.concatenate([r[0, 0] for r in kx_refs], axis=1)
    rhs = jnp.concatenate([kxt, kxt], axis=0).astype(BF16)
    start = pl.multiple_of(j * (ppi * PAGE_SIZE), ppi * PAGE_SIZE)
    isc_ref[:, pl.ds(start, ppi * PAGE_SIZE)] = head_sum(jnp.dot(qstack, rhs, preferred_element_type=F32))

    @pl.when(j == n_steps - 1)
    def _():
        scn = lax.dot_general(qstack, nkx_ref[0], _NT, preferred_element_type=F32)
        row = lax.broadcasted_iota(jnp.int32, (t_new, LANES), 0)
        isc_ref[:, l_past:l_past + LANES] = jnp.where((lane <= row) & (lane < t_new), head_sum(scn), -jnp.inf)
        if lp > l_past + LANES:
            isc_ref[:, l_past + LANES:] = jnp.full((t_new, lp - l_past - LANES), -jnp.inf, F32)
        o_ref[0] = isc_ref[...]


def _topk_bias_kernel(x_ref, o_ref, *, sweep, topk, l_past, t_new, idx_bits):
    rows, lp = x_ref.shape
    lane = lax.broadcasted_iota(jnp.int32, (rows, LANES), 1)

    def count(pred):
        def body(j, acc):
            x = x_ref[:, pl.ds(pl.multiple_of(j * sweep, sweep), sweep)]
            for cc in range(sweep // LANES):
                idx = j * sweep + cc * LANES + lane
                acc = acc + jnp.where(pred(x[:, cc * LANES:(cc + 1) * LANES], idx), 1.0, 0.0)
            return acc
        acc = lax.fori_loop(0, lp // sweep, body, jnp.zeros((rows, LANES), F32))
        return jnp.sum(acc, axis=-1, keepdims=True)

    qpos1 = (lax.broadcasted_iota(jnp.int32, (rows, 1), 0) & (t_new - 1)) + l_past
    thr, cut = _topk_threshold(count, (rows, 1), jnp.float32(topk), qpos1 + 1 <= topk, idx_bits)
    kiota = lax.broadcasted_iota(jnp.int32, (rows, sweep), 1)

    def bias_body(j, carry):
        start = pl.multiple_of(j * sweep, sweep)
        x = x_ref[:, pl.ds(start, sweep)]
        keep = (x > thr) | ((x == thr) & (j * sweep + kiota <= cut))
        o_ref[:, pl.ds(start, sweep)] = jnp.where(keep, 0.0, NEG)
        return carry

    lax.fori_loop(0, lp // sweep, bias_body, 0)


def _topk_bias(isc, l_past, t_new, rows_per_step):
    rows, lp = isc.shape
    sweep = min(DSA_SWEEP, lp)
    assert lp % sweep == 0 and rows % rows_per_step == 0 and rows_per_step % t_new == 0
    kern = functools.partial(_topk_bias_kernel, sweep=sweep, topk=min(DSA_TOPK, (l_past + t_new) // 4),
                             l_past=l_past, t_new=t_new, idx_bits=max(1, int(lp).bit_length()))
    return pl.pallas_call(
        kern,
        grid=(rows // rows_per_step,),
        in_specs=[pl.BlockSpec((rows_per_step, lp), lambda i: (i, 0))],
        out_specs=pl.BlockSpec((rows_per_step, lp), lambda i: (i, 0)),
        out_shape=jax.ShapeDtypeStruct((rows, lp), F32),
        compiler_params=_cp("arbitrary"),
        name="dsa_topk_bias",
    )(isc)


def _sample_index(cache_kxt, page_table, layer, qi, wi, nkx2, lp):
    nseq, n_pages = page_table.shape
    t_new = qi.shape[1]
    ppi = INDEX_PAGES_PER_STEP if n_pages % INDEX_PAGES_PER_STEP == 0 else PAGES_PER_STEP
    n_steps = n_pages // ppi
    l_past = n_pages * PAGE_SIZE
    pg = lambda pp: (lambda i, j, pt: (layer, pt[i, j * ppi + pp], 0, 0))
    grid_spec = pltpu.PrefetchScalarGridSpec(
        num_scalar_prefetch=1,
        grid=(nseq, n_steps),
        in_specs=[pl.BlockSpec((1, 1, IDX_DIM, PAGE_SIZE), pg(pp)) for pp in range(ppi)] + [
            pl.BlockSpec((1, t_new, MIX_WIDTH), lambda i, j, pt: (i, 0, 0)),
            pl.BlockSpec((1, t_new, IDX_HEADS), lambda i, j, pt: (i, 0, 0)),
            pl.BlockSpec((1, PAGE_SIZE, 2 * IDX_DIM), lambda i, j, pt: (i, 0, 0)),
        ],
        out_specs=pl.BlockSpec((1, t_new, lp), lambda i, j, pt: (i, 0, 0)),
        scratch_shapes=[pltpu.VMEM((t_new, lp), F32)],
    )
    kern = functools.partial(_sample_index_kernel, ppi=ppi, n_steps=n_steps, l_past=l_past, t_new=t_new)
    return pl.pallas_call(
        kern,
        grid_spec=grid_spec,
        out_shape=jax.ShapeDtypeStruct((nseq, t_new, lp), F32),
        compiler_params=_cp("arbitrary", "arbitrary"),
        name="dsa_index_paged",
    )(page_table, *([cache_kxt] * ppi), qi, wi, nkx2)


def _block_diag_q(q):
    t = q.shape[0]
    lane = lax.broadcasted_iota(jnp.int32, (t, LANES), 1)
    lo = lane < HEAD_DIM
    zero = jnp.zeros((t, LANES), F32)
    out = []
    for h in range(N_HEADS):
        src = q[:, (h // 2) * LANES:(h // 2 + 1) * LANES]
        kvh = h // (N_HEADS // N_KV_HEADS)
        chunk, half = divmod(kvh, 2)
        piece = src if h % 2 == half else pltpu.roll(src, HEAD_DIM, 1)
        piece = jnp.where(lo, piece, 0.0) if half == 0 else jnp.where(lo, 0.0, piece)
        out.append(jnp.concatenate([piece, zero] if chunk == 0 else [zero, piece], axis=1))
    return jnp.concatenate(out, axis=0)


def _token_major(o, t):
    lane = lax.broadcasted_iota(jnp.int32, (t, LANES), 1)
    lo = lane < HEAD_DIM
    out = []
    for co in range(MIX_WIDTH // LANES):
        chunk, half = divmod(co, 2)
        x = o[(2 * co) * t:(2 * co + 1) * t, chunk * LANES:(chunk + 1) * LANES]
        y = o[(2 * co + 1) * t:(2 * co + 2) * t, chunk * LANES:(chunk + 1) * LANES]
        if half == 0:
            out.append(jnp.where(lo, x, pltpu.roll(y, HEAD_DIM, 1)))
        else:
            out.append(jnp.where(lo, pltpu.roll(x, HEAD_DIM, 1), y))
    return jnp.concatenate(out, axis=1)


def _lane_cumsum(x):
    lane = lax.broadcasted_iota(jnp.int32, x.shape, 1)
    d = 1
    while d < LANES:
        x = x + jnp.where(lane >= d, pltpu.roll(x, d, 1), 0.0)
        d *= 2
    return x


def _rows_from_heads(c, t):
    return jnp.concatenate([jnp.broadcast_to(c[h:h + 1], (t, c.shape[1])) for h in range(N_HEADS)], axis=0)


def _online_update(s, pv_fn, m_ref, l_ref, acc_ref):
    m_old = m_ref[...]
    m_new = jnp.maximum(m_old, jnp.max(s, axis=-1, keepdims=True))
    alpha = jnp.exp2(m_old - m_new)
    p = jnp.exp2(s - jnp.concatenate([m_new] * (s.shape[1] // LANES), axis=1))
    l_ref[...] = alpha * l_ref[...] + jnp.sum(p, axis=-1, keepdims=True)
    acc_ref[...] = jnp.concatenate([alpha] * (acc_ref.shape[1] // LANES), axis=1) * acc_ref[...] + pv_fn(p.astype(BF16))
    m_ref[...] = m_new


def _sample_attn_kernel(pt_ref, cache_ref, *refs, nb, pps, t_new, topk, layer):
    lf_refs = refs[:pps]
    (bias_ref, qa_ref, qb_ref, qc_ref, qaf_ref, nkv_ref, nlf_ref, oa_ref, ob_ref, oc_ref,
     qbd_ref, qbdf_ref, km_ref, ms_ref, ls_ref, accs_ref,
     fm_ref, fl_ref, facc_ref, car_ref, dm_ref, dl_ref, dacc_ref, s_ref, vt_ref, pbuf_ref, sem_ref) = refs[pps:]
    j = pl.program_id(1)
    rows = N_HEADS * t_new
    ppb = MOBA_BLOCK // PAGE_SIZE
    bps = pps // ppb
    n_steps = nb // bps
    lane = lax.broadcasted_iota(jnp.int32, (rows, LANES), 1)

    g = pl.program_id(0) * n_steps + j
    n_page_steps = pl.num_programs(0) * n_steps
    slot = lax.rem(g, 2)

    def page_copies(gg):
        seq, step = lax.div(gg, n_steps), lax.rem(gg, n_steps)
        return [pltpu.make_async_copy(cache_ref.at[layer, pt_ref[seq, step * pps + k]],
                                      pbuf_ref.at[lax.rem(gg, 2), k], sem_ref.at[lax.rem(gg, 2), k])
                for k in range(pps)]

    def start_pages(gg):
        for k, cp in enumerate(page_copies(gg)):
            cp.start(priority=k % 2)

    def fetch_pages():
        @pl.when(g == 0)
        def _():
            start_pages(g)

        @pl.when(g + 1 < n_page_steps)
        def _():
            start_pages(g + 1)

        for cp in page_copies(g):
            cp.wait()

    def init():
        for m, q_ref in enumerate((qa_ref, qb_ref, qc_ref)):
            qbd_ref[m] = _block_diag_q(q_ref[0].astype(F32)).astype(BF16)
        qbdf_ref[...] = _block_diag_q(qaf_ref[0])
        km_ref[...] = jnp.zeros(km_ref.shape, F32)
        ms_ref[...] = jnp.zeros(ms_ref.shape, F32)
        ls_ref[...] = jnp.zeros(ls_ref.shape, F32)
        car_ref[...] = jnp.zeros(car_ref.shape, F32)
        for m_ref, l_ref, acc_ref in ((fm_ref, fl_ref, facc_ref), (dm_ref, dl_ref, dacc_ref)):
            m_ref[...] = jnp.full(m_ref.shape, -jnp.inf, F32)
            l_ref[...] = jnp.zeros(l_ref.shape, F32)
            acc_ref[...] = jnp.zeros(acc_ref.shape, F32)

    def kt(m, off):
        lo_r = m * MIXER_COLS + off
        return jnp.concatenate([pbuf_ref[slot, k, lo_r:lo_r + KV_WIDTH, :] for k in range(pps)], axis=1)

    def score_stage():
        fetch_pages()
        k0 = kt(0, 0)
        blk_lane = lax.broadcasted_iota(jnp.int32, km_ref.shape, 1)
        km = km_ref[...]
        for b in range(bps):
            cols = slice(b * MOBA_BLOCK, (b + 1) * MOBA_BLOCK)
            km = jnp.where(blk_lane == j * bps + b,
                           jnp.sum(k0[:, cols], axis=1, keepdims=True) * (1.0 / MOBA_BLOCK), km)
        km_ref[...] = km
        s_ref[0] = jnp.dot(qbd_ref[0], k0.astype(BF16), preferred_element_type=F32)
        carry, cums = car_ref[...][:, :1], []
        for r in lf_refs:
            cums.append(_lane_cumsum(r[0, 0]) + carry)
            carry = cums[-1][:, LANES - 1:LANES]
        car_ref[...] = jnp.broadcast_to(carry, car_ref.shape)
        s = jnp.dot(qbd_ref[1], kt(1, 0).astype(BF16), preferred_element_type=F32)
        s_ref[1] = s - _rows_from_heads(jnp.concatenate(cums, axis=1) * LOG2E, t_new)
        s = jnp.dot(qbd_ref[2], kt(2, 0).astype(BF16), preferred_element_type=F32)
        s_ref[2] = s + jnp.concatenate([bias_ref[0]] * N_HEADS, axis=0)
        for m in range(N_MIXERS):
            vt_ref[m] = kt(m, KV_WIDTH).astype(BF16)

    def update_stage(jp):
        s, v0 = s_ref[0], vt_ref[0]
        ms, ls = ms_ref[...], ls_ref[...]
        for b in range(bps):
            n = jp * bps + b
            cols = slice(b * MOBA_BLOCK, (b + 1) * MOBA_BLOCK)
            m_n = jnp.max(s[:, cols], axis=-1, keepdims=True)
            p = jnp.exp2(s[:, cols] - m_n)
            ms = jnp.where(lane == n, m_n, ms)
            ls = jnp.where(lane == n, jnp.sum(p, axis=-1, keepdims=True), ls)
            accs_ref[n] = lax.dot_general(p.astype(BF16), v0[:, cols], _NT, preferred_element_type=F32)
        ms_ref[...], ls_ref[...] = ms, ls
        pv_nt = lambda m: (lambda p: lax.dot_general(p, vt_ref[m], _NT, preferred_element_type=F32))
        _online_update(s_ref[1], pv_nt(1), fm_ref, fl_ref, facc_ref)
        _online_update(s_ref[2], pv_nt(2), dm_ref, dl_ref, dacc_ref)

    @pl.when(j == 0)
    def _():
        init()
        score_stage()

    @pl.when((j > 0) & (j < n_steps))
    def _():
        update_stage(j - 1)
        score_stage()

    @pl.when(j == n_steps)
    def _():
        update_stage(j - 1)
        nk = lambda m: nkv_ref[0, :, m * MIXER_COLS:m * MIXER_COLS + KV_WIDTH]
        nv = lambda m: nkv_ref[0, :, m * MIXER_COLS + KV_WIDTH:(m + 1) * MIXER_COLS]
        pv = lambda m: (lambda p: jnp.dot(p, nv(m), preferred_element_type=F32))
        row_t = lax.broadcasted_iota(jnp.int32, (rows, LANES), 0) & (t_new - 1)
        causal = (lane <= row_t) & (lane < t_new)

        s = lax.dot_general(qbd_ref[0], nk(0), _NT, preferred_element_type=F32)
        s = jnp.where(causal, s, NEG)
        m_o = jnp.max(s, axis=-1, keepdims=True)
        p = jnp.exp2(s - m_o)
        l_o = jnp.sum(p, axis=-1, keepdims=True)
        acc_o = pv(0)(p.astype(BF16))
        gate = jnp.dot(qbdf_ref[...], km_ref[...], precision=HIGHEST, preferred_element_type=F32)
        gate = jnp.where(lane < nb, gate, -jnp.inf)
        lane_f = lane.astype(F32)
        sel = lane < 0
        for _ in range(topk):
            mx = jnp.max(gate, axis=-1, keepdims=True)
            is_max = (gate == mx) & (mx > -jnp.inf)
            first = jnp.min(jnp.where(is_max, lane_f, 1e9), axis=-1, keepdims=True)
            pick = lane_f == first
            sel = sel | pick
            gate = jnp.where(pick, -jnp.inf, gate)
        ms = ms_ref[...]
        m_all = jnp.maximum(m_o, jnp.max(jnp.where(sel, ms, -jnp.inf), axis=-1, keepdims=True))
        w = jnp.where(sel, jnp.exp2(ms - m_all), 0.0)
        w_o = jnp.exp2(m_o - m_all)
        l_all = jnp.sum(w * ls_ref[...], axis=-1, keepdims=True) + w_o * l_o

        def merge(n, o):
            wn = jnp.sum(jnp.where(lane == n, w, 0.0), axis=-1, keepdims=True)
            return o + wn * accs_ref[n]

        o_a = lax.fori_loop(0, nb, merge, w_o * acc_o) / l_all
        oa_ref[0] = _token_major(o_a, t_new)

        cn = _lane_cumsum(nlf_ref[0]) + car_ref[...]
        s = lax.dot_general(qbd_ref[1], nk(1), _NT, preferred_element_type=F32) - _rows_from_heads(cn * LOG2E, t_new)
        _online_update(jnp.where(causal, s, NEG), pv(1), fm_ref, fl_ref, facc_ref)
        wide = lambda l_ref: jnp.concatenate([l_ref[...]] * (KV_WIDTH // LANES), axis=1)
        ob_ref[0] = _token_major(facc_ref[...] / wide(fl_ref), t_new)

        s = lax.dot_general(qbd_ref[2], nk(2), _NT, preferred_element_type=F32)
        s = s + jnp.concatenate([bias_ref[0, :, :LANES]] * N_HEADS, axis=0)
        _online_update(s, pv(2), dm_ref, dl_ref, dacc_ref)
        oc_ref[0] = _token_major(dacc_ref[...] / wide(dl_ref), t_new)


def _sample_attn(cache_t, cache_lft, page_table, layer, bias, qa, qb, qc, qaf, nkv, nlf):
    nseq, n_pages = page_table.shape
    t_new = qa.shape[1]
    nb = n_pages // (MOBA_BLOCK // PAGE_SIZE)
    pps = PAGES_PER_STEP
    rows = N_HEADS * t_new
    width = cache_t.shape[2]
    last = n_pages - 1
    page = lambda k: (lambda i, j, pt: (layer, pt[i, jnp.minimum(pps * j + k, last)], 0, 0))
    seq_blk = lambda shape: pl.BlockSpec((1,) + shape, lambda i, j, pt: (i, 0, 0))
    grid_spec = pltpu.PrefetchScalarGridSpec(
        num_scalar_prefetch=1,
        grid=(nseq, n_pages // pps + 1),
        in_specs=[pl.BlockSpec(memory_space=pl.ANY)] + [
            pl.BlockSpec((1, 1, N_HEADS, PAGE_SIZE), page(k)) for k in range(pps)] + [
            pl.BlockSpec((1, t_new, pps * PAGE_SIZE), lambda i, j, pt: (i, 0, j)),
            seq_blk((t_new, MIX_WIDTH)), seq_blk((t_new, MIX_WIDTH)), seq_blk((t_new, MIX_WIDTH)),
            seq_blk((t_new, MIX_WIDTH)),
            seq_blk((PAGE_SIZE, width)), seq_blk((N_HEADS, PAGE_SIZE)),
        ],
        out_specs=[seq_blk((t_new, MIX_WIDTH))] * 3,
        scratch_shapes=[
            pltpu.VMEM((N_MIXERS, rows, KV_WIDTH), BF16), pltpu.VMEM((rows, KV_WIDTH), F32),
            pltpu.VMEM((KV_WIDTH, KMEAN_ROWS), F32), pltpu.VMEM((rows, LANES), F32), pltpu.VMEM((rows, LANES), F32),
            pltpu.VMEM((nb, rows, KV_WIDTH), F32),
            pltpu.VMEM((rows, LANES), F32), pltpu.VMEM((rows, LANES), F32), pltpu.VMEM((rows, KV_WIDTH), F32),
            pltpu.VMEM((N_HEADS, LANES), F32),
            pltpu.VMEM((rows, LANES), F32), pltpu.VMEM((rows, LANES), F32), pltpu.VMEM((rows, KV_WIDTH), F32),
            pltpu.VMEM((N_MIXERS, rows, pps * PAGE_SIZE), F32), pltpu.VMEM((N_MIXERS, KV_WIDTH, pps * PAGE_SIZE), BF16),
            pltpu.VMEM((2, pps, width, PAGE_SIZE), F32), pltpu.SemaphoreType.DMA((2, pps)),
        ],
    )
    kern = functools.partial(_sample_attn_kernel, nb=nb, pps=pps, t_new=t_new, topk=min(MOBA_TOPK, nb + 1),
                             layer=layer)
    out = jax.ShapeDtypeStruct((nseq, t_new, MIX_WIDTH), F32)
    return pl.pallas_call(
        kern,
        grid_spec=grid_spec,
        out_shape=[out, out, out],
        compiler_params=_cp("arbitrary", "arbitrary"),
        name="sample_attn",
    )(page_table, cache_t, *([cache_lft] * pps), bias, qa, qb, qc, qaf, nkv, nlf)


def _out_kernel(oa_ref, ob_ref, oc_ref, sz_ref, sg_ref, x_ref, gate_ref, wb_ref, wo_ref, fw_ref, y_ref,
                *, final):
    d = x_ref.shape[2]
    merged = None
    for i, o_ref in enumerate((oa_ref, ob_ref, oc_ref)):
        t = (o_ref[0].astype(F32) * sz_ref[0, :, i * MIX_WIDTH:(i + 1) * MIX_WIDTH].astype(F32)).astype(BF16)
        br = sg_ref[0, :, i * d:(i + 1) * d].astype(F32) * jnp.dot(t, wb_ref[i], preferred_element_type=F32)
        merged = br if merged is None else merged + br
    y = x_ref[0] + gate_ref[0] * jnp.dot(merged.astype(BF16), wo_ref[...], preferred_element_type=F32)
    if final:
        ms = jnp.mean(y * y, axis=-1, keepdims=True)
        y = y * lax.rsqrt(ms + NORM_EPS) * fw_ref[...]
    y_ref[0] = y


def _out_proj(oa, ob, oc, sz, sg, x3, gate3, wb, wo, fw, tm, final):
    g, r, d = x3.shape
    rm = gate3.shape[1]
    tmod = 1 if rm == 1 else tm
    mod_map = (lambda b, i: (b, 0, 0)) if rm == 1 else (lambda b, i: (b, i, 0))
    row = lambda w: pl.BlockSpec((1, tm, w), lambda b, i: (b, i, 0))
    as3 = lambda a: a.reshape(g, r, a.shape[-1])
    return pl.pallas_call(
        functools.partial(_out_kernel, final=final),
        grid=(g, r // tm),
        in_specs=[
            row(MIX_WIDTH), row(MIX_WIDTH), row(MIX_WIDTH), row(N_MIXERS * MIX_WIDTH), row(N_MIXERS * d), row(d),
            pl.BlockSpec((1, tmod, d), mod_map),
            pl.BlockSpec(wb.shape, lambda b, i: (0, 0, 0)),
            pl.BlockSpec(wo.shape, lambda b, i: (0, 0)),
            pl.BlockSpec((1, d), lambda b, i: (0, 0)),
        ],
        out_specs=row(d),
        out_shape=jax.ShapeDtypeStruct((g, r, d), F32),
        compiler_params=_cp("arbitrary", "arbitrary"),
        name="out_proj",
    )(as3(oa), as3(ob), as3(oc), as3(sz), as3(sg), x3, gate3, wb, wo, fw.reshape(1, d))


def _rope_tables(pos):
    half = ROT_DIM // 2
    expo = jnp.arange(0, ROT_DIM, 2, dtype=F32) / ROT_DIM
    inv_freq = jnp.power(jnp.float32(ROPE_THETA), -expo)
    ang = pos.astype(F32)[:, None] * inv_freq[None, :]
    cos, sin = jnp.cos(ang), jnp.sin(ang)
    n = pos.shape[0]
    rest = HEAD_DIM - ROT_DIM
    a = jnp.concatenate([cos, cos, jnp.ones((n, rest), F32)], axis=1)
    bm = jnp.concatenate([-sin, jnp.zeros((n, half + rest), F32)], axis=1)
    cm = jnp.concatenate([jnp.zeros((n, half), F32), sin, jnp.zeros((n, rest), F32)], axis=1)
    rep = LANES // HEAD_DIM
    return tuple(jnp.tile(t, (1, rep)) for t in (a, bm, cm))


def _split_weights(w_in_l, b_f_l):
    o = np.cumsum([0, MIX_WIDTH, KV_WIDTH, KV_WIDTH, MIX_WIDTH,
                   MIX_WIDTH, KV_WIDTH, KV_WIDTH, N_HEADS, MIX_WIDTH,
                   MIX_WIDTH, KV_WIDTH, KV_WIDTH, IDX_HEADS * IDX_DIM, IDX_DIM, IDX_HEADS, MIX_WIDTH]).tolist()
    col = lambda i: w_in_l[:, o[i]:o[i + 1]]
    qa, ka, va, za, qb, kb, vb, fb, zb, qc, kc, vc, qi, ki, wi, zc = (col(i) for i in range(16))
    g = w_in_l[:, o[16]:]
    d = w_in_l.shape[0]
    zpad = lambda w: jnp.zeros((d, w), w_in_l.dtype)
    w_q = jnp.concatenate([qa, qb, qc], axis=1)
    w_kv = jnp.concatenate([ka, va, kb, vb, kc, vc], axis=1)
    w_z = jnp.concatenate([za, zb, zc], axis=1)
    w_misc = jnp.concatenate([qi, ki, zpad(LANES - IDX_DIM), wi, zpad(LANES - IDX_HEADS),
                              fb, zpad(LANES - N_HEADS)], axis=1)
    bf_row = jnp.concatenate([b_f_l, jnp.zeros((LANES - N_HEADS,), b_f_l.dtype)]).reshape(1, LANES)
    cast = lambda w: w.astype(BF16)
    return cast(w_q), cast(w_kv), cast(w_z), cast(g), cast(w_misc), bf_row.astype(F32)


def _project(x3, scale3, shift3, tabs, weights, norm_w_l, tm, kv_out=None):
    g, r, d = x3.shape
    w_q, w_kv, w_z, w_g, w_misc, bf_row = weights
    h = _norm_mod(x3, norm_w_l, scale3, shift3, tm).reshape(g * r, d)
    q_outs = _proj_call(
        _projq_kernel, h, w_q, tabs, (), [(MIX_WIDTH, BF16, True)] * 3 + [(MIX_WIDTH, F32, True)],
        tm, w_q.shape[1], "proj_q")
    if kv_out is None:
        kv, kvb = _proj_call(
            _projkv_kernel, h, w_kv, tabs, (), [(w_kv.shape[1], F32, True), (w_kv.shape[1], BF16, True)],
            tm, w_kv.shape[1], "proj_kv")
    else:
        kvt_prev, layer, depth = kv_out
        kvt, kvb, km = _proj_kv_prompt(h, w_kv, tabs, kvt_prev, layer, depth, g, r, tm)
        kv = (kvt, km)
    (sz,) = _proj_call(_projz_kernel, h, w_z, (), (), [(w_z.shape[1], BF16, True)], tm, w_z.shape[1], "proj_z")
    (sg,) = _proj_call(_projg_kernel, h, w_g, (), (), [(w_g.shape[1], BF16, False)], tm, d, "proj_g")
    misc = _proj_call(
        _projmisc_kernel, h, w_misc, tabs, (bf_row,),
        [(MIX_WIDTH, BF16, True), (IDX_DIM, F32, True), (2 * IDX_DIM, BF16, True),
         (IDX_HEADS, F32, True), (N_HEADS, F32, True)],
        tm, MISC_N, "proj_misc")
    return q_outs, kv, kvb, sz, sg, misc


def _prompt_layer(x3, scale3, shift3, gate3, tabs, weights, norm_w_l, wb_l, wo_l, fw, kvt_prev, layer, depth,
                  *, tm, tm_out, tq, tk, final):
    b, t, _ = x3.shape
    (qa, qb, qc, qaf), (kvt, km8), kvb, sz, sg, (qi, ki, ki2, wi, lf) = _project(
        x3, scale3, shift3, tabs, weights, norm_w_l, tm, kv_out=(kvt_prev, layer, depth))
    s3 = lambda a: a.reshape(b, t, a.shape[-1])
    kvb3, lf3 = s3(kvb), s3(lf)
    km = km8[:, :tm // MOBA_BLOCK].reshape(b, t // MOBA_BLOCK, KV_WIDTH)
    km = jnp.pad(km, ((0, 0), (0, KMEAN_ROWS - km.shape[1]), (0, 0)))
    cum = _cumsum(jnp.swapaxes(lf3, 1, 2))
    o_a = _moba_attn(s3(qaf), s3(qa), km, kvb3, tq)
    o_b = _fox_attn(s3(qb), cum, kvb3, tq, tk)
    o_c = _dsa_attn(s3(qi), s3(wi), s3(ki2), s3(qc), kvb3, tq, tk)
    x_new = _out_proj(o_a, o_b, o_c, sz, sg, x3, gate3, wb_l, wo_l, fw, tm_out, final)
    return x_new, kvt, lf3, s3(ki)


def _sample_layer(x3, scale3, shift3, gate3, tabs, weights, norm_w_l, wb_l, wo_l, fw, caches, page_table, layer,
                  *, nseq, final):
    _, m, _ = x3.shape
    t = m // nseq
    cache_t, cache_kxt, cache_lft = caches
    (qa, qb, qc, qaf), kv, kvb, sz, sg, (qi, ki, ki2, wi, lf) = _project(
        x3, scale3, shift3, tabs, weights, norm_w_l, m)
    s3 = lambda a: a.reshape(nseq, t, a.shape[-1])
    pad_slots = lambda a: jnp.pad(a, ((0, 0), (0, PAGE_SIZE - t), (0, 0)))
    lp = (page_table.shape[1] + PAGES_PER_STEP) * PAGE_SIZE
    isc = _sample_index(cache_kxt, page_table, layer, s3(qi), s3(wi), pad_slots(s3(ki2)), lp)
    bias = _topk_bias(isc.reshape(m, lp), page_table.shape[1] * PAGE_SIZE, t, min(m, 64)).reshape(nseq, t, lp)
    nlf = jnp.pad(jnp.swapaxes(s3(lf), 1, 2), ((0, 0), (0, 0), (0, PAGE_SIZE - t)))
    o_a, o_b, o_c = _sample_attn(cache_t, cache_lft, page_table, layer, bias, s3(qa), s3(qb), s3(qc), s3(qaf),
                                 pad_slots(s3(kvb)), nlf)
    x_new = _out_proj(o_a, o_b, o_c, sz, sg, x3, gate3, wb_l, wo_l, fw, m, final)
    return x_new, s3(kv), s3(lf), s3(ki)


def kernel(x_prompt, x_sample, cache_kv, cache_logf, cache_kidx, page_table, c_prompt, c_sample,
           norm_w, w_ada, b_ada, w_in, b_f, w_branch, w_out, final_norm_w):
    depth = norm_w.shape[0]
    bp, tp, d = x_prompt.shape
    bs, ts, _ = x_sample.shape
    n_pool = cache_kv.shape[1]
    n_pages = page_table.shape[1]
    past_len = n_pages * PAGE_SIZE
    assert ts == SUBLANES and n_pages % PAGES_PER_STEP == 0 and PAGES_PER_STEP % (MOBA_BLOCK // PAGE_SIZE) == 0
    assert tp % MOBA_BLOCK == 0 and tp % min(PROJ_ROWS, tp) == 0 and tp % min(512, tp) == 0

    nc = bp + bs
    rc = -(-nc // SUBLANES) * SUBLANES
    c_all = jnp.concatenate([c_prompt, c_sample, jnp.zeros((rc - nc, d), F32)], axis=0)
    mod = _modulation(c_all, w_ada, b_ada)

    tabs_p = _rope_tables(jnp.arange(tp, dtype=jnp.int32))
    tabs_s = _rope_tables(jnp.tile(past_len + jnp.arange(ts, dtype=jnp.int32), bs))

    cache_t = jnp.transpose(cache_kv, (0, 1, 3, 4, 5, 6, 2)).reshape(depth, n_pool, N_MIXERS * MIXER_COLS, PAGE_SIZE)
    cache_kxt = jnp.swapaxes(cache_kidx, 2, 3)
    cache_lft = jnp.swapaxes(cache_logf, 2, 3)

    ms = bs * ts
    xp = x_prompt
    xs = x_sample.reshape(1, ms, d)
    new_p, new_s = [], []
    kvt = jnp.zeros((depth, bp, N_MIXERS * MIXER_COLS, tp), F32)
    for l in range(depth):
        weights = _split_weights(w_in[l], b_f[l])
        wb_l = w_branch[l].astype(BF16)
        wo_l = w_out[l].astype(BF16)
        final = l == depth - 1
        shift, scale, gate = (mod[l, :, i * d:(i + 1) * d] for i in range(3))
        p3 = lambda a: a[:bp].reshape(bp, 1, d)
        s3 = lambda a: jnp.repeat(a[bp:nc], ts, axis=0).reshape(1, ms, d)

        xs, kv_s, lf_s, ki_s = _sample_layer(
            xs, s3(scale), s3(shift), s3(gate), tabs_s, weights, norm_w[l], wb_l, wo_l, final_norm_w,
            (cache_t, cache_kxt, cache_lft), page_table, l, nseq=bs, final=final)
        xp, kvt, lf_p, ki_p = _prompt_layer(
            xp, p3(scale), p3(shift), p3(gate), tabs_p, weights, norm_w[l], wb_l, wo_l, final_norm_w, kvt, l, depth,
            tm=min(PROJ_ROWS, tp), tm_out=min(512, tp), tq=min(128, tp), tk=MOBA_BLOCK, final=final)
        new_p.append((None, lf_p, ki_p))
        new_s.append((kv_s, lf_s, ki_s))

    kv_shape = (N_MIXERS, 2, N_KV_HEADS, HEAD_DIM)
    stack = lambda items, i: jnp.stack([n[i] for n in items])
    kv_prompt = jnp.transpose(kvt.reshape(depth, bp, *kv_shape, tp), (0, 1, 6, 2, 3, 4, 5))
    kv_sample = stack(new_s, 0).reshape(depth, bs, ts, *kv_shape)
    return (xp, xs.reshape(bs, ts, d), kv_prompt, stack(new_p, 1), stack(new_p, 2),
            kv_sample, stack(new_s, 1), stack(new_s, 2))
```

```python
import functools

import numpy as np
import jax
import jax.numpy as jnp
from jax import lax
from jax.experimental import pallas as pl
from jax.experimental.pallas import tpu as pltpu

F32 = jnp.float32
BF16 = jnp.bfloat16
HIGHEST = lax.Precision.HIGHEST

N_MIXERS = 3
N_HEADS = 8
N_KV_HEADS = 4
HEAD_DIM = 64
MIX_WIDTH = N_HEADS * HEAD_DIM
KV_WIDTH = N_KV_HEADS * HEAD_DIM
MIXER_COLS = 2 * KV_WIDTH
ROT_DIM = HEAD_DIM // 4
ROPE_THETA = 500000.0
ATTN_SCALE = HEAD_DIM ** -0.5
LOG2E = 1.4426950408889634
MOBA_BLOCK = 256
MOBA_TOPK = 3
IDX_HEADS = 8
IDX_DIM = 64
IDX_SCALE = (IDX_DIM * IDX_HEADS) ** -0.5
DSA_TOPK = 256
NORM_EPS = 1e-6
PAGE_SIZE = 128

LANES = 128
SUBLANES = 8
KMEAN_ROWS = 128
NEG = -1e30
F32_LOWEST = -3.0e38
VMEM_LIMIT = 56 * 1024 * 1024
PAGES_PER_STEP = 8
INDEX_PAGES_PER_STEP = 16
DSA_SWEEP = 1024
PROJ_ROWS = 1024
TOPK_COARSE_BITS = 28

_NT = (((1,), (1,)), ((), ()))


def _cp(*sem):
    return pltpu.CompilerParams(dimension_semantics=sem, vmem_limit_bytes=VMEM_LIMIT)


def _sigmoid(x):
    return 1.0 / (1.0 + jnp.exp(-x))


def _mod_kernel(c_ref, w_ref, b_ref, o_ref):
    c = c_ref[...]
    sc = c * _sigmoid(c)
    o_ref[0] = jnp.dot(sc, w_ref[0], precision=HIGHEST, preferred_element_type=F32) + b_ref[0]


def _modulation(c_all, w_ada, b_ada):
    depth, d, d3 = w_ada.shape
    rc = c_all.shape[0]
    nj = d3 // d
    return pl.pallas_call(
        _mod_kernel,
        grid=(depth, nj),
        in_specs=[
            pl.BlockSpec((rc, d), lambda l, j: (0, 0)),
            pl.BlockSpec((1, d, d), lambda l, j: (l, 0, j)),
            pl.BlockSpec((1, 1, d), lambda l, j: (l, 0, j)),
        ],
        out_specs=pl.BlockSpec((1, rc, d), lambda l, j: (l, 0, j)),
        out_shape=jax.ShapeDtypeStruct((depth, rc, d3), F32),
        compiler_params=_cp("arbitrary", "arbitrary"),
        name="adaln_mod",
    )(c_all, w_ada, b_ada.reshape(depth, 1, d3))


def _norm_kernel(x_ref, w_ref, sc_ref, sh_ref, h_ref):
    x = x_ref[0]
    ms = jnp.mean(x * x, axis=-1, keepdims=True)
    y = x * lax.rsqrt(ms + NORM_EPS) * w_ref[...]
    h_ref[0] = (y * (1.0 + sc_ref[0]) + sh_ref[0]).astype(BF16)


def _norm_mod(x3, norm_w, scale3, shift3, tm):
    g, r, d = x3.shape
    rm = scale3.shape[1]
    tmod = 1 if rm == 1 else tm
    mod_map = (lambda b, i: (b, 0, 0)) if rm == 1 else (lambda b, i: (b, i, 0))
    return pl.pallas_call(
        _norm_kernel,
        grid=(g, r // tm),
        in_specs=[
            pl.BlockSpec((1, tm, d), lambda b, i: (b, i, 0)),
            pl.BlockSpec((1, d), lambda b, i: (0, 0)),
            pl.BlockSpec((1, tmod, d), mod_map),
            pl.BlockSpec((1, tmod, d), mod_map),
        ],
        out_specs=pl.BlockSpec((1, tm, d), lambda b, i: (b, i, 0)),
        out_shape=jax.ShapeDtypeStruct((g, r, d), BF16),
        compiler_params=_cp("arbitrary", "arbitrary"),
        name="norm_mod",
    )(x3, norm_w.reshape(1, d), scale3, shift3)


def _rope_chunk(x, a, bm, cm):
    return x * a + pltpu.roll(x, LANES - ROT_DIM // 2, 1) * bm + pltpu.roll(x, ROT_DIM // 2, 1) * cm


def _projq_kernel(h_ref, w_ref, ra_ref, rb_ref, rc_ref, qa_ref, qb_ref, qc_ref, qaf_ref):
    y = jnp.dot(h_ref[...], w_ref[...], preferred_element_type=F32)
    a, bm, cm = ra_ref[...], rb_ref[...], rc_ref[...]
    for m, o_ref in enumerate((qa_ref, qb_ref, qc_ref)):
        for c in range(MIX_WIDTH // LANES):
            x = y[:, m * MIX_WIDTH + c * LANES:m * MIX_WIDTH + (c + 1) * LANES]
            if m != 1:
                x = _rope_chunk(x, a, bm, cm)
            if m == 0:
                qaf_ref[:, c * LANES:(c + 1) * LANES] = x
            o_ref[:, c * LANES:(c + 1) * LANES] = (x * (ATTN_SCALE * LOG2E)).astype(BF16)


def _projkv_kernel(h_ref, w_ref, ra_ref, rb_ref, rc_ref, kv_ref, kvb_ref):
    y = jnp.dot(h_ref[...], w_ref[...], preferred_element_type=F32)
    a, bm, cm = ra_ref[...], rb_ref[...], rc_ref[...]
    per_mixer = MIXER_COLS // LANES
    for c in range(N_MIXERS * per_mixer):
        x = y[:, c * LANES:(c + 1) * LANES]
        mixer, within = divmod(c, per_mixer)
        if mixer != 1 and within < KV_WIDTH // LANES:
            x = _rope_chunk(x, a, bm, cm)
        kv_ref[:, c * LANES:(c + 1) * LANES] = x
        kvb_ref[:, c * LANES:(c + 1) * LANES] = x.astype(BF16)


def _projkv_prompt_kernel(h_ref, w_ref, ra_ref, rb_ref, rc_ref, *refs):
    kvt_ref, kvb_ref, km_ref = refs[-3:]
    y = jnp.dot(h_ref[...], w_ref[...], preferred_element_type=F32)
    a, bm, cm = ra_ref[...], rb_ref[...], rc_ref[...]
    tm = y.shape[0]
    per_mixer = MIXER_COLS // LANES
    moba_k = []
    for c in range(N_MIXERS * per_mixer):
        x = y[:, c * LANES:(c + 1) * LANES]
        mixer, within = divmod(c, per_mixer)
        if mixer != 1 and within < KV_WIDTH // LANES:
            x = _rope_chunk(x, a, bm, cm)
        if mixer == 0 and within < KV_WIDTH // LANES:
            moba_k.append(x)
        kvb_ref[:, c * LANES:(c + 1) * LANES] = x.astype(BF16)
        kvt_ref[0, 0, c * LANES:(c + 1) * LANES, :] = x.T
    means = [jnp.concatenate([jnp.sum(x[g * MOBA_BLOCK:(g + 1) * MOBA_BLOCK], axis=0, keepdims=True)
                              for x in moba_k], axis=1) * (1.0 / MOBA_BLOCK) for g in range(tm // MOBA_BLOCK)]
    km_ref[0] = jnp.concatenate(means + [jnp.zeros((SUBLANES - len(means), KV_WIDTH), F32)], axis=0)


def _proj_kv_prompt(h, w, tabs, kvt_prev, layer, depth, b, t, tm):
    m, d = h.shape
    n = w.shape[1]
    nt = t // tm
    assert tm % MOBA_BLOCK == 0 and tm // MOBA_BLOCK <= SUBLANES
    in_specs = [pl.BlockSpec((tm, d), lambda j, i: (i, 0)), pl.BlockSpec((d, n), lambda j, i: (0, 0))]
    in_specs += [pl.BlockSpec((tm, LANES), lambda j, i: (i % nt, 0)) for _ in tabs]
    in_specs.append(pl.BlockSpec(memory_space=pl.ANY))
    args = [h, w, *tabs, kvt_prev]
    aliases = {len(args) - 1: 0}
    return pl.pallas_call(
        _projkv_prompt_kernel,
        grid=(1, m // tm),
        in_specs=in_specs,
        out_specs=[pl.BlockSpec((1, 1, n, tm), lambda j, i: (layer, i // nt, 0, i % nt)),
                   pl.BlockSpec((tm, n), lambda j, i: (i, 0)),
                   pl.BlockSpec((1, SUBLANES, KV_WIDTH), lambda j, i: (i, 0, 0))],
        out_shape=[jax.ShapeDtypeStruct((depth, b, n, t), F32), jax.ShapeDtypeStruct((m, n), BF16),
                   jax.ShapeDtypeStruct((m // tm, SUBLANES, KV_WIDTH), F32)],
        input_output_aliases=aliases,
        compiler_params=_cp("arbitrary", "arbitrary"),
        name="proj_kv_prompt",
    )(*args)


def _projz_kernel(h_ref, w_ref, o_ref):
    y = jnp.dot(h_ref[...], w_ref[...], preferred_element_type=F32)
    o_ref[...] = (y * _sigmoid(y)).astype(o_ref.dtype)


def _projg_kernel(h_ref, w_ref, o_ref):
    y = jnp.dot(h_ref[...], w_ref[...], preferred_element_type=F32)
    o_ref[...] = _sigmoid(y).astype(o_ref.dtype)


MISC_KI = MIX_WIDTH
MISC_WI = MISC_KI + LANES
MISC_FB = MISC_WI + LANES
MISC_N = MISC_FB + LANES


def _projmisc_kernel(h_ref, w_ref, ra_ref, rb_ref, rc_ref, bf_ref,
                     qi_ref, ki_ref, ki2_ref, wi_ref, lf_ref):
    y = jnp.dot(h_ref[...], w_ref[...], preferred_element_type=F32)
    a, bm, cm = ra_ref[...], rb_ref[...], rc_ref[...]
    for c in range(MIX_WIDTH // LANES):
        x = _rope_chunk(y[:, c * LANES:(c + 1) * LANES], a, bm, cm)
        qi_ref[:, c * LANES:(c + 1) * LANES] = x.astype(BF16)
    ki = _rope_chunk(y[:, MISC_KI:MISC_KI + LANES], a, bm, cm)
    ki_ref[...] = ki[:, :IDX_DIM]
    ki2_ref[...] = (ki + pltpu.roll(ki, IDX_DIM, 1)).astype(BF16)
    wi_ref[...] = y[:, MISC_WI:MISC_WI + IDX_HEADS]
    f = y[:, MISC_FB:MISC_FB + LANES] + bf_ref[...]
    logf = jnp.minimum(f, 0.0) - jnp.log(1.0 + jnp.exp(-jnp.abs(f)))
    lf_ref[...] = logf[:, :N_HEADS]


def _proj_call(kernel, h, w, tabs, extra, outs, tm, tn, name):
    m, d = h.shape
    n = w.shape[1]
    nt = tabs[0].shape[0] // tm if tabs else 1
    in_specs = [pl.BlockSpec((tm, d), lambda j, i: (i, 0)),
                pl.BlockSpec((d, tn), lambda j, i: (0, j))]
    in_specs += [pl.BlockSpec((tm, LANES), lambda j, i: (i % nt, 0)) for _ in tabs]
    in_specs += [pl.BlockSpec(e.shape, lambda j, i: (0, 0)) for e in extra]
    out_specs = [pl.BlockSpec((tm, wd if full else tn), (lambda j, i: (i, 0)) if full else (lambda j, i: (i, j)))
                 for (wd, _, full) in outs]
    out_shape = [jax.ShapeDtypeStruct((m, wd), dt) for (wd, dt, _) in outs]
    return pl.pallas_call(
        kernel,
        grid=(n // tn, m // tm),
        in_specs=in_specs,
        out_specs=out_specs,
        out_shape=out_shape,
        compiler_params=_cp("arbitrary", "arbitrary"),
        name=name,
    )(h, w, *tabs, *extra)


CUM_CHUNK = 256


def _cumsum_kernel(x_ref, o_ref):
    n = x_ref.shape[2] // CUM_CHUNK
    r = lax.broadcasted_iota(jnp.int32, (CUM_CHUNK, CUM_CHUNK), 0)
    c = lax.broadcasted_iota(jnp.int32, (CUM_CHUNK, CUM_CHUNK), 1)
    tri = (r <= c).astype(F32)

    def body(i, carry):
        st = pl.multiple_of(i * CUM_CHUNK, CUM_CHUNK)
        x = x_ref[0, :, pl.ds(st, CUM_CHUNK)]
        y = jnp.dot(x, tri, precision=HIGHEST, preferred_element_type=F32) + carry
        o_ref[0, :, pl.ds(st, CUM_CHUNK)] = y * LOG2E
        return y[:, CUM_CHUNK - 1:CUM_CHUNK]

    lax.fori_loop(0, n, body, jnp.zeros((x_ref.shape[1], 1), F32))


def _cumsum(lft):
    b, h, lp = lft.shape
    return pl.pallas_call(
        _cumsum_kernel,
        grid=(b,),
        in_specs=[pl.BlockSpec((1, h, lp), lambda i: (i, 0, 0))],
        out_specs=pl.BlockSpec((1, h, lp), lambda i: (i, 0, 0)),
        out_shape=jax.ShapeDtypeStruct((b, h, lp), F32),
        compiler_params=_cp("arbitrary"),
        name="fox_cumsum",
    )(lft)


def _stack4(blk0, blk1):
    lane = lax.broadcasted_iota(jnp.int32, blk0.shape, 1)
    lo = lane < HEAD_DIM
    return jnp.concatenate([
        jnp.where(lo, blk0, 0.0),
        jnp.where(lo, pltpu.roll(blk0, HEAD_DIM, 1), 0.0),
        jnp.where(lo, 0.0, pltpu.roll(blk1, HEAD_DIM, 1)),
        jnp.where(lo, 0.0, blk1)], axis=0)


def _unstack4(o, tq):
    lane = lax.broadcasted_iota(jnp.int32, (tq, LANES), 1)
    lo = lane < HEAD_DIM
    b0 = jnp.where(lo, o[:tq], pltpu.roll(o[tq:2 * tq], HEAD_DIM, 1))
    b1 = jnp.where(lo, pltpu.roll(o[2 * tq:3 * tq], HEAD_DIM, 1), o[3 * tq:])
    return b0, b1


def _stacked_queries(q_ref):
    return [_stack4(q_ref[0, :, (2 * c) * LANES:(2 * c + 1) * LANES].astype(F32),
                    q_ref[0, :, (2 * c + 1) * LANES:(2 * c + 2) * LANES].astype(F32)) for c in range(2)]


def _flash_scratch(tq, tk):
    rows = 4 * tq
    return [pltpu.VMEM((2, rows, tk), F32), pltpu.VMEM((2, rows, tk), BF16),
            pltpu.VMEM((2, rows, LANES), F32), pltpu.VMEM((2, rows, LANES), F32), pltpu.VMEM((2, rows, LANES), F32),
            pltpu.VMEM((2, rows, 2 * LANES), F32), pltpu.VMEM((2, rows, tk), F32)]


def _flash_pipeline(n_all, qs, k_ref, v_ref, add_bias, diag_mask, scratch, o_ref, tq, tk, k_aug=None,
                    tile_bias=None):
    s_ref, p_ref, m0_ref, m1_ref, mx_ref, acc_ref, mb_ref = scratch
    m_slots = (m0_ref, m1_ref)
    m0_ref[...] = jnp.full(m0_ref.shape, -jnp.inf, F32)
    acc_ref[...] = jnp.zeros(acc_ref.shape, F32)
    if diag_mask is not None:
        @pl.when((pl.program_id(0) == 0) & (pl.program_id(1) == 0))
        def _():
            mb_ref[0] = jnp.zeros(mb_ref.shape[1:], F32)
        mb_ref[1] = diag_mask
    last = n_all - 1
    ones = jnp.ones((tk, LANES), BF16)

    def stage_pv(i, par):
        jc = jnp.clip(i - 2, 0, last)
        stc = pl.multiple_of(jc * tk, tk)
        for c in range(2):
            vo = jnp.concatenate([v_ref[0, pl.ds(stc, tk), c * LANES:(c + 1) * LANES], ones], axis=1)
            al = jnp.exp2(m_slots[par][c] - m_slots[1 - par][c])
            acc_ref[c] = (jnp.concatenate([al, al], axis=1) * acc_ref[c]
                          + jnp.dot(p_ref[c], vo, preferred_element_type=F32))

    def stage_softmax(par):
        for c in range(2):
            m_new = jnp.maximum(m_slots[1 - par][c], jnp.max(mx_ref[c], axis=-1, keepdims=True))
            m_slots[par][c] = m_new
            p_ref[c] = jnp.exp2(s_ref[c] - jnp.concatenate([m_new] * (tk // LANES), axis=1)).astype(BF16)

    def stage_scores(i):
        ja = jnp.minimum(i, last)
        sta = pl.multiple_of(ja * tk, tk)
        pen = jnp.where(i <= last, 0.0, NEG).astype(F32)
        shared = None if tile_bias is None else tile_bias(ja, pen)
        for c in range(2):
            kt = k_ref[0, pl.ds(sta, tk), c * LANES:(c + 1) * LANES]
            if k_aug is not None:
                kt = jnp.concatenate([kt, k_aug(c, ja, i <= last)], axis=1)
            s = lax.dot_general(qs[c], kt, _NT, preferred_element_type=F32)
            s = add_bias(c, s, ja, pen)
            if shared is not None:
                s = s + jnp.concatenate([shared] * 4, axis=0)
            if diag_mask is not None:
                s = s + mb_ref[(ja == last).astype(jnp.int32)]
            s_ref[c] = s
            mx_ref[c] = functools.reduce(jnp.maximum, [s[:, k * LANES:(k + 1) * LANES] for k in range(tk // LANES)])

    def step(i, par):
        stage_pv(i, par)
        stage_softmax(par)
        stage_scores(i)

    def step2(ii, carry):
        step(2 * ii + 2, 0)
        step(2 * ii + 3, 1)
        return carry

    stage_scores(0)
    stage_softmax(1)
    stage_scores(1)
    lax.fori_loop(0, (n_all + 1) // 2, step2, 0)
    for c in range(2):
        b0, b1 = _unstack4(acc_ref[c, :, :LANES] / acc_ref[c, :, LANES:], tq)
        o_ref[0, :, (2 * c) * LANES:(2 * c + 1) * LANES] = b0.astype(o_ref.dtype)
        o_ref[0, :, (2 * c + 1) * LANES:(2 * c + 2) * LANES] = b1.astype(o_ref.dtype)


def _diag_mask(q_lo, n_all, tq, tk):
    r1 = lax.broadcasted_iota(jnp.int32, (tq, 1), 0) + q_lo
    qpos = jnp.concatenate([r1] * 4, axis=0)
    kpos = (n_all - 1) * tk + lax.broadcasted_iota(jnp.int32, (4 * tq, tk), 1)
    return jnp.where(kpos <= qpos, 0.0, NEG)


def _attn_call(kern, name, ins, in_specs, b, t, tq, scratch):
    return pl.pallas_call(
        kern,
        grid=(b, t // tq),
        in_specs=in_specs,
        out_specs=pl.BlockSpec((1, tq, MIX_WIDTH), lambda i, j: (i, j, 0)),
        out_shape=jax.ShapeDtypeStruct((b, t, MIX_WIDTH), BF16),
        scratch_shapes=scratch,
        compiler_params=_cp("arbitrary", "arbitrary"),
        name=name,
    )(*ins)


_q_spec = lambda tq: pl.BlockSpec((1, tq, MIX_WIDTH), lambda i, j: (i, j, 0))
_kv_spec = lambda lp, col: pl.BlockSpec((1, lp, KV_WIDTH), lambda i, j: (i, 0, col))


def _moba_kernel(qf_ref, q_ref, km_ref, k_ref, v_ref, o_ref, *scratch, tq, topk, nblk):
    tk = MOBA_BLOCK
    q_lo = pl.program_id(1) * tq
    own = q_lo // MOBA_BLOCK
    rows = 4 * tq
    blk = lax.broadcasted_iota(jnp.int32, (nblk, rows), 0)
    blk_f = blk.astype(F32)
    q_st = _stacked_queries(q_ref)
    qs = []
    for c, qf in enumerate(_stacked_queries(qf_ref)):
        gate = lax.dot_general(km_ref[0, :nblk, c * LANES:(c + 1) * LANES], qf, _NT, precision=HIGHEST,
                               preferred_element_type=F32)
        gate = jnp.where(blk < own, gate, -jnp.inf)
        sb = jnp.where(blk == own, 0.0, NEG)
        for _ in range(topk):
            mx = jnp.max(gate, axis=0, keepdims=True)
            is_max = (gate == mx) & (mx > -jnp.inf)
            first = jnp.min(jnp.where(is_max, blk_f, 1e9), axis=0, keepdims=True)
            pick = blk_f == first
            sb = jnp.where(pick, 0.0, sb)
            gate = jnp.where(pick, -jnp.inf, gate)
        sb = jnp.concatenate([sb, jnp.full((LANES - nblk, rows), NEG, F32)], axis=0)
        qs.append(jnp.concatenate([q_st[c], sb.T], axis=1).astype(BF16))

    klane = lax.broadcasted_iota(jnp.int32, (tk, LANES), 1)

    def k_aug(c, j, valid):
        return jnp.where(klane == jnp.where(valid, j, LANES - 1), 1.0, 0.0).astype(BF16)

    _flash_pipeline(own + 1, qs, k_ref, v_ref, lambda c, s, j, pen: s, _diag_mask(q_lo, own + 1, tq, tk), scratch,
                    o_ref, tq, tk, k_aug=k_aug)


def _moba_attn(qf, q, km, kvb, tq):
    b, t, _ = q.shape
    lp = kvb.shape[1]
    nblk = -(-(lp // MOBA_BLOCK) // SUBLANES) * SUBLANES
    assert nblk < LANES
    kern = functools.partial(_moba_kernel, tq=tq, topk=min(MOBA_TOPK, lp // MOBA_BLOCK), nblk=nblk)
    specs = [_q_spec(tq), _q_spec(tq), pl.BlockSpec((1, KMEAN_ROWS, KV_WIDTH), lambda i, j: (i, 0, 0)),
             _kv_spec(lp, 0), _kv_spec(lp, 1)]
    return _attn_call(kern, "moba_attn", (qf, q, km, kvb, kvb), specs, b, t, tq, _flash_scratch(tq, MOBA_BLOCK))


def _fox_kernel(q_ref, cum_ref, k_ref, v_ref, o_ref, *scratch, tq, tk):
    q_lo = pl.program_id(1) * tq
    n_all = (q_lo + tq + tk - 1) // tk
    assert tq == LANES
    r_tok = lax.broadcasted_iota(jnp.int32, (tq, LANES), 0)
    f_lane = lax.broadcasted_iota(jnp.int32, (tq, LANES), 1)
    stair = jnp.concatenate([jnp.where(f_lane >= r_tok, NEG, 0.0)] * 4, axis=0)
    qs = [jnp.concatenate([q, stair], axis=1).astype(BF16) for q in _stacked_queries(q_ref)]
    krow = lax.broadcasted_iota(jnp.int32, (tk, 1), 0)
    klane = lax.broadcasted_iota(jnp.int32, (tk, LANES), 1)

    def k_aug(c, j, valid):
        f = jnp.minimum(j * tk + krow - q_lo - 1, LANES - 1)
        return jnp.where(klane == f, 1.0, 0.0).astype(BF16)

    def add_bias(c, s, j, pen):
        st = pl.multiple_of(j * tk, tk)
        parts = [s[hh * tq:(hh + 1) * tq] - (cum_ref[0, 4 * c + hh:4 * c + hh + 1, pl.ds(st, tk)] - pen)
                 for hh in range(4)]
        return jnp.concatenate(parts, axis=0)

    _flash_pipeline(n_all, qs, k_ref, v_ref, add_bias, None, scratch, o_ref, tq, tk, k_aug=k_aug)


def _fox_attn(q, cum, kvb, tq, tk):
    b, t, _ = q.shape
    lp = kvb.shape[1]
    kern = functools.partial(_fox_kernel, tq=tq, tk=tk)
    specs = [_q_spec(tq), pl.BlockSpec((1, N_HEADS, lp), lambda i, j: (i, 0, 0)), _kv_spec(lp, 2), _kv_spec(lp, 3)]
    return _attn_call(kern, "fox_attn", (q, cum, kvb, kvb), specs, b, t, tq, _flash_scratch(tq, tk))


def _key_to_float(key):
    bits = jnp.where(key < 0, key & jnp.int32(0x7FFFFFFF), ~key)
    return lax.bitcast_convert_type(bits, F32)


def _topk_threshold(count, vshape, need, few, idx_bits):
    def bit_body(i, key):
        cand = key | lax.shift_left(jnp.int32(1), 31 - i)
        thr_c = _key_to_float(cand)
        return jnp.where(count(lambda x, idx: x >= thr_c) >= need, cand, key)

    no_cut = jnp.full(vshape, 2 ** 30, jnp.int32)

    def too_many(key):
        thr = _key_to_float(key)
        return jnp.max(jnp.where(few, 0.0, count(lambda x, idx: x >= thr) - need)) > 0.0

    def tie_cut(thr):
        r = need - count(lambda x, idx: x > thr)

        def jb(i, cut):
            cand = cut | lax.shift_left(jnp.int32(1), idx_bits - 1 - i)
            cnt = count(lambda x, idx: (x == thr) & (idx < cand))
            return jnp.where(cnt < r, cand, cut)

        return lax.fori_loop(0, idx_bits, jb, jnp.zeros(vshape, jnp.int32))

    def refine(key):
        key = lax.fori_loop(TOPK_COARSE_BITS, 32, bit_body, key)
        return key, lax.cond(too_many(key), lambda k: tie_cut(_key_to_float(k)), lambda k: no_cut, key)

    key = lax.fori_loop(0, TOPK_COARSE_BITS, bit_body, jnp.zeros(vshape, jnp.int32))
    key, cut = lax.cond(too_many(key), refine, lambda k: (k, no_cut), key)
    return jnp.where(few, F32_LOWEST, _key_to_float(key)), jnp.where(few, 2 ** 30, cut)


def _dsa_kernel(qi_ref, w_ref, kx_ref, q_ref, k_ref, v_ref, o_ref, isc_ref, *scratch, tq, tk, sweep,
                topk, idx_bits):
    q_lo = pl.program_id(1) * tq
    n_all = (q_lo + tq + tk - 1) // tk
    lane = lax.broadcasted_iota(jnp.int32, (tq, LANES), 1)
    lo = lane < HEAD_DIM
    qpos = lax.broadcasted_iota(jnp.int32, (1, tq), 1) + q_lo
    krow = lax.broadcasted_iota(jnp.int32, (tk, 1), 0)

    pieces = []
    for h in range(IDX_HEADS):
        chunk = qi_ref[0, :, (h // 2) * LANES:(h // 2 + 1) * LANES].astype(F32)
        pieces.append(jnp.where(lo, chunk, 0.0) if h % 2 == 0 else jnp.where(lo, 0.0, chunk))
    qstack_t = jnp.concatenate(pieces, axis=0).T.astype(BF16)
    wrow = jnp.concatenate([w_ref[0, h:h + 1, :] for h in range(IDX_HEADS)], axis=1)

    def idx_tile(j):
        start = pl.multiple_of(j * tk, tk)
        sc = jnp.dot(kx_ref[0, pl.ds(start, tk), :], qstack_t, preferred_element_type=F32)
        contrib = jnp.maximum(sc, 0.0) * wrow
        isc = contrib[:, :tq]
        for h in range(1, IDX_HEADS):
            isc = isc + contrib[:, h * tq:(h + 1) * tq]
        isc_ref[pl.ds(start, tk), :] = jnp.where(j * tk + krow <= qpos, isc * IDX_SCALE, -jnp.inf)

    def idx_body(jj, carry):
        idx_tile(2 * jj)
        idx_tile(2 * jj + 1)
        return carry

    n_idx = (n_all + 1) // 2 * 2
    lax.fori_loop(0, n_idx // 2, idx_body, 0)

    tiles_per_sweep = sweep // tk
    n_sweeps = (n_all + tiles_per_sweep - 1) // tiles_per_sweep

    def pad_body(j, carry):
        isc_ref[pl.ds(pl.multiple_of(j * tk, tk), tk), :] = jnp.full((tk, tq), -jnp.inf, F32)
        return carry

    lax.fori_loop(n_idx, n_sweeps * tiles_per_sweep, pad_body, 0)
    grp = 8 * SUBLANES
    sub = lax.broadcasted_iota(jnp.int32, (grp, tq), 0)

    def count(pred):
        def body(j, acc):
            start = pl.multiple_of(j * sweep, sweep)
            x = isc_ref[pl.ds(start, sweep), :]
            for r in range(sweep // grp):
                idx = j * sweep + r * grp + sub
                acc = acc + jnp.where(pred(x[r * grp:(r + 1) * grp], idx), 1.0, 0.0)
            return acc
        acc = lax.fori_loop(0, n_sweeps, body, jnp.zeros((grp, tq), F32))
        return jnp.sum(acc, axis=0, keepdims=True)

    few = qpos + 1 <= topk
    thr, cut = _topk_threshold(count, (1, tq), jnp.float32(topk), few, idx_bits)

    qs = [q.astype(BF16) for q in _stacked_queries(q_ref)]

    def tile_bias(j, pen):
        x = isc_ref[pl.ds(pl.multiple_of(j * tk, tk), tk), :]
        keep = (x > thr) | ((x == thr) & (j * tk + krow <= cut))
        return jnp.where(keep, 0.0, NEG).T + pen

    _flash_pipeline(n_all, qs, k_ref, v_ref, lambda c, s, j, pen: s, None, scratch, o_ref, tq, tk,
                    tile_bias=tile_bias)


def _dsa_attn(qi, wi, kx2, q, kvb, tq, tk):
    b, t, _ = q.shape
    lp = kvb.shape[1]
    sweep = max(tk, min(DSA_SWEEP, lp))
    assert lp % sweep == 0 and sweep % tk == 0 and lp % (2 * tk) == 0
    kern = functools.partial(_dsa_kernel, tq=tq, tk=tk, sweep=sweep, topk=min(DSA_TOPK, lp // 4),
                             idx_bits=max(1, int(lp).bit_length()))
    specs = [_q_spec(tq), pl.BlockSpec((1, IDX_HEADS, tq), lambda i, j: (i, 0, j)),
             pl.BlockSpec((1, lp, 2 * IDX_DIM), lambda i, j: (i, 0, 0)), _q_spec(tq), _kv_spec(lp, 4), _kv_spec(lp, 5)]
    scratch = [pltpu.VMEM((lp, tq), F32)] + _flash_scratch(tq, tk)
    return _attn_call(kern, "dsa_attn", (qi, jnp.swapaxes(wi, 1, 2), kx2, q, kvb, kvb), specs, b, t, tq, scratch)


def _sample_index_kernel(pt_ref, *refs, ppi, n_steps, l_past, t_new):
    del pt_ref
    kx_refs = refs[:ppi]
    qi_ref, w_ref, nkx_ref, o_ref, isc_ref = refs[ppi:]
    j = pl.program_id(1)
    lp = isc_ref.shape[1]
    lane = lax.broadcasted_iota(jnp.int32, (t_new, LANES), 1)
    lo = lane < HEAD_DIM
    pieces, wpieces = [], []
    for h in range(IDX_HEADS):
        chunk = qi_ref[0, :, (h // 2) * LANES:(h // 2 + 1) * LANES].astype(F32)
        pieces.append(jnp.where(lo, chunk, 0.0) if h % 2 == 0 else jnp.where(lo, 0.0, chunk))
        wpieces.append(jnp.broadcast_to(w_ref[0, :, h:h + 1], (t_new, LANES)))
    qstack = jnp.concatenate(pieces, axis=0).astype(BF16)
    wb = jnp.concatenate(wpieces, axis=0)

    def head_sum(sc):
        contrib = jnp.maximum(sc, 0.0) * jnp.concatenate([wb] * (sc.shape[1] // LANES), axis=1)
        isc = contrib[:t_new]
        for h in range(1, IDX_HEADS):
            isc = isc + contrib[h * t_new:(h + 1) * t_new]
        return isc * IDX_SCALE

    kxt = jnp.concatenate([r[0, 0] for r in kx_refs], axis=1)
    rhs = jnp.concatenate([kxt, kxt], axis=0).astype(BF16)
    start = pl.multiple_of(j * (ppi * PAGE_SIZE), ppi * PAGE_SIZE)
    isc_ref[:, pl.ds(start, ppi * PAGE_SIZE)] = head_sum(jnp.dot(qstack, rhs, preferred_element_type=F32))

    @pl.when(j == n_steps - 1)
    def _():
        scn = lax.dot_general(qstack, nkx_ref[0], _NT, preferred_element_type=F32)
        row = lax.broadcasted_iota(jnp.int32, (t_new, LANES), 0)
        isc_ref[:, l_past:l_past + LANES] = jnp.where((lane <= row) & (lane < t_new), head_sum(scn), -jnp.inf)
        if lp > l_past + LANES:
            isc_ref[:, l_past + LANES:] = jnp.full((t_new, lp - l_past - LANES), -jnp.inf, F32)
        o_ref[0] = isc_ref[...]


def _topk_bias_kernel(x_ref, o_ref, *, sweep, topk, l_past, t_new, idx_bits):
    rows, lp = x_ref.shape
    lane = lax.broadcasted_iota(jnp.int32, (rows, LANES), 1)

    def count(pred):
        def body(j, acc):
            x = x_ref[:, pl.ds(pl.multiple_of(j * sweep, sweep), sweep)]
            for cc in range(sweep // LANES):
                idx = j * sweep + cc * LANES + lane
                acc = acc + jnp.where(pred(x[:, cc * LANES:(cc + 1) * LANES], idx), 1.0, 0.0)
            return acc
        acc = lax.fori_loop(0, lp // sweep, body, jnp.zeros((rows, LANES), F32))
        return jnp.sum(acc, axis=-1, keepdims=True)

    qpos1 = (lax.broadcasted_iota(jnp.int32, (rows, 1), 0) & (t_new - 1)) + l_past
    thr, cut = _topk_threshold(count, (rows, 1), jnp.float32(topk), qpos1 + 1 <= topk, idx_bits)
    kiota = lax.broadcasted_iota(jnp.int32, (rows, sweep), 1)

    def bias_body(j, carry):
        start = pl.multiple_of(j * sweep, sweep)
        x = x_ref[:, pl.ds(start, sweep)]
        keep = (x > thr) | ((x == thr) & (j * sweep + kiota <= cut))
        o_ref[:, pl.ds(start, sweep)] = jnp.where(keep, 0.0, NEG)
        return carry

    lax.fori_loop(0, lp // sweep, bias_body, 0)


def _topk_bias(isc, l_past, t_new, rows_per_step):
    rows, lp = isc.shape
    sweep = min(DSA_SWEEP, lp)
    assert lp % sweep == 0 and rows % rows_per_step == 0 and rows_per_step % t_new == 0
    kern = functools.partial(_topk_bias_kernel, sweep=sweep, topk=min(DSA_TOPK, (l_past + t_new) // 4),
                             l_past=l_past, t_new=t_new, idx_bits=max(1, int(lp).bit_length()))
    return pl.pallas_call(
        kern,
        grid=(rows // rows_per_step,),
        in_specs=[pl.BlockSpec((rows_per_step, lp), lambda i: (i, 0))],
        out_specs=pl.BlockSpec((rows_per_step, lp), lambda i: (i, 0)),
        out_shape=jax.ShapeDtypeStruct((rows, lp), F32),
        compiler_params=_cp("arbitrary"),
        name="dsa_topk_bias",
    )(isc)


def _sample_index(cache_kxt, page_table, layer, qi, wi, nkx2, lp):
    nseq, n_pages = page_table.shape
    t_new = qi.shape[1]
    ppi = INDEX_PAGES_PER_STEP if n_pages % INDEX_PAGES_PER_STEP == 0 else PAGES_PER_STEP
    n_steps = n_pages // ppi
    l_past = n_pages * PAGE_SIZE
    pg = lambda pp: (lambda i, j, pt: (layer, pt[i, j * ppi + pp], 0, 0))
    grid_spec = pltpu.PrefetchScalarGridSpec(
        num_scalar_prefetch=1,
        grid=(nseq, n_steps),
        in_specs=[pl.BlockSpec((1, 1, IDX_DIM, PAGE_SIZE), pg(pp)) for pp in range(ppi)] + [
            pl.BlockSpec((1, t_new, MIX_WIDTH), lambda i, j, pt: (i, 0, 0)),
            pl.BlockSpec((1, t_new, IDX_HEADS), lambda i, j, pt: (i, 0, 0)),
            pl.BlockSpec((1, PAGE_SIZE, 2 * IDX_DIM), lambda i, j, pt: (i, 0, 0)),
        ],
        out_specs=pl.BlockSpec((1, t_new, lp), lambda i, j, pt: (i, 0, 0)),
        scratch_shapes=[pltpu.VMEM((t_new, lp), F32)],
    )
    kern = functools.partial(_sample_index_kernel, ppi=ppi, n_steps=n_steps, l_past=l_past, t_new=t_new)
    return pl.pallas_call(
        kern,
        grid_spec=grid_spec,
        out_shape=jax.ShapeDtypeStruct((nseq, t_new, lp), F32),
        compiler_params=_cp("arbitrary", "arbitrary"),
        name="dsa_index_paged",
    )(page_table, *([cache_kxt] * ppi), qi, wi, nkx2)


def _block_diag_q(q):
    t = q.shape[0]
    lane = lax.broadcasted_iota(jnp.int32, (t, LANES), 1)
    lo = lane < HEAD_DIM
    zero = jnp.zeros((t, LANES), F32)
    out = []
    for h in range(N_HEADS):
        src = q[:, (h // 2) * LANES:(h // 2 + 1) * LANES]
        kvh = h // (N_HEADS // N_KV_HEADS)
        chunk, half = divmod(kvh, 2)
        piece = src if h % 2 == half else pltpu.roll(src, HEAD_DIM, 1)
        piece = jnp.where(lo, piece, 0.0) if half == 0 else jnp.where(lo, 0.0, piece)
        out.append(jnp.concatenate([piece, zero] if chunk == 0 else [zero, piece], axis=1))
    return jnp.concatenate(out, axis=0)


def _token_major(o, t):
    lane = lax.broadcasted_iota(jnp.int32, (t, LANES), 1)
    lo = lane < HEAD_DIM
    out = []
    for co in range(MIX_WIDTH // LANES):
        chunk, half = divmod(co, 2)
        x = o[(2 * co) * t:(2 * co + 1) * t, chunk * LANES:(chunk + 1) * LANES]
        y = o[(2 * co + 1) * t:(2 * co + 2) * t, chunk * LANES:(chunk + 1) * LANES]
        if half == 0:
            out.append(jnp.where(lo, x, pltpu.roll(y, HEAD_DIM, 1)))
        else:
            out.append(jnp.where(lo, pltpu.roll(x, HEAD_DIM, 1), y))
    return jnp.concatenate(out, axis=1)


def _lane_cumsum(x):
    lane = lax.broadcasted_iota(jnp.int32, x.shape, 1)
    d = 1
    while d < LANES:
        x = x + jnp.where(lane >= d, pltpu.roll(x, d, 1), 0.0)
        d *= 2
    return x


def _rows_from_heads(c, t):
    return jnp.concatenate([jnp.broadcast_to(c[h:h + 1], (t, c.shape[1])) for h in range(N_HEADS)], axis=0)


def _online_update(s, pv_fn, m_ref, l_ref, acc_ref):
    m_old = m_ref[...]
    m_new = jnp.maximum(m_old, jnp.max(s, axis=-1, keepdims=True))
    alpha = jnp.exp2(m_old - m_new)
    p = jnp.exp2(s - jnp.concatenate([m_new] * (s.shape[1] // LANES), axis=1))
    l_ref[...] = alpha * l_ref[...] + jnp.sum(p, axis=-1, keepdims=True)
    acc_ref[...] = jnp.concatenate([alpha] * (acc_ref.shape[1] // LANES), axis=1) * acc_ref[...] + pv_fn(p.astype(BF16))
    m_ref[...] = m_new


def _sample_attn_kernel(pt_ref, cache_ref, *refs, nb, pps, t_new, topk, layer):
    lf_refs = refs[:pps]
    (bias_ref, qa_ref, qb_ref, qc_ref, qaf_ref, nkv_ref, nlf_ref, oa_ref, ob_ref, oc_ref,
     qbd_ref, qbdf_ref, km_ref, ms_ref, ls_ref, accs_ref,
     fm_ref, fl_ref, facc_ref, car_ref, dm_ref, dl_ref, dacc_ref, s_ref, vt_ref, pbuf_ref, sem_ref) = refs[pps:]
    j = pl.program_id(1)
    rows = N_HEADS * t_new
    ppb = MOBA_BLOCK // PAGE_SIZE
    bps = pps // ppb
    n_steps = nb // bps
    lane = lax.broadcasted_iota(jnp.int32, (rows, LANES), 1)

    g = pl.program_id(0) * n_steps + j
    n_page_steps = pl.num_programs(0) * n_steps
    slot = lax.rem(g, 2)

    def page_copies(gg):
        seq, step = lax.div(gg, n_steps), lax.rem(gg, n_steps)
        return [pltpu.make_async_copy(cache_ref.at[layer, pt_ref[seq, step * pps + k]],
                                      pbuf_ref.at[lax.rem(gg, 2), k], sem_ref.at[lax.rem(gg, 2), k])
                for k in range(pps)]

    def start_pages(gg):
        for k, cp in enumerate(page_copies(gg)):
            cp.start(priority=k % 2)

    def fetch_pages():
        @pl.when(g == 0)
        def _():
            start_pages(g)

        @pl.when(g + 1 < n_page_steps)
        def _():
            start_pages(g + 1)

        for cp in page_copies(g):
            cp.wait()

    def init():
        for m, q_ref in enumerate((qa_ref, qb_ref, qc_ref)):
            qbd_ref[m] = _block_diag_q(q_ref[0].astype(F32)).astype(BF16)
        qbdf_ref[...] = _block_diag_q(qaf_ref[0])
        km_ref[...] = jnp.zeros(km_ref.shape, F32)
        ms_ref[...] = jnp.zeros(ms_ref.shape, F32)
        ls_ref[...] = jnp.zeros(ls_ref.shape, F32)
        car_ref[...] = jnp.zeros(car_ref.shape, F32)
        for m_ref, l_ref, acc_ref in ((fm_ref, fl_ref, facc_ref), (dm_ref, dl_ref, dacc_ref)):
            m_ref[...] = jnp.full(m_ref.shape, -jnp.inf, F32)
            l_ref[...] = jnp.zeros(l_ref.shape, F32)
            acc_ref[...] = jnp.zeros(acc_ref.shape, F32)

    def kt(m, off):
        lo_r = m * MIXER_COLS + off
        return jnp.concatenate([pbuf_ref[slot, k, lo_r:lo_r + KV_WIDTH, :] for k in range(pps)], axis=1)

    def score_stage():
        fetch_pages()
        k0 = kt(0, 0)
        blk_lane = lax.broadcasted_iota(jnp.int32, km_ref.shape, 1)
        km = km_ref[...]
        for b in range(bps):
            cols = slice(b * MOBA_BLOCK, (b + 1) * MOBA_BLOCK)
            km = jnp.where(blk_lane == j * bps + b,
                           jnp.sum(k0[:, cols], axis=1, keepdims=True) * (1.0 / MOBA_BLOCK), km)
        km_ref[...] = km
        s_ref[0] = jnp.dot(qbd_ref[0], k0.astype(BF16), preferred_element_type=F32)
        carry, cums = car_ref[...][:, :1], []
        for r in lf_refs:
            cums.append(_lane_cumsum(r[0, 0]) + carry)
            carry = cums[-1][:, LANES - 1:LANES]
        car_ref[...] = jnp.broadcast_to(carry, car_ref.shape)
        s = jnp.dot(qbd_ref[1], kt(1, 0).astype(BF16), preferred_element_type=F32)
        s_ref[1] = s - _rows_from_heads(jnp.concatenate(cums, axis=1) * LOG2E, t_new)
        s = jnp.dot(qbd_ref[2], kt(2, 0).astype(BF16), preferred_element_type=F32)
        s_ref[2] = s + jnp.concatenate([bias_ref[0]] * N_HEADS, axis=0)
        for m in range(N_MIXERS):
            vt_ref[m] = kt(m, KV_WIDTH).astype(BF16)

    def update_stage(jp):
        s, v0 = s_ref[0], vt_ref[0]
        ms, ls = ms_ref[...], ls_ref[...]
        for b in range(bps):
            n = jp * bps + b
            cols = slice(b * MOBA_BLOCK, (b + 1) * MOBA_BLOCK)
            m_n = jnp.max(s[:, cols], axis=-1, keepdims=True)
            p = jnp.exp2(s[:, cols] - m_n)
            ms = jnp.where(lane == n, m_n, ms)
            ls = jnp.where(lane == n, jnp.sum(p, axis=-1, keepdims=True), ls)
            accs_ref[n] = lax.dot_general(p.astype(BF16), v0[:, cols], _NT, preferred_element_type=F32)
        ms_ref[...], ls_ref[...] = ms, ls
        pv_nt = lambda m: (lambda p: lax.dot_general(p, vt_ref[m], _NT, preferred_element_type=F32))
        _online_update(s_ref[1], pv_nt(1), fm_ref, fl_ref, facc_ref)
        _online_update(s_ref[2], pv_nt(2), dm_ref, dl_ref, dacc_ref)

    @pl.when(j == 0)
    def _():
        init()
        score_stage()

    @pl.when((j > 0) & (j < n_steps))
    def _():
        update_stage(j - 1)
        score_stage()

    @pl.when(j == n_steps)
    def _():
        update_stage(j - 1)
        nk = lambda m: nkv_ref[0, :, m * MIXER_COLS:m * MIXER_COLS + KV_WIDTH]
        nv = lambda m: nkv_ref[0, :, m * MIXER_COLS + KV_WIDTH:(m + 1) * MIXER_COLS]
        pv = lambda m: (lambda p: jnp.dot(p, nv(m), preferred_element_type=F32))
        row_t = lax.broadcasted_iota(jnp.int32, (rows, LANES), 0) & (t_new - 1)
        causal = (lane <= row_t) & (lane < t_new)

        s = lax.dot_general(qbd_ref[0], nk(0), _NT, preferred_element_type=F32)
        s = jnp.where(causal, s, NEG)
        m_o = jnp.max(s, axis=-1, keepdims=True)
        p = jnp.exp2(s - m_o)
        l_o = jnp.sum(p, axis=-1, keepdims=True)
        acc_o = pv(0)(p.astype(BF16))
        gate = jnp.dot(qbdf_ref[...], km_ref[...], precision=HIGHEST, preferred_element_type=F32)
        gate = jnp.where(lane < nb, gate, -jnp.inf)
        lane_f = lane.astype(F32)
        sel = lane < 0
        for _ in range(topk):
            mx = jnp.max(gate, axis=-1, keepdims=True)
            is_max = (gate == mx) & (mx > -jnp.inf)
            first = jnp.min(jnp.where(is_max, lane_f, 1e9), axis=-1, keepdims=True)
            pick = lane_f == first
            sel = sel | pick
            gate = jnp.where(pick, -jnp.inf, gate)
        ms = ms_ref[...]
        m_all = jnp.maximum(m_o, jnp.max(jnp.where(sel, ms, -jnp.inf), axis=-1, keepdims=True))
        w = jnp.where(sel, jnp.exp2(ms - m_all), 0.0)
        w_o = jnp.exp2(m_o - m_all)
        l_all = jnp.sum(w * ls_ref[...], axis=-1, keepdims=True) + w_o * l_o

        def merge(n, o):
            wn = jnp.sum(jnp.where(lane == n, w, 0.0), axis=-1, keepdims=True)
            return o + wn * accs_ref[n]

        o_a = lax.fori_loop(0, nb, merge, w_o * acc_o) / l_all
        oa_ref[0] = _token_major(o_a, t_new)

        cn = _lane_cumsum(nlf_ref[0]) + car_ref[...]
        s = lax.dot_general(qbd_ref[1], nk(1), _NT, preferred_element_type=F32) - _rows_from_heads(cn * LOG2E, t_new)
        _online_update(jnp.where(causal, s, NEG), pv(1), fm_ref, fl_ref, facc_ref)
        wide = lambda l_ref: jnp.concatenate([l_ref[...]] * (KV_WIDTH // LANES), axis=1)
        ob_ref[0] = _token_major(facc_ref[...] / wide(fl_ref), t_new)

        s = lax.dot_general(qbd_ref[2], nk(2), _NT, preferred_element_type=F32)
        s = s + jnp.concatenate([bias_ref[0, :, :LANES]] * N_HEADS, axis=0)
        _online_update(s, pv(2), dm_ref, dl_ref, dacc_ref)
        oc_ref[0] = _token_major(dacc_ref[...] / wide(dl_ref), t_new)


def _sample_attn(cache_t, cache_lft, page_table, layer, bias, qa, qb, qc, qaf, nkv, nlf):
    nseq, n_pages = page_table.shape
    t_new = qa.shape[1]
    nb = n_pages // (MOBA_BLOCK // PAGE_SIZE)
    pps = PAGES_PER_STEP
    rows = N_HEADS * t_new
    width = cache_t.shape[2]
    last = n_pages - 1
    page = lambda k: (lambda i, j, pt: (layer, pt[i, jnp.minimum(pps * j + k, last)], 0, 0))
    seq_blk = lambda shape: pl.BlockSpec((1,) + shape, lambda i, j, pt: (i, 0, 0))
    grid_spec = pltpu.PrefetchScalarGridSpec(
        num_scalar_prefetch=1,
        grid=(nseq, n_pages // pps + 1),
        in_specs=[pl.BlockSpec(memory_space=pl.ANY)] + [
            pl.BlockSpec((1, 1, N_HEADS, PAGE_SIZE), page(k)) for k in range(pps)] + [
            pl.BlockSpec((1, t_new, pps * PAGE_SIZE), lambda i, j, pt: (i, 0, j)),
            seq_blk((t_new, MIX_WIDTH)), seq_blk((t_new, MIX_WIDTH)), seq_blk((t_new, MIX_WIDTH)),
            seq_blk((t_new, MIX_WIDTH)),
            seq_blk((PAGE_SIZE, width)), seq_blk((N_HEADS, PAGE_SIZE)),
        ],
        out_specs=[seq_blk((t_new, MIX_WIDTH))] * 3,
        scratch_shapes=[
            pltpu.VMEM((N_MIXERS, rows, KV_WIDTH), BF16), pltpu.VMEM((rows, KV_WIDTH), F32),
            pltpu.VMEM((KV_WIDTH, KMEAN_ROWS), F32), pltpu.VMEM((rows, LANES), F32), pltpu.VMEM((rows, LANES), F32),
            pltpu.VMEM((nb, rows, KV_WIDTH), F32),
            pltpu.VMEM((rows, LANES), F32), pltpu.VMEM((rows, LANES), F32), pltpu.VMEM((rows, KV_WIDTH), F32),
            pltpu.VMEM((N_HEADS, LANES), F32),
            pltpu.VMEM((rows, LANES), F32), pltpu.VMEM((rows, LANES), F32), pltpu.VMEM((rows, KV_WIDTH), F32),
            pltpu.VMEM((N_MIXERS, rows, pps * PAGE_SIZE), F32), pltpu.VMEM((N_MIXERS, KV_WIDTH, pps * PAGE_SIZE), BF16),
            pltpu.VMEM((2, pps, width, PAGE_SIZE), F32), pltpu.SemaphoreType.DMA((2, pps)),
        ],
    )
    kern = functools.partial(_sample_attn_kernel, nb=nb, pps=pps, t_new=t_new, topk=min(MOBA_TOPK, nb + 1),
                             layer=layer)
    out = jax.ShapeDtypeStruct((nseq, t_new, MIX_WIDTH), F32)
    return pl.pallas_call(
        kern,
        grid_spec=grid_spec,
        out_shape=[out, out, out],
        compiler_params=_cp("arbitrary", "arbitrary"),
        name="sample_attn",
    )(page_table, cache_t, *([cache_lft] * pps), bias, qa, qb, qc, qaf, nkv, nlf)


def _out_kernel(oa_ref, ob_ref, oc_ref, sz_ref, sg_ref, x_ref, gate_ref, wb_ref, wo_ref, fw_ref, y_ref,
                *, final):
    d = x_ref.shape[2]
    merged = None
    for i, o_ref in enumerate((oa_ref, ob_ref, oc_ref)):
        t = (o_ref[0].astype(F32) * sz_ref[0, :, i * MIX_WIDTH:(i + 1) * MIX_WIDTH].astype(F32)).astype(BF16)
        br = sg_ref[0, :, i * d:(i + 1) * d].astype(F32) * jnp.dot(t, wb_ref[i], preferred_element_type=F32)
        merged = br if merged is None else merged + br
    y = x_ref[0] + gate_ref[0] * jnp.dot(merged.astype(BF16), wo_ref[...], preferred_element_type=F32)
    if final:
        ms = jnp.mean(y * y, axis=-1, keepdims=True)
        y = y * lax.rsqrt(ms + NORM_EPS) * fw_ref[...]
    y_ref[0] = y


def _out_proj(oa, ob, oc, sz, sg, x3, gate3, wb, wo, fw, tm, final):
    g, r, d = x3.shape
    rm = gate3.shape[1]
    tmod = 1 if rm == 1 else tm
    mod_map = (lambda b, i: (b, 0, 0)) if rm == 1 else (lambda b, i: (b, i, 0))
    row = lambda w: pl.BlockSpec((1, tm, w), lambda b, i: (b, i, 0))
    as3 = lambda a: a.reshape(g, r, a.shape[-1])
    return pl.pallas_call(
        functools.partial(_out_kernel, final=final),
        grid=(g, r // tm),
        in_specs=[
            row(MIX_WIDTH), row(MIX_WIDTH), row(MIX_WIDTH), row(N_MIXERS * MIX_WIDTH), row(N_MIXERS * d), row(d),
            pl.BlockSpec((1, tmod, d), mod_map),
            pl.BlockSpec(wb.shape, lambda b, i: (0, 0, 0)),
            pl.BlockSpec(wo.shape, lambda b, i: (0, 0)),
            pl.BlockSpec((1, d), lambda b, i: (0, 0)),
        ],
        out_specs=row(d),
        out_shape=jax.ShapeDtypeStruct((g, r, d), F32),
        compiler_params=_cp("arbitrary", "arbitrary"),
        name="out_proj",
    )(as3(oa), as3(ob), as3(oc), as3(sz), as3(sg), x3, gate3, wb, wo, fw.reshape(1, d))


def _rope_tables(pos):
    half = ROT_DIM // 2
    expo = jnp.arange(0, ROT_DIM, 2, dtype=F32) / ROT_DIM
    inv_freq = jnp.power(jnp.float32(ROPE_THETA), -expo)
    ang = pos.astype(F32)[:, None] * inv_freq[None, :]
    cos, sin = jnp.cos(ang), jnp.sin(ang)
    n = pos.shape[0]
    rest = HEAD_DIM - ROT_DIM
    a = jnp.concatenate([cos, cos, jnp.ones((n, rest), F32)], axis=1)
    bm = jnp.concatenate([-sin, jnp.zeros((n, half + rest), F32)], axis=1)
    cm = jnp.concatenate([jnp.zeros((n, half), F32), sin, jnp.zeros((n, rest), F32)], axis=1)
    rep = LANES // HEAD_DIM
    return tuple(jnp.tile(t, (1, rep)) for t in (a, bm, cm))


def _split_weights(w_in_l, b_f_l):
    o = np.cumsum([0, MIX_WIDTH, KV_WIDTH, KV_WIDTH, MIX_WIDTH,
                   MIX_WIDTH, KV_WIDTH, KV_WIDTH, N_HEADS, MIX_WIDTH,
                   MIX_WIDTH, KV_WIDTH, KV_WIDTH, IDX_HEADS * IDX_DIM, IDX_DIM, IDX_HEADS, MIX_WIDTH]).tolist()
    col = lambda i: w_in_l[:, o[i]:o[i + 1]]
    qa, ka, va, za, qb, kb, vb, fb, zb, qc, kc, vc, qi, ki, wi, zc = (col(i) for i in range(16))
    g = w_in_l[:, o[16]:]
    d = w_in_l.shape[0]
    zpad = lambda w: jnp.zeros((d, w), w_in_l.dtype)
    w_q = jnp.concatenate([qa, qb, qc], axis=1)
    w_kv = jnp.concatenate([ka, va, kb, vb, kc, vc], axis=1)
    w_z = jnp.concatenate([za, zb, zc], axis=1)
    w_misc = jnp.concatenate([qi, ki, zpad(LANES - IDX_DIM), wi, zpad(LANES - IDX_HEADS),
                              fb, zpad(LANES - N_HEADS)], axis=1)
    bf_row = jnp.concatenate([b_f_l, jnp.zeros((LANES - N_HEADS,), b_f_l.dtype)]).reshape(1, LANES)
    cast = lambda w: w.astype(BF16)
    return cast(w_q), cast(w_kv), cast(w_z), cast(g), cast(w_misc), bf_row.astype(F32)


def _project(x3, scale3, shift3, tabs, weights, norm_w_l, tm, kv_out=None):
    g, r, d = x3.shape
    w_q, w_kv, w_z, w_g, w_misc, bf_row = weights
    h = _norm_mod(x3, norm_w_l, scale3, shift3, tm).reshape(g * r, d)
    q_outs = _proj_call(
        _projq_kernel, h, w_q, tabs, (), [(MIX_WIDTH, BF16, True)] * 3 + [(MIX_WIDTH, F32, True)],
        tm, w_q.shape[1], "proj_q")
    if kv_out is None:
        kv, kvb = _proj_call(
            _projkv_kernel, h, w_kv, tabs, (), [(w_kv.shape[1], F32, True), (w_kv.shape[1], BF16, True)],
            tm, w_kv.shape[1], "proj_kv")
    else:
        kvt_prev, layer, depth = kv_out
        kvt, kvb, km = _proj_kv_prompt(h, w_kv, tabs, kvt_prev, layer, depth, g, r, tm)
        kv = (kvt, km)
    (sz,) = _proj_call(_projz_kernel, h, w_z, (), (), [(w_z.shape[1], BF16, True)], tm, w_z.shape[1], "proj_z")
    (sg,) = _proj_call(_projg_kernel, h, w_g, (), (), [(w_g.shape[1], BF16, False)], tm, d, "proj_g")
    misc = _proj_call(
        _projmisc_kernel, h, w_misc, tabs, (bf_row,),
        [(MIX_WIDTH, BF16, True), (IDX_DIM, F32, True), (2 * IDX_DIM, BF16, True),
         (IDX_HEADS, F32, True), (N_HEADS, F32, True)],
        tm, MISC_N, "proj_misc")
    return q_outs, kv, kvb, sz, sg, misc


def _prompt_layer(x3, scale3, shift3, gate3, tabs, weights, norm_w_l, wb_l, wo_l, fw, kvt_prev, layer, depth,
                  *, tm, tm_out, tq, tk, final):
    b, t, _ = x3.shape
    (qa, qb, qc, qaf), (kvt, km8), kvb, sz, sg, (qi, ki, ki2, wi, lf) = _project(
        x3, scale3, shift3, tabs, weights, norm_w_l, tm, kv_out=(kvt_prev, layer, depth))
    s3 = lambda a: a.reshape(b, t, a.shape[-1])
    kvb3, lf3 = s3(kvb), s3(lf)
    km = km8[:, :tm // MOBA_BLOCK].reshape(b, t // MOBA_BLOCK, KV_WIDTH)
    km = jnp.pad(km, ((0, 0), (0, KMEAN_ROWS - km.shape[1]), (0, 0)))
    cum = _cumsum(jnp.swapaxes(lf3, 1, 2))
    o_a = _moba_attn(s3(qaf), s3(qa), km, kvb3, tq)
    o_b = _fox_attn(s3(qb), cum, kvb3, tq, tk)
    o_c = _dsa_attn(s3(qi), s3(wi), s3(ki2), s3(qc), kvb3, tq, tk)
    x_new = _out_proj(o_a, o_b, o_c, sz, sg, x3, gate3, wb_l, wo_l, fw, tm_out, final)
    return x_new, kvt, lf3, s3(ki)


def _sample_layer(x3, scale3, shift3, gate3, tabs, weights, norm_w_l, wb_l, wo_l, fw, caches, page_table, layer,
                  *, nseq, final):
    _, m, _ = x3.shape
    t = m // nseq
    cache_t, cache_kxt, cache_lft = caches
    (qa, qb, qc, qaf), kv, kvb, sz, sg, (qi, ki, ki2, wi, lf) = _project(
        x3, scale3, shift3, tabs, weights, norm_w_l, m)
    s3 = lambda a: a.reshape(nseq, t, a.shape[-1])
    pad_slots = lambda a: jnp.pad(a, ((0, 0), (0, PAGE_SIZE - t), (0, 0)))
    lp = (page_table.shape[1] + PAGES_PER_STEP) * PAGE_SIZE
    isc = _sample_index(cache_kxt, page_table, layer, s3(qi), s3(wi), pad_slots(s3(ki2)), lp)
    bias = _topk_bias(isc.reshape(m, lp), page_table.shape[1] * PAGE_SIZE, t, min(m, 64)).reshape(nseq, t, lp)
    nlf = jnp.pad(jnp.swapaxes(s3(lf), 1, 2), ((0, 0), (0, 0), (0, PAGE_SIZE - t)))
    o_a, o_b, o_c = _sample_attn(cache_t, cache_lft, page_table, layer, bias, s3(qa), s3(qb), s3(qc), s3(qaf),
                                 pad_slots(s3(kvb)), nlf)
    x_new = _out_proj(o_a, o_b, o_c, sz, sg, x3, gate3, wb_l, wo_l, fw, m, final)
    return x_new, s3(kv), s3(lf), s3(ki)


def kernel(x_prompt, x_sample, cache_kv, cache_logf, cache_kidx, page_table, c_prompt, c_sample,
           norm_w, w_ada, b_ada, w_in, b_f, w_branch, w_out, final_norm_w):
    depth = norm_w.shape[0]
    bp, tp, d = x_prompt.shape
    bs, ts, _ = x_sample.shape
    n_pool = cache_kv.shape[1]
    n_pages = page_table.shape[1]
    past_len = n_pages * PAGE_SIZE
    assert ts == SUBLANES and n_pages % PAGES_PER_STEP == 0 and PAGES_PER_STEP % (MOBA_BLOCK // PAGE_SIZE) == 0
    assert tp % MOBA_BLOCK == 0 and tp % min(PROJ_ROWS, tp) == 0 and tp % min(512, tp) == 0

    nc = bp + bs
    rc = -(-nc // SUBLANES) * SUBLANES
    c_all = jnp.concatenate([c_prompt, c_sample, jnp.zeros((rc - nc, d), F32)], axis=0)
    mod = _modulation(c_all, w_ada, b_ada)

    tabs_p = _rope_tables(jnp.arange(tp, dtype=jnp.int32))
    tabs_s = _rope_tables(jnp.tile(past_len + jnp.arange(ts, dtype=jnp.int32), bs))

    cache_t = jnp.transpose(cache_kv, (0, 1, 3, 4, 5, 6, 2)).reshape(depth, n_pool, N_MIXERS * MIXER_COLS, PAGE_SIZE)
    cache_kxt = jnp.swapaxes(cache_kidx, 2, 3)
    cache_lft = jnp.swapaxes(cache_logf, 2, 3)

    ms = bs * ts
    xp = x_prompt
    xs = x_sample.reshape(1, ms, d)
    new_p, new_s = [], []
    kvt = jnp.zeros((depth, bp, N_MIXERS * MIXER_COLS, tp), F32)
    for l in range(depth):
        weights = _split_weights(w_in[l], b_f[l])
        wb_l = w_branch[l].astype(BF16)
        wo_l = w_out[l].astype(BF16)
        final = l == depth - 1
        shift, scale, gate = (mod[l, :, i * d:(i + 1) * d] for i in range(3))
        p3 = lambda a: a[:bp].reshape(bp, 1, d)
        s3 = lambda a: jnp.repeat(a[bp:nc], ts, axis=0).reshape(1, ms, d)

        xs, kv_s, lf_s, ki_s = _sample_layer(
            xs, s3(scale), s3(shift), s3(gate), tabs_s, weights, norm_w[l], wb_l, wo_l, final_norm_w,
            (cache_t, cache_kxt, cache_lft), page_table, l, nseq=bs, final=final)
        xp, kvt, lf_p, ki_p = _prompt_layer(
            xp, p3(scale), p3(shift), p3(gate), tabs_p, weights, norm_w[l], wb_l, wo_l, final_norm_w, kvt, l, depth,
            tm=min(PROJ_ROWS, tp), tm_out=min(512, tp), tq=min(128, tp), tk=MOBA_BLOCK, final=final)
        new_p.append((None, lf_p, ki_p))
        new_s.append((kv_s, lf_s, ki_s))

    kv_shape = (N_MIXERS, 2, N_KV_HEADS, HEAD_DIM)
    stack = lambda items, i: jnp.stack([n[i] for n in items])
    kv_prompt = jnp.transpose(kvt.reshape(depth, bp, *kv_shape, tp), (0, 1, 6, 2, 3, 4, 5))
    kv_sample = stack(new_s, 0).reshape(depth, bs, ts, *kv_shape)
    return (xp, xs.reshape(bs, ts, d), kv_prompt, stack(new_p, 1), stack(new_p, 2),
            kv_sample, stack(new_s, 1), stack(new_s, 2))
```
